```python
import math
import jax
import jax.numpy as jnp
from jax import lax
import numpy as np

D_MODEL = 1024
BATCH = 16
SEQ = 256
DEPTH = 4
DEC_BATCH = 8
DEC_SEQ = 4096
PAST_LEN = 256

GRID_W = 64
HEAD_DIM = 64
ROPE_BASE = 10000.0
Q_BLOCK = 128
NORM_EPS = 1e-6
NEG_INF = -1e30

DA_HEADS = 4
DA_QK = 32
DA_V = 2 * DA_QK
NA_HEADS = 4
NA_WIN_R = 8
NA_WIN_C = 16
NA_QCOLS = 16
NA_KCOLS = 2 * NA_WIN_C
SW_HEADS = 4
SW_KV_HEADS = 2
SW_WINDOW = 128
SW_BLOCK = SW_WINDOW
MLA_HEADS = 4
MLA_Q_RANK = 256
MLA_KV_RANK = 128
MLA_NOPE = 64
MLA_ROPE = 32
MLA_V = 64
N_BRANCH = 4
BRANCH_WIDTH = D_MODEL // 4
N_EXPERTS = 16
EC_CAPACITY = 2
D_EXPERT = 1024

IN_SIZES = (
    DA_HEADS * 2 * DA_QK, DA_HEADS * 2 * DA_QK, DA_HEADS * DA_V,
    NA_HEADS * HEAD_DIM, NA_HEADS * HEAD_DIM, NA_HEADS * HEAD_DIM,
    SW_HEADS * HEAD_DIM, SW_KV_HEADS * HEAD_DIM, SW_KV_HEADS * HEAD_DIM,
    MLA_Q_RANK, MLA_KV_RANK, MLA_ROPE,
)
IN_DIM = sum(IN_SIZES)

kernel_name = "hybrid_diffusion_prefix_step"


def rms_norm(x, g):
    xf = x.astype(jnp.float32)
    y = xf * lax.rsqrt(jnp.mean(xf * xf, axis=-1, keepdims=True) + NORM_EPS)
    return (y * g.astype(jnp.float32)).astype(x.dtype)


def split_cols(u, sizes):
    parts, off = [], 0
    for s in sizes:
        parts.append(u[..., off:off + s])
        off += s
    return parts


def rope_1d(x, pos):
    half = x.shape[-1] // 2
    freqs = jnp.float32(ROPE_BASE) ** (-jnp.arange(half, dtype=jnp.float32) / half)
    ang = pos.astype(jnp.float32)[:, None] * freqs
    ang = ang.reshape((pos.shape[0],) + (1,) * (x.ndim - 3) + (half,))
    cos, sin = jnp.cos(ang), jnp.sin(ang)
    x1 = x[..., :half].astype(jnp.float32)
    x2 = x[..., half:].astype(jnp.float32)
    return jnp.concatenate([x1 * cos - x2 * sin, x2 * cos + x1 * sin], axis=-1).astype(x.dtype)


def rope_2d(x):
    n = x.shape[1]
    t = jnp.arange(n)
    half = x.shape[-1] // 2
    return jnp.concatenate([rope_1d(x[..., :half], t // GRID_W), rope_1d(x[..., half:], t % GRID_W)], axis=-1)


def sweep_query_blocks(fn, q):
    B, N = q.shape[:2]
    nb = N // Q_BLOCK
    qb = jnp.moveaxis(q.reshape((B, nb, Q_BLOCK) + q.shape[2:]), 1, 0)
    out = jnp.moveaxis(lax.map(fn, qb), 0, 1)
    return out.reshape((B, N) + out.shape[3:])


def dense_attention(q, k, v):
    scale = q.shape[-1] ** -0.5

    def block(qb):
        s = jnp.einsum("bqhd,bkhd->bhqk", qb, k, preferred_element_type=jnp.float32) * scale
        p = jax.nn.softmax(s, axis=-1)
        return jnp.einsum("bhqk,bkhd->bqhd", p.astype(v.dtype), v)

    return sweep_query_blocks(block, q)


def diff_attention(q, k, v, lam):
    scale = DA_QK ** -0.5

    def block(qb):
        s = jnp.einsum("bqhmd,bkhmd->bhmqk", qb, k, preferred_element_type=jnp.float32) * scale
        p = jax.nn.softmax(s, axis=-1)
        w = p[:, :, 0] - lam * p[:, :, 1]
        return jnp.einsum("bhqk,bkhd->bqhd", w.astype(v.dtype), v)

    return sweep_query_blocks(block, q)


def sink_softmax(s, sink):
    sink_col = jnp.broadcast_to(sink.astype(jnp.float32), s.shape[:-1] + (1,))
    return jax.nn.softmax(jnp.concatenate([s, sink_col], axis=-1), axis=-1)[..., :-1]


def gqa_sink_dense(q, k, v, sink):
    B, N, H, d = q.shape
    G = H // SW_KV_HEADS
    qg = q.reshape(B, N, SW_KV_HEADS, G, d)
    s = jnp.einsum("bqhgd,bkhd->bhgqk", qg, k, preferred_element_type=jnp.float32) * d ** -0.5
    p = sink_softmax(s, sink.reshape(SW_KV_HEADS, G, 1, 1))
    o = jnp.einsum("bhgqk,bkhd->bqhgd", p.astype(v.dtype), v)
    return o.reshape(B, N, H, d)


def window_gqa_sink(q, k, v, ck, cv, sink):
    B, N, H, d = q.shape
    KVH = SW_KV_HEADS
    G = H // KVH
    blk = SW_BLOCK
    nb = N // blk
    scale = d ** -0.5
    qb = q.reshape(B, nb, blk, KVH, G, d)

    def band(t):
        tp = jnp.pad(t, ((0, 0), (blk, blk), (0, 0), (0, 0))).reshape(B, nb + 2, blk, KVH, d)
        return jnp.concatenate([tp[:, :-2], tp[:, 1:-1], tp[:, 2:]], axis=2)

    kb, vb = band(k), band(v)
    s_loc = jnp.einsum("bnqhgd,bnkhd->bnhgqk", qb, kb, preferred_element_type=jnp.float32) * scale
    qpos = jnp.arange(nb)[:, None] * blk + jnp.arange(blk)
    kpos = jnp.arange(nb)[:, None] * blk - blk + jnp.arange(3 * blk)
    valid = ((kpos[:, None, :] >= 0) & (kpos[:, None, :] < N)
             & (jnp.abs(qpos[:, :, None] - kpos[:, None, :]) <= SW_WINDOW))
    s_loc = jnp.where(valid[None, :, None, None], s_loc, NEG_INF)
    s_ctx = jnp.einsum("bnqhgd,bphd->bnhgqp", qb, ck, preferred_element_type=jnp.float32) * scale
    p = sink_softmax(jnp.concatenate([s_loc, s_ctx], axis=-1), sink.reshape(KVH, G, 1, 1))
    nl = 3 * blk
    o = (jnp.einsum("bnhgqk,bnkhd->bnqhgd", p[..., :nl].astype(v.dtype), vb)
         + jnp.einsum("bnhgqp,bphd->bnqhgd", p[..., nl:].astype(cv.dtype), cv))
    return o.reshape(B, N, H, d)


def neighbourhood_attention(q, k, v, ck, cv, rpb):
    B, N, H, d = q.shape
    R = N // GRID_W
    WR = min(NA_WIN_R, R)
    ncb = GRID_W // NA_QCOLS
    scale = d ** -0.5
    qg = q.reshape(B, R, ncb, NA_QCOLS, H, d)
    kg = k.reshape(B, R, GRID_W, H, d)
    vg = v.reshape(B, R, GRID_W, H, d)
    rows = jnp.arange(R)
    row_start = jnp.clip(rows - WR // 2, 0, R - WR)
    qcol = jnp.arange(GRID_W).reshape(ncb, NA_QCOLS)
    band_start = jnp.clip(jnp.arange(ncb) * NA_QCOLS - NA_WIN_C // 2, 0, GRID_W - NA_KCOLS)
    kcol = band_start[:, None] + jnp.arange(NA_KCOLS)
    col_start = jnp.clip(qcol - NA_WIN_C // 2, 0, GRID_W - NA_WIN_C)
    col_ok = ((kcol[:, None, :] >= col_start[..., None])
              & (kcol[:, None, :] < col_start[..., None] + NA_WIN_C))
    dc_idx = jnp.clip(kcol[:, None, :] - qcol[:, :, None] + NA_WIN_C - 1, 0, 2 * NA_WIN_C - 2)
    n_loc = WR * NA_KCOLS

    def one_row(args):
        r, rs, q_row = args
        kb = lax.dynamic_slice_in_dim(kg, rs, WR, axis=1)[:, :, kcol]
        vb = lax.dynamic_slice_in_dim(vg, rs, WR, axis=1)[:, :, kcol]
        s_loc = jnp.einsum("bjqhd,brjkhd->bhjqrk", q_row, kb, preferred_element_type=jnp.float32) * scale
        dr_idx = rs + jnp.arange(WR) - r + NA_WIN_R - 1
        bias = rpb[:, dr_idx[None, None, :, None], dc_idx[:, :, None, :]]
        s_loc = jnp.where(col_ok[:, :, None, :], s_loc + bias.astype(jnp.float32), NEG_INF)
        s_loc = s_loc.reshape(B, H, ncb, NA_QCOLS, n_loc)
        s_ctx = jnp.einsum("bjqhd,bphd->bhjqp", q_row, ck, preferred_element_type=jnp.float32) * scale
        p = jax.nn.softmax(jnp.concatenate([s_loc, s_ctx], axis=-1), axis=-1)
        p_loc = p[..., :n_loc].reshape(B, H, ncb, NA_QCOLS, WR, NA_KCOLS)
        return (jnp.einsum("bhjqrk,brjkhd->bjqhd", p_loc.astype(vb.dtype), vb)
                + jnp.einsum("bhjqp,bphd->bjqhd", p[..., n_loc:].astype(cv.dtype), cv))

    out = lax.map(one_row, (rows, row_start, jnp.moveaxis(qg, 1, 0)))
    return jnp.moveaxis(out, 0, 1).reshape(B, N, H, d)


def token_mixing(h, lp, lam_init, ctx):
    B, N, _ = h.shape
    latent = ctx is not None
    u = h @ lp["w_in"]
    a_q, a_k, a_v, b_q, b_k, b_v, c_q, c_k, c_v, d_cq, d_ckv, d_kpe = split_cols(u, IN_SIZES)
    a_q = a_q.reshape(B, N, DA_HEADS, 2, DA_QK)
    a_k = a_k.reshape(B, N, DA_HEADS, 2, DA_QK)
    a_v = a_v.reshape(B, N, DA_HEADS, DA_V)
    b_q = b_q.reshape(B, N, NA_HEADS, HEAD_DIM)
    b_k = b_k.reshape(B, N, NA_HEADS, HEAD_DIM)
    b_v = b_v.reshape(B, N, NA_HEADS, HEAD_DIM)
    c_q = c_q.reshape(B, N, SW_HEADS, HEAD_DIM)
    c_k = c_k.reshape(B, N, SW_KV_HEADS, HEAD_DIM)
    c_v = c_v.reshape(B, N, SW_KV_HEADS, HEAD_DIM)
    d_ckv = rms_norm(d_ckv, lp["mla_kv_norm"])
    q_mla = (rms_norm(d_cq, lp["mla_q_norm"]) @ lp["mla_w_uq"]).reshape(B, N, MLA_HEADS, MLA_NOPE + MLA_ROPE)
    q_nope, q_pe = q_mla[..., :MLA_NOPE], q_mla[..., MLA_NOPE:]

    if latent:
        a_q, a_k, c_q, c_k = rope_2d(a_q), rope_2d(a_k), rope_2d(c_q), rope_2d(c_k)
        q_pe = rope_2d(q_pe)
        d_kpe = rope_2d(d_kpe[:, :, None, :])[:, :, 0]
        ctx_t = ctx
        ck_a, cv_a, ck_b, cv_b, ck_c, cv_c, ck_d, kpe_d = ctx
        ka = jnp.concatenate([ck_a.reshape(B, -1, DA_HEADS, 2, DA_QK), a_k], axis=1)
        va = jnp.concatenate([cv_a, a_v], axis=1)
        o_b = neighbourhood_attention(b_q, b_k, b_v, ck_b, cv_b, lp["na_rpb"])
        o_c = window_gqa_sink(c_q, c_k, c_v, ck_c, cv_c, lp["sw_sink"])
        ckv_all = jnp.concatenate([ck_d, d_ckv], axis=1)
        kpe_all = jnp.concatenate([kpe_d, d_kpe], axis=1)
    else:
        ctx_t = (a_k.reshape(B, N, DA_HEADS, 2 * DA_QK), a_v, b_k, b_v, c_k, c_v, d_ckv, d_kpe)
        ka, va = a_k, a_v
        o_b = dense_attention(b_q, b_k, b_v)
        o_c = gqa_sink_dense(c_q, c_k, c_v, lp["sw_sink"])
        ckv_all, kpe_all = d_ckv, d_kpe

    lq1, lk1, lq2, lk2 = lp["da_lambda"].astype(jnp.float32)
    lam = jnp.exp(jnp.sum(lq1 * lk1)) - jnp.exp(jnp.sum(lq2 * lk2)) + lam_init
    o_a = rms_norm(diff_attention(a_q, ka, va, lam), lp["da_subln"]) * (1.0 - lam_init)

    K = ckv_all.shape[1]
    kv = (ckv_all @ lp["mla_w_ukv"]).reshape(B, K, MLA_HEADS, MLA_NOPE + MLA_V)
    k_mla = jnp.concatenate(
        [kv[..., :MLA_NOPE], jnp.broadcast_to(kpe_all[:, :, None, :], (B, K, MLA_HEADS, MLA_ROPE))], axis=-1)
    o_d = dense_attention(jnp.concatenate([q_nope, q_pe], axis=-1), k_mla, kv[..., MLA_NOPE:])

    gates = jax.nn.sigmoid(h @ lp["w_gate"]).reshape(B, N, N_BRANCH, D_MODEL)
    merged = None
    for i, o in enumerate((o_a, o_b, o_c, o_d)):
        term = gates[:, :, i] * (o.reshape(B, N, BRANCH_WIDTH) @ lp["w_branch"][i])
        merged = term if merged is None else merged + term
    return merged @ lp["w_out"], ctx_t


def expert_choice_ffn(h, w_router, w_g, w_u, w_d):
    B, N, D = h.shape
    T = B * N
    xt = h.reshape(T, D)
    aff = jax.nn.softmax((xt @ w_router).astype(jnp.float32), axis=-1)
    cap = EC_CAPACITY * T // N_EXPERTS
    gate, idx = lax.top_k(aff.T, cap)
    xe = xt[idx]
    hg = jnp.einsum("ecd,edf->ecf", xe, w_g)
    hu = jnp.einsum("ecd,edf->ecf", xe, w_u)
    y = jnp.einsum("ecf,efd->ecd", jax.nn.silu(hg) * hu, w_d)
    y = y * gate[..., None].astype(y.dtype)
    out = jnp.zeros_like(xt).at[idx.reshape(-1)].add(y.reshape(-1, D))
    return out.reshape(B, N, D)


def trunk_layer(x, mod, lp, lam_init, ctx):
    shift1, scale1, gate1, shift2, scale2, gate2 = jnp.split(mod, 6, axis=-1)
    h = rms_norm(x, lp["g_norm1"]) * (1 + scale1) + shift1
    mixed, ctx_t = token_mixing(h, lp, lam_init, ctx)
    x = x + gate1 * mixed
    h = rms_norm(x, lp["g_norm2"]) * (1 + scale2) + shift2
    x = x + gate2 * expert_choice_ffn(h, lp["w_router"], lp["w_e_gate"], lp["w_e_up"], lp["w_e_down"])
    return x, ctx_t


def setup_inputs(seed: int = 0) -> dict:
    key = jax.random.key(seed)
    ks = jax.random.split(key, 40)

    def nrm(i, shape, scale):
        return jax.random.normal(ks[i], shape, jnp.float32) * scale

    D, L, P = D_MODEL, DEPTH, PAST_LEN
    dm = D ** -0.5
    return {
        "x_prompt": nrm(0, (BATCH, SEQ, D), 1.0),
        "x_sample": nrm(1, (DEC_BATCH, DEC_SEQ, D), 1.0),
        "cache_diff_k": nrm(2, (DEC_BATCH, L, P, DA_HEADS, 2 * DA_QK), 1.0),
        "cache_diff_v": nrm(3, (DEC_BATCH, L, P, DA_HEADS, DA_V), 1.0),
        "cache_na_k": nrm(4, (DEC_BATCH, L, P, NA_HEADS, HEAD_DIM), 1.0),
        "cache_na_v": nrm(5, (DEC_BATCH, L, P, NA_HEADS, HEAD_DIM), 1.0),
        "cache_swa_k": nrm(6, (DEC_BATCH, L, P, SW_KV_HEADS, HEAD_DIM), 1.0),
        "cache_swa_v": nrm(7, (DEC_BATCH, L, P, SW_KV_HEADS, HEAD_DIM), 1.0),
        "cache_mla_ckv": nrm(8, (DEC_BATCH, L, P, MLA_KV_RANK), 1.0),
        "cache_mla_kpe": nrm(9, (DEC_BATCH, L, P, MLA_ROPE), 1.0),
        "c": nrm(10, (DEC_BATCH, D), 1.0),
        "c_ctx": nrm(11, (D,), 1.0),
        "w_mod": nrm(12, (L, D, 6 * D), 0.5 * dm),
        "b_mod": nrm(13, (L, 6 * D), 0.02),
        "g_norm1": 1.0 + nrm(14, (L, D), 0.02),
        "g_norm2": 1.0 + nrm(15, (L, D), 0.02),
        "w_in": nrm(16, (L, D, IN_DIM), dm),
        "da_lambda": nrm(17, (L, 4, DA_QK), 0.1),
        "da_subln": 1.0 + nrm(18, (L, DA_V), 0.02),
        "na_rpb": nrm(19, (L, NA_HEADS, 2 * NA_WIN_R - 1, 2 * NA_WIN_C - 1), 0.1),
        "sw_sink": nrm(20, (L, SW_HEADS), 0.5),
        "mla_q_norm": 1.0 + nrm(21, (L, MLA_Q_RANK), 0.02),
        "mla_w_uq": nrm(22, (L, MLA_Q_RANK, MLA_HEADS * (MLA_NOPE + MLA_ROPE)), MLA_Q_RANK ** -0.5),
        "mla_kv_norm": 1.0 + nrm(23, (L, MLA_KV_RANK), 0.02),
        "mla_w_ukv": nrm(24, (L, MLA_KV_RANK, MLA_HEADS * (MLA_NOPE + MLA_V)), MLA_KV_RANK ** -0.5),
        "w_gate": nrm(25, (L, D, N_BRANCH * D), dm),
        "w_branch": nrm(26, (L, N_BRANCH, BRANCH_WIDTH, D), BRANCH_WIDTH ** -0.5),
        "w_out": nrm(27, (L, D, D), dm),
        "w_router": nrm(28, (L, D, N_EXPERTS), dm),
        "w_e_gate": nrm(29, (L, N_EXPERTS, D, D_EXPERT), dm),
        "w_e_up": nrm(30, (L, N_EXPERTS, D, D_EXPERT), dm),
        "w_e_down": nrm(31, (L, N_EXPERTS, D_EXPERT, D), D_EXPERT ** -0.5),
        "g_final": 1.0 + nrm(32, (D,), 0.02),
    }


def reference(x_prompt, x_sample, cache_diff_k, cache_diff_v, cache_na_k, cache_na_v, cache_swa_k,
              cache_swa_v, cache_mla_ckv, cache_mla_kpe, c, c_ctx, w_mod, b_mod, g_norm1, g_norm2, w_in,
              da_lambda, da_subln, na_rpb, sw_sink, mla_q_norm, mla_w_uq, mla_kv_norm, mla_w_ukv, w_gate,
              w_branch, w_out, w_router, w_e_gate, w_e_up, w_e_down, g_final):
    xp, xs = x_prompt, x_sample
    ctx_layers = []
    for l in range(DEPTH):
        lp = {
            "g_norm1": g_norm1[l], "g_norm2": g_norm2[l], "w_in": w_in[l],
            "da_lambda": da_lambda[l], "da_subln": da_subln[l], "na_rpb": na_rpb[l], "sw_sink": sw_sink[l],
            "mla_q_norm": mla_q_norm[l], "mla_w_uq": mla_w_uq[l], "mla_kv_norm": mla_kv_norm[l],
            "mla_w_ukv": mla_w_ukv[l], "w_gate": w_gate[l], "w_branch": w_branch[l], "w_out": w_out[l],
            "w_router": w_router[l], "w_e_gate": w_e_gate[l], "w_e_up": w_e_up[l], "w_e_down": w_e_down[l],
        }
        lam_init = 0.8 - 0.6 * math.exp(-0.3 * l)
        mod_ctx = (jax.nn.silu(c_ctx) @ w_mod[l] + b_mod[l])[None, None, :]
        mod_lat = (jax.nn.silu(c) @ w_mod[l] + b_mod[l])[:, None, :]
        xp, ctx_l = trunk_layer(xp, mod_ctx, lp, lam_init, None)
        ctx_layers.append(ctx_l)
        cached_l = (cache_diff_k[:, l], cache_diff_v[:, l], cache_na_k[:, l], cache_na_v[:, l],
                    cache_swa_k[:, l], cache_swa_v[:, l], cache_mla_ckv[:, l], cache_mla_kpe[:, l])
        xs, _ = trunk_layer(xs, mod_lat, lp, lam_init, cached_l)
    y_prompt = rms_norm(xp, g_final)
    y_sample = rms_norm(xs, g_final)
    new_diff_k = jnp.stack([t[0] for t in ctx_layers], axis=1)
    new_diff_v = jnp.stack([t[1] for t in ctx_layers], axis=1)
    new_na_k = jnp.stack([t[2] for t in ctx_layers], axis=1)
    new_na_v = jnp.stack([t[3] for t in ctx_layers], axis=1)
    new_swa_k = jnp.stack([t[4] for t in ctx_layers], axis=1)
    new_swa_v = jnp.stack([t[5] for t in ctx_layers], axis=1)
    new_mla_ckv = jnp.stack([t[6] for t in ctx_layers], axis=1)
    new_mla_kpe = jnp.stack([t[7] for t in ctx_layers], axis=1)
    return (y_prompt, y_sample, new_diff_k, new_diff_v, new_na_k, new_na_v, new_swa_k, new_swa_v,
            new_mla_ckv, new_mla_kpe)
```

```python
import functools
import math

import numpy as np
import jax
import jax.numpy as jnp
from jax import lax
from jax.experimental import pallas as pl
from jax.experimental.pallas import tpu as pltpu

BF = jnp.bfloat16
F32 = jnp.float32

D_MODEL = 1024
DEPTH = 4
GRID_W = 64
ROPE_BASE = 10000.0
NORM_EPS = 1e-6
NEG_INF = -1e30

DA_QK = 32
NA_WIN_R = 8
NA_WIN_C = 16
SW_WINDOW = 128
MLA_NOPE = 64
MLA_ROPE = 32
N_EXPERTS = 16
EC_CAPACITY = 2

IN_DIM = 2464
IN_PAD = 2560
LANES = 128

VMEM_BIG = 56 * 1024 * 1024
VMEM_MID = 40 * 1024 * 1024


def _cparams(sem, vmem=None):
    return pltpu.CompilerParams(dimension_semantics=sem, vmem_limit_bytes=vmem)


def _dot(a, b):
    return jnp.dot(a, b, preferred_element_type=F32)


def _dot_nt(a, b):
    return lax.dot_general(a, b, (((1,), (1,)), ((), ())), preferred_element_type=F32)


def _rms(x, g):
    var = jnp.mean(x * x, axis=-1, keepdims=True)
    return x * lax.rsqrt(var + NORM_EPS) * g


def _lane_mask(width, lo, length):
    lane = lax.broadcasted_iota(jnp.int32, (1, width), 1)
    return (lane >= lo) & (lane < lo + length)


def _mod_kernel(c_ref, w_ref, b_ref, o_ref):
    c = c_ref[...]
    s = c * (1.0 / (1.0 + jnp.exp(-c)))
    o_ref[0] = jnp.dot(s, w_ref[0], preferred_element_type=F32, precision=lax.Precision.HIGHEST) + b_ref[0]


def _modulation(cond, w_mod, b_mod):
    R, D = cond.shape
    L = w_mod.shape[0]
    nj = w_mod.shape[2] // D
    return pl.pallas_call(
        _mod_kernel,
        grid=(L, nj),
        in_specs=[
            pl.BlockSpec((R, D), lambda l, j: (0, 0)),
            pl.BlockSpec((1, D, D), lambda l, j: (l, 0, j)),
            pl.BlockSpec((1, 1, D), lambda l, j: (l, 0, j)),
        ],
        out_specs=pl.BlockSpec((1, R, D), lambda l, j: (l, 0, j)),
        out_shape=jax.ShapeDtypeStruct((L, R, nj * D), F32),
        compiler_params=_cparams(("parallel", "parallel")),
        name="modulation",
    )(cond, w_mod, b_mod.reshape(L, 1, nj * D))


def _rope(x, c, s, d):
    w = x.shape[-1]
    lane = lax.broadcasted_iota(jnp.int32, (1, LANES), 1)
    first = (lane % (2 * d)) < d
    outs = []
    for b in range(w // LANES):
        sl = slice(b * LANES, (b + 1) * LANES)
        xb = x[:, sl]
        partner = jnp.where(first, pltpu.roll(xb, LANES - d, 1), pltpu.roll(xb, d, 1))
        outs.append(xb * c[:, sl] + partner * s[:, sl])
    return outs[0] if len(outs) == 1 else jnp.concatenate(outs, axis=-1)


def _inproj_kernel(*refs, rope):
    if rope:
        (x_ref, mod_ref, g1_ref, w_ref, qn_ref, wuq_ref, kvn_ref,
         tac, tas, tcc, tcs, tqc, tqs, tkc, tks,
         oa, ob, oc, oq, ockv, okpe) = refs
    else:
        (x_ref, mod_ref, g1_ref, w_ref, qn_ref, wuq_ref, kvn_ref,
         oa, ob, oc, oq, ockv, okpe) = refs
    dt = oa.dtype
    x = x_ref[0]
    h = (_rms(x, g1_ref[...]) * (1.0 + mod_ref[0, 1:2, :]) + mod_ref[0, 0:1, :]).astype(BF)

    def seg(lo, w):
        return _dot(h, w_ref[:, lo:lo + w])

    ua = seg(0, 768)
    qk = ua[:, :512]
    if rope:
        qk = _rope(qk, tac[...], tas[...], 8)
    oa[0, :, 0:256] = (qk[:, :256] * (DA_QK ** -0.5)).astype(dt)
    oa[0, :, 256:512] = qk[:, 256:].astype(dt)
    oa[0, :, 512:768] = ua[:, 512:].astype(dt)
    ub = seg(768, 768)
    ob[0, :, 0:256] = (ub[:, :256] * 0.125).astype(dt)
    ob[0, :, 256:768] = ub[:, 256:].astype(dt)
    uc = seg(1536, 512)
    qk = uc[:, :384]
    if rope:
        qk = _rope(qk, tcc[...], tcs[...], 16)
    oc[0, :, 0:256] = (qk[:, :256] * 0.125).astype(dt)
    oc[0, :, 256:384] = qk[:, 256:].astype(dt)
    oc[0, :, 384:512] = uc[:, 384:].astype(dt)
    cq = _rms(seg(2048, 256), qn_ref[...]).astype(BF)
    qd = _dot(cq, wuq_ref[...])
    if rope:
        qd = _rope(qd, tqc[...], tqs[...], 8)
    oq[0] = (qd * ((MLA_NOPE + MLA_ROPE) ** -0.5)).astype(dt)
    ockv[0] = _rms(seg(2304, 128), kvn_ref[...]).astype(dt)
    kpe = seg(2432, 128)
    if rope:
        kpe = _rope(kpe, tkc[...], tks[...], 8)
    okpe[0] = kpe.astype(dt)


def _in_proj(x, mod, g1, w_in_p, q_norm, w_uq_p, kv_norm, tables, out_dtype, tm):
    B, N, D = x.shape
    rope = tables is not None
    nmod = mod.shape[0]
    mod_map = (lambda j, b: (b, 0, 0)) if nmod > 1 else (lambda j, b: (0, 0, 0))
    const2 = lambda j, b: (0, 0)
    in_specs = [
        pl.BlockSpec((1, tm, D), lambda j, b: (b, j, 0)),
        pl.BlockSpec((1, 6, D), mod_map),
        pl.BlockSpec((1, D), const2),
        pl.BlockSpec((D, IN_PAD), const2),
        pl.BlockSpec((1, 256), const2),
        pl.BlockSpec((256, 512), const2),
        pl.BlockSpec((1, 128), const2),
    ]
    args = [x, mod, g1, w_in_p, q_norm, w_uq_p, kv_norm]
    if rope:
        for t in tables:
            in_specs.append(pl.BlockSpec((tm, t.shape[1]), lambda j, b: (j, 0)))
            args.append(t)
    widths = (768, 768, 512, 512, 128, 128)
    out_specs = [pl.BlockSpec((1, tm, w), lambda j, b: (b, j, 0)) for w in widths]
    out_shape = [jax.ShapeDtypeStruct((B, N, w), out_dtype) for w in widths]
    return pl.pallas_call(
        functools.partial(_inproj_kernel, rope=rope),
        grid=(N // tm, B),
        in_specs=in_specs,
        out_specs=out_specs,
        out_shape=out_shape,
        compiler_params=_cparams(("parallel", "parallel"), VMEM_MID),
        name="in_proj_rope" if rope else "in_proj",
    )(*args)


def _mla_expand_kernel(ckv_ref, kpe_ref, wk_ref, wv_ref, e_ref, ok, ov):
    ckv = ckv_ref[0].astype(BF)
    kpe = kpe_ref[0].astype(BF)
    ok[0] = (_dot(ckv, wk_ref[...]) + _dot(kpe, e_ref[...])).astype(ok.dtype)
    ov[0] = _dot(ckv, wv_ref[...]).astype(ov.dtype)


def _mla_expand(ckv, kpe, w_k, w_v, e_place, tk):
    B, K, _ = ckv.shape
    const2 = lambda b, j: (0, 0)
    return pl.pallas_call(
        _mla_expand_kernel,
        grid=(B, K // tk),
        in_specs=[
            pl.BlockSpec((1, tk, 128), lambda b, j: (b, j, 0)),
            pl.BlockSpec((1, tk, 128), lambda b, j: (b, j, 0)),
            pl.BlockSpec((128, 512), const2),
            pl.BlockSpec((128, 256), const2),
            pl.BlockSpec((128, 512), const2),
        ],
        out_specs=[pl.BlockSpec((1, tk, 512), lambda b, j: (b, j, 0)),
                   pl.BlockSpec((1, tk, 256), lambda b, j: (b, j, 0))],
        out_shape=[jax.ShapeDtypeStruct((B, K, 512), BF), jax.ShapeDtypeStruct((B, K, 256), BF)],
        compiler_params=_cparams(("parallel", "parallel")),
        name="mla_expand",
    )(ckv, kpe, w_k, w_v, e_place)


def _softmax_init(tq, vw):
    return (jnp.full((tq, 1), NEG_INF, F32), jnp.zeros((tq, 1), F32), jnp.zeros((tq, vw), F32))


def _softmax_step(state, s, v):
    m, l, acc = state
    m_new = jnp.maximum(m, jnp.max(s, axis=-1, keepdims=True))
    alpha = jnp.exp(m - m_new)
    p = jnp.exp(s - m_new)
    l = alpha * l + jnp.sum(p, axis=-1, keepdims=True)
    acc = alpha * acc + _dot(p.astype(BF), v)
    return m_new, l, acc


def _dense_attn_kernel(*refs, groups, first, tk, n_chunks, has_ctx, has_sink, diff, lam_init):
    it = iter(refs)
    q_ref, k_ref, v_ref = next(it), next(it), next(it)
    ck_ref = cv_ref = sink_ref = lam_ref = subln_ref = None
    if has_ctx:
        ck_ref, cv_ref = next(it), next(it)
    if has_sink:
        sink_ref = next(it)
    if diff:
        lam_ref, subln_ref = next(it), next(it)
    o_ref = next(it)
    tq = q_ref.shape[1]
    ow = o_ref.shape[2]

    lam = None
    if diff:
        dl = lam_ref[...]
        lam = (jnp.exp(jnp.sum(dl[0:1] * dl[1:2], axis=-1, keepdims=True))
               - jnp.exp(jnp.sum(dl[2:3] * dl[3:4], axis=-1, keepdims=True)) + lam_init)

    out_parts = {}
    for (q_lo, k_lo, w, qmask, v_lo, vw, o_lo, omask, sink_idx, sign) in groups:
        qg = q_ref[0, :, q_lo:q_lo + w].astype(BF)
        if qmask is not None:
            qg = jnp.where(_lane_mask(w, *qmask), qg, jnp.zeros_like(qg))
        st = _softmax_init(tq, vw)
        if has_ctx:
            st = _softmax_step(st, _dot_nt(qg, ck_ref[0, :, k_lo:k_lo + w].astype(BF)),
                               cv_ref[0, :, v_lo:v_lo + vw].astype(BF))
        if first:
            st = _softmax_step(st, _dot_nt(qg, k_ref[0, 0:first, k_lo:k_lo + w].astype(BF)),
                               v_ref[0, 0:first, v_lo:v_lo + vw].astype(BF))

        def body(c, st, qg=qg, k_lo=k_lo, w=w, v_lo=v_lo, vw=vw):
            off = pl.multiple_of(first + c * tk, 128)
            kc = k_ref[0, pl.ds(off, tk), k_lo:k_lo + w].astype(BF)
            vc = v_ref[0, pl.ds(off, tk), v_lo:v_lo + vw].astype(BF)
            return _softmax_step(st, _dot_nt(qg, kc), vc)

        if n_chunks <= 2:
            for c in range(n_chunks):
                st = body(c, st)
        else:
            st = lax.fori_loop(0, n_chunks, body, st)
        m, l, acc = st
        if sink_idx is not None:
            l = l + jnp.exp(sink_ref[sink_idx] - m)
        part = acc * (1.0 / l)
        if sign < 0:
            part = -lam * part
        part = jnp.where(_lane_mask(vw, *omask), part, 0.0)
        key = (o_lo, vw)
        out_parts[key] = part if key not in out_parts else out_parts[key] + part

    if diff:
        (o,) = out_parts.values()
        o2 = o * o
        inv = jnp.zeros_like(o)
        for hh in range(4):
            hm = _lane_mask(ow, 64 * hh, 64)
            ms = jnp.sum(jnp.where(hm, o2, 0.0), axis=-1, keepdims=True) * (1.0 / 64.0)
            inv = jnp.where(hm, lax.rsqrt(ms + NORM_EPS), inv)
        o_ref[0] = (o * inv * subln_ref[...] * (1.0 - lam_init)).astype(o_ref.dtype)
    else:
        for (o_lo, vw), part in out_parts.items():
            o_ref[0, :, o_lo:o_lo + vw] = part.astype(o_ref.dtype)


def _dense_attn(q_arr, q_blk, k_arr, k_blk, v_arr, v_blk, *, groups, out_w, tq, tk, first=0,
                ctx=None, sink=None, diff=None, name):
    B, N, _ = q_arr.shape
    K = k_arr.shape[1]
    n_chunks = (K - first) // tk
    assert first + n_chunks * tk == K
    qw, qi = q_blk
    kw, ki = k_blk
    vw, vi = v_blk
    in_specs = [
        pl.BlockSpec((1, tq, qw), lambda b, i: (b, i, qi)),
        pl.BlockSpec((1, K, kw), lambda b, i: (b, 0, ki)),
        pl.BlockSpec((1, K, vw), lambda b, i: (b, 0, vi)),
    ]
    args = [q_arr, k_arr, v_arr]
    if ctx is not None:
        ck, cv = ctx
        in_specs += [pl.BlockSpec((1,) + ck.shape[1:], lambda b, i: (b, 0, 0)),
                     pl.BlockSpec((1,) + cv.shape[1:], lambda b, i: (b, 0, 0))]
        args += [ck, cv]
    if sink is not None:
        in_specs.append(pl.BlockSpec(memory_space=pltpu.SMEM))
        args.append(sink)
    lam_init = 0.0
    if diff is not None:
        da_lambda, subln, lam_init = diff
        in_specs += [pl.BlockSpec((4, DA_QK), lambda b, i: (0, 0)),
                     pl.BlockSpec((1, out_w), lambda b, i: (0, 0))]
        args += [da_lambda, subln]
    kern = functools.partial(
        _dense_attn_kernel, groups=groups, first=first, tk=tk, n_chunks=n_chunks,
        has_ctx=ctx is not None, has_sink=sink is not None, diff=diff is not None, lam_init=lam_init)
    return pl.pallas_call(
        kern,
        grid=(B, N // tq),
        in_specs=in_specs,
        out_specs=pl.BlockSpec((1, tq, out_w), lambda b, i: (b, i, 0)),
        out_shape=jax.ShapeDtypeStruct((B, N, out_w), BF),
        compiler_params=_cparams(("parallel", "parallel"), VMEM_MID),
        name=name,
    )(*args)


GROUPS_A = tuple((0, 0, 256, (32 * g, 32), 0, 256, 0, (64 * (g // 2), 64), None, 1 if g % 2 == 0 else -1)
                 for g in range(8))
GROUPS_B = tuple((0, 0, 256, (64 * h, 64), 0, 256, 0, (64 * h, 64), None, 1) for h in range(4))
GROUPS_C = tuple((128 * g, 0, 128, (64 * j, 64), 0, 128, 128 * g, (64 * j, 64), 2 * j + g, 1)
                 for g in range(2) for j in range(2))
GROUPS_D = tuple((128 * h, 128 * h, 128, None, 0, 256, 0, (64 * h, 64), None, 1) for h in range(4))


def _window_attn_kernel(q_ref, k_ref, v_ref, ck_ref, cv_ref, sink_ref, o_ref, *, n, band):
    tq = q_ref.shape[1]
    i = pl.program_id(1)
    start = jnp.clip(i * tq - SW_WINDOW, 0, n - band)
    start = pl.multiple_of(start, 128)
    kb = k_ref[0, pl.ds(start, band), :]
    vb = v_ref[0, pl.ds(start, band), :]
    ck = ck_ref[0].astype(BF)
    cv = cv_ref[0].astype(BF)
    qpos = i * tq + lax.broadcasted_iota(jnp.int32, (tq, band), 0)
    kpos = start + lax.broadcasted_iota(jnp.int32, (tq, band), 1)
    valid = jnp.abs(qpos - kpos) <= SW_WINDOW
    for g in range(2):
        part = jnp.zeros((tq, 128), F32)
        for j in range(2):
            hm = _lane_mask(128, 64 * j, 64)
            qg = q_ref[0, :, 128 * g:128 * (g + 1)]
            qg = jnp.where(hm, qg, jnp.zeros_like(qg))
            st = _softmax_init(tq, 128)
            st = _softmax_step(st, _dot_nt(qg, ck), cv)
            st = _softmax_step(st, jnp.where(valid, _dot_nt(qg, kb), NEG_INF), vb)
            m, l, acc = st
            l = l + jnp.exp(sink_ref[2 * j + g] - m)
            part = jnp.where(hm, acc * (1.0 / l), part)
        o_ref[0, :, 128 * g:128 * (g + 1)] = part.astype(o_ref.dtype)


def _window_attn(qkv_c, ck, cv, sink, tq):
    B, N, _ = qkv_c.shape
    band = tq + 2 * SW_WINDOW
    P = ck.shape[1]
    return pl.pallas_call(
        functools.partial(_window_attn_kernel, n=N, band=band),
        grid=(B, N // tq),
        in_specs=[
            pl.BlockSpec((1, tq, 256), lambda b, i: (b, i, 0)),
            pl.BlockSpec((1, N, 128), lambda b, i: (b, 0, 2)),
            pl.BlockSpec((1, N, 128), lambda b, i: (b, 0, 3)),
            pl.BlockSpec((1, P, 128), lambda b, i: (b, 0, 0)),
            pl.BlockSpec((1, P, 128), lambda b, i: (b, 0, 0)),
            pl.BlockSpec(memory_space=pltpu.SMEM),
        ],
        out_specs=pl.BlockSpec((1, tq, 256), lambda b, i: (b, i, 0)),
        out_shape=jax.ShapeDtypeStruct((B, N, 256), BF),
        compiler_params=_cparams(("parallel", "parallel")),
        name="window_attn",
    )(qkv_c, qkv_c, qkv_c, ck, cv, sink)


NA_TILE_ROWS = 4
NA_KEY_ROWS = NA_TILE_ROWS + NA_WIN_R


def _nbr_attn_kernel(q_ref, k_ref, v_ref, ck_ref, cv_ref, bias_ref, o_ref, *, rows):
    tq = q_ref.shape[1]
    i = pl.program_id(1)
    krow = jnp.clip(i * NA_TILE_ROWS - NA_WIN_R // 2, 0, rows - NA_KEY_ROWS)
    start = pl.multiple_of(krow * GRID_W, GRID_W)
    nk = NA_KEY_ROWS * GRID_W
    kb = k_ref[0, pl.ds(start, nk), :]
    vb = v_ref[0, pl.ds(start, nk), :]
    ck = ck_ref[0].astype(BF)
    cv = cv_ref[0].astype(BF)
    q = q_ref[0]
    o = jnp.zeros((tq, 256), F32)
    for h in range(4):
        hm = _lane_mask(256, 64 * h, 64)
        qg = jnp.where(hm, q, jnp.zeros_like(q))
        st = _softmax_init(tq, 256)
        st = _softmax_step(st, _dot_nt(qg, ck), cv)
        st = _softmax_step(st, _dot_nt(qg, kb) + bias_ref[0, h], vb)
        m, l, acc = st
        o = jnp.where(hm, acc * (1.0 / l), o)
    o_ref[0] = o.astype(o_ref.dtype)


def _nbr_attn(qkv_b, ck, cv, bias):
    B, N, _ = qkv_b.shape
    rows = N // GRID_W
    tq = NA_TILE_ROWS * GRID_W
    nt = N // tq
    nk = NA_KEY_ROWS * GRID_W
    P = ck.shape[1]

    def bias_map(b, i):
        return (jnp.where(i == 0, 0, jnp.where(i == nt - 1, 2, 1)), 0, 0, 0)

    return pl.pallas_call(
        functools.partial(_nbr_attn_kernel, rows=rows),
        grid=(B, nt),
        in_specs=[
            pl.BlockSpec((1, tq, 256), lambda b, i: (b, i, 0)),
            pl.BlockSpec((1, N, 256), lambda b, i: (b, 0, 1)),
            pl.BlockSpec((1, N, 256), lambda b, i: (b, 0, 2)),
            pl.BlockSpec((1, P, 256), lambda b, i: (b, 0, 0)),
            pl.BlockSpec((1, P, 256), lambda b, i: (b, 0, 0)),
            pl.BlockSpec((1, 4, tq, nk), bias_map),
        ],
        out_specs=pl.BlockSpec((1, tq, 256), lambda b, i: (b, i, 0)),
        out_shape=jax.ShapeDtypeStruct((B, N, 256), BF),
        compiler_params=_cparams(("parallel", "parallel"), VMEM_MID),
        name="nbr_attn",
    )(qkv_b, qkv_b, qkv_b, ck, cv, bias)


def _nbr_bias_tables(rpb, rows):
    tabs = []
    nt = rows // NA_TILE_ROWS
    for r0 in (0, NA_TILE_ROWS * min(1, nt - 1) + NA_TILE_ROWS, rows - NA_TILE_ROWS):
        ks = int(np.clip(r0 - NA_WIN_R // 2, 0, rows - NA_KEY_ROWS))
        r = r0 + np.arange(NA_TILE_ROWS)[:, None, None, None]
        qc = np.arange(GRID_W)[None, :, None, None]
        kr = ks + np.arange(NA_KEY_ROWS)[None, None, :, None]
        kc = np.arange(GRID_W)[None, None, None, :]
        rs = np.clip(r - NA_WIN_R // 2, 0, rows - NA_WIN_R)
        cs = np.clip(qc - NA_WIN_C // 2, 0, GRID_W - NA_WIN_C)
        ok = (kr >= rs) & (kr < rs + NA_WIN_R) & (kc >= cs) & (kc < cs + NA_WIN_C)
        dr = np.clip(kr - r + NA_WIN_R - 1, 0, 2 * NA_WIN_R - 2)
        dc = np.clip(kc - qc + NA_WIN_C - 1, 0, 2 * NA_WIN_C - 2)
        shape = (NA_TILE_ROWS, GRID_W, NA_KEY_ROWS, GRID_W)
        dr, dc, ok = (np.broadcast_to(a, shape) for a in (dr, dc, ok))
        t = jnp.where(ok[None], rpb[:, dr, dc].astype(F32), NEG_INF)
        tabs.append(t.reshape(rpb.shape[0], NA_TILE_ROWS * GRID_W, NA_KEY_ROWS * GRID_W))
    return jnp.stack(tabs)


def _merge_kernel(x_ref, mod_ref, g1_ref, g2_ref, oa, ob, oc, od, wg_ref, wb_ref, wo_ref, wr_ref,
                  xo_ref, h2_ref, lg_ref):
    D = x_ref.shape[2]
    x = x_ref[0]
    h = (_rms(x, g1_ref[...]) * (1.0 + mod_ref[0, 1:2, :]) + mod_ref[0, 0:1, :]).astype(BF)
    merged = None
    for i, o_ref in enumerate((oa, ob, oc, od)):
        z = _dot(h, wg_ref[:, i * D:(i + 1) * D])
        gate = 1.0 / (1.0 + jnp.exp(-z))
        term = gate * _dot(o_ref[0], wb_ref[i])
        merged = term if merged is None else merged + term
    mixed = _dot(merged.astype(BF), wo_ref[...])
    xn = x + mod_ref[0, 2:3, :] * mixed
    xo_ref[0] = xn
    h2 = (_rms(xn, g2_ref[...]) * (1.0 + mod_ref[0, 4:5, :]) + mod_ref[0, 3:4, :]).astype(BF)
    h2_ref[0] = h2
    lg_ref[0] = _dot(h2, wr_ref[...])


def _merge(x, mod, g1, g2, o_a, o_b, o_c, o_d, w_gate, w_branch, w_out, w_router_p, tm):
    B, N, D = x.shape
    nmod = mod.shape[0]
    mod_map = (lambda b, j: (b, 0, 0)) if nmod > 1 else (lambda b, j: (0, 0, 0))
    tok = lambda w: pl.BlockSpec((1, tm, w), lambda b, j: (b, j, 0))
    c2 = lambda b, j: (0, 0)
    return pl.pallas_call(
        _merge_kernel,
        grid=(B, N // tm),
        in_specs=[
            tok(D), pl.BlockSpec((1, 6, D), mod_map), pl.BlockSpec((1, D), c2), pl.BlockSpec((1, D), c2),
            tok(256), tok(256), tok(256), tok(256),
            pl.BlockSpec((D, 4 * D), c2),
            pl.BlockSpec((4, 256, D), lambda b, j: (0, 0, 0)),
            pl.BlockSpec((D, D), c2),
            pl.BlockSpec((D, LANES), c2),
        ],
        out_specs=[tok(D), tok(D), tok(LANES)],
        out_shape=[jax.ShapeDtypeStruct((B, N, D), F32), jax.ShapeDtypeStruct((B, N, D), BF),
                   jax.ShapeDtypeStruct((B, N, LANES), F32)],
        compiler_params=_cparams(("parallel", "parallel"), VMEM_BIG),
        name="merge",
    )(x, mod, g1, g2, o_a, o_b, o_c, o_d, w_gate, w_branch, w_out, w_router_p)


def _ffn_kernel(x_ref, wg_ref, wu_ref, wd_ref, y_ref, wg_s, wu_s, wd_s):
    @pl.when(pl.program_id(1) == 0)
    def _():
        wg_s[...] = wg_ref[0].astype(BF)
        wu_s[...] = wu_ref[0].astype(BF)
        wd_s[...] = wd_ref[0].astype(BF)

    x = x_ref[0]
    hg = _dot(x, wg_s[...])
    hu = _dot(x, wu_s[...])
    act = (hg * (1.0 / (1.0 + jnp.exp(-hg))) * hu).astype(BF)
    y_ref[0] = _dot(act, wd_s[...])


def _expert_ffn(xe, w_g, w_u, w_d, tm):
    E, C, D = xe.shape
    DE = w_g.shape[2]
    return pl.pallas_call(
        _ffn_kernel,
        grid=(E, C // tm),
        in_specs=[
            pl.BlockSpec((1, tm, D), lambda e, j: (e, j, 0)),
            pl.BlockSpec((1, D, DE), lambda e, j: (e, 0, 0)),
            pl.BlockSpec((1, D, DE), lambda e, j: (e, 0, 0)),
            pl.BlockSpec((1, DE, D), lambda e, j: (e, 0, 0)),
        ],
        out_specs=pl.BlockSpec((1, tm, D), lambda e, j: (e, j, 0)),
        out_shape=jax.ShapeDtypeStruct((E, C, D), F32),
        scratch_shapes=[pltpu.VMEM((D, DE), BF), pltpu.VMEM((D, DE), BF), pltpu.VMEM((DE, D), BF)],
        compiler_params=_cparams(("parallel", "arbitrary"), VMEM_BIG),
        name="expert_ffn",
    )(xe, w_g, w_u, w_d)


def _residual_kernel(x_ref, y_ref, mod_ref, gf_ref, o_ref, *, final):
    xn = x_ref[0] + mod_ref[0, 5:6, :] * y_ref[0]
    if final:
        xn = _rms(xn, gf_ref[...])
    o_ref[0] = xn


def _residual(x, y, mod, g_final, final, tm):
    B, N, D = x.shape
    nmod = mod.shape[0]
    mod_map = (lambda b, j: (b, 0, 0)) if nmod > 1 else (lambda b, j: (0, 0, 0))
    tok = pl.BlockSpec((1, tm, D), lambda b, j: (b, j, 0))
    return pl.pallas_call(
        functools.partial(_residual_kernel, final=final),
        grid=(B, N // tm),
        in_specs=[tok, tok, pl.BlockSpec((1, 6, D), mod_map), pl.BlockSpec((1, D), lambda b, j: (0, 0))],
        out_specs=tok,
        out_shape=jax.ShapeDtypeStruct((B, N, D), F32),
        compiler_params=_cparams(("parallel", "parallel")),
        name="residual_final" if final else "residual",
    )(x, y, mod, g_final)


def _rope_tables(n):
    t = jnp.arange(n)
    pos = {1: (t // GRID_W).astype(F32), 2: (t % GRID_W).astype(F32)}

    def group(quarter):
        freqs = jnp.float32(ROPE_BASE) ** (-jnp.arange(quarter, dtype=F32) / quarter)
        cs, ss = [], []
        for kind in (1, 2):
            ang = pos[kind][:, None] * freqs
            c, s = jnp.cos(ang), jnp.sin(ang)
            cs += [c, c]
            ss += [-s, s]
        return jnp.concatenate(cs, axis=1), jnp.concatenate(ss, axis=1)

    c32, s32 = group(8)
    c64, s64 = group(16)
    one = lambda w: jnp.ones((n, w), F32)
    zero = lambda w: jnp.zeros((n, w), F32)
    ta = (jnp.tile(c32, (1, 16)), jnp.tile(s32, (1, 16)))
    tc = (jnp.tile(c64, (1, 6)), jnp.tile(s64, (1, 6)))
    qc = jnp.concatenate([one(64), c32, one(32)], axis=1)
    qs = jnp.concatenate([zero(64), s32, zero(32)], axis=1)
    tq = (jnp.tile(qc, (1, 4)), jnp.tile(qs, (1, 4)))
    tk = (jnp.concatenate([c32, one(96)], axis=1), jnp.concatenate([s32, zero(96)], axis=1))
    return (ta[0], ta[1], tc[0], tc[1], tq[0], tq[1], tk[0], tk[1])


C_HEAD_ORDER = (0, 2, 1, 3)


def _prep_layer(l, w_in, mla_w_uq, mla_w_ukv, w_gate, w_branch, w_out, w_router):
    D = D_MODEL
    wi = w_in[l]
    cq = wi[:, 1536:1792].reshape(D, 4, 64)[:, C_HEAD_ORDER, :].reshape(D, 256)
    w_in_p = jnp.concatenate(
        [wi[:, :1536], cq, wi[:, 1792:], jnp.zeros((D, IN_PAD - IN_DIM), F32)], axis=1).astype(BF)
    uq = mla_w_uq[l].reshape(256, 4, MLA_NOPE + MLA_ROPE)
    w_uq_p = jnp.concatenate([uq, jnp.zeros((256, 4, 32), F32)], axis=2).reshape(256, 512).astype(BF)
    ukv = mla_w_ukv[l].reshape(128, 4, 128)
    w_k = jnp.concatenate([ukv[:, :, :64], jnp.zeros((128, 4, 64), F32)], axis=2).reshape(128, 512).astype(BF)
    w_v = ukv[:, :, 64:].reshape(128, 256).astype(BF)
    wb = w_branch[l]
    wb_c = wb[2].reshape(4, 64, D)[C_HEAD_ORDER, :, :].reshape(256, D)
    w_branch_p = jnp.stack([wb[0], wb[1], wb_c, wb[3]]).astype(BF)
    w_router_p = jnp.concatenate([w_router[l], jnp.zeros((D, LANES - N_EXPERTS), F32)], axis=1).astype(BF)
    return dict(w_in=w_in_p, w_uq=w_uq_p, w_k=w_k, w_v=w_v, w_gate=w_gate[l].astype(BF),
                w_branch=w_branch_p, w_out=w_out[l].astype(BF), w_router=w_router_p)


def _kpe_placement():
    e = np.zeros((128, 512), np.float32)
    for h in range(4):
        for i in range(MLA_ROPE):
            e[i, 128 * h + MLA_NOPE + i] = 1.0
    return jnp.asarray(e, BF)


def _moe(h2, logits, w_g, w_u, w_d):
    B, N, D = h2.shape
    T = B * N
    xt = h2.reshape(T, D)
    aff = jax.nn.softmax(logits.reshape(T, LANES)[:, :N_EXPERTS], axis=-1)
    cap = EC_CAPACITY * T // N_EXPERTS
    gate, idx = lax.top_k(aff.T, cap)
    xe = xt[idx]
    y = _expert_ffn(xe, w_g, w_u, w_d, min(512, cap))
    y = y * gate[..., None]
    out = jnp.zeros((T, D), F32).at[idx.reshape(-1)].add(y.reshape(-1, D))
    return out.reshape(B, N, D)


def _layer(x, mod, lw, lp, lam_init, ctx, tables, final, g_final):
    B, N, D = x.shape
    latent = ctx is not None
    tm = 512 if latent else 256
    out_dtype = BF if latent else F32
    qkv_a, qkv_b, qkv_c, q_d, ckv_n, kpe_r = _in_proj(
        x, mod, lp["g_norm1"], lw["w_in"], lp["mla_q_norm"], lw["w_uq"], lp["mla_kv_norm"],
        tables if latent else None, out_dtype, tm)
    diff = (lp["da_lambda"], jnp.tile(lp["da_subln"], (1, 4)), lam_init)
    if latent:
        ck_a, cv_a, ck_b, cv_b, ck_c, cv_c, ckv_c, kpe_c = ctx
        P = ck_a.shape[1]
        o_a = _dense_attn(qkv_a, (256, 0), qkv_a, (256, 1), qkv_a, (256, 2), groups=GROUPS_A, out_w=256,
                          tq=256, tk=512, ctx=(ck_a, cv_a), diff=diff, name="diff_attn")
        o_b = _nbr_attn(qkv_b, ck_b, cv_b, lp["na_bias"])
        o_c = _window_attn(qkv_c, ck_c, cv_c, lp["sw_sink"], 256)
        ckv_all = jnp.concatenate([ckv_c.astype(BF), ckv_n], axis=1)
        kpe_all = jnp.concatenate([jnp.pad(kpe_c, ((0, 0), (0, 0), (0, 128 - MLA_ROPE))).astype(BF), kpe_r], axis=1)
        k_d, v_d = _mla_expand(ckv_all, kpe_all, lw["w_k"], lw["w_v"], lp["e_place"], 256)
        o_d = _dense_attn(q_d, (512, 0), k_d, (512, 0), v_d, (256, 0), groups=GROUPS_D, out_w=256,
                          tq=256, tk=512, first=P, name="mla_attn")
    else:
        o_a = _dense_attn(qkv_a, (256, 0), qkv_a, (256, 1), qkv_a, (256, 2), groups=GROUPS_A, out_w=256,
                          tq=N, tk=N, diff=diff, name="diff_attn_ctx")
        o_b = _dense_attn(qkv_b, (256, 0), qkv_b, (256, 1), qkv_b, (256, 2), groups=GROUPS_B, out_w=256,
                          tq=N, tk=N, name="dense_attn_ctx")
        o_c = _dense_attn(qkv_c, (256, 0), qkv_c, (128, 2), qkv_c, (128, 3), groups=GROUPS_C, out_w=256,
                          tq=N, tk=N, sink=lp["sw_sink"], name="gqa_attn_ctx")
        k_d, v_d = _mla_expand(ckv_n, kpe_r, lw["w_k"], lw["w_v"], lp["e_place"], N)
        o_d = _dense_attn(q_d, (512, 0), k_d, (512, 0), v_d, (256, 0), groups=GROUPS_D, out_w=256,
                          tq=N, tk=N, name="mla_attn_ctx")
    x_new, h2, logits = _merge(x, mod, lp["g_norm1"], lp["g_norm2"], o_a, o_b, o_c, o_d,
                               lw["w_gate"], lw["w_branch"], lw["w_out"], lw["w_router"], tm)
    y = _moe(h2, logits, lp["w_e_gate"], lp["w_e_up"], lp["w_e_down"])
    x_out = _residual(x_new, y, mod, g_final, final, tm)
    cache = None
    if not latent:
        cache = (qkv_a[:, :, 256:512].reshape(B, N, 4, 64), qkv_a[:, :, 512:768].reshape(B, N, 4, 64),
                 qkv_b[:, :, 256:512].reshape(B, N, 4, 64), qkv_b[:, :, 512:768].reshape(B, N, 4, 64),
                 qkv_c[:, :, 256:384].reshape(B, N, 2, 64), qkv_c[:, :, 384:512].reshape(B, N, 2, 64),
                 ckv_n, kpe_r[:, :, :MLA_ROPE])
    return x_out, cache


def kernel(x_prompt, x_sample, cache_diff_k, cache_diff_v, cache_na_k, cache_na_v, cache_swa_k, cache_swa_v, cache_mla_ckv, cache_mla_kpe, c, c_ctx, w_mod, b_mod, g_norm1, g_norm2, w_in, da_lambda, da_subln, na_rpb, sw_sink, mla_q_norm, mla_w_uq, mla_kv_norm, mla_w_ukv, w_gate, w_branch, w_out, w_router, w_e_gate, w_e_up, w_e_down, g_final):
    D = D_MODEL
    BS, NS, _ = x_sample.shape
    P = cache_diff_k.shape[2]
    cond = jnp.concatenate([c_ctx[None], c, jnp.zeros((16 - 1 - BS, D), F32)], axis=0)
    mod_all = _modulation(cond, w_mod, b_mod).reshape(DEPTH, 16, 6, D)
    tables = _rope_tables(NS)
    e_place = _kpe_placement()
    gf = g_final.reshape(1, D)
    xp, xs = x_prompt, x_sample
    caches = []
    for l in range(DEPTH):
        lw = _prep_layer(l, w_in, mla_w_uq, mla_w_ukv, w_gate, w_branch, w_out, w_router)
        lp = dict(
            g_norm1=g_norm1[l].reshape(1, D), g_norm2=g_norm2[l].reshape(1, D),
            mla_q_norm=mla_q_norm[l].reshape(1, 256), mla_kv_norm=mla_kv_norm[l].reshape(1, 128),
            da_lambda=da_lambda[l], da_subln=da_subln[l].reshape(1, 64), sw_sink=sw_sink[l],
            w_e_gate=w_e_gate[l], w_e_up=w_e_up[l], w_e_down=w_e_down[l], e_place=e_place,
            na_bias=_nbr_bias_tables(na_rpb[l], NS // GRID_W))
        lam_init = 0.8 - 0.6 * math.exp(-0.3 * l)
        final = l == DEPTH - 1
        xp, cache_l = _layer(xp, mod_all[l, 0:1], lw, lp, lam_init, None, None, final, gf)
        caches.append(cache_l)
        ctx = (cache_diff_k[:, l].reshape(BS, P, 256), cache_diff_v[:, l].reshape(BS, P, 256),
               cache_na_k[:, l].reshape(BS, P, 256), cache_na_v[:, l].reshape(BS, P, 256),
               cache_swa_k[:, l].reshape(BS, P, 128), cache_swa_v[:, l].reshape(BS, P, 128),
               cache_mla_ckv[:, l], cache_mla_kpe[:, l])
        xs, _ = _layer(xs, mod_all[l, 1:1 + BS], lw, lp, lam_init, ctx, tables, final, gf)
    outs = tuple(jnp.stack([cl[i] for cl in caches], axis=1) for i in range(8))
    return (xp, xs) + outs
```

```python
import functools
import math

import numpy as np
import jax
import jax.numpy as jnp
from jax import lax
from jax.experimental import pallas as pl
from jax.experimental.pallas import tpu as pltpu

BF = jnp.bfloat16
F32 = jnp.float32

D_MODEL = 1024
DEPTH = 4
GRID_W = 64
ROPE_BASE = 10000.0
NORM_EPS = 1e-6
NEG_INF = -1e30
LOG2E = math.log2(math.e)

DA_QK = 32
NA_WIN_R = 8
NA_WIN_C = 16
SW_WINDOW = 128
MLA_NOPE = 64
MLA_ROPE = 32
N_EXPERTS = 16
EC_CAPACITY = 2

IN_DIM = 2464
IN_PAD = 2560
LANES = 128

VMEM_BIG = 56 * 1024 * 1024
VMEM_MID = 40 * 1024 * 1024


def _cparams(sem, vmem=None):
    return pltpu.CompilerParams(dimension_semantics=sem, vmem_limit_bytes=vmem)


def _dot(a, b):
    return jnp.dot(a, b, preferred_element_type=F32)


def _dot_nt(a, b):
    return lax.dot_general(a, b, (((1,), (1,)), ((), ())), preferred_element_type=F32)


def _rms(x, g):
    var = jnp.mean(x * x, axis=-1, keepdims=True)
    return x * lax.rsqrt(var + NORM_EPS) * g


def _lane_mask(width, lo, length):
    lane = lax.broadcasted_iota(jnp.int32, (1, width), 1)
    return (lane >= lo) & (lane < lo + length)


def _mod_kernel(c_ref, w_ref, b_ref, o_ref):
    c = c_ref[...]
    s = c * (1.0 / (1.0 + jnp.exp(-c)))
    o_ref[0] = jnp.dot(s, w_ref[0], preferred_element_type=F32, precision=lax.Precision.HIGHEST) + b_ref[0]


def _modulation(cond, w_mod, b_mod):
    R, D = cond.shape
    L = w_mod.shape[0]
    nj = w_mod.shape[2] // D
    return pl.pallas_call(
        _mod_kernel,
        grid=(L, nj),
        in_specs=[
            pl.BlockSpec((R, D), lambda l, j: (0, 0)),
            pl.BlockSpec((1, D, D), lambda l, j: (l, 0, j)),
            pl.BlockSpec((1, 1, D), lambda l, j: (l, 0, j)),
        ],
        out_specs=pl.BlockSpec((1, R, D), lambda l, j: (l, 0, j)),
        out_shape=jax.ShapeDtypeStruct((L, R, nj * D), F32),
        compiler_params=_cparams(("parallel", "parallel")),
        name="modulation",
    )(cond, w_mod, b_mod.reshape(L, 1, nj * D))


def _rope(x, c, s, d):
    w = x.shape[-1]
    lane = lax.broadcasted_iota(jnp.int32, (1, LANES), 1)
    first = (lane % (2 * d)) < d
    outs = []
    for b in range(w // LANES):
        sl = slice(b * LANES, (b + 1) * LANES)
        xb = x[:, sl]
        partner = jnp.where(first, pltpu.roll(xb, LANES - d, 1), pltpu.roll(xb, d, 1))
        outs.append(xb * c[:, sl] + partner * s[:, sl])
    return outs[0] if len(outs) == 1 else jnp.concatenate(outs, axis=-1)


def _inproj_kernel(*refs, rope):
    if rope:
        (x_ref, mod_ref, g1_ref, w_ref, qn_ref, wuq_ref, kvn_ref,
         tac, tas, tcc, tcs, tqc, tqs, tkc, tks,
         oa, ob, oc, oq, ockv, okpe) = refs
    else:
        (x_ref, mod_ref, g1_ref, w_ref, qn_ref, wuq_ref, kvn_ref,
         oa, ob, oc, oq, ockv, okpe) = refs
    dt = oa.dtype
    x = x_ref[0]
    h = (_rms(x, g1_ref[...]) * (1.0 + mod_ref[0, 1:2, :]) + mod_ref[0, 0:1, :]).astype(BF)

    def seg(lo, w):
        return _dot(h, w_ref[:, lo:lo + w])

    ua = seg(0, 768)
    qk = ua[:, :512]
    if rope:
        qk = _rope(qk, tac[...], tas[...], 8)
    oa[0, :, 0:256] = (qk[:, :256] * (DA_QK ** -0.5 * LOG2E)).astype(dt)
    oa[0, :, 256:512] = qk[:, 256:].astype(dt)
    oa[0, :, 512:768] = ua[:, 512:].astype(dt)
    ub = seg(768, 768)
    ob[0, :, 0:256] = (ub[:, :256] * (0.125 * LOG2E)).astype(dt)
    ob[0, :, 256:768] = ub[:, 256:].astype(dt)
    uc = seg(1536, 512)
    qk = uc[:, :384]
    if rope:
        qk = _rope(qk, tcc[...], tcs[...], 16)
    oc[0, :, 0:256] = (qk[:, :256] * (0.125 * LOG2E)).astype(dt)
    oc[0, :, 256:384] = qk[:, 256:].astype(dt)
    oc[0, :, 384:512] = uc[:, 384:].astype(dt)
    cq = _rms(seg(2048, 256), qn_ref[...]).astype(BF)
    qd = _dot(cq, wuq_ref[...])
    if rope:
        qd = _rope(qd, tqc[...], tqs[...], 8)
    oq[0] = (qd * ((MLA_NOPE + MLA_ROPE) ** -0.5 * LOG2E)).astype(dt)
    ockv[0] = _rms(seg(2304, 128), kvn_ref[...]).astype(dt)
    kpe = seg(2432, 128)
    if rope:
        kpe = _rope(kpe, tkc[...], tks[...], 8)
    okpe[0] = kpe.astype(dt)


def _in_proj(x, mod, g1, w_in_p, q_norm, w_uq_p, kv_norm, tables, out_dtype, tm):
    B, N, D = x.shape
    rope = tables is not None
    nmod = mod.shape[0]
    mod_map = (lambda j, b: (b, 0, 0)) if nmod > 1 else (lambda j, b: (0, 0, 0))
    const2 = lambda j, b: (0, 0)
    in_specs = [
        pl.BlockSpec((1, tm, D), lambda j, b: (b, j, 0)),
        pl.BlockSpec((1, 6, D), mod_map),
        pl.BlockSpec((1, D), const2),
        pl.BlockSpec((D, IN_PAD), const2),
        pl.BlockSpec((1, 256), const2),
        pl.BlockSpec((256, 512), const2),
        pl.BlockSpec((1, 128), const2),
    ]
    args = [x, mod, g1, w_in_p, q_norm, w_uq_p, kv_norm]
    if rope:
        for t in tables:
            in_specs.append(pl.BlockSpec((tm, t.shape[1]), lambda j, b: (j, 0)))
            args.append(t)
    widths = (768, 768, 512, 512, 128, 128)
    out_specs = [pl.BlockSpec((1, tm, w), lambda j, b: (b, j, 0)) for w in widths]
    out_shape = [jax.ShapeDtypeStruct((B, N, w), out_dtype) for w in widths]
    return pl.pallas_call(
        functools.partial(_inproj_kernel, rope=rope),
        grid=(N // tm, B),
        in_specs=in_specs,
        out_specs=out_specs,
        out_shape=out_shape,
        compiler_params=_cparams(("parallel", "parallel"), VMEM_MID),
        name="in_proj_rope" if rope else "in_proj",
    )(*args)


def _mla_expand_kernel(ckv_ref, kpe_ref, wk_ref, wv_ref, e_ref, ok, ov):
    ckv = ckv_ref[0].astype(BF)
    kpe = kpe_ref[0].astype(BF)
    ok[0] = (_dot(ckv, wk_ref[...]) + _dot(kpe, e_ref[...])).astype(ok.dtype)
    ov[0] = _dot(ckv, wv_ref[...]).astype(ov.dtype)


def _mla_expand(ckv, kpe, w_k, w_v, e_place, tk):
    B, K, _ = ckv.shape
    const2 = lambda b, j: (0, 0)
    return pl.pallas_call(
        _mla_expand_kernel,
        grid=(B, K // tk),
        in_specs=[
            pl.BlockSpec((1, tk, 128), lambda b, j: (b, j, 0)),
            pl.BlockSpec((1, tk, 128), lambda b, j: (b, j, 0)),
            pl.BlockSpec((128, 512), const2),
            pl.BlockSpec((128, 256), const2),
            pl.BlockSpec((128, 512), const2),
        ],
        out_specs=[pl.BlockSpec((1, tk, 512), lambda b, j: (b, j, 0)),
                   pl.BlockSpec((1, tk, 256), lambda b, j: (b, j, 0))],
        out_shape=[jax.ShapeDtypeStruct((B, K, 512), BF), jax.ShapeDtypeStruct((B, K, 256), BF)],
        compiler_params=_cparams(("parallel", "parallel")),
        name="mla_expand",
    )(ckv, kpe, w_k, w_v, e_place)


def _softmax_init(tq, vw):
    return (jnp.full((tq, 1), NEG_INF, F32), jnp.zeros((tq, 1), F32), jnp.zeros((tq, vw), F32))


def _softmax_step(state, s, v):
    m, l, acc = state
    m_new = jnp.maximum(m, jnp.max(s, axis=-1, keepdims=True))
    alpha = jnp.exp2(m - m_new)
    p = jnp.exp2(s - m_new)
    l = alpha * l + jnp.sum(p, axis=-1, keepdims=True)
    acc = alpha * acc + _dot(p.astype(BF), v)
    return m_new, l, acc


def _dense_attn_kernel(*refs, groups, first, tk, n_chunks, has_ctx, has_sink, diff, lam_init):
    it = iter(refs)
    q_ref, k_ref, v_ref = next(it), next(it), next(it)
    ck_ref = cv_ref = sink_ref = lam_ref = subln_ref = None
    if has_ctx:
        ck_ref, cv_ref = next(it), next(it)
    if has_sink:
        sink_ref = next(it)
    if diff:
        lam_ref, subln_ref = next(it), next(it)
    o_ref = next(it)
    tq = q_ref.shape[1]
    ow = o_ref.shape[2]

    lam = None
    if diff:
        dl = lam_ref[...]
        lam = (jnp.exp(jnp.sum(dl[0:1] * dl[1:2], axis=-1, keepdims=True))
               - jnp.exp(jnp.sum(dl[2:3] * dl[3:4], axis=-1, keepdims=True)) + lam_init)

    ng = len(groups)
    scratch = list(it)
    qg_s, m_s, l_s, acc_s = (scratch[j * ng:(j + 1) * ng] for j in range(4))
    for gi, (q_lo, k_lo, w, qmask, v_lo, vw, o_lo, omask, sink_idx, sign) in enumerate(groups):
        qg = q_ref[0, :, q_lo:q_lo + w].astype(BF)
        if qmask is not None:
            qg = jnp.where(_lane_mask(w, *qmask), qg, jnp.zeros_like(qg))
        qg_s[gi][...] = qg
        m_s[gi][...] = jnp.full((tq, 1), NEG_INF, F32)
        l_s[gi][...] = jnp.zeros((tq, 1), F32)
        acc_s[gi][...] = jnp.zeros((tq, vw), F32)

    def keys_step(kk, vv):
        for gi, (q_lo, k_lo, w, qmask, v_lo, vw, o_lo, omask, sink_idx, sign) in enumerate(groups):
            st = (m_s[gi][...], l_s[gi][...], acc_s[gi][...])
            m, l, acc = _softmax_step(st, _dot_nt(qg_s[gi][...], kk[:, k_lo:k_lo + w]), vv[:, v_lo:v_lo + vw])
            m_s[gi][...] = m
            l_s[gi][...] = l
            acc_s[gi][...] = acc

    if has_ctx:
        keys_step(ck_ref[0].astype(BF), cv_ref[0].astype(BF))
    if first:
        keys_step(k_ref[0, 0:first, :].astype(BF), v_ref[0, 0:first, :].astype(BF))

    def body(c, carry):
        off = pl.multiple_of(first + c * tk, 128)
        keys_step(k_ref[0, pl.ds(off, tk), :].astype(BF), v_ref[0, pl.ds(off, tk), :].astype(BF))
        return carry

    if n_chunks <= 2:
        for c in range(n_chunks):
            body(c, 0)
    else:
        lax.fori_loop(0, n_chunks, body, 0)

    out_parts = {}
    for gi, (q_lo, k_lo, w, qmask, v_lo, vw, o_lo, omask, sink_idx, sign) in enumerate(groups):
        m, l, acc = m_s[gi][...], l_s[gi][...], acc_s[gi][...]
        if sink_idx is not None:
            l = l + jnp.exp2(sink_ref[sink_idx] * LOG2E - m)
        part = acc * (1.0 / l)
        if sign < 0:
            part = -lam * part
        part = jnp.where(_lane_mask(vw, *omask), part, 0.0)
        key = (o_lo, vw)
        out_parts[key] = part if key not in out_parts else out_parts[key] + part

    if diff:
        (o,) = out_parts.values()
        o2 = o * o
        inv = jnp.zeros_like(o)
        for hh in range(4):
            hm = _lane_mask(ow, 64 * hh, 64)
            ms = jnp.sum(jnp.where(hm, o2, 0.0), axis=-1, keepdims=True) * (1.0 / 64.0)
            inv = jnp.where(hm, lax.rsqrt(ms + NORM_EPS), inv)
        o_ref[0] = (o * inv * subln_ref[...] * (1.0 - lam_init)).astype(o_ref.dtype)
    else:
        for (o_lo, vw), part in out_parts.items():
            o_ref[0, :, o_lo:o_lo + vw] = part.astype(o_ref.dtype)


def _dense_attn(q_arr, q_blk, k_arr, k_blk, v_arr, v_blk, *, groups, out_w, tq, tk, first=0,
                ctx=None, sink=None, diff=None, name):
    B, N, _ = q_arr.shape
    K = k_arr.shape[1]
    n_chunks = (K - first) // tk
    assert first + n_chunks * tk == K
    qw, qi = q_blk
    kw, ki = k_blk
    vw, vi = v_blk
    in_specs = [
        pl.BlockSpec((1, tq, qw), lambda b, i: (b, i, qi)),
        pl.BlockSpec((1, K, kw), lambda b, i: (b, 0, ki)),
        pl.BlockSpec((1, K, vw), lambda b, i: (b, 0, vi)),
    ]
    args = [q_arr, k_arr, v_arr]
    if ctx is not None:
        ck, cv = ctx
        in_specs += [pl.BlockSpec((1,) + ck.shape[1:], lambda b, i: (b, 0, 0)),
                     pl.BlockSpec((1,) + cv.shape[1:], lambda b, i: (b, 0, 0))]
        args += [ck, cv]
    if sink is not None:
        in_specs.append(pl.BlockSpec(memory_space=pltpu.SMEM))
        args.append(sink)
    lam_init = 0.0
    if diff is not None:
        da_lambda, subln, lam_init = diff
        in_specs += [pl.BlockSpec((4, DA_QK), lambda b, i: (0, 0)),
                     pl.BlockSpec((1, out_w), lambda b, i: (0, 0))]
        args += [da_lambda, subln]
    kern = functools.partial(
        _dense_attn_kernel, groups=groups, first=first, tk=tk, n_chunks=n_chunks,
        has_ctx=ctx is not None, has_sink=sink is not None, diff=diff is not None, lam_init=lam_init)
    scratch = ([pltpu.VMEM((tq, g[2]), BF) for g in groups] + [pltpu.VMEM((tq, 1), F32) for _ in groups] * 2
               + [pltpu.VMEM((tq, g[5]), F32) for g in groups])
    return pl.pallas_call(
        kern,
        grid=(B, N // tq),
        in_specs=in_specs,
        out_specs=pl.BlockSpec((1, tq, out_w), lambda b, i: (b, i, 0)),
        out_shape=jax.ShapeDtypeStruct((B, N, out_w), BF),
        scratch_shapes=scratch,
        compiler_params=_cparams(("parallel", "parallel"), VMEM_MID),
        name=name,
    )(*args)


GROUPS_A = tuple((0, 0, 256, (32 * g, 32), 0, 256, 0, (64 * (g // 2), 64), None, 1 if g % 2 == 0 else -1)
                 for g in range(8))
GROUPS_B = tuple((0, 0, 256, (64 * h, 64), 0, 256, 0, (64 * h, 64), None, 1) for h in range(4))
GROUPS_C = tuple((128 * g, 0, 128, (64 * j, 64), 0, 128, 128 * g, (64 * j, 64), 2 * j + g, 1)
                 for g in range(2) for j in range(2))
GROUPS_D = tuple((128 * h, 128 * h, 128, None, 0, 256, 0, (64 * h, 64), None, 1) for h in range(4))


def _window_attn_kernel(q_ref, k_ref, v_ref, ck_ref, cv_ref, sink_ref, o_ref, *, n, band):
    tq = q_ref.shape[1]
    i = pl.program_id(1)
    start = jnp.clip(i * tq - SW_WINDOW, 0, n - band)
    start = pl.multiple_of(start, 128)
    kb = k_ref[0, pl.ds(start, band), :]
    vb = v_ref[0, pl.ds(start, band), :]
    ck = ck_ref[0].astype(BF)
    cv = cv_ref[0].astype(BF)
    qpos = i * tq + lax.broadcasted_iota(jnp.int32, (tq, band), 0)
    kpos = start + lax.broadcasted_iota(jnp.int32, (tq, band), 1)
    valid = jnp.abs(qpos - kpos) <= SW_WINDOW
    for g in range(2):
        part = jnp.zeros((tq, 128), F32)
        for j in range(2):
            hm = _lane_mask(128, 64 * j, 64)
            qg = q_ref[0, :, 128 * g:128 * (g + 1)]
            qg = jnp.where(hm, qg, jnp.zeros_like(qg))
            st = _softmax_init(tq, 128)
            st = _softmax_step(st, _dot_nt(qg, ck), cv)
            st = _softmax_step(st, jnp.where(valid, _dot_nt(qg, kb), NEG_INF), vb)
            m, l, acc = st
            l = l + jnp.exp2(sink_ref[2 * j + g] * LOG2E - m)
            part = jnp.where(hm, acc * (1.0 / l), part)
        o_ref[0, :, 128 * g:128 * (g + 1)] = part.astype(o_ref.dtype)


def _window_attn(qkv_c, ck, cv, sink, tq):
    B, N, _ = qkv_c.shape
    band = tq + 2 * SW_WINDOW
    P = ck.shape[1]
    return pl.pallas_call(
        functools.partial(_window_attn_kernel, n=N, band=band),
        grid=(B, N // tq),
        in_specs=[
            pl.BlockSpec((1, tq, 256), lambda b, i: (b, i, 0)),
            pl.BlockSpec((1, N, 128), lambda b, i: (b, 0, 2)),
            pl.BlockSpec((1, N, 128), lambda b, i: (b, 0, 3)),
            pl.BlockSpec((1, P, 128), lambda b, i: (b, 0, 0)),
            pl.BlockSpec((1, P, 128), lambda b, i: (b, 0, 0)),
            pl.BlockSpec(memory_space=pltpu.SMEM),
        ],
        out_specs=pl.BlockSpec((1, tq, 256), lambda b, i: (b, i, 0)),
        out_shape=jax.ShapeDtypeStruct((B, N, 256), BF),
        compiler_params=_cparams(("parallel", "parallel")),
        name="window_attn",
    )(qkv_c, qkv_c, qkv_c, ck, cv, sink)


NA_TILE_ROWS = 4
NA_KEY_ROWS = NA_TILE_ROWS + NA_WIN_R


def _nbr_attn_kernel(q_ref, k_ref, v_ref, ck_ref, cv_ref, bias_ref, o_ref, *, rows):
    tq = q_ref.shape[1]
    i = pl.program_id(1)
    krow = jnp.clip(i * NA_TILE_ROWS - NA_WIN_R // 2, 0, rows - NA_KEY_ROWS)
    start = pl.multiple_of(krow * GRID_W, GRID_W)
    nk = NA_KEY_ROWS * GRID_W
    kb = k_ref[0, pl.ds(start, nk), :]
    vb = v_ref[0, pl.ds(start, nk), :]
    ck = ck_ref[0].astype(BF)
    cv = cv_ref[0].astype(BF)
    q = q_ref[0]
    o = jnp.zeros((tq, 256), F32)
    for h in range(4):
        hm = _lane_mask(256, 64 * h, 64)
        qg = jnp.where(hm, q, jnp.zeros_like(q))
        st = _softmax_init(tq, 256)
        st = _softmax_step(st, _dot_nt(qg, ck), cv)
        st = _softmax_step(st, _dot_nt(qg, kb) + bias_ref[0, h], vb)
        m, l, acc = st
        o = jnp.where(hm, acc * (1.0 / l), o)
    o_ref[0] = o.astype(o_ref.dtype)


def _nbr_attn(qkv_b, ck, cv, bias):
    B, N, _ = qkv_b.shape
    rows = N // GRID_W
    tq = NA_TILE_ROWS * GRID_W
    nt = N // tq
    nk = NA_KEY_ROWS * GRID_W
    P = ck.shape[1]

    def bias_map(b, i):
        return (jnp.where(i == 0, 0, jnp.where(i == nt - 1, 2, 1)), 0, 0, 0)

    return pl.pallas_call(
        functools.partial(_nbr_attn_kernel, rows=rows),
        grid=(B, nt),
        in_specs=[
            pl.BlockSpec((1, tq, 256), lambda b, i: (b, i, 0)),
            pl.BlockSpec((1, N, 256), lambda b, i: (b, 0, 1)),
            pl.BlockSpec((1, N, 256), lambda b, i: (b, 0, 2)),
            pl.BlockSpec((1, P, 256), lambda b, i: (b, 0, 0)),
            pl.BlockSpec((1, P, 256), lambda b, i: (b, 0, 0)),
            pl.BlockSpec((1, 4, tq, nk), bias_map),
        ],
        out_specs=pl.BlockSpec((1, tq, 256), lambda b, i: (b, i, 0)),
        out_shape=jax.ShapeDtypeStruct((B, N, 256), BF),
        compiler_params=_cparams(("parallel", "parallel"), VMEM_MID),
        name="nbr_attn",
    )(qkv_b, qkv_b, qkv_b, ck, cv, bias)


def _nbr_bias_tables(rpb, rows):
    H = rpb.shape[0]
    rp = jnp.pad(rpb.astype(F32), ((0, 0), (NA_KEY_ROWS, NA_KEY_ROWS), (GRID_W, GRID_W)))
    c0 = NA_WIN_C - 1 + GRID_W
    tcol = jnp.stack([rp[:, :, c0 - qc:c0 - qc + GRID_W] for qc in range(GRID_W)], axis=2)
    tabs = []
    nt = rows // NA_TILE_ROWS
    for r0 in (0, NA_TILE_ROWS * min(1, nt - 1) + NA_TILE_ROWS, rows - NA_TILE_ROWS):
        ks = int(np.clip(r0 - NA_WIN_R // 2, 0, rows - NA_KEY_ROWS))
        per_row = []
        for a in range(NA_TILE_ROWS):
            s0 = ks - (r0 + a) + NA_WIN_R - 1 + NA_KEY_ROWS
            per_row.append(tcol[:, s0:s0 + NA_KEY_ROWS])
        t = jnp.stack(per_row, axis=1).transpose(0, 1, 3, 2, 4)
        r = r0 + np.arange(NA_TILE_ROWS)[:, None, None, None]
        qc = np.arange(GRID_W)[None, :, None, None]
        kr = ks + np.arange(NA_KEY_ROWS)[None, None, :, None]
        kc = np.arange(GRID_W)[None, None, None, :]
        rs = np.clip(r - NA_WIN_R // 2, 0, rows - NA_WIN_R)
        cs = np.clip(qc - NA_WIN_C // 2, 0, GRID_W - NA_WIN_C)
        ok = (kr >= rs) & (kr < rs + NA_WIN_R) & (kc >= cs) & (kc < cs + NA_WIN_C)
        t = jnp.where(ok[None], t * LOG2E, NEG_INF)
        tabs.append(t.reshape(H, NA_TILE_ROWS * GRID_W, NA_KEY_ROWS * GRID_W))
    return jnp.stack(tabs)


def _merge_kernel(x_ref, mod_ref, g1_ref, g2_ref, oa, ob, oc, od, wg_ref, wb_ref, wo_ref, wr_ref,
                  xo_ref, h2_ref, lg_ref):
    D = x_ref.shape[2]
    x = x_ref[0]
    h = (_rms(x, g1_ref[...]) * (1.0 + mod_ref[0, 1:2, :]) + mod_ref[0, 0:1, :]).astype(BF)
    merged = None
    for i, o_ref in enumerate((oa, ob, oc, od)):
        z = _dot(h, wg_ref[:, i * D:(i + 1) * D])
        gate = 1.0 / (1.0 + jnp.exp(-z))
        term = gate * _dot(o_ref[0], wb_ref[i])
        merged = term if merged is None else merged + term
    mixed = _dot(merged.astype(BF), wo_ref[...])
    xn = x + mod_ref[0, 2:3, :] * mixed
    xo_ref[0] = xn
    h2 = (_rms(xn, g2_ref[...]) * (1.0 + mod_ref[0, 4:5, :]) + mod_ref[0, 3:4, :]).astype(BF)
    h2_ref[0] = h2
    lg_ref[0] = _dot(h2, wr_ref[...])


def _merge(x, mod, g1, g2, o_a, o_b, o_c, o_d, w_gate, w_branch, w_out, w_router_p, tm):
    B, N, D = x.shape
    nmod = mod.shape[0]
    mod_map = (lambda b, j: (b, 0, 0)) if nmod > 1 else (lambda b, j: (0, 0, 0))
    tok = lambda w: pl.BlockSpec((1, tm, w), lambda b, j: (b, j, 0))
    c2 = lambda b, j: (0, 0)
    return pl.pallas_call(
        _merge_kernel,
        grid=(B, N // tm),
        in_specs=[
            tok(D), pl.BlockSpec((1, 6, D), mod_map), pl.BlockSpec((1, D), c2), pl.BlockSpec((1, D), c2),
            tok(256), tok(256), tok(256), tok(256),
            pl.BlockSpec((D, 4 * D), c2),
            pl.BlockSpec((4, 256, D), lambda b, j: (0, 0, 0)),
            pl.BlockSpec((D, D), c2),
            pl.BlockSpec((D, LANES), c2),
        ],
        out_specs=[tok(D), tok(D), tok(LANES)],
        out_shape=[jax.ShapeDtypeStruct((B, N, D), F32), jax.ShapeDtypeStruct((B, N, D), BF),
                   jax.ShapeDtypeStruct((B, N, LANES), F32)],
        compiler_params=_cparams(("parallel", "parallel"), VMEM_BIG),
        name="merge",
    )(x, mod, g1, g2, o_a, o_b, o_c, o_d, w_gate, w_branch, w_out, w_router_p)


def _ffn_kernel(x_ref, wg_ref, wu_ref, wd_ref, y_ref, wg_s, wu_s, wd_s):
    @pl.when(pl.program_id(1) == 0)
    def _():
        wg_s[...] = wg_ref[0].astype(BF)
        wu_s[...] = wu_ref[0].astype(BF)
        wd_s[...] = wd_ref[0].astype(BF)

    x = x_ref[0]
    hg = _dot(x, wg_s[...])
    hu = _dot(x, wu_s[...])
    act = (hg * (1.0 / (1.0 + jnp.exp(-hg))) * hu).astype(BF)
    y_ref[0] = _dot(act, wd_s[...])


def _expert_ffn(xe, w_g, w_u, w_d, tm):
    E, C, D = xe.shape
    DE = w_g.shape[2]
    return pl.pallas_call(
        _ffn_kernel,
        grid=(E, C // tm),
        in_specs=[
            pl.BlockSpec((1, tm, D), lambda e, j: (e, j, 0)),
            pl.BlockSpec((1, D, DE), lambda e, j: (e, 0, 0)),
            pl.BlockSpec((1, D, DE), lambda e, j: (e, 0, 0)),
            pl.BlockSpec((1, DE, D), lambda e, j: (e, 0, 0)),
        ],
        out_specs=pl.BlockSpec((1, tm, D), lambda e, j: (e, j, 0)),
        out_shape=jax.ShapeDtypeStruct((E, C, D), F32),
        scratch_shapes=[pltpu.VMEM((D, DE), BF), pltpu.VMEM((D, DE), BF), pltpu.VMEM((DE, D), BF)],
        compiler_params=_cparams(("parallel", "arbitrary"), VMEM_BIG),
        name="expert_ffn",
    )(xe, w_g, w_u, w_d)


def _residual_kernel(x_ref, y_ref, mod_ref, gf_ref, o_ref, *, final):
    xn = x_ref[0] + mod_ref[0, 5:6, :] * y_ref[0]
    if final:
        xn = _rms(xn, gf_ref[...])
    o_ref[0] = xn


def _residual(x, y, mod, g_final, final, tm):
    B, N, D = x.shape
    nmod = mod.shape[0]
    mod_map = (lambda b, j: (b, 0, 0)) if nmod > 1 else (lambda b, j: (0, 0, 0))
    tok = pl.BlockSpec((1, tm, D), lambda b, j: (b, j, 0))
    return pl.pallas_call(
        functools.partial(_residual_kernel, final=final),
        grid=(B, N // tm),
        in_specs=[tok, tok, pl.BlockSpec((1, 6, D), mod_map), pl.BlockSpec((1, D), lambda b, j: (0, 0))],
        out_specs=tok,
        out_shape=jax.ShapeDtypeStruct((B, N, D), F32),
        compiler_params=_cparams(("parallel", "parallel")),
        name="residual_final" if final else "residual",
    )(x, y, mod, g_final)


def _rope_tables(n):
    t = jnp.arange(n)
    pos = {1: (t // GRID_W).astype(F32), 2: (t % GRID_W).astype(F32)}

    def group(quarter):
        freqs = jnp.float32(ROPE_BASE) ** (-jnp.arange(quarter, dtype=F32) / quarter)
        cs, ss = [], []
        for kind in (1, 2):
            ang = pos[kind][:, None] * freqs
            c, s = jnp.cos(ang), jnp.sin(ang)
            cs += [c, c]
            ss += [-s, s]
        return jnp.concatenate(cs, axis=1), jnp.concatenate(ss, axis=1)

    c32, s32 = group(8)
    c64, s64 = group(16)
    one = lambda w: jnp.ones((n, w), F32)
    zero = lambda w: jnp.zeros((n, w), F32)
    ta = (jnp.tile(c32, (1, 16)), jnp.tile(s32, (1, 16)))
    tc = (jnp.tile(c64, (1, 6)), jnp.tile(s64, (1, 6)))
    qc = jnp.concatenate([one(64), c32, one(32)], axis=1)
    qs = jnp.concatenate([zero(64), s32, zero(32)], axis=1)
    tq = (jnp.tile(qc, (1, 4)), jnp.tile(qs, (1, 4)))
    tk = (jnp.concatenate([c32, one(96)], axis=1), jnp.concatenate([s32, zero(96)], axis=1))
    return (ta[0], ta[1], tc[0], tc[1], tq[0], tq[1], tk[0], tk[1])


C_HEAD_ORDER = (0, 2, 1, 3)


def _prep_layer(l, w_in, mla_w_uq, mla_w_ukv, w_gate, w_branch, w_out, w_router):
    D = D_MODEL
    wi = w_in[l]
    cq = wi[:, 1536:1792].reshape(D, 4, 64)[:, C_HEAD_ORDER, :].reshape(D, 256)
    w_in_p = jnp.concatenate(
        [wi[:, :1536], cq, wi[:, 1792:], jnp.zeros((D, IN_PAD - IN_DIM), F32)], axis=1).astype(BF)
    uq = mla_w_uq[l].reshape(256, 4, MLA_NOPE + MLA_ROPE)
    w_uq_p = jnp.concatenate([uq, jnp.zeros((256, 4, 32), F32)], axis=2).reshape(256, 512).astype(BF)
    ukv = mla_w_ukv[l].reshape(128, 4, 128)
    w_k = jnp.concatenate([ukv[:, :, :64], jnp.zeros((128, 4, 64), F32)], axis=2).reshape(128, 512).astype(BF)
    w_v = ukv[:, :, 64:].reshape(128, 256).astype(BF)
    wb = w_branch[l]
    wb_c = wb[2].reshape(4, 64, D)[C_HEAD_ORDER, :, :].reshape(256, D)
    w_branch_p = jnp.stack([wb[0], wb[1], wb_c, wb[3]]).astype(BF)
    w_router_p = jnp.concatenate([w_router[l], jnp.zeros((D, LANES - N_EXPERTS), F32)], axis=1).astype(BF)
    return dict(w_in=w_in_p, w_uq=w_uq_p, w_k=w_k, w_v=w_v, w_gate=w_gate[l].astype(BF),
                w_branch=w_branch_p, w_out=w_out[l].astype(BF), w_router=w_router_p)


def _kpe_placement():
    e = np.zeros((128, 512), np.float32)
    for h in range(4):
        for i in range(MLA_ROPE):
            e[i, 128 * h + MLA_NOPE + i] = 1.0
    return jnp.asarray(e, BF)


def _moe(h2, logits, w_g, w_u, w_d):
    B, N, D = h2.shape
    T = B * N
    xt = h2.reshape(T, D)
    aff = jax.nn.softmax(logits.reshape(T, LANES)[:, :N_EXPERTS], axis=-1)
    cap = EC_CAPACITY * T // N_EXPERTS
    gate, idx = lax.top_k(aff.T, cap)
    xe = xt[idx]
    y = _expert_ffn(xe, w_g, w_u, w_d, min(512, cap))
    y = y * gate[..., None]
    out = jnp.zeros((T, D), F32).at[idx.reshape(-1)].add(y.reshape(-1, D))
    return out.reshape(B, N, D)


def _layer(x, mod, lw, lp, lam_init, ctx, tables, final, g_final):
    B, N, D = x.shape
    latent = ctx is not None
    tm = 512 if latent else 256
    out_dtype = BF if latent else F32
    qkv_a, qkv_b, qkv_c, q_d, ckv_n, kpe_r = _in_proj(
        x, mod, lp["g_norm1"], lw["w_in"], lp["mla_q_norm"], lw["w_uq"], lp["mla_kv_norm"],
        tables if latent else None, out_dtype, tm)
    diff = (lp["da_lambda"], jnp.tile(lp["da_subln"], (1, 4)), lam_init)
    if latent:
        ck_a, cv_a, ck_b, cv_b, ck_c, cv_c, ckv_c, kpe_c = ctx
        P = ck_a.shape[1]
        o_a = _dense_attn(qkv_a, (256, 0), qkv_a, (256, 1), qkv_a, (256, 2), groups=GROUPS_A, out_w=256,
                          tq=256, tk=512, ctx=(ck_a, cv_a), diff=diff, name="diff_attn")
        o_b = _nbr_attn(qkv_b, ck_b, cv_b, lp["na_bias"])
        o_c = _window_attn(qkv_c, ck_c, cv_c, lp["sw_sink"], 256)
        ckv_all = jnp.concatenate([ckv_c.astype(BF), ckv_n], axis=1)
        kpe_all = jnp.concatenate([jnp.pad(kpe_c, ((0, 0), (0, 0), (0, 128 - MLA_ROPE))).astype(BF), kpe_r], axis=1)
        k_d, v_d = _mla_expand(ckv_all, kpe_all, lw["w_k"], lw["w_v"], lp["e_place"], 256)
        o_d = _dense_attn(q_d, (512, 0), k_d, (512, 0), v_d, (256, 0), groups=GROUPS_D, out_w=256,
                          tq=256, tk=512, first=P, name="mla_attn")
    else:
        o_a = _dense_attn(qkv_a, (256, 0), qkv_a, (256, 1), qkv_a, (256, 2), groups=GROUPS_A, out_w=256,
                          tq=N, tk=N, diff=diff, name="diff_attn_ctx")
        o_b = _dense_attn(qkv_b, (256, 0), qkv_b, (256, 1), qkv_b, (256, 2), groups=GROUPS_B, out_w=256,
                          tq=N, tk=N, name="dense_attn_ctx")
        o_c = _dense_attn(qkv_c, (256, 0), qkv_c, (128, 2), qkv_c, (128, 3), groups=GROUPS_C, out_w=256,
                          tq=N, tk=N, sink=lp["sw_sink"], name="gqa_attn_ctx")
        k_d, v_d = _mla_expand(ckv_n, kpe_r, lw["w_k"], lw["w_v"], lp["e_place"], N)
        o_d = _dense_attn(q_d, (512, 0), k_d, (512, 0), v_d, (256, 0), groups=GROUPS_D, out_w=256,
                          tq=N, tk=N, name="mla_attn_ctx")
    x_new, h2, logits = _merge(x, mod, lp["g_norm1"], lp["g_norm2"], o_a, o_b, o_c, o_d,
                               lw["w_gate"], lw["w_branch"], lw["w_out"], lw["w_router"], tm)
    y = _moe(h2, logits, lp["w_e_gate"], lp["w_e_up"], lp["w_e_down"])
    x_out = _residual(x_new, y, mod, g_final, final, tm)
    cache = None
    if not latent:
        cache = (qkv_a[:, :, 256:512].reshape(B, N, 4, 64), qkv_a[:, :, 512:768].reshape(B, N, 4, 64),
                 qkv_b[:, :, 256:512].reshape(B, N, 4, 64), qkv_b[:, :, 512:768].reshape(B, N, 4, 64),
                 qkv_c[:, :, 256:384].reshape(B, N, 2, 64), qkv_c[:, :, 384:512].reshape(B, N, 2, 64),
                 ckv_n, kpe_r[:, :, :MLA_ROPE])
    return x_out, cache


def kernel(x_prompt, x_sample, cache_diff_k, cache_diff_v, cache_na_k, cache_na_v, cache_swa_k, cache_swa_v, cache_mla_ckv, cache_mla_kpe, c, c_ctx, w_mod, b_mod, g_norm1, g_norm2, w_in, da_lambda, da_subln, na_rpb, sw_sink, mla_q_norm, mla_w_uq, mla_kv_norm, mla_w_ukv, w_gate, w_branch, w_out, w_router, w_e_gate, w_e_up, w_e_down, g_final):
    D = D_MODEL
    BS, NS, _ = x_sample.shape
    P = cache_diff_k.shape[2]
    cond = jnp.concatenate([c_ctx[None], c, jnp.zeros((16 - 1 - BS, D), F32)], axis=0)
    mod_all = _modulation(cond, w_mod, b_mod).reshape(DEPTH, 16, 6, D)
    tables = _rope_tables(NS)
    e_place = _kpe_placement()
    gf = g_final.reshape(1, D)
    xp, xs = x_prompt, x_sample
    caches = []
    for l in range(DEPTH):
        lw = _prep_layer(l, w_in, mla_w_uq, mla_w_ukv, w_gate, w_branch, w_out, w_router)
        lp = dict(
            g_norm1=g_norm1[l].reshape(1, D), g_norm2=g_norm2[l].reshape(1, D),
            mla_q_norm=mla_q_norm[l].reshape(1, 256), mla_kv_norm=mla_kv_norm[l].reshape(1, 128),
            da_lambda=da_lambda[l], da_subln=da_subln[l].reshape(1, 64), sw_sink=sw_sink[l],
            w_e_gate=w_e_gate[l], w_e_up=w_e_up[l], w_e_down=w_e_down[l], e_place=e_place,
            na_bias=_nbr_bias_tables(na_rpb[l], NS // GRID_W))
        lam_init = 0.8 - 0.6 * math.exp(-0.3 * l)
        final = l == DEPTH - 1
        xp, cache_l = _layer(xp, mod_all[l, 0:1], lw, lp, lam_init, None, None, final, gf)
        caches.append(cache_l)
        ctx = (cache_diff_k[:, l].reshape(BS, P, 256), cache_diff_v[:, l].reshape(BS, P, 256),
               cache_na_k[:, l].reshape(BS, P, 256), cache_na_v[:, l].reshape(BS, P, 256),
               cache_swa_k[:, l].reshape(BS, P, 128), cache_swa_v[:, l].reshape(BS, P, 128),
               cache_mla_ckv[:, l], cache_mla_kpe[:, l])
        xs, _ = _layer(xs, mod_all[l, 1:1 + BS], lw, lp, lam_init, ctx, tables, final, gf)
    outs = tuple(jnp.stack([cl[i] for cl in caches], axis=1) for i in range(8))
    return (xp, xs) + outs
```

```python
import functools
import math

import numpy as np
import jax
import jax.numpy as jnp
from jax import lax
from jax.experimental import pallas as pl
from jax.experimental.pallas import tpu as pltpu

BF = jnp.bfloat16
F32 = jnp.float32

D_MODEL = 1024
DEPTH = 4
GRID_W = 64
ROPE_BASE = 10000.0
NORM_EPS = 1e-6
NEG_INF = -1e30
LOG2E = math.log2(math.e)

DA_QK = 32
NA_WIN_R = 8
NA_WIN_C = 16
SW_WINDOW = 128
MLA_NOPE = 64
MLA_ROPE = 32
N_EXPERTS = 16
EC_CAPACITY = 2

IN_DIM = 2464
IN_PAD = 2560
LANES = 128

VMEM_BIG = 56 * 1024 * 1024
VMEM_MID = 40 * 1024 * 1024


def _cparams(sem, vmem=None):
    return pltpu.CompilerParams(dimension_semantics=sem, vmem_limit_bytes=vmem)


def _dot(a, b):
    return jnp.dot(a, b, preferred_element_type=F32)


def _dot_nt(a, b):
    return lax.dot_general(a, b, (((1,), (1,)), ((), ())), preferred_element_type=F32)


def _rms(x, g):
    var = jnp.mean(x * x, axis=-1, keepdims=True)
    return x * lax.rsqrt(var + NORM_EPS) * g


def _lane_mask(width, lo, length):
    lane = lax.broadcasted_iota(jnp.int32, (1, width), 1)
    return (lane >= lo) & (lane < lo + length)


def _mod_kernel(c_ref, w_ref, b_ref, o_ref):
    c = c_ref[...]
    s = c * (1.0 / (1.0 + jnp.exp(-c)))
    o_ref[0] = jnp.dot(s, w_ref[0], preferred_element_type=F32, precision=lax.Precision.HIGHEST) + b_ref[0]


def _modulation(cond, w_mod, b_mod):
    R, D = cond.shape
    L = w_mod.shape[0]
    nj = w_mod.shape[2] // D
    return pl.pallas_call(
        _mod_kernel,
        grid=(L, nj),
        in_specs=[
            pl.BlockSpec((R, D), lambda l, j: (0, 0)),
            pl.BlockSpec((1, D, D), lambda l, j: (l, 0, j)),
            pl.BlockSpec((1, 1, D), lambda l, j: (l, 0, j)),
        ],
        out_specs=pl.BlockSpec((1, R, D), lambda l, j: (l, 0, j)),
        out_shape=jax.ShapeDtypeStruct((L, R, nj * D), F32),
        compiler_params=_cparams(("parallel", "parallel")),
        name="modulation",
    )(cond, w_mod, b_mod.reshape(L, 1, nj * D))


def _rope(x, c, s, d):
    w = x.shape[-1]
    lane = lax.broadcasted_iota(jnp.int32, (1, LANES), 1)
    first = (lane % (2 * d)) < d
    outs = []
    for b in range(w // LANES):
        sl = slice(b * LANES, (b + 1) * LANES)
        xb = x[:, sl]
        partner = jnp.where(first, pltpu.roll(xb, LANES - d, 1), pltpu.roll(xb, d, 1))
        outs.append(xb * c[:, sl] + partner * s[:, sl])
    return outs[0] if len(outs) == 1 else jnp.concatenate(outs, axis=-1)


def _inproj_kernel(*refs, rope):
    if rope:
        (x_ref, mod_ref, g1_ref, w_ref, qn_ref, wuq_ref, kvn_ref,
         tac, tas, tcc, tcs, tqc, tqs, tkc, tks,
         oa, ob, oc, oq, ockv, okpe) = refs
    else:
        (x_ref, mod_ref, g1_ref, w_ref, qn_ref, wuq_ref, kvn_ref,
         oa, ob, oc, oq, ockv, okpe) = refs
    dt = oa.dtype
    x = x_ref[0]
    h = (_rms(x, g1_ref[...]) * (1.0 + mod_ref[0, 1:2, :]) + mod_ref[0, 0:1, :]).astype(BF)

    def seg(lo, w):
        return _dot(h, w_ref[:, lo:lo + w])

    ua = seg(0, 768)
    qk = ua[:, :512]
    if rope:
        qk = _rope(qk, tac[...], tas[...], 8)
    oa[0, :, 0:256] = (qk[:, :256] * (DA_QK ** -0.5 * LOG2E)).astype(dt)
    oa[0, :, 256:512] = qk[:, 256:].astype(dt)
    oa[0, :, 512:768] = ua[:, 512:].astype(dt)
    ub = seg(768, 768)
    ob[0, :, 0:256] = (ub[:, :256] * (0.125 * LOG2E)).astype(dt)
    ob[0, :, 256:768] = ub[:, 256:].astype(dt)
    uc = seg(1536, 512)
    qk = uc[:, :384]
    if rope:
        qk = _rope(qk, tcc[...], tcs[...], 16)
    oc[0, :, 0:256] = (qk[:, :256] * (0.125 * LOG2E)).astype(dt)
    oc[0, :, 256:384] = qk[:, 256:].astype(dt)
    oc[0, :, 384:512] = uc[:, 384:].astype(dt)
    cq = _rms(seg(2048, 256), qn_ref[...]).astype(BF)
    qd = _dot(cq, wuq_ref[...])
    if rope:
        qd = _rope(qd, tqc[...], tqs[...], 8)
    oq[0] = (qd * ((MLA_NOPE + MLA_ROPE) ** -0.5 * LOG2E)).astype(dt)
    ockv[0] = _rms(seg(2304, 128), kvn_ref[...]).astype(dt)
    kpe = seg(2432, 128)
    if rope:
        kpe = _rope(kpe, tkc[...], tks[...], 8)
    okpe[0] = kpe.astype(dt)


def _in_proj(x, mod, g1, w_in_p, q_norm, w_uq_p, kv_norm, tables, out_dtype, tm):
    B, N, D = x.shape
    rope = tables is not None
    nmod = mod.shape[0]
    mod_map = (lambda j, b: (b, 0, 0)) if nmod > 1 else (lambda j, b: (0, 0, 0))
    const2 = lambda j, b: (0, 0)
    in_specs = [
        pl.BlockSpec((1, tm, D), lambda j, b: (b, j, 0)),
        pl.BlockSpec((1, 6, D), mod_map),
        pl.BlockSpec((1, D), const2),
        pl.BlockSpec((D, IN_PAD), const2),
        pl.BlockSpec((1, 256), const2),
        pl.BlockSpec((256, 512), const2),
        pl.BlockSpec((1, 128), const2),
    ]
    args = [x, mod, g1, w_in_p, q_norm, w_uq_p, kv_norm]
    if rope:
        for t in tables:
            in_specs.append(pl.BlockSpec((tm, t.shape[1]), lambda j, b: (j, 0)))
            args.append(t)
    widths = (768, 768, 512, 512, 128, 128)
    out_specs = [pl.BlockSpec((1, tm, w), lambda j, b: (b, j, 0)) for w in widths]
    out_shape = [jax.ShapeDtypeStruct((B, N, w), out_dtype) for w in widths]
    return pl.pallas_call(
        functools.partial(_inproj_kernel, rope=rope),
        grid=(N // tm, B),
        in_specs=in_specs,
        out_specs=out_specs,
        out_shape=out_shape,
        compiler_params=_cparams(("parallel", "parallel"), VMEM_MID),
        name="in_proj_rope" if rope else "in_proj",
    )(*args)


def _mla_expand_kernel(ckv_ref, kpe_ref, wk_ref, wv_ref, e_ref, ok, ov):
    ckv = ckv_ref[0].astype(BF)
    kpe = kpe_ref[0].astype(BF)
    ok[0] = (_dot(ckv, wk_ref[...]) + _dot(kpe, e_ref[...])).astype(ok.dtype)
    ov[0] = _dot(ckv, wv_ref[...]).astype(ov.dtype)


def _mla_expand(ckv, kpe, w_k, w_v, e_place, tk):
    B, K, _ = ckv.shape
    const2 = lambda b, j: (0, 0)
    return pl.pallas_call(
        _mla_expand_kernel,
        grid=(B, K // tk),
        in_specs=[
            pl.BlockSpec((1, tk, 128), lambda b, j: (b, j, 0)),
            pl.BlockSpec((1, tk, 128), lambda b, j: (b, j, 0)),
            pl.BlockSpec((128, 512), const2),
            pl.BlockSpec((128, 256), const2),
            pl.BlockSpec((128, 512), const2),
        ],
        out_specs=[pl.BlockSpec((1, tk, 512), lambda b, j: (b, j, 0)),
                   pl.BlockSpec((1, tk, 256), lambda b, j: (b, j, 0))],
        out_shape=[jax.ShapeDtypeStruct((B, K, 512), BF), jax.ShapeDtypeStruct((B, K, 256), BF)],
        compiler_params=_cparams(("parallel", "parallel")),
        name="mla_expand",
    )(ckv, kpe, w_k, w_v, e_place)


def _softmax_init(tq, vw):
    return (jnp.full((tq, 1), NEG_INF, F32), jnp.zeros((tq, 1), F32), jnp.zeros((tq, vw), F32))


def _softmax_step(state, s, v):
    m, l, acc = state
    m_new = jnp.maximum(m, jnp.max(s, axis=-1, keepdims=True))
    alpha = jnp.exp2(m - m_new)
    p = jnp.exp2(s - m_new)
    l = alpha * l + jnp.sum(p, axis=-1, keepdims=True)
    acc = alpha * acc + _dot(p.astype(BF), v)
    return m_new, l, acc


def _dense_attn_kernel(*refs, groups, first, tk, n_chunks, has_ctx, has_sink, diff, lam_init):
    it = iter(refs)
    q_ref, k_ref, v_ref = next(it), next(it), next(it)
    ck_ref = cv_ref = sink_ref = lam_ref = subln_ref = None
    if has_ctx:
        ck_ref, cv_ref = next(it), next(it)
    if has_sink:
        sink_ref = next(it)
    if diff:
        lam_ref, subln_ref = next(it), next(it)
    o_ref = next(it)
    tq = q_ref.shape[1]
    ow = o_ref.shape[2]

    lam = None
    if diff:
        dl = lam_ref[...]
        lam = (jnp.exp(jnp.sum(dl[0:1] * dl[1:2], axis=-1, keepdims=True))
               - jnp.exp(jnp.sum(dl[2:3] * dl[3:4], axis=-1, keepdims=True)) + lam_init)

    ng = len(groups)
    scratch = list(it)
    qg_s, m_s, l_s, acc_s = (scratch[j * ng:(j + 1) * ng] for j in range(4))
    for gi, (q_lo, k_lo, w, qmask, v_lo, vw, o_lo, omask, sink_idx, sign) in enumerate(groups):
        qg = q_ref[0, :, q_lo:q_lo + w].astype(BF)
        if qmask is not None:
            qg = jnp.where(_lane_mask(w, *qmask), qg, jnp.zeros_like(qg))
        qg_s[gi][...] = qg
        m_s[gi][...] = jnp.full((tq, 1), NEG_INF, F32)
        l_s[gi][...] = jnp.zeros((tq, 1), F32)
        acc_s[gi][...] = jnp.zeros((tq, vw), F32)

    def keys_step(kk, vv):
        for gi, (q_lo, k_lo, w, qmask, v_lo, vw, o_lo, omask, sink_idx, sign) in enumerate(groups):
            st = (m_s[gi][...], l_s[gi][...], acc_s[gi][...])
            m, l, acc = _softmax_step(st, _dot_nt(qg_s[gi][...], kk[:, k_lo:k_lo + w]), vv[:, v_lo:v_lo + vw])
            m_s[gi][...] = m
            l_s[gi][...] = l
            acc_s[gi][...] = acc

    if has_ctx:
        keys_step(ck_ref[0].astype(BF), cv_ref[0].astype(BF))
    if first:
        keys_step(k_ref[0, 0:first, :].astype(BF), v_ref[0, 0:first, :].astype(BF))

    def body(c, carry):
        off = pl.multiple_of(first + c * tk, 128)
        keys_step(k_ref[0, pl.ds(off, tk), :].astype(BF), v_ref[0, pl.ds(off, tk), :].astype(BF))
        return carry

    if n_chunks <= 2:
        for c in range(n_chunks):
            body(c, 0)
    else:
        lax.fori_loop(0, n_chunks, body, 0)

    out_parts = {}
    for gi, (q_lo, k_lo, w, qmask, v_lo, vw, o_lo, omask, sink_idx, sign) in enumerate(groups):
        m, l, acc = m_s[gi][...], l_s[gi][...], acc_s[gi][...]
        if sink_idx is not None:
            l = l + jnp.exp2(sink_ref[sink_idx] * LOG2E - m)
        part = acc * (1.0 / l)
        if sign < 0:
            part = -lam * part
        part = jnp.where(_lane_mask(vw, *omask), part, 0.0)
        key = (o_lo, vw)
        out_parts[key] = part if key not in out_parts else out_parts[key] + part

    if diff:
        (o,) = out_parts.values()
        o2 = o * o
        inv = jnp.zeros_like(o)
        for hh in range(4):
            hm = _lane_mask(ow, 64 * hh, 64)
            ms = jnp.sum(jnp.where(hm, o2, 0.0), axis=-1, keepdims=True) * (1.0 / 64.0)
            inv = jnp.where(hm, lax.rsqrt(ms + NORM_EPS), inv)
        o_ref[0] = (o * inv * subln_ref[...] * (1.0 - lam_init)).astype(o_ref.dtype)
    else:
        for (o_lo, vw), part in out_parts.items():
            o_ref[0, :, o_lo:o_lo + vw] = part.astype(o_ref.dtype)


def _dense_attn(q_arr, q_blk, k_arr, k_blk, v_arr, v_blk, *, groups, out_w, tq, tk, first=0,
                ctx=None, sink=None, diff=None, name):
    B, N, _ = q_arr.shape
    K = k_arr.shape[1]
    n_chunks = (K - first) // tk
    assert first + n_chunks * tk == K
    qw, qi = q_blk
    kw, ki = k_blk
    vw, vi = v_blk
    in_specs = [
        pl.BlockSpec((1, tq, qw), lambda b, i: (b, i, qi)),
        pl.BlockSpec((1, K, kw), lambda b, i: (b, 0, ki)),
        pl.BlockSpec((1, K, vw), lambda b, i: (b, 0, vi)),
    ]
    args = [q_arr, k_arr, v_arr]
    if ctx is not None:
        ck, cv = ctx
        in_specs += [pl.BlockSpec((1,) + ck.shape[1:], lambda b, i: (b, 0, 0)),
                     pl.BlockSpec((1,) + cv.shape[1:], lambda b, i: (b, 0, 0))]
        args += [ck, cv]
    if sink is not None:
        in_specs.append(pl.BlockSpec(memory_space=pltpu.SMEM))
        args.append(sink)
    lam_init = 0.0
    if diff is not None:
        da_lambda, subln, lam_init = diff
        in_specs += [pl.BlockSpec((4, DA_QK), lambda b, i: (0, 0)),
                     pl.BlockSpec((1, out_w), lambda b, i: (0, 0))]
        args += [da_lambda, subln]
    kern = functools.partial(
        _dense_attn_kernel, groups=groups, first=first, tk=tk, n_chunks=n_chunks,
        has_ctx=ctx is not None, has_sink=sink is not None, diff=diff is not None, lam_init=lam_init)
    scratch = ([pltpu.VMEM((tq, g[2]), BF) for g in groups] + [pltpu.VMEM((tq, 1), F32) for _ in groups] * 2
               + [pltpu.VMEM((tq, g[5]), F32) for g in groups])
    return pl.pallas_call(
        kern,
        grid=(B, N // tq),
        in_specs=in_specs,
        out_specs=pl.BlockSpec((1, tq, out_w), lambda b, i: (b, i, 0)),
        out_shape=jax.ShapeDtypeStruct((B, N, out_w), BF),
        scratch_shapes=scratch,
        compiler_params=_cparams(("parallel", "parallel"), VMEM_MID),
        name=name,
    )(*args)


GROUPS_A = tuple((0, 0, 256, (32 * g, 32), 0, 256, 0, (64 * (g // 2), 64), None, 1 if g % 2 == 0 else -1)
                 for g in range(8))
GROUPS_B = tuple((0, 0, 256, (64 * h, 64), 0, 256, 0, (64 * h, 64), None, 1) for h in range(4))
GROUPS_C = tuple((128 * g, 0, 128, (64 * j, 64), 0, 128, 128 * g, (64 * j, 64), 2 * j + g, 1)
                 for g in range(2) for j in range(2))
GROUPS_D = tuple((128 * h, 128 * h, 128, None, 0, 256, 0, (64 * h, 64), None, 1) for h in range(4))


def _window_attn_kernel(q_ref, k_ref, v_ref, ck_ref, cv_ref, sink_ref, o_ref, *, n, band):
    tq = q_ref.shape[1]
    i = pl.program_id(1)
    start = jnp.clip(i * tq - SW_WINDOW, 0, n - band)
    start = pl.multiple_of(start, 128)
    kb = k_ref[0, pl.ds(start, band), :]
    vb = v_ref[0, pl.ds(start, band), :]
    ck = ck_ref[0].astype(BF)
    cv = cv_ref[0].astype(BF)
    qpos = i * tq + lax.broadcasted_iota(jnp.int32, (tq, band), 0)
    kpos = start + lax.broadcasted_iota(jnp.int32, (tq, band), 1)
    valid = jnp.abs(qpos - kpos) <= SW_WINDOW
    for g in range(2):
        part = jnp.zeros((tq, 128), F32)
        for j in range(2):
            hm = _lane_mask(128, 64 * j, 64)
            qg = q_ref[0, :, 128 * g:128 * (g + 1)]
            qg = jnp.where(hm, qg, jnp.zeros_like(qg))
            st = _softmax_init(tq, 128)
            st = _softmax_step(st, _dot_nt(qg, ck), cv)
            st = _softmax_step(st, jnp.where(valid, _dot_nt(qg, kb), NEG_INF), vb)
            m, l, acc = st
            l = l + jnp.exp2(sink_ref[2 * j + g] * LOG2E - m)
            part = jnp.where(hm, acc * (1.0 / l), part)
        o_ref[0, :, 128 * g:128 * (g + 1)] = part.astype(o_ref.dtype)


def _window_attn(qkv_c, ck, cv, sink, tq):
    B, N, _ = qkv_c.shape
    band = tq + 2 * SW_WINDOW
    P = ck.shape[1]
    return pl.pallas_call(
        functools.partial(_window_attn_kernel, n=N, band=band),
        grid=(B, N // tq),
        in_specs=[
            pl.BlockSpec((1, tq, 256), lambda b, i: (b, i, 0)),
            pl.BlockSpec((1, N, 128), lambda b, i: (b, 0, 2)),
            pl.BlockSpec((1, N, 128), lambda b, i: (b, 0, 3)),
            pl.BlockSpec((1, P, 128), lambda b, i: (b, 0, 0)),
            pl.BlockSpec((1, P, 128), lambda b, i: (b, 0, 0)),
            pl.BlockSpec(memory_space=pltpu.SMEM),
        ],
        out_specs=pl.BlockSpec((1, tq, 256), lambda b, i: (b, i, 0)),
        out_shape=jax.ShapeDtypeStruct((B, N, 256), BF),
        compiler_params=_cparams(("parallel", "parallel")),
        name="window_attn",
    )(qkv_c, qkv_c, qkv_c, ck, cv, sink)


NA_TILE_ROWS = 4
NA_KEY_ROWS = NA_TILE_ROWS + NA_WIN_R


def _nbr_attn_kernel(q_ref, k_ref, v_ref, ck_ref, cv_ref, bias_ref, o_ref, *, rows):
    tq = q_ref.shape[1]
    i = pl.program_id(1)
    krow = jnp.clip(i * NA_TILE_ROWS - NA_WIN_R // 2, 0, rows - NA_KEY_ROWS)
    start = pl.multiple_of(krow * GRID_W, GRID_W)
    nk = NA_KEY_ROWS * GRID_W
    kb = k_ref[0, pl.ds(start, nk), :]
    vb = v_ref[0, pl.ds(start, nk), :]
    ck = ck_ref[0].astype(BF)
    cv = cv_ref[0].astype(BF)
    q = q_ref[0]
    o = jnp.zeros((tq, 256), F32)
    for h in range(4):
        hm = _lane_mask(256, 64 * h, 64)
        qg = jnp.where(hm, q, jnp.zeros_like(q))
        st = _softmax_init(tq, 256)
        st = _softmax_step(st, _dot_nt(qg, ck), cv)
        st = _softmax_step(st, _dot_nt(qg, kb) + bias_ref[0, h], vb)
        m, l, acc = st
        o = jnp.where(hm, acc * (1.0 / l), o)
    o_ref[0] = o.astype(o_ref.dtype)


def _nbr_attn(qkv_b, ck, cv, bias):
    B, N, _ = qkv_b.shape
    rows = N // GRID_W
    tq = NA_TILE_ROWS * GRID_W
    nt = N // tq
    nk = NA_KEY_ROWS * GRID_W
    P = ck.shape[1]

    def bias_map(b, i):
        return (jnp.where(i == 0, 0, jnp.where(i == nt - 1, 2, 1)), 0, 0, 0)

    return pl.pallas_call(
        functools.partial(_nbr_attn_kernel, rows=rows),
        grid=(B, nt),
        in_specs=[
            pl.BlockSpec((1, tq, 256), lambda b, i: (b, i, 0)),
            pl.BlockSpec((1, N, 256), lambda b, i: (b, 0, 1)),
            pl.BlockSpec((1, N, 256), lambda b, i: (b, 0, 2)),
            pl.BlockSpec((1, P, 256), lambda b, i: (b, 0, 0)),
            pl.BlockSpec((1, P, 256), lambda b, i: (b, 0, 0)),
            pl.BlockSpec((1, 4, tq, nk), bias_map),
        ],
        out_specs=pl.BlockSpec((1, tq, 256), lambda b, i: (b, i, 0)),
        out_shape=jax.ShapeDtypeStruct((B, N, 256), BF),
        compiler_params=_cparams(("parallel", "parallel"), VMEM_MID),
        name="nbr_attn",
    )(qkv_b, qkv_b, qkv_b, ck, cv, bias)


def _nbr_bias_tables(rpb, rows):
    H = rpb.shape[0]
    rp = jnp.pad(rpb.astype(F32), ((0, 0), (NA_KEY_ROWS, NA_KEY_ROWS), (GRID_W, GRID_W)))
    c0 = NA_WIN_C - 1 + GRID_W
    tcol = jnp.stack([rp[:, :, c0 - qc:c0 - qc + GRID_W] for qc in range(GRID_W)], axis=2)
    tabs = []
    nt = rows // NA_TILE_ROWS
    for r0 in (0, NA_TILE_ROWS * min(1, nt - 1) + NA_TILE_ROWS, rows - NA_TILE_ROWS):
        ks = int(np.clip(r0 - NA_WIN_R // 2, 0, rows - NA_KEY_ROWS))
        per_row = []
        for a in range(NA_TILE_ROWS):
            s0 = ks - (r0 + a) + NA_WIN_R - 1 + NA_KEY_ROWS
            per_row.append(tcol[:, s0:s0 + NA_KEY_ROWS])
        t = jnp.stack(per_row, axis=1).transpose(0, 1, 3, 2, 4)
        r = r0 + np.arange(NA_TILE_ROWS)[:, None, None, None]
        qc = np.arange(GRID_W)[None, :, None, None]
        kr = ks + np.arange(NA_KEY_ROWS)[None, None, :, None]
        kc = np.arange(GRID_W)[None, None, None, :]
        rs = np.clip(r - NA_WIN_R // 2, 0, rows - NA_WIN_R)
        cs = np.clip(qc - NA_WIN_C // 2, 0, GRID_W - NA_WIN_C)
        ok = (kr >= rs) & (kr < rs + NA_WIN_R) & (kc >= cs) & (kc < cs + NA_WIN_C)
        t = jnp.where(ok[None], t * LOG2E, NEG_INF)
        tabs.append(t.reshape(H, NA_TILE_ROWS * GRID_W, NA_KEY_ROWS * GRID_W))
    return jnp.stack(tabs)


def _merge_kernel(x_ref, mod_ref, g1_ref, g2_ref, oa, ob, oc, od, wg_ref, wb_ref, wo_ref, wr_ref,
                  xo_ref, h2_ref, lg_ref):
    D = x_ref.shape[2]
    x = x_ref[0]
    h = (_rms(x, g1_ref[...]) * (1.0 + mod_ref[0, 1:2, :]) + mod_ref[0, 0:1, :]).astype(BF)
    merged = None
    for i, o_ref in enumerate((oa, ob, oc, od)):
        z = _dot(h, wg_ref[:, i * D:(i + 1) * D])
        gate = 1.0 / (1.0 + jnp.exp(-z))
        term = gate * _dot(o_ref[0], wb_ref[i])
        merged = term if merged is None else merged + term
    mixed = _dot(merged.astype(BF), wo_ref[...])
    xn = x + mod_ref[0, 2:3, :] * mixed
    xo_ref[0] = xn
    h2 = (_rms(xn, g2_ref[...]) * (1.0 + mod_ref[0, 4:5, :]) + mod_ref[0, 3:4, :]).astype(BF)
    h2_ref[0] = h2
    lg_ref[0] = _dot_nt(wr_ref[...], h2)


def _merge(x, mod, g1, g2, o_a, o_b, o_c, o_d, w_gate, w_branch, w_out, w_router_p, tm):
    B, N, D = x.shape
    nmod = mod.shape[0]
    mod_map = (lambda b, j: (b, 0, 0)) if nmod > 1 else (lambda b, j: (0, 0, 0))
    tok = lambda w: pl.BlockSpec((1, tm, w), lambda b, j: (b, j, 0))
    c2 = lambda b, j: (0, 0)
    return pl.pallas_call(
        _merge_kernel,
        grid=(B, N // tm),
        in_specs=[
            tok(D), pl.BlockSpec((1, 6, D), mod_map), pl.BlockSpec((1, D), c2), pl.BlockSpec((1, D), c2),
            tok(256), tok(256), tok(256), tok(256),
            pl.BlockSpec((D, 4 * D), c2),
            pl.BlockSpec((4, 256, D), lambda b, j: (0, 0, 0)),
            pl.BlockSpec((D, D), c2),
            pl.BlockSpec((N_EXPERTS, D), c2),
        ],
        out_specs=[tok(D), tok(D), pl.BlockSpec((1, N_EXPERTS, tm), lambda b, j: (b, 0, j))],
        out_shape=[jax.ShapeDtypeStruct((B, N, D), F32), jax.ShapeDtypeStruct((B, N, D), BF),
                   jax.ShapeDtypeStruct((B, N_EXPERTS, N), F32)],
        compiler_params=_cparams(("parallel", "parallel"), VMEM_BIG),
        name="merge",
    )(x, mod, g1, g2, o_a, o_b, o_c, o_d, w_gate, w_branch, w_out, w_router_p)


MOE_TT = 256
MOE_RC = 64
SLOT_ALIGN = 16


def _excl_cumsum(x, tri, reset_every):
    R, N = x.shape
    outs = []
    carry = jnp.zeros((R, 1), F32)
    for c in range(N // LANES):
        if reset_every and (c * LANES) % reset_every == 0:
            carry = jnp.zeros((R, 1), F32)
        xc = x[:, c * LANES:(c + 1) * LANES]
        inc = _dot(xc.astype(BF), tri)
        outs.append(inc - xc + carry)
        carry = carry + inc[:, LANES - 1:LANES]
    return (outs[0] if len(outs) == 1 else jnp.concatenate(outs, axis=1)), carry


def _route_kernel(lg_ref, aff_ref, rel_ref, base_ref, *, cap, tt):
    B, E, N = lg_ref.shape
    R = B * E
    nt = N // tt
    lg = lg_ref[...]
    ex = jnp.exp(lg - jnp.max(lg, axis=1, keepdims=True))
    aff = ex / jnp.sum(ex, axis=1, keepdims=True)
    aff_ref[...] = aff
    bits = pltpu.bitcast(aff.reshape(R, N), jnp.int32)

    def per_expert(v):
        t = v[0:E]
        for b in range(1, B):
            t = t + v[b * E:(b + 1) * E]
        return t

    def per_row(v):
        return jnp.concatenate([v] * B, axis=0) if B > 1 else v

    def batch_prefix(v):
        parts, run = [], jnp.zeros((E, 1), F32)
        for b in range(B):
            parts.append(run)
            run = run + v[b * E:(b + 1) * E]
        return jnp.concatenate(parts, axis=0) if B > 1 else parts[0]

    def count(mask):
        return per_expert(jnp.sum(jnp.where(mask, 1.0, 0.0), axis=1, keepdims=True))

    def search(i, thr):
        cand = thr | jnp.left_shift(jnp.int32(1), 30 - i)
        return jnp.where(count(bits >= per_row(cand)) >= float(cap), cand, thr)

    thr = per_row(lax.fori_loop(0, 31, search, jnp.zeros((E, 1), jnp.int32)))
    gt = bits > thr
    eq = bits == thr
    need = float(cap) - count(gt)
    ri = lax.broadcasted_iota(jnp.int32, (LANES, LANES), 0)
    ci = lax.broadcasted_iota(jnp.int32, (LANES, LANES), 1)
    tri = jnp.where(ri <= ci, 1.0, 0.0).astype(BF)
    rank_eq, eq_tot = _excl_cumsum(jnp.where(eq, 1.0, 0.0), tri, None)
    rank_eq = rank_eq + batch_prefix(eq_tot)
    sel = gt | (eq & (rank_eq < per_row(need)))
    self = jnp.where(sel, 1.0, 0.0)
    local, _ = _excl_cumsum(self, tri, tt)
    rel_ref[...] = jnp.where(sel, local, -1.0).astype(jnp.int32).reshape(B, E, N)
    tn = jnp.right_shift(lax.broadcasted_iota(jnp.int32, (N, LANES), 0), tt.bit_length() - 1)
    tj = lax.broadcasted_iota(jnp.int32, (N, LANES), 1)
    cnt = _dot(self.astype(BF), jnp.where(tn == tj, 1.0, 0.0).astype(BF)).astype(jnp.int32)
    sh = SLOT_ALIGN.bit_length() - 1
    ru = jnp.left_shift(jnp.right_shift(cnt + (SLOT_ALIGN - 1), sh), sh).astype(F32)
    base_b = _dot(ru.astype(BF), jnp.where(ri < ci, 1.0, 0.0).astype(BF))
    base = base_b + batch_prefix(base_b[:, nt:nt + 1])
    base_ref[...] = base.astype(jnp.int32).reshape(B, E, LANES)


def _route(lg_t, cap):
    B, E, N = lg_t.shape
    full = lambda s: pl.BlockSpec(s, lambda i: (0,) * len(s))
    return pl.pallas_call(
        functools.partial(_route_kernel, cap=cap, tt=MOE_TT),
        grid=(1,),
        in_specs=[full((B, E, N))],
        out_specs=[full((B, E, N)), full((B, E, N)), full((B, E, LANES))],
        out_shape=[jax.ShapeDtypeStruct((B, E, N), F32), jax.ShapeDtypeStruct((B, E, N), jnp.int32),
                   jax.ShapeDtypeStruct((B, E, LANES), jnp.int32)],
        compiler_params=_cparams(("arbitrary",), VMEM_BIG),
        name="route",
    )(lg_t)


def _slot_onehot(rel, values, c, rc):
    E, tt = rel.shape
    r = lax.broadcasted_iota(jnp.int32, (E, rc, tt), 1) + c * rc
    return jnp.where(rel[:, None, :] == r, values[:, None, :], 0.0).astype(BF).reshape(E * rc, tt)


def _dispatch_kernel(base_sm, x_ref, rel_ref, xe_in, xe_hbm, stage, stage2, sem, sem2, *, nt, rc):
    del xe_in
    b, j = pl.program_id(0), pl.program_id(1)
    E = rel_ref.shape[1]
    step = b * nt + j
    last = pl.num_programs(0) * nt - 1
    base = lambda e, jj: base_sm[(b * E + e) * (nt + 1) + jj]
    rel = rel_ref[0]
    x = x_ref[0]
    ones = jnp.ones(rel.shape, F32)

    def copy(buf, s, e, c):
        start = pl.multiple_of(base(e, j) + c * rc, SLOT_ALIGN)
        return pltpu.make_async_copy(buf.at[pl.ds(e * rc, rc)], xe_hbm.at[e, pl.ds(start, rc)], s.at[e])

    strip = _dot(_slot_onehot(rel, ones, 0, rc), x).astype(BF)

    @pl.when(step > 0)
    def _():
        for e in range(E):
            copy(stage, sem, e, 0).wait()

    stage[...] = strip
    for e in range(E):
        copy(stage, sem, e, 0).start()

    span = [base(e, j + 1) - base(e, j) for e in range(E)]
    widest = functools.reduce(jnp.maximum, span)

    def extra(c, carry):
        stage2[...] = _dot(_slot_onehot(rel, ones, c, rc), x).astype(BF)
        for e in range(E):
            @pl.when(span[e] > c * rc)
            def _():
                cp = copy(stage2, sem2, e, c)
                cp.start()
                cp.wait()
        return carry

    lax.fori_loop(1, (widest + rc - 1) // rc, extra, 0)

    @pl.when(step == last)
    def _():
        for e in range(E):
            copy(stage, sem, e, 0).wait()


def _dispatch(h2, rel, base_flat, rows):
    B, N, D = h2.shape
    E = rel.shape[1]
    nt = N // MOE_TT
    xe0 = jnp.zeros((E, rows, D), BF)
    return pl.pallas_call(
        functools.partial(_dispatch_kernel, nt=nt, rc=MOE_RC),
        grid_spec=pltpu.PrefetchScalarGridSpec(
            num_scalar_prefetch=1,
            grid=(B, nt),
            in_specs=[
                pl.BlockSpec((1, MOE_TT, D), lambda b, j, s: (b, j, 0)),
                pl.BlockSpec((1, E, MOE_TT), lambda b, j, s: (b, 0, j)),
                pl.BlockSpec(memory_space=pl.ANY),
            ],
            out_specs=pl.BlockSpec(memory_space=pl.ANY),
            scratch_shapes=[pltpu.VMEM((E * MOE_RC, D), BF), pltpu.VMEM((E * MOE_RC, D), BF),
                            pltpu.SemaphoreType.DMA((E,)), pltpu.SemaphoreType.DMA((E,))],
        ),
        out_shape=jax.ShapeDtypeStruct((E, rows, D), BF),
        input_output_aliases={3: 0},
        compiler_params=_cparams(("arbitrary", "arbitrary")),
        name="dispatch",
    )(base_flat, h2, rel, xe0)


def _ffn_kernel(tot_sm, x_ref, wg_ref, wu_ref, wd_ref, y_ref, wg_s, wu_s, wd_s, *, tm, rc):
    e, j = pl.program_id(0), pl.program_id(1)

    @pl.when(j == 0)
    def _():
        wg_s[...] = wg_ref[0].astype(BF)
        wu_s[...] = wu_ref[0].astype(BF)
        wd_s[...] = wd_ref[0].astype(BF)

    used = j * tm < tot_sm[e] + rc

    @pl.when(used)
    def _():
        x = x_ref[0]
        hg = _dot(x, wg_s[...])
        hu = _dot(x, wu_s[...])
        act = (hg * (1.0 / (1.0 + jnp.exp(-hg))) * hu).astype(BF)
        y_ref[0] = _dot(act, wd_s[...]).astype(y_ref.dtype)

    @pl.when(jnp.logical_not(used))
    def _():
        y_ref[...] = jnp.zeros_like(y_ref)


def _expert_ffn(xe, total, w_g, w_u, w_d, tm):
    E, rows, D = xe.shape
    DE = w_g.shape[2]
    tok = lambda e, j, tot: (e, jnp.minimum(j, (tot[e] + MOE_RC - 1) // tm), 0)
    wmap = lambda e, j, tot: (e, 0, 0)
    return pl.pallas_call(
        functools.partial(_ffn_kernel, tm=tm, rc=MOE_RC),
        grid_spec=pltpu.PrefetchScalarGridSpec(
            num_scalar_prefetch=1,
            grid=(E, rows // tm),
            in_specs=[pl.BlockSpec((1, tm, D), tok), pl.BlockSpec((1, D, DE), wmap),
                      pl.BlockSpec((1, D, DE), wmap), pl.BlockSpec((1, DE, D), wmap)],
            out_specs=pl.BlockSpec((1, tm, D), lambda e, j, tot: (e, j, 0)),
            scratch_shapes=[pltpu.VMEM((D, DE), BF), pltpu.VMEM((D, DE), BF), pltpu.VMEM((DE, D), BF)],
        ),
        out_shape=jax.ShapeDtypeStruct((E, rows, D), BF),
        compiler_params=_cparams(("arbitrary", "arbitrary"), VMEM_BIG),
        name="expert_ffn",
    )(total, xe, w_g, w_u, w_d)


def _combine_kernel(base_sm, x_ref, rel_ref, aff_ref, mod_ref, gf_ref, ye_hbm, o_ref, stage, sem, *,
                    nt, rc, final):
    b, j = pl.program_id(0), pl.program_id(1)
    E = rel_ref.shape[1]
    base = lambda e, jj: base_sm[(b * E + e) * (nt + 1) + jj]
    rel = rel_ref[0]
    aff = aff_ref[0]

    def copy(e, c):
        start = pl.multiple_of(base(e, j) + c * rc, SLOT_ALIGN)
        return pltpu.make_async_copy(ye_hbm.at[e, pl.ds(start, rc)], stage.at[pl.ds(e * rc, rc)], sem.at[e])

    def scatter(c):
        w = _slot_onehot(rel, aff, c, rc)
        return lax.dot_general(w, stage[...], (((0,), (0,)), ((), ())), preferred_element_type=F32)

    for e in range(E):
        copy(e, 0).start()
    for e in range(E):
        copy(e, 0).wait()
    acc = scatter(0)
    span = [base(e, j + 1) - base(e, j) for e in range(E)]
    widest = functools.reduce(jnp.maximum, span)

    def extra(c, acc):
        for e in range(E):
            @pl.when(span[e] > c * rc)
            def _():
                cp = copy(e, c)
                cp.start()
                cp.wait()
        return acc + scatter(c)

    acc = lax.fori_loop(1, (widest + rc - 1) // rc, extra, acc)
    xn = x_ref[0] + mod_ref[0, 5:6, :] * acc
    if final:
        xn = _rms(xn, gf_ref[...])
    o_ref[0] = xn


def _combine(x, rel, aff, mod, g_final, ye, base_flat, final):
    B, N, D = x.shape
    E = rel.shape[1]
    nt = N // MOE_TT
    nmod = mod.shape[0]
    mod_map = (lambda b, j, s: (b, 0, 0)) if nmod > 1 else (lambda b, j, s: (0, 0, 0))
    tok = pl.BlockSpec((1, MOE_TT, D), lambda b, j, s: (b, j, 0))
    slots = pl.BlockSpec((1, E, MOE_TT), lambda b, j, s: (b, 0, j))
    return pl.pallas_call(
        functools.partial(_combine_kernel, nt=nt, rc=MOE_RC, final=final),
        grid_spec=pltpu.PrefetchScalarGridSpec(
            num_scalar_prefetch=1,
            grid=(B, nt),
            in_specs=[tok, slots, slots, pl.BlockSpec((1, 6, D), mod_map),
                      pl.BlockSpec((1, D), lambda b, j, s: (0, 0)), pl.BlockSpec(memory_space=pl.ANY)],
            out_specs=tok,
            scratch_shapes=[pltpu.VMEM((E * MOE_RC, D), BF), pltpu.SemaphoreType.DMA((E,))],
        ),
        out_shape=jax.ShapeDtypeStruct((B, N, D), F32),
        compiler_params=_cparams(("arbitrary", "arbitrary")),
        name="combine_final" if final else "combine",
    )(base_flat, x, rel, aff, mod, g_final, ye)


def _moe(x, h2, lg_t, mod, g_final, w_g, w_u, w_d, final):
    B, N, D = h2.shape
    T = B * N
    nt = N // MOE_TT
    cap = EC_CAPACITY * T // N_EXPERTS
    tm = 512 if cap >= 2048 else 256
    worst = cap + (SLOT_ALIGN - 1) * (T // MOE_TT) + MOE_RC
    rows = -(-worst // tm) * tm
    aff, rel, base = _route(lg_t, cap)
    base_flat = base[:, :, :nt + 1].reshape(-1)
    total = base[B - 1, :, nt]
    xe = _dispatch(h2, rel, base_flat, rows)
    ye = _expert_ffn(xe, total, w_g, w_u, w_d, tm)
    return _combine(x, rel, aff, mod, g_final, ye, base_flat, final)


def _rope_tables(n):
    t = jnp.arange(n)
    pos = {1: (t // GRID_W).astype(F32), 2: (t % GRID_W).astype(F32)}

    def group(quarter):
        freqs = jnp.float32(ROPE_BASE) ** (-jnp.arange(quarter, dtype=F32) / quarter)
        cs, ss = [], []
        for kind in (1, 2):
            ang = pos[kind][:, None] * freqs
            c, s = jnp.cos(ang), jnp.sin(ang)
            cs += [c, c]
            ss += [-s, s]
        return jnp.concatenate(cs, axis=1), jnp.concatenate(ss, axis=1)

    c32, s32 = group(8)
    c64, s64 = group(16)
    one = lambda w: jnp.ones((n, w), F32)
    zero = lambda w: jnp.zeros((n, w), F32)
    ta = (jnp.tile(c32, (1, 16)), jnp.tile(s32, (1, 16)))
    tc = (jnp.tile(c64, (1, 6)), jnp.tile(s64, (1, 6)))
    qc = jnp.concatenate([one(64), c32, one(32)], axis=1)
    qs = jnp.concatenate([zero(64), s32, zero(32)], axis=1)
    tq = (jnp.tile(qc, (1, 4)), jnp.tile(qs, (1, 4)))
    tk = (jnp.concatenate([c32, one(96)], axis=1), jnp.concatenate([s32, zero(96)], axis=1))
    return (ta[0], ta[1], tc[0], tc[1], tq[0], tq[1], tk[0], tk[1])


C_HEAD_ORDER = (0, 2, 1, 3)


def _prep_layer(l, w_in, mla_w_uq, mla_w_ukv, w_gate, w_branch, w_out, w_router):
    D = D_MODEL
    wi = w_in[l]
    cq = wi[:, 1536:1792].reshape(D, 4, 64)[:, C_HEAD_ORDER, :].reshape(D, 256)
    w_in_p = jnp.concatenate(
        [wi[:, :1536], cq, wi[:, 1792:], jnp.zeros((D, IN_PAD - IN_DIM), F32)], axis=1).astype(BF)
    uq = mla_w_uq[l].reshape(256, 4, MLA_NOPE + MLA_ROPE)
    w_uq_p = jnp.concatenate([uq, jnp.zeros((256, 4, 32), F32)], axis=2).reshape(256, 512).astype(BF)
    ukv = mla_w_ukv[l].reshape(128, 4, 128)
    w_k = jnp.concatenate([ukv[:, :, :64], jnp.zeros((128, 4, 64), F32)], axis=2).reshape(128, 512).astype(BF)
    w_v = ukv[:, :, 64:].reshape(128, 256).astype(BF)
    wb = w_branch[l]
    wb_c = wb[2].reshape(4, 64, D)[C_HEAD_ORDER, :, :].reshape(256, D)
    w_branch_p = jnp.stack([wb[0], wb[1], wb_c, wb[3]]).astype(BF)
    w_router_p = w_router[l].T.astype(BF)
    return dict(w_in=w_in_p, w_uq=w_uq_p, w_k=w_k, w_v=w_v, w_gate=w_gate[l].astype(BF),
                w_branch=w_branch_p, w_out=w_out[l].astype(BF), w_router=w_router_p)


def _kpe_placement():
    e = np.zeros((128, 512), np.float32)
    for h in range(4):
        for i in range(MLA_ROPE):
            e[i, 128 * h + MLA_NOPE + i] = 1.0
    return jnp.asarray(e, BF)


def _layer(x, mod, lw, lp, lam_init, ctx, tables, final, g_final):
    B, N, D = x.shape
    latent = ctx is not None
    tm = 512 if latent else 256
    out_dtype = BF if latent else F32
    qkv_a, qkv_b, qkv_c, q_d, ckv_n, kpe_r = _in_proj(
        x, mod, lp["g_norm1"], lw["w_in"], lp["mla_q_norm"], lw["w_uq"], lp["mla_kv_norm"],
        tables if latent else None, out_dtype, tm)
    diff = (lp["da_lambda"], jnp.tile(lp["da_subln"], (1, 4)), lam_init)
    if latent:
        ck_a, cv_a, ck_b, cv_b, ck_c, cv_c, ckv_c, kpe_c = ctx
        P = ck_a.shape[1]
        o_a = _dense_attn(qkv_a, (256, 0), qkv_a, (256, 1), qkv_a, (256, 2), groups=GROUPS_A, out_w=256,
                          tq=256, tk=512, ctx=(ck_a, cv_a), diff=diff, name="diff_attn")
        o_b = _nbr_attn(qkv_b, ck_b, cv_b, lp["na_bias"])
        o_c = _window_attn(qkv_c, ck_c, cv_c, lp["sw_sink"], 256)
        ckv_all = jnp.concatenate([ckv_c.astype(BF), ckv_n], axis=1)
        kpe_all = jnp.concatenate([jnp.pad(kpe_c, ((0, 0), (0, 0), (0, 128 - MLA_ROPE))).astype(BF), kpe_r], axis=1)
        k_d, v_d = _mla_expand(ckv_all, kpe_all, lw["w_k"], lw["w_v"], lp["e_place"], 256)
        o_d = _dense_attn(q_d, (512, 0), k_d, (512, 0), v_d, (256, 0), groups=GROUPS_D, out_w=256,
                          tq=256, tk=512, first=P, name="mla_attn")
    else:
        o_a = _dense_attn(qkv_a, (256, 0), qkv_a, (256, 1), qkv_a, (256, 2), groups=GROUPS_A, out_w=256,
                          tq=N, tk=N, diff=diff, name="diff_attn_ctx")
        o_b = _dense_attn(qkv_b, (256, 0), qkv_b, (256, 1), qkv_b, (256, 2), groups=GROUPS_B, out_w=256,
                          tq=N, tk=N, name="dense_attn_ctx")
        o_c = _dense_attn(qkv_c, (256, 0), qkv_c, (128, 2), qkv_c, (128, 3), groups=GROUPS_C, out_w=256,
                          tq=N, tk=N, sink=lp["sw_sink"], name="gqa_attn_ctx")
        k_d, v_d = _mla_expand(ckv_n, kpe_r, lw["w_k"], lw["w_v"], lp["e_place"], N)
        o_d = _dense_attn(q_d, (512, 0), k_d, (512, 0), v_d, (256, 0), groups=GROUPS_D, out_w=256,
                          tq=N, tk=N, name="mla_attn_ctx")
    x_new, h2, logits = _merge(x, mod, lp["g_norm1"], lp["g_norm2"], o_a, o_b, o_c, o_d,
                               lw["w_gate"], lw["w_branch"], lw["w_out"], lw["w_router"], tm)
    x_out = _moe(x_new, h2, logits, mod, g_final, lp["w_e_gate"], lp["w_e_up"], lp["w_e_down"], final)
    cache = None
    if not latent:
        cache = (qkv_a[:, :, 256:512].reshape(B, N, 4, 64), qkv_a[:, :, 512:768].reshape(B, N, 4, 64),
                 qkv_b[:, :, 256:512].reshape(B, N, 4, 64), qkv_b[:, :, 512:768].reshape(B, N, 4, 64),
                 qkv_c[:, :, 256:384].reshape(B, N, 2, 64), qkv_c[:, :, 384:512].reshape(B, N, 2, 64),
                 ckv_n, kpe_r[:, :, :MLA_ROPE])
    return x_out, cache


def kernel(x_prompt, x_sample, cache_diff_k, cache_diff_v, cache_na_k, cache_na_v, cache_swa_k, cache_swa_v, cache_mla_ckv, cache_mla_kpe, c, c_ctx, w_mod, b_mod, g_norm1, g_norm2, w_in, da_lambda, da_subln, na_rpb, sw_sink, mla_q_norm, mla_w_uq, mla_kv_norm, mla_w_ukv, w_gate, w_branch, w_out, w_router, w_e_gate, w_e_up, w_e_down, g_final):
    D = D_MODEL
    BS, NS, _ = x_sample.shape
    P = cache_diff_k.shape[2]
    cond = jnp.concatenate([c_ctx[None], c, jnp.zeros((16 - 1 - BS, D), F32)], axis=0)
    mod_all = _modulation(cond, w_mod, b_mod).reshape(DEPTH, 16, 6, D)
    tables = _rope_tables(NS)
    e_place = _kpe_placement()
    gf = g_final.reshape(1, D)
    xp, xs = x_prompt, x_sample
    caches = []
    for l in range(DEPTH):
        lw = _prep_layer(l, w_in, mla_w_uq, mla_w_ukv, w_gate, w_branch, w_out, w_router)
        lp = dict(
            g_norm1=g_norm1[l].reshape(1, D), g_norm2=g_norm2[l].reshape(1, D),
            mla_q_norm=mla_q_norm[l].reshape(1, 256), mla_kv_norm=mla_kv_norm[l].reshape(1, 128),
            da_lambda=da_lambda[l], da_subln=da_subln[l].reshape(1, 64), sw_sink=sw_sink[l],
            w_e_gate=w_e_gate[l], w_e_up=w_e_up[l], w_e_down=w_e_down[l], e_place=e_place,
            na_bias=_nbr_bias_tables(na_rpb[l], NS // GRID_W))
        lam_init = 0.8 - 0.6 * math.exp(-0.3 * l)
        final = l == DEPTH - 1
        xp, cache_l = _layer(xp, mod_all[l, 0:1], lw, lp, lam_init, None, None, final, gf)
        caches.append(cache_l)
        ctx = (cache_diff_k[:, l].reshape(BS, P, 256), cache_diff_v[:, l].reshape(BS, P, 256),
               cache_na_k[:, l].reshape(BS, P, 256), cache_na_v[:, l].reshape(BS, P, 256),
               cache_swa_k[:, l].reshape(BS, P, 128), cache_swa_v[:, l].reshape(BS, P, 128),
               cache_mla_ckv[:, l], cache_mla_kpe[:, l])
        xs, _ = _layer(xs, mod_all[l, 1:1 + BS], lw, lp, lam_init, ctx, tables, final, gf)
    outs = tuple(jnp.stack([cl[i] for cl in caches], axis=1) for i in range(8))
    return (xp, xs) + outs
```

```python
import functools
import math

import numpy as np
import jax
import jax.numpy as jnp
from jax import lax
from jax.experimental import pallas as pl
from jax.experimental.pallas import tpu as pltpu

BF = jnp.bfloat16
F32 = jnp.float32

D_MODEL = 1024
DEPTH = 4
GRID_W = 64
ROPE_BASE = 10000.0
NORM_EPS = 1e-6
NEG_INF = -1e30
LOG2E = math.log2(math.e)

DA_QK = 32
NA_WIN_R = 8
NA_WIN_C = 16
SW_WINDOW = 128
MLA_NOPE = 64
MLA_ROPE = 32
N_EXPERTS = 16
EC_CAPACITY = 2

IN_DIM = 2464
IN_PAD = 2816
LANES = 128

VMEM_BIG = 56 * 1024 * 1024
VMEM_MID = 40 * 1024 * 1024


def _cparams(sem, vmem=None):
    return pltpu.CompilerParams(dimension_semantics=sem, vmem_limit_bytes=vmem)


def _dot(a, b):
    return jnp.dot(a, b, preferred_element_type=F32)


def _dot_nt(a, b):
    return lax.dot_general(a, b, (((1,), (1,)), ((), ())), preferred_element_type=F32)


def _rms(x, g):
    var = jnp.mean(x * x, axis=-1, keepdims=True)
    return x * lax.rsqrt(var + NORM_EPS) * g


def _lane_mask(width, lo, length):
    lane = lax.broadcasted_iota(jnp.int32, (1, width), 1)
    return (lane >= lo) & (lane < lo + length)


def _mod_kernel(c_ref, w_ref, b_ref, o_ref):
    c = c_ref[...]
    s = c * (1.0 / (1.0 + jnp.exp(-c)))
    o_ref[0] = jnp.dot(s, w_ref[0], preferred_element_type=F32, precision=lax.Precision.HIGHEST) + b_ref[0]


def _modulation(cond, w_mod, b_mod):
    R, D = cond.shape
    L = w_mod.shape[0]
    nj = w_mod.shape[2] // D
    return pl.pallas_call(
        _mod_kernel,
        grid=(L, nj),
        in_specs=[
            pl.BlockSpec((R, D), lambda l, j: (0, 0)),
            pl.BlockSpec((1, D, D), lambda l, j: (l, 0, j)),
            pl.BlockSpec((1, 1, D), lambda l, j: (l, 0, j)),
        ],
        out_specs=pl.BlockSpec((1, R, D), lambda l, j: (l, 0, j)),
        out_shape=jax.ShapeDtypeStruct((L, R, nj * D), F32),
        compiler_params=_cparams(("parallel", "parallel")),
        name="modulation",
    )(cond, w_mod, b_mod.reshape(L, 1, nj * D))


def _rope(x, c, s, d):
    w = x.shape[-1]
    lane = lax.broadcasted_iota(jnp.int32, (1, LANES), 1)
    first = (lane % (2 * d)) < d
    outs = []
    for b in range(w // LANES):
        sl = slice(b * LANES, (b + 1) * LANES)
        xb = x[:, sl]
        partner = jnp.where(first, pltpu.roll(xb, LANES - d, 1), pltpu.roll(xb, d, 1))
        outs.append(xb * c[:, sl] + partner * s[:, sl])
    return outs[0] if len(outs) == 1 else jnp.concatenate(outs, axis=-1)


def _inproj_kernel(*refs, rope):
    if rope:
        (x_ref, mod_ref, g1_ref, w_ref, qn_ref, wuq_ref, kvn_ref,
         tac, tas, tcc, tcs, tqc, tqs, tkc, tks,
         oa, ob, oc, oq, ockv, okpe) = refs
    else:
        (x_ref, mod_ref, g1_ref, w_ref, qn_ref, wuq_ref, kvn_ref,
         oa, ob, oc, oq, ockv, okpe) = refs
    dt = oa.dtype
    x = x_ref[0]
    h = (_rms(x, g1_ref[...]) * (1.0 + mod_ref[0, 1:2, :]) + mod_ref[0, 0:1, :]).astype(BF)

    def seg(lo, w):
        return _dot(h, w_ref[:, lo:lo + w])

    ua = seg(0, 1024)
    qk = ua[:, :512]
    if rope:
        qk = _rope(qk, tac[...], tas[...], 8)
    oa[0, :, 0:256] = (qk[:, :256] * (DA_QK ** -0.5 * LOG2E)).astype(dt)
    oa[0, :, 256:512] = qk[:, 256:].astype(dt)
    ones_lane = lax.broadcasted_iota(jnp.int32, (1, 512), 1) % LANES >= 64
    oa[0, :, 512:1024] = jnp.where(ones_lane, 1.0, ua[:, 512:]).astype(dt)
    ub = seg(1024, 768)
    ob[0, :, 0:256] = (ub[:, :256] * (0.125 * LOG2E)).astype(dt)
    ob[0, :, 256:768] = ub[:, 256:].astype(dt)
    uc = seg(1792, 512)
    qk = uc[:, :384]
    if rope:
        qk = _rope(qk, tcc[...], tcs[...], 16)
    oc[0, :, 0:256] = (qk[:, :256] * (0.125 * LOG2E)).astype(dt)
    oc[0, :, 256:384] = qk[:, 256:].astype(dt)
    oc[0, :, 384:512] = uc[:, 384:].astype(dt)
    cq = _rms(seg(2304, 256), qn_ref[...]).astype(BF)
    qd = _dot(cq, wuq_ref[...])
    if rope:
        qd = _rope(qd, tqc[...], tqs[...], 8)
    oq[0] = (qd * ((MLA_NOPE + MLA_ROPE) ** -0.5 * LOG2E)).astype(dt)
    ockv[0] = _rms(seg(2560, 128), kvn_ref[...]).astype(dt)
    kpe = seg(2688, 128)
    if rope:
        kpe = _rope(kpe, tkc[...], tks[...], 8)
    okpe[0] = kpe.astype(dt)


def _in_proj(x, mod, g1, w_in_p, q_norm, w_uq_p, kv_norm, tables, out_dtype, tm):
    B, N, D = x.shape
    rope = tables is not None
    nmod = mod.shape[0]
    mod_map = (lambda j, b: (b, 0, 0)) if nmod > 1 else (lambda j, b: (0, 0, 0))
    const2 = lambda j, b: (0, 0)
    in_specs = [
        pl.BlockSpec((1, tm, D), lambda j, b: (b, j, 0)),
        pl.BlockSpec((1, 6, D), mod_map),
        pl.BlockSpec((1, D), const2),
        pl.BlockSpec((D, IN_PAD), const2),
        pl.BlockSpec((1, 256), const2),
        pl.BlockSpec((256, 512), const2),
        pl.BlockSpec((1, 128), const2),
    ]
    args = [x, mod, g1, w_in_p, q_norm, w_uq_p, kv_norm]
    if rope:
        for t in tables:
            in_specs.append(pl.BlockSpec((tm, t.shape[1]), lambda j, b: (j, 0)))
            args.append(t)
    widths = (1024, 768, 512, 512, 128, 128)
    out_specs = [pl.BlockSpec((1, tm, w), lambda j, b: (b, j, 0)) for w in widths]
    out_shape = [jax.ShapeDtypeStruct((B, N, w), out_dtype) for w in widths]
    return pl.pallas_call(
        functools.partial(_inproj_kernel, rope=rope),
        grid=(N // tm, B),
        in_specs=in_specs,
        out_specs=out_specs,
        out_shape=out_shape,
        compiler_params=_cparams(("parallel", "parallel"), VMEM_BIG),
        name="in_proj_rope" if rope else "in_proj",
    )(*args)


def _mla_expand_kernel(ckv_ref, kpe_ref, wk_ref, wv_ref, e_ref, ok, ov):
    ckv = ckv_ref[0].astype(BF)
    kpe = kpe_ref[0].astype(BF)
    ok[0] = (_dot(ckv, wk_ref[...]) + _dot(kpe, e_ref[...])).astype(ok.dtype)
    ones_lane = lax.broadcasted_iota(jnp.int32, (1, 512), 1) % LANES >= 64
    ov[0] = jnp.where(ones_lane, 1.0, _dot(ckv, wv_ref[...])).astype(ov.dtype)


def _mla_expand(ckv, kpe, w_k, w_v, e_place, tk):
    B, K, _ = ckv.shape
    const2 = lambda b, j: (0, 0)
    return pl.pallas_call(
        _mla_expand_kernel,
        grid=(B, K // tk),
        in_specs=[
            pl.BlockSpec((1, tk, 128), lambda b, j: (b, j, 0)),
            pl.BlockSpec((1, tk, 128), lambda b, j: (b, j, 0)),
            pl.BlockSpec((128, 512), const2),
            pl.BlockSpec((128, 512), const2),
            pl.BlockSpec((128, 512), const2),
        ],
        out_specs=[pl.BlockSpec((1, tk, 512), lambda b, j: (b, j, 0)),
                   pl.BlockSpec((1, tk, 512), lambda b, j: (b, j, 0))],
        out_shape=[jax.ShapeDtypeStruct((B, K, 512), BF), jax.ShapeDtypeStruct((B, K, 512), BF)],
        compiler_params=_cparams(("parallel", "parallel")),
        name="mla_expand",
    )(ckv, kpe, w_k, w_v, e_place)


def _softmax_init(tq, vw):
    return (jnp.full((tq, 1), NEG_INF, F32), jnp.zeros((tq, 1), F32), jnp.zeros((tq, vw), F32))


def _softmax_step(state, s, v):
    m, l, acc = state
    m_new = jnp.maximum(m, jnp.max(s, axis=-1, keepdims=True))
    alpha = jnp.exp2(m - m_new)
    p = jnp.exp2(s - m_new)
    l = alpha * l + jnp.sum(p, axis=-1, keepdims=True)
    acc = alpha * acc + _dot(p.astype(BF), v)
    return m_new, l, acc


def _dense_attn_kernel(*refs, groups, first, tk, n_chunks, has_ctx, has_sink, diff, aug):
    it = iter(refs)
    q_ref, k_ref, v_ref = next(it), next(it), next(it)
    ck_ref = cv_ref = sink_ref = lam_ref = subln_ref = None
    if has_ctx:
        ck_ref, cv_ref = next(it), next(it)
    if has_sink:
        sink_ref = next(it)
    if diff:
        lam_ref, subln_ref, lam_init_ref = next(it), next(it), next(it)
    o_ref = next(it)
    tq = q_ref.shape[1]

    lam = lam_init = None
    if diff:
        lam_init = lam_init_ref[0]
        dl = lam_ref[...]
        lam = (jnp.exp(jnp.sum(dl[0:1] * dl[1:2], axis=-1, keepdims=True))
               - jnp.exp(jnp.sum(dl[2:3] * dl[3:4], axis=-1, keepdims=True)) + lam_init)

    ng = len(groups)
    scratch = list(it)
    qg_s, m_s, l_s, acc_s = (scratch[j * ng:(j + 1) * ng] for j in range(4))
    for gi, (q_lo, k_lo, w, qmask, v_lo, vw, o_lo, omask, sink_idx, sign) in enumerate(groups):
        qg = q_ref[0, :, q_lo:q_lo + w].astype(BF)
        if qmask is not None:
            qg = jnp.where(_lane_mask(w, *qmask), qg, jnp.zeros_like(qg))
        qg_s[gi][...] = qg
        m_s[gi][...] = jnp.full((tq, 1), NEG_INF, F32)
        l_s[gi][...] = jnp.zeros((tq, 1), F32)
        acc_s[gi][...] = jnp.zeros((tq, vw), F32)

    def keys_step(kk, vv):
        for gi, (q_lo, k_lo, w, qmask, v_lo, vw, o_lo, omask, sink_idx, sign) in enumerate(groups):
            s = _dot_nt(qg_s[gi][...], kk[:, k_lo:k_lo + w])
            if aug:
                m = m_s[gi][...]
                m_new = jnp.maximum(m, jnp.max(s, axis=-1, keepdims=True))
                p = jnp.exp2(s - m_new).astype(BF)
                acc_s[gi][...] = jnp.exp2(m - m_new) * acc_s[gi][...] + _dot(p, vv[:, v_lo:v_lo + vw])
                m_s[gi][...] = m_new
                continue
            st = (m_s[gi][...], l_s[gi][...], acc_s[gi][...])
            m, l, acc = _softmax_step(st, s, vv[:, v_lo:v_lo + vw])
            m_s[gi][...] = m
            l_s[gi][...] = l
            acc_s[gi][...] = acc

    if has_ctx:
        keys_step(ck_ref[0].astype(BF), cv_ref[0].astype(BF))
    if first:
        keys_step(k_ref[0, 0:first, :].astype(BF), v_ref[0, 0:first, :].astype(BF))

    def body(c, carry):
        off = pl.multiple_of(first + c * tk, 128)
        keys_step(k_ref[0, pl.ds(off, tk), :].astype(BF), v_ref[0, pl.ds(off, tk), :].astype(BF))
        return carry

    if n_chunks == 1:
        for c in range(n_chunks):
            body(c, 0)
    else:
        lax.fori_loop(0, n_chunks, body, 0)

    if aug:
        half = _lane_mask(LANES, 0, 64)
        heads = {}
        for gi, (q_lo, k_lo, w, qmask, v_lo, vw, head, omask, sink_idx, sign) in enumerate(groups):
            acc = acc_s[gi][...]
            part = acc * pltpu.roll(1.0 / acc, 64, 1)
            if sign < 0:
                part = -lam * part
            heads[head] = part if head not in heads else heads[head] + part
        for hp in range(len(heads) // 2):
            pair = []
            for o in (heads[2 * hp], heads[2 * hp + 1]):
                if diff:
                    ms = jnp.sum(jnp.where(half, o * o, 0.0), axis=-1, keepdims=True) * (1.0 / 64.0)
                    o = o * lax.rsqrt(ms + NORM_EPS) * subln_ref[...] * (1.0 - lam_init)
                pair.append(o)
            slab = jnp.where(half, pair[0], pltpu.roll(pair[1], 64, 1))
            o_ref[0, :, hp * LANES:(hp + 1) * LANES] = slab.astype(o_ref.dtype)
        return

    out_parts = {}
    for gi, (q_lo, k_lo, w, qmask, v_lo, vw, o_lo, omask, sink_idx, sign) in enumerate(groups):
        m, l, acc = m_s[gi][...], l_s[gi][...], acc_s[gi][...]
        if sink_idx is not None:
            l = l + jnp.exp2(sink_ref[sink_idx] * LOG2E - m)
        part = jnp.where(_lane_mask(vw, *omask), acc * (1.0 / l), 0.0)
        key = (o_lo, vw)
        out_parts[key] = part if key not in out_parts else out_parts[key] + part
    for (o_lo, vw), part in out_parts.items():
        o_ref[0, :, o_lo:o_lo + vw] = part.astype(o_ref.dtype)


def _dense_attn(q_arr, q_blk, k_arr, k_blk, v_arr, v_blk, *, groups, out_w, tq, tk, first=0,
                ctx=None, sink=None, diff=None, aug=False, name):
    B, N, _ = q_arr.shape
    K = k_arr.shape[1]
    n_chunks = (K - first) // tk
    assert first + n_chunks * tk == K
    qw, qi = q_blk
    kw, ki = k_blk
    vw, vi = v_blk
    in_specs = [
        pl.BlockSpec((1, tq, qw), lambda b, i: (b, i, qi)),
        pl.BlockSpec((1, K, kw), lambda b, i: (b, 0, ki)),
        pl.BlockSpec((1, K, vw), lambda b, i: (b, 0, vi)),
    ]
    args = [q_arr, k_arr, v_arr]
    if ctx is not None:
        ck, cv = ctx
        in_specs += [pl.BlockSpec((1,) + ck.shape[1:], lambda b, i: (b, 0, 0)),
                     pl.BlockSpec((1,) + cv.shape[1:], lambda b, i: (b, 0, 0))]
        args += [ck, cv]
    if sink is not None:
        in_specs.append(pl.BlockSpec(memory_space=pltpu.SMEM))
        args.append(sink)
    if diff is not None:
        da_lambda, subln, lam_init = diff
        in_specs += [pl.BlockSpec((4, DA_QK), lambda b, i: (0, 0)),
                     pl.BlockSpec((1, LANES), lambda b, i: (0, 0)),
                     pl.BlockSpec(memory_space=pltpu.SMEM)]
        args += [da_lambda, subln, jnp.full((1,), lam_init, F32)]
    kern = functools.partial(
        _dense_attn_kernel, groups=groups, first=first, tk=tk, n_chunks=n_chunks,
        has_ctx=ctx is not None, has_sink=sink is not None, diff=diff is not None, aug=aug)
    scratch = ([pltpu.VMEM((tq, g[2]), BF) for g in groups] + [pltpu.VMEM((tq, 1), F32) for _ in groups] * 2
               + [pltpu.VMEM((tq, g[5]), F32) for g in groups])
    return pl.pallas_call(
        kern,
        grid=(B, N // tq),
        in_specs=in_specs,
        out_specs=pl.BlockSpec((1, tq, out_w), lambda b, i: (b, i, 0)),
        out_shape=jax.ShapeDtypeStruct((B, N, out_w), BF),
        scratch_shapes=scratch,
        compiler_params=_cparams(("parallel", "parallel"), VMEM_BIG),
        name=name,
    )(*args)


GROUPS_A = tuple((0, 0, 256, (32 * g, 32), 128 * (g // 2), 128, g // 2, None, None, 1 if g % 2 == 0 else -1)
                 for g in range(8))
GROUPS_B = tuple((0, 0, 256, (64 * h, 64), 0, 256, 0, (64 * h, 64), None, 1) for h in range(4))
GROUPS_C = tuple((128 * g, 0, 128, (64 * j, 64), 0, 128, 128 * g, (64 * j, 64), 2 * j + g, 1)
                 for g in range(2) for j in range(2))
GROUPS_D = tuple((128 * h, 128 * h, 128, None, 128 * h, 128, h, None, None, 1) for h in range(4))


def _window_attn_kernel(q_ref, k_ref, v_ref, ck_ref, cv_ref, sink_ref, o_ref, *, n, band):
    tq = q_ref.shape[1]
    i = pl.program_id(1)
    start = jnp.clip(i * tq - SW_WINDOW, 0, n - band)
    start = pl.multiple_of(start, 128)
    kb = k_ref[0, pl.ds(start, band), :]
    vb = v_ref[0, pl.ds(start, band), :]
    ck = ck_ref[0].astype(BF)
    cv = cv_ref[0].astype(BF)
    qpos = i * tq + lax.broadcasted_iota(jnp.int32, (tq, band), 0)
    kpos = start + lax.broadcasted_iota(jnp.int32, (tq, band), 1)
    valid = jnp.abs(qpos - kpos) <= SW_WINDOW
    for g in range(2):
        part = jnp.zeros((tq, 128), F32)
        for j in range(2):
            hm = _lane_mask(128, 64 * j, 64)
            qg = q_ref[0, :, 128 * g:128 * (g + 1)]
            qg = jnp.where(hm, qg, jnp.zeros_like(qg))
            st = _softmax_init(tq, 128)
            st = _softmax_step(st, _dot_nt(qg, ck), cv)
            st = _softmax_step(st, jnp.where(valid, _dot_nt(qg, kb), NEG_INF), vb)
            m, l, acc = st
            l = l + jnp.exp2(sink_ref[2 * j + g] * LOG2E - m)
            part = jnp.where(hm, acc * (1.0 / l), part)
        o_ref[0, :, 128 * g:128 * (g + 1)] = part.astype(o_ref.dtype)


def _window_attn(qkv_c, ck, cv, sink, tq):
    B, N, _ = qkv_c.shape
    band = tq + 2 * SW_WINDOW
    P = ck.shape[1]
    return pl.pallas_call(
        functools.partial(_window_attn_kernel, n=N, band=band),
        grid=(B, N // tq),
        in_specs=[
            pl.BlockSpec((1, tq, 256), lambda b, i: (b, i, 0)),
            pl.BlockSpec((1, N, 128), lambda b, i: (b, 0, 2)),
            pl.BlockSpec((1, N, 128), lambda b, i: (b, 0, 3)),
            pl.BlockSpec((1, P, 128), lambda b, i: (b, 0, 0)),
            pl.BlockSpec((1, P, 128), lambda b, i: (b, 0, 0)),
            pl.BlockSpec(memory_space=pltpu.SMEM),
        ],
        out_specs=pl.BlockSpec((1, tq, 256), lambda b, i: (b, i, 0)),
        out_shape=jax.ShapeDtypeStruct((B, N, 256), BF),
        compiler_params=_cparams(("parallel", "parallel")),
        name="window_attn",
    )(qkv_c, qkv_c, qkv_c, ck, cv, sink)


NA_TILE_ROWS = 4
NA_KEY_ROWS = NA_TILE_ROWS + NA_WIN_R


def _nbr_attn_kernel(q_ref, k_ref, v_ref, ck_ref, cv_ref, bias_ref, o_ref, *, rows):
    tq = q_ref.shape[1]
    i = pl.program_id(1)
    krow = jnp.clip(i * NA_TILE_ROWS - NA_WIN_R // 2, 0, rows - NA_KEY_ROWS)
    start = pl.multiple_of(krow * GRID_W, GRID_W)
    nk = NA_KEY_ROWS * GRID_W
    kb = k_ref[0, pl.ds(start, nk), :]
    vb = v_ref[0, pl.ds(start, nk), :]
    ck = ck_ref[0].astype(BF)
    cv = cv_ref[0].astype(BF)
    q = q_ref[0]
    o = jnp.zeros((tq, 256), F32)
    for h in range(4):
        hm = _lane_mask(256, 64 * h, 64)
        qg = jnp.where(hm, q, jnp.zeros_like(q))
        st = _softmax_init(tq, 256)
        st = _softmax_step(st, _dot_nt(qg, ck), cv)
        st = _softmax_step(st, _dot_nt(qg, kb) + bias_ref[0, h], vb)
        m, l, acc = st
        o = jnp.where(hm, acc * (1.0 / l), o)
    o_ref[0] = o.astype(o_ref.dtype)


def _nbr_attn(qkv_b, ck, cv, bias):
    B, N, _ = qkv_b.shape
    rows = N // GRID_W
    tq = NA_TILE_ROWS * GRID_W
    nt = N // tq
    nk = NA_KEY_ROWS * GRID_W
    P = ck.shape[1]

    def bias_map(b, i):
        return (jnp.where(i == 0, 0, jnp.where(i == nt - 1, 2, 1)), 0, 0, 0)

    return pl.pallas_call(
        functools.partial(_nbr_attn_kernel, rows=rows),
        grid=(B, nt),
        in_specs=[
            pl.BlockSpec((1, tq, 256), lambda b, i: (b, i, 0)),
            pl.BlockSpec((1, N, 256), lambda b, i: (b, 0, 1)),
            pl.BlockSpec((1, N, 256), lambda b, i: (b, 0, 2)),
            pl.BlockSpec((1, P, 256), lambda b, i: (b, 0, 0)),
            pl.BlockSpec((1, P, 256), lambda b, i: (b, 0, 0)),
            pl.BlockSpec((1, 4, tq, nk), bias_map),
        ],
        out_specs=pl.BlockSpec((1, tq, 256), lambda b, i: (b, i, 0)),
        out_shape=jax.ShapeDtypeStruct((B, N, 256), BF),
        compiler_params=_cparams(("parallel", "parallel"), VMEM_MID),
        name="nbr_attn",
    )(qkv_b, qkv_b, qkv_b, ck, cv, bias)


def _nbr_bias_tables(rpb, rows):
    H = rpb.shape[0]
    rp = jnp.pad(rpb.astype(F32), ((0, 0), (NA_KEY_ROWS, NA_KEY_ROWS), (GRID_W, GRID_W)))
    c0 = NA_WIN_C - 1 + GRID_W
    tcol = jnp.stack([rp[:, :, c0 - qc:c0 - qc + GRID_W] for qc in range(GRID_W)], axis=2)
    tabs = []
    nt = rows // NA_TILE_ROWS
    for r0 in (0, NA_TILE_ROWS * min(1, nt - 1) + NA_TILE_ROWS, rows - NA_TILE_ROWS):
        ks = int(np.clip(r0 - NA_WIN_R // 2, 0, rows - NA_KEY_ROWS))
        per_row = []
        for a in range(NA_TILE_ROWS):
            s0 = ks - (r0 + a) + NA_WIN_R - 1 + NA_KEY_ROWS
            per_row.append(tcol[:, s0:s0 + NA_KEY_ROWS])
        t = jnp.stack(per_row, axis=1).transpose(0, 1, 3, 2, 4)
        r = r0 + np.arange(NA_TILE_ROWS)[:, None, None, None]
        qc = np.arange(GRID_W)[None, :, None, None]
        kr = ks + np.arange(NA_KEY_ROWS)[None, None, :, None]
        kc = np.arange(GRID_W)[None, None, None, :]
        rs = np.clip(r - NA_WIN_R // 2, 0, rows - NA_WIN_R)
        cs = np.clip(qc - NA_WIN_C // 2, 0, GRID_W - NA_WIN_C)
        ok = (kr >= rs) & (kr < rs + NA_WIN_R) & (kc >= cs) & (kc < cs + NA_WIN_C)
        t = jnp.where(ok[None], t * LOG2E, NEG_INF)
        tabs.append(t.reshape(H, NA_TILE_ROWS * GRID_W, NA_KEY_ROWS * GRID_W))
    return jnp.stack(tabs)


def _merge_kernel(x_ref, mod_ref, g1_ref, g2_ref, oa, ob, oc, od, wg_ref, wb_ref, wo_ref, wr_ref,
                  xo_ref, h2_ref, lg_ref):
    D = x_ref.shape[2]
    x = x_ref[0]
    h = (_rms(x, g1_ref[...]) * (1.0 + mod_ref[0, 1:2, :]) + mod_ref[0, 0:1, :]).astype(BF)
    merged = None
    for i, o_ref in enumerate((oa, ob, oc, od)):
        z = _dot(h, wg_ref[:, i * D:(i + 1) * D])
        gate = 1.0 / (1.0 + jnp.exp(-z))
        term = gate * _dot(o_ref[0], wb_ref[i])
        merged = term if merged is None else merged + term
    mixed = _dot(merged.astype(BF), wo_ref[...])
    xn = x + mod_ref[0, 2:3, :] * mixed
    xo_ref[0] = xn
    h2 = (_rms(xn, g2_ref[...]) * (1.0 + mod_ref[0, 4:5, :]) + mod_ref[0, 3:4, :]).astype(BF)
    h2_ref[0] = h2
    lg_ref[0] = _dot_nt(wr_ref[...], h2)


def _merge(x, mod, g1, g2, o_a, o_b, o_c, o_d, w_gate, w_branch, w_out, w_router_p, tm):
    B, N, D = x.shape
    nmod = mod.shape[0]
    mod_map = (lambda b, j: (b, 0, 0)) if nmod > 1 else (lambda b, j: (0, 0, 0))
    tok = lambda w: pl.BlockSpec((1, tm, w), lambda b, j: (b, j, 0))
    c2 = lambda b, j: (0, 0)
    return pl.pallas_call(
        _merge_kernel,
        grid=(B, N // tm),
        in_specs=[
            tok(D), pl.BlockSpec((1, 6, D), mod_map), pl.BlockSpec((1, D), c2), pl.BlockSpec((1, D), c2),
            tok(256), tok(256), tok(256), tok(256),
            pl.BlockSpec((D, 4 * D), c2),
            pl.BlockSpec((4, 256, D), lambda b, j: (0, 0, 0)),
            pl.BlockSpec((D, D), c2),
            pl.BlockSpec((N_EXPERTS, D), c2),
        ],
        out_specs=[tok(D), tok(D), pl.BlockSpec((1, N_EXPERTS, tm), lambda b, j: (b, 0, j))],
        out_shape=[jax.ShapeDtypeStruct((B, N, D), F32), jax.ShapeDtypeStruct((B, N, D), BF),
                   jax.ShapeDtypeStruct((B, N_EXPERTS, N), F32)],
        compiler_params=_cparams(("parallel", "parallel"), VMEM_BIG),
        name="merge",
    )(x, mod, g1, g2, o_a, o_b, o_c, o_d, w_gate, w_branch, w_out, w_router_p)


MOE_TT = 256
MOE_RC = 64
SLOT_ALIGN = 16


def _excl_cumsum(x, tri, reset_every):
    R, N = x.shape
    outs = []
    carry = jnp.zeros((R, 1), F32)
    for c in range(N // LANES):
        if reset_every and (c * LANES) % reset_every == 0:
            carry = jnp.zeros((R, 1), F32)
        xc = x[:, c * LANES:(c + 1) * LANES]
        inc = _dot(xc.astype(BF), tri)
        outs.append(inc - xc + carry)
        carry = carry + inc[:, LANES - 1:LANES]
    return (outs[0] if len(outs) == 1 else jnp.concatenate(outs, axis=1)), carry


def _route_kernel(lg_ref, aff_ref, rel_ref, base_ref, *, cap, tt):
    B, E, N = lg_ref.shape
    R = B * E
    nt = N // tt
    lg = lg_ref[...]
    ex = jnp.exp(lg - jnp.max(lg, axis=1, keepdims=True))
    aff = ex / jnp.sum(ex, axis=1, keepdims=True)
    aff_ref[...] = aff
    bits = pltpu.bitcast(aff.reshape(R, N), jnp.int32)

    def per_expert(v):
        t = v[0:E]
        for b in range(1, B):
            t = t + v[b * E:(b + 1) * E]
        return t

    def per_row(v):
        return jnp.concatenate([v] * B, axis=0) if B > 1 else v

    def batch_prefix(v):
        parts, run = [], jnp.zeros((E, 1), F32)
        for b in range(B):
            parts.append(run)
            run = run + v[b * E:(b + 1) * E]
        return jnp.concatenate(parts, axis=0) if B > 1 else parts[0]

    def count(mask):
        return per_expert(jnp.sum(jnp.where(mask, 1.0, 0.0), axis=1, keepdims=True))

    def search(i, thr):
        cand = thr | jnp.left_shift(jnp.int32(1), 30 - i)
        return jnp.where(count(bits >= per_row(cand)) >= float(cap), cand, thr)

    thr = per_row(lax.fori_loop(0, 31, search, jnp.zeros((E, 1), jnp.int32)))
    gt = bits > thr
    eq = bits == thr
    need = float(cap) - count(gt)
    ri = lax.broadcasted_iota(jnp.int32, (LANES, LANES), 0)
    ci = lax.broadcasted_iota(jnp.int32, (LANES, LANES), 1)
    tri = jnp.where(ri <= ci, 1.0, 0.0).astype(BF)
    rank_eq, eq_tot = _excl_cumsum(jnp.where(eq, 1.0, 0.0), tri, None)
    rank_eq = rank_eq + batch_prefix(eq_tot)
    sel = gt | (eq & (rank_eq < per_row(need)))
    self = jnp.where(sel, 1.0, 0.0)
    local, _ = _excl_cumsum(self, tri, tt)
    rel_ref[...] = jnp.where(sel, local, -1.0).astype(jnp.int32).reshape(B, E, N)
    tn = jnp.right_shift(lax.broadcasted_iota(jnp.int32, (N, LANES), 0), tt.bit_length() - 1)
    tj = lax.broadcasted_iota(jnp.int32, (N, LANES), 1)
    cnt = _dot(self.astype(BF), jnp.where(tn == tj, 1.0, 0.0).astype(BF)).astype(jnp.int32)
    sh = SLOT_ALIGN.bit_length() - 1
    ru = jnp.left_shift(jnp.right_shift(cnt + (SLOT_ALIGN - 1), sh), sh).astype(F32)
    base_b = _dot(ru.astype(BF), jnp.where(ri < ci, 1.0, 0.0).astype(BF))
    base = base_b + batch_prefix(base_b[:, nt:nt + 1])
    base_ref[...] = base.astype(jnp.int32).reshape(B, E, LANES)


def _route(lg_t, cap):
    B, E, N = lg_t.shape
    full = lambda s: pl.BlockSpec(s, lambda i: (0,) * len(s))
    return pl.pallas_call(
        functools.partial(_route_kernel, cap=cap, tt=MOE_TT),
        grid=(1,),
        in_specs=[full((B, E, N))],
        out_specs=[full((B, E, N)), full((B, E, N)), full((B, E, LANES))],
        out_shape=[jax.ShapeDtypeStruct((B, E, N), F32), jax.ShapeDtypeStruct((B, E, N), jnp.int32),
                   jax.ShapeDtypeStruct((B, E, LANES), jnp.int32)],
        compiler_params=_cparams(("arbitrary",), VMEM_BIG),
        name="route",
    )(lg_t)


def _slot_onehot(rel, values, c, rc):
    E, tt = rel.shape
    r = lax.broadcasted_iota(jnp.int32, (E, rc, tt), 1) + c * rc
    return jnp.where(rel[:, None, :] == r, values[:, None, :], 0.0).astype(BF).reshape(E * rc, tt)


def _dispatch_kernel(base_sm, x_ref, rel_ref, xe_in, xe_hbm, stage, stage2, sem, sem2, *, nt, rc):
    del xe_in
    b, j = pl.program_id(0), pl.program_id(1)
    E = rel_ref.shape[1]
    step = b * nt + j
    last = pl.num_programs(0) * nt - 1
    base = lambda e, jj: base_sm[(b * E + e) * (nt + 1) + jj]
    rel = rel_ref[0]
    x = x_ref[0]
    ones = jnp.ones(rel.shape, F32)

    def copy(buf, s, e, c):
        start = pl.multiple_of(base(e, j) + c * rc, SLOT_ALIGN)
        return pltpu.make_async_copy(buf.at[pl.ds(e * rc, rc)], xe_hbm.at[e, pl.ds(start, rc)], s.at[e])

    strip = _dot(_slot_onehot(rel, ones, 0, rc), x).astype(BF)

    @pl.when(step > 0)
    def _():
        for e in range(E):
            copy(stage, sem, e, 0).wait()

    stage[...] = strip
    for e in range(E):
        copy(stage, sem, e, 0).start()

    span = [base(e, j + 1) - base(e, j) for e in range(E)]
    widest = functools.reduce(jnp.maximum, span)

    def extra(c, carry):
        stage2[...] = _dot(_slot_onehot(rel, ones, c, rc), x).astype(BF)
        for e in range(E):
            @pl.when(span[e] > c * rc)
            def _():
                cp = copy(stage2, sem2, e, c)
                cp.start()
                cp.wait()
        return carry

    lax.fori_loop(1, (widest + rc - 1) // rc, extra, 0)

    @pl.when(step == last)
    def _():
        for e in range(E):
            copy(stage, sem, e, 0).wait()


def _dispatch(h2, rel, base_flat, rows):
    B, N, D = h2.shape
    E = rel.shape[1]
    nt = N // MOE_TT
    xe0 = jnp.zeros((E, rows, D), BF)
    return pl.pallas_call(
        functools.partial(_dispatch_kernel, nt=nt, rc=MOE_RC),
        grid_spec=pltpu.PrefetchScalarGridSpec(
            num_scalar_prefetch=1,
            grid=(B, nt),
            in_specs=[
                pl.BlockSpec((1, MOE_TT, D), lambda b, j, s: (b, j, 0)),
                pl.BlockSpec((1, E, MOE_TT), lambda b, j, s: (b, 0, j)),
                pl.BlockSpec(memory_space=pl.ANY),
            ],
            out_specs=pl.BlockSpec(memory_space=pl.ANY),
            scratch_shapes=[pltpu.VMEM((E * MOE_RC, D), BF), pltpu.VMEM((E * MOE_RC, D), BF),
                            pltpu.SemaphoreType.DMA((E,)), pltpu.SemaphoreType.DMA((E,))],
        ),
        out_shape=jax.ShapeDtypeStruct((E, rows, D), BF),
        input_output_aliases={3: 0},
        compiler_params=_cparams(("arbitrary", "arbitrary")),
        name="dispatch",
    )(base_flat, h2, rel, xe0)


def _ffn_kernel(tot_sm, x_ref, wg_ref, wu_ref, wd_ref, y_ref, wg_s, wu_s, wd_s, *, tm, rc):
    e, j = pl.program_id(0), pl.program_id(1)

    @pl.when(j == 0)
    def _():
        wg_s[...] = wg_ref[0].astype(BF)
        wu_s[...] = wu_ref[0].astype(BF)
        wd_s[...] = wd_ref[0].astype(BF)

    used = j * tm < tot_sm[e] + rc

    @pl.when(used)
    def _():
        x = x_ref[0]
        hg = _dot(x, wg_s[...])
        hu = _dot(x, wu_s[...])
        act = (hg * (1.0 / (1.0 + jnp.exp(-hg))) * hu).astype(BF)
        y_ref[0] = _dot(act, wd_s[...]).astype(y_ref.dtype)

    @pl.when(jnp.logical_not(used))
    def _():
        y_ref[...] = jnp.zeros_like(y_ref)


def _expert_ffn(xe, total, w_g, w_u, w_d, tm):
    E, rows, D = xe.shape
    DE = w_g.shape[2]
    tok = lambda e, j, tot: (e, jnp.minimum(j, (tot[e] + MOE_RC - 1) // tm), 0)
    wmap = lambda e, j, tot: (e, 0, 0)
    return pl.pallas_call(
        functools.partial(_ffn_kernel, tm=tm, rc=MOE_RC),
        grid_spec=pltpu.PrefetchScalarGridSpec(
            num_scalar_prefetch=1,
            grid=(E, rows // tm),
            in_specs=[pl.BlockSpec((1, tm, D), tok), pl.BlockSpec((1, D, DE), wmap),
                      pl.BlockSpec((1, D, DE), wmap), pl.BlockSpec((1, DE, D), wmap)],
            out_specs=pl.BlockSpec((1, tm, D), lambda e, j, tot: (e, j, 0)),
            scratch_shapes=[pltpu.VMEM((D, DE), BF), pltpu.VMEM((D, DE), BF), pltpu.VMEM((DE, D), BF)],
        ),
        out_shape=jax.ShapeDtypeStruct((E, rows, D), BF),
        compiler_params=_cparams(("arbitrary", "arbitrary"), VMEM_BIG),
        name="expert_ffn",
    )(total, xe, w_g, w_u, w_d)


def _combine_kernel(base_sm, x_ref, rel_ref, aff_ref, mod_ref, gf_ref, ye_hbm, o_ref, stage, sem, *,
                    nt, rc, final):
    b, j = pl.program_id(0), pl.program_id(1)
    E = rel_ref.shape[1]
    base = lambda e, jj: base_sm[(b * E + e) * (nt + 1) + jj]
    rel = rel_ref[0]
    aff = aff_ref[0]

    def copy(e, c):
        start = pl.multiple_of(base(e, j) + c * rc, SLOT_ALIGN)
        return pltpu.make_async_copy(ye_hbm.at[e, pl.ds(start, rc)], stage.at[pl.ds(e * rc, rc)], sem.at[e])

    def scatter(c):
        w = _slot_onehot(rel, aff, c, rc)
        return lax.dot_general(w, stage[...], (((0,), (0,)), ((), ())), preferred_element_type=F32)

    for e in range(E):
        copy(e, 0).start()
    for e in range(E):
        copy(e, 0).wait()
    acc = scatter(0)
    span = [base(e, j + 1) - base(e, j) for e in range(E)]
    widest = functools.reduce(jnp.maximum, span)

    def extra(c, acc):
        for e in range(E):
            @pl.when(span[e] > c * rc)
            def _():
                cp = copy(e, c)
                cp.start()
                cp.wait()
        return acc + scatter(c)

    acc = lax.fori_loop(1, (widest + rc - 1) // rc, extra, acc)
    xn = x_ref[0] + mod_ref[0, 5:6, :] * acc
    if final:
        xn = _rms(xn, gf_ref[...])
    o_ref[0] = xn


def _combine(x, rel, aff, mod, g_final, ye, base_flat, final):
    B, N, D = x.shape
    E = rel.shape[1]
    nt = N // MOE_TT
    nmod = mod.shape[0]
    mod_map = (lambda b, j, s: (b, 0, 0)) if nmod > 1 else (lambda b, j, s: (0, 0, 0))
    tok = pl.BlockSpec((1, MOE_TT, D), lambda b, j, s: (b, j, 0))
    slots = pl.BlockSpec((1, E, MOE_TT), lambda b, j, s: (b, 0, j))
    return pl.pallas_call(
        functools.partial(_combine_kernel, nt=nt, rc=MOE_RC, final=final),
        grid_spec=pltpu.PrefetchScalarGridSpec(
            num_scalar_prefetch=1,
            grid=(B, nt),
            in_specs=[tok, slots, slots, pl.BlockSpec((1, 6, D), mod_map),
                      pl.BlockSpec((1, D), lambda b, j, s: (0, 0)), pl.BlockSpec(memory_space=pl.ANY)],
            out_specs=tok,
            scratch_shapes=[pltpu.VMEM((E * MOE_RC, D), BF), pltpu.SemaphoreType.DMA((E,))],
        ),
        out_shape=jax.ShapeDtypeStruct((B, N, D), F32),
        compiler_params=_cparams(("arbitrary", "arbitrary")),
        name="combine_final" if final else "combine",
    )(base_flat, x, rel, aff, mod, g_final, ye)


def _moe(x, h2, lg_t, mod, g_final, w_g, w_u, w_d, final):
    B, N, D = h2.shape
    T = B * N
    nt = N // MOE_TT
    cap = EC_CAPACITY * T // N_EXPERTS
    tm = 512 if cap >= 2048 else 256
    worst = cap + (SLOT_ALIGN - 1) * (T // MOE_TT) + MOE_RC
    rows = -(-worst // tm) * tm
    aff, rel, base = _route(lg_t, cap)
    base_flat = base[:, :, :nt + 1].reshape(-1)
    total = base[B - 1, :, nt]
    xe = _dispatch(h2, rel, base_flat, rows)
    ye = _expert_ffn(xe, total, w_g, w_u, w_d, tm)
    return _combine(x, rel, aff, mod, g_final, ye, base_flat, final)


def _rope_tables(n):
    t = jnp.arange(n)
    pos = {1: (t // GRID_W).astype(F32), 2: (t % GRID_W).astype(F32)}

    def group(quarter):
        freqs = jnp.float32(ROPE_BASE) ** (-jnp.arange(quarter, dtype=F32) / quarter)
        cs, ss = [], []
        for kind in (1, 2):
            ang = pos[kind][:, None] * freqs
            c, s = jnp.cos(ang), jnp.sin(ang)
            cs += [c, c]
            ss += [-s, s]
        return jnp.concatenate(cs, axis=1), jnp.concatenate(ss, axis=1)

    c32, s32 = group(8)
    c64, s64 = group(16)
    one = lambda w: jnp.ones((n, w), F32)
    zero = lambda w: jnp.zeros((n, w), F32)
    ta = (jnp.tile(c32, (1, 16)), jnp.tile(s32, (1, 16)))
    tc = (jnp.tile(c64, (1, 6)), jnp.tile(s64, (1, 6)))
    qc = jnp.concatenate([one(64), c32, one(32)], axis=1)
    qs = jnp.concatenate([zero(64), s32, zero(32)], axis=1)
    tq = (jnp.tile(qc, (1, 4)), jnp.tile(qs, (1, 4)))
    tk = (jnp.concatenate([c32, one(96)], axis=1), jnp.concatenate([s32, zero(96)], axis=1))
    return (ta[0], ta[1], tc[0], tc[1], tq[0], tq[1], tk[0], tk[1])


LAT_TQ = 512
C_HEAD_ORDER = (0, 2, 1, 3)


def _prep_layer(l, w_in, mla_w_uq, mla_w_ukv, w_gate, w_branch, w_out, w_router):
    D = D_MODEL
    wi = w_in[l]
    cq = wi[:, 1536:1792].reshape(D, 4, 64)[:, C_HEAD_ORDER, :].reshape(D, 256)
    av = jnp.pad(wi[:, 512:768].reshape(D, 4, 64), ((0, 0), (0, 0), (0, 64))).reshape(D, 512)
    w_in_p = jnp.concatenate(
        [wi[:, :512], av, wi[:, 768:1536], cq, wi[:, 1792:], jnp.zeros((D, LANES - MLA_ROPE), F32)],
        axis=1).astype(BF)
    uq = mla_w_uq[l].reshape(256, 4, MLA_NOPE + MLA_ROPE)
    w_uq_p = jnp.concatenate([uq, jnp.zeros((256, 4, 32), F32)], axis=2).reshape(256, 512).astype(BF)
    ukv = mla_w_ukv[l].reshape(128, 4, 128)
    w_k = jnp.concatenate([ukv[:, :, :64], jnp.zeros((128, 4, 64), F32)], axis=2).reshape(128, 512).astype(BF)
    w_v = jnp.concatenate([ukv[:, :, 64:], jnp.zeros((128, 4, 64), F32)], axis=2).reshape(128, 512).astype(BF)
    wb = w_branch[l]
    wb_c = wb[2].reshape(4, 64, D)[C_HEAD_ORDER, :, :].reshape(256, D)
    w_branch_p = jnp.stack([wb[0], wb[1], wb_c, wb[3]]).astype(BF)
    w_router_p = w_router[l].T.astype(BF)
    return dict(w_in=w_in_p, w_uq=w_uq_p, w_k=w_k, w_v=w_v, w_gate=w_gate[l].astype(BF),
                w_branch=w_branch_p, w_out=w_out[l].astype(BF), w_router=w_router_p)


def _kpe_placement():
    e = np.zeros((128, 512), np.float32)
    for h in range(4):
        for i in range(MLA_ROPE):
            e[i, 128 * h + MLA_NOPE + i] = 1.0
    return jnp.asarray(e, BF)


def _layer(x, mod, lw, lp, lam_init, ctx, tables, final, g_final):
    B, N, D = x.shape
    latent = ctx is not None
    tm = 512 if latent else 256
    out_dtype = BF if latent else F32
    qkv_a, qkv_b, qkv_c, q_d, ckv_n, kpe_r = _in_proj(
        x, mod, lp["g_norm1"], lw["w_in"], lp["mla_q_norm"], lw["w_uq"], lp["mla_kv_norm"],
        tables if latent else None, out_dtype, tm)
    diff = (lp["da_lambda"], jnp.tile(lp["da_subln"], (1, 2)), lam_init)
    if latent:
        ck_a, cv_a, ck_b, cv_b, ck_c, cv_c, ckv_c, kpe_c = ctx
        P = ck_a.shape[1]
        ones = jnp.ones(cv_a.shape[:2] + (4, 64), BF)
        cv_aug = jnp.concatenate([cv_a.reshape(B, P, 4, 64).astype(BF), ones], axis=-1).reshape(B, P, 512)
        k_a = jnp.concatenate([ck_a.astype(BF), qkv_a[:, :, 256:512]], axis=1)
        v_a = jnp.concatenate([cv_aug, qkv_a[:, :, 512:1024]], axis=1)
        o_a = _dense_attn(qkv_a, (256, 0), k_a, (256, 0), v_a, (512, 0), groups=GROUPS_A, out_w=256,
                          tq=LAT_TQ, tk=(N + P) // 2, diff=diff, aug=True, name="diff_attn")
        o_b = _nbr_attn(qkv_b, ck_b, cv_b, lp["na_bias"])
        o_c = _window_attn(qkv_c, ck_c, cv_c, lp["sw_sink"], 256)
        ckv_all = jnp.concatenate([ckv_c.astype(BF), ckv_n], axis=1)
        kpe_all = jnp.concatenate([jnp.pad(kpe_c, ((0, 0), (0, 0), (0, 128 - MLA_ROPE))).astype(BF), kpe_r], axis=1)
        k_d, v_d = _mla_expand(ckv_all, kpe_all, lw["w_k"], lw["w_v"], lp["e_place"], 256)
        o_d = _dense_attn(q_d, (512, 0), k_d, (512, 0), v_d, (512, 0), groups=GROUPS_D, out_w=256,
                          tq=LAT_TQ, tk=(N + P) // 2, aug=True, name="mla_attn")
    else:
        o_a = _dense_attn(qkv_a, (256, 0), qkv_a, (256, 1), qkv_a, (512, 1), groups=GROUPS_A, out_w=256,
                          tq=N, tk=N, diff=diff, aug=True, name="diff_attn_ctx")
        o_b = _dense_attn(qkv_b, (256, 0), qkv_b, (256, 1), qkv_b, (256, 2), groups=GROUPS_B, out_w=256,
                          tq=N, tk=N, name="dense_attn_ctx")
        o_c = _dense_attn(qkv_c, (256, 0), qkv_c, (128, 2), qkv_c, (128, 3), groups=GROUPS_C, out_w=256,
                          tq=N, tk=N, sink=lp["sw_sink"], name="gqa_attn_ctx")
        k_d, v_d = _mla_expand(ckv_n, kpe_r, lw["w_k"], lw["w_v"], lp["e_place"], N)
        o_d = _dense_attn(q_d, (512, 0), k_d, (512, 0), v_d, (512, 0), groups=GROUPS_D, out_w=256,
                          tq=N, tk=N, aug=True, name="mla_attn_ctx")
    x_new, h2, logits = _merge(x, mod, lp["g_norm1"], lp["g_norm2"], o_a, o_b, o_c, o_d,
                               lw["w_gate"], lw["w_branch"], lw["w_out"], lw["w_router"], tm)
    x_out = _moe(x_new, h2, logits, mod, g_final, lp["w_e_gate"], lp["w_e_up"], lp["w_e_down"], final)
    cache = None
    if not latent:
        cache = (qkv_a[:, :, 256:512].reshape(B, N, 4, 64),
                 qkv_a[:, :, 512:1024].reshape(B, N, 4, 128)[..., :64],
                 qkv_b[:, :, 256:512].reshape(B, N, 4, 64), qkv_b[:, :, 512:768].reshape(B, N, 4, 64),
                 qkv_c[:, :, 256:384].reshape(B, N, 2, 64), qkv_c[:, :, 384:512].reshape(B, N, 2, 64),
                 ckv_n, kpe_r[:, :, :MLA_ROPE])
    return x_out, cache


def kernel(x_prompt, x_sample, cache_diff_k, cache_diff_v, cache_na_k, cache_na_v, cache_swa_k, cache_swa_v, cache_mla_ckv, cache_mla_kpe, c, c_ctx, w_mod, b_mod, g_norm1, g_norm2, w_in, da_lambda, da_subln, na_rpb, sw_sink, mla_q_norm, mla_w_uq, mla_kv_norm, mla_w_ukv, w_gate, w_branch, w_out, w_router, w_e_gate, w_e_up, w_e_down, g_final):
    D = D_MODEL
    BS, NS, _ = x_sample.shape
    P = cache_diff_k.shape[2]
    cond = jnp.concatenate([c_ctx[None], c, jnp.zeros((16 - 1 - BS, D), F32)], axis=0)
    mod_all = _modulation(cond, w_mod, b_mod).reshape(DEPTH, 16, 6, D)
    tables = _rope_tables(NS)
    e_place = _kpe_placement()
    gf = g_final.reshape(1, D)
    xp, xs = x_prompt, x_sample
    caches = []
    for l in range(DEPTH):
        lw = _prep_layer(l, w_in, mla_w_uq, mla_w_ukv, w_gate, w_branch, w_out, w_router)
        lp = dict(
            g_norm1=g_norm1[l].reshape(1, D), g_norm2=g_norm2[l].reshape(1, D),
            mla_q_norm=mla_q_norm[l].reshape(1, 256), mla_kv_norm=mla_kv_norm[l].reshape(1, 128),
            da_lambda=da_lambda[l], da_subln=da_subln[l].reshape(1, 64), sw_sink=sw_sink[l],
            w_e_gate=w_e_gate[l], w_e_up=w_e_up[l], w_e_down=w_e_down[l], e_place=e_place,
            na_bias=_nbr_bias_tables(na_rpb[l], NS // GRID_W))
        lam_init = 0.8 - 0.6 * math.exp(-0.3 * l)
        final = l == DEPTH - 1
        xp, cache_l = _layer(xp, mod_all[l, 0:1], lw, lp, lam_init, None, None, final, gf)
        caches.append(cache_l)
        ctx = (cache_diff_k[:, l].reshape(BS, P, 256), cache_diff_v[:, l].reshape(BS, P, 256),
               cache_na_k[:, l].reshape(BS, P, 256), cache_na_v[:, l].reshape(BS, P, 256),
               cache_swa_k[:, l].reshape(BS, P, 128), cache_swa_v[:, l].reshape(BS, P, 128),
               cache_mla_ckv[:, l], cache_mla_kpe[:, l])
        xs, _ = _layer(xs, mod_all[l, 1:1 + BS], lw, lp, lam_init, ctx, tables, final, gf)
    outs = tuple(jnp.stack([cl[i] for cl in caches], axis=1) for i in range(8))
    return (xp, xs) + outs
```

```python
import functools
import math

import numpy as np
import jax
import jax.numpy as jnp
from jax import lax
from jax.experimental import pallas as pl
from jax.experimental.pallas import tpu as pltpu

BF = jnp.bfloat16
F32 = jnp.float32

D_MODEL = 1024
DEPTH = 4
GRID_W = 64
ROPE_BASE = 10000.0
NORM_EPS = 1e-6
NEG_INF = -1e30
LOG2E = math.log2(math.e)

DA_QK = 32
NA_WIN_R = 8
NA_WIN_C = 16
SW_WINDOW = 128
MLA_NOPE = 64
MLA_ROPE = 32
N_EXPERTS = 16
EC_CAPACITY = 2

IN_DIM = 2464
IN_PAD = 2816
LANES = 128

VMEM_BIG = 56 * 1024 * 1024
VMEM_MID = 40 * 1024 * 1024


def _cparams(sem, vmem=None):
    return pltpu.CompilerParams(dimension_semantics=sem, vmem_limit_bytes=vmem)


def _dot(a, b):
    return jnp.dot(a, b, preferred_element_type=F32)


def _dot_nt(a, b):
    return lax.dot_general(a, b, (((1,), (1,)), ((), ())), preferred_element_type=F32)


def _rms(x, g):
    var = jnp.mean(x * x, axis=-1, keepdims=True)
    return x * lax.rsqrt(var + NORM_EPS) * g


def _lane_mask(width, lo, length):
    lane = lax.broadcasted_iota(jnp.int32, (1, width), 1)
    return (lane >= lo) & (lane < lo + length)


def _mod_kernel(c_ref, w_ref, b_ref, o_ref):
    c = c_ref[...]
    s = c * (1.0 / (1.0 + jnp.exp(-c)))
    o_ref[0] = jnp.dot(s, w_ref[0], preferred_element_type=F32, precision=lax.Precision.HIGHEST) + b_ref[0]


def _modulation(cond, w_mod, b_mod):
    R, D = cond.shape
    L = w_mod.shape[0]
    nj = w_mod.shape[2] // D
    return pl.pallas_call(
        _mod_kernel,
        grid=(L, nj),
        in_specs=[
            pl.BlockSpec((R, D), lambda l, j: (0, 0)),
            pl.BlockSpec((1, D, D), lambda l, j: (l, 0, j)),
            pl.BlockSpec((1, 1, D), lambda l, j: (l, 0, j)),
        ],
        out_specs=pl.BlockSpec((1, R, D), lambda l, j: (l, 0, j)),
        out_shape=jax.ShapeDtypeStruct((L, R, nj * D), F32),
        compiler_params=_cparams(("parallel", "parallel")),
        name="modulation",
    )(cond, w_mod, b_mod.reshape(L, 1, nj * D))


def _rope(x, c, s, d):
    w = x.shape[-1]
    lane = lax.broadcasted_iota(jnp.int32, (1, LANES), 1)
    first = (lane % (2 * d)) < d
    outs = []
    for b in range(w // LANES):
        sl = slice(b * LANES, (b + 1) * LANES)
        xb = x[:, sl]
        partner = jnp.where(first, pltpu.roll(xb, LANES - d, 1), pltpu.roll(xb, d, 1))
        outs.append(xb * c[:, sl] + partner * s[:, sl])
    return outs[0] if len(outs) == 1 else jnp.concatenate(outs, axis=-1)


def _inproj_kernel(*refs, rope):
    if rope:
        (x_ref, mod_ref, g1_ref, w_ref, qn_ref, wuq_ref, kvn_ref,
         tac, tas, tcc, tcs, tqc, tqs, tkc, tks,
         oa, ob, oc, oq, ockv, okpe) = refs
    else:
        (x_ref, mod_ref, g1_ref, w_ref, qn_ref, wuq_ref, kvn_ref,
         oa, ob, oc, oq, ockv, okpe) = refs
    dt = oa.dtype
    x = x_ref[0]
    h = (_rms(x, g1_ref[...]) * (1.0 + mod_ref[0, 1:2, :]) + mod_ref[0, 0:1, :]).astype(BF)

    def seg(lo, w):
        return _dot(h, w_ref[:, lo:lo + w])

    ua = seg(0, 1024)
    qk = ua[:, :512]
    if rope:
        qk = _rope(qk, tac[...], tas[...], 8)
    oa[0, :, 0:256] = (qk[:, :256] * (DA_QK ** -0.5 * LOG2E)).astype(dt)
    oa[0, :, 256:512] = qk[:, 256:].astype(dt)
    ones_lane = lax.broadcasted_iota(jnp.int32, (1, 512), 1) % LANES >= 64
    oa[0, :, 512:1024] = jnp.where(ones_lane, 1.0, ua[:, 512:]).astype(dt)
    ub = seg(1024, 768)
    ob[0, :, 0:256] = (ub[:, :256] * (0.125 * LOG2E)).astype(dt)
    ob[0, :, 256:768] = ub[:, 256:].astype(dt)
    uc = seg(1792, 512)
    qk = uc[:, :384]
    if rope:
        qk = _rope(qk, tcc[...], tcs[...], 16)
    oc[0, :, 0:256] = (qk[:, :256] * (0.125 * LOG2E)).astype(dt)
    oc[0, :, 256:384] = qk[:, 256:].astype(dt)
    oc[0, :, 384:512] = uc[:, 384:].astype(dt)
    cq = _rms(seg(2304, 256), qn_ref[...]).astype(BF)
    qd = _dot(cq, wuq_ref[...])
    if rope:
        qd = _rope(qd, tqc[...], tqs[...], 8)
    oq[0] = (qd * ((MLA_NOPE + MLA_ROPE) ** -0.5 * LOG2E)).astype(dt)
    ockv[0] = _rms(seg(2560, 128), kvn_ref[...]).astype(dt)
    kpe = seg(2688, 128)
    if rope:
        kpe = _rope(kpe, tkc[...], tks[...], 8)
    okpe[0] = kpe.astype(dt)


def _in_proj(x, mod, g1, w_in_p, q_norm, w_uq_p, kv_norm, tables, out_dtype, tm):
    B, N, D = x.shape
    rope = tables is not None
    nmod = mod.shape[0]
    mod_map = (lambda j, b: (b, 0, 0)) if nmod > 1 else (lambda j, b: (0, 0, 0))
    const2 = lambda j, b: (0, 0)
    in_specs = [
        pl.BlockSpec((1, tm, D), lambda j, b: (b, j, 0)),
        pl.BlockSpec((1, 6, D), mod_map),
        pl.BlockSpec((1, D), const2),
        pl.BlockSpec((D, IN_PAD), const2),
        pl.BlockSpec((1, 256), const2),
        pl.BlockSpec((256, 512), const2),
        pl.BlockSpec((1, 128), const2),
    ]
    args = [x, mod, g1, w_in_p, q_norm, w_uq_p, kv_norm]
    if rope:
        for t in tables:
            in_specs.append(pl.BlockSpec((tm, t.shape[1]), lambda j, b: (j, 0)))
            args.append(t)
    widths = (1024, 768, 512, 512, 128, 128)
    out_specs = [pl.BlockSpec((1, tm, w), lambda j, b: (b, j, 0)) for w in widths]
    out_shape = [jax.ShapeDtypeStruct((B, N, w), out_dtype) for w in widths]
    return pl.pallas_call(
        functools.partial(_inproj_kernel, rope=rope),
        grid=(N // tm, B),
        in_specs=in_specs,
        out_specs=out_specs,
        out_shape=out_shape,
        compiler_params=_cparams(("parallel", "parallel"), VMEM_BIG),
        name="in_proj_rope" if rope else "in_proj",
    )(*args)


def _mla_expand_kernel(ckv_ref, kpe_ref, wk_ref, wv_ref, e_ref, ok, ov):
    ckv = ckv_ref[0].astype(BF)
    kpe = kpe_ref[0].astype(BF)
    ok[0] = (_dot(ckv, wk_ref[...]) + _dot(kpe, e_ref[...])).astype(ok.dtype)
    ones_lane = lax.broadcasted_iota(jnp.int32, (1, 512), 1) % LANES >= 64
    ov[0] = jnp.where(ones_lane, 1.0, _dot(ckv, wv_ref[...])).astype(ov.dtype)


def _mla_expand(ckv, kpe, w_k, w_v, e_place, tk):
    B, K, _ = ckv.shape
    const2 = lambda b, j: (0, 0)
    return pl.pallas_call(
        _mla_expand_kernel,
        grid=(B, K // tk),
        in_specs=[
            pl.BlockSpec((1, tk, 128), lambda b, j: (b, j, 0)),
            pl.BlockSpec((1, tk, 128), lambda b, j: (b, j, 0)),
            pl.BlockSpec((128, 512), const2),
            pl.BlockSpec((128, 512), const2),
            pl.BlockSpec((128, 512), const2),
        ],
        out_specs=[pl.BlockSpec((1, tk, 512), lambda b, j: (b, j, 0)),
                   pl.BlockSpec((1, tk, 512), lambda b, j: (b, j, 0))],
        out_shape=[jax.ShapeDtypeStruct((B, K, 512), BF), jax.ShapeDtypeStruct((B, K, 512), BF)],
        compiler_params=_cparams(("parallel", "parallel")),
        name="mla_expand",
    )(ckv, kpe, w_k, w_v, e_place)


def _softmax_init(tq, vw):
    return (jnp.full((tq, 1), NEG_INF, F32), jnp.zeros((tq, 1), F32), jnp.zeros((tq, vw), F32))


def _softmax_step(state, s, v):
    m, l, acc = state
    m_new = jnp.maximum(m, jnp.max(s, axis=-1, keepdims=True))
    alpha = jnp.exp2(m - m_new)
    p = jnp.exp2(s - m_new)
    l = alpha * l + jnp.sum(p, axis=-1, keepdims=True)
    acc = alpha * acc + _dot(p.astype(BF), v)
    return m_new, l, acc


def _dense_attn_kernel(*refs, groups, first, tk, n_chunks, has_ctx, has_sink, diff, aug):
    it = iter(refs)
    q_ref, k_ref, v_ref = next(it), next(it), next(it)
    ck_ref = cv_ref = sink_ref = lam_ref = subln_ref = None
    if has_ctx:
        ck_ref, cv_ref = next(it), next(it)
    if has_sink:
        sink_ref = next(it)
    if diff:
        lam_ref, subln_ref, lam_init_ref = next(it), next(it), next(it)
    o_ref = next(it)
    tq = q_ref.shape[1]

    lam = lam_init = None
    if diff:
        lam_init = lam_init_ref[0]
        dl = lam_ref[...]
        lam = (jnp.exp(jnp.sum(dl[0:1] * dl[1:2], axis=-1, keepdims=True))
               - jnp.exp(jnp.sum(dl[2:3] * dl[3:4], axis=-1, keepdims=True)) + lam_init)

    ng = len(groups)
    scratch = list(it)
    qg_s, m_s, l_s, acc_s = (scratch[j * ng:(j + 1) * ng] for j in range(4))
    for gi, (q_lo, k_lo, w, qmask, v_lo, vw, o_lo, omask, sink_idx, sign) in enumerate(groups):
        qg = q_ref[0, :, q_lo:q_lo + w].astype(BF)
        if qmask is not None:
            qg = jnp.where(_lane_mask(w, *qmask), qg, jnp.zeros_like(qg))
        qg_s[gi][...] = qg
        m_s[gi][...] = jnp.full((tq, 1), NEG_INF, F32)
        l_s[gi][...] = jnp.zeros((tq, 1), F32)
        acc_s[gi][...] = jnp.zeros((tq, vw), F32)

    def keys_step(kk, vv):
        for gi, (q_lo, k_lo, w, qmask, v_lo, vw, o_lo, omask, sink_idx, sign) in enumerate(groups):
            s = _dot_nt(qg_s[gi][...], kk[:, k_lo:k_lo + w])
            if aug:
                m = m_s[gi][...]
                m_new = jnp.maximum(m, jnp.max(s, axis=-1, keepdims=True))
                p = jnp.exp2(s - m_new).astype(BF)
                acc_s[gi][...] = jnp.exp2(m - m_new) * acc_s[gi][...] + _dot(p, vv[:, v_lo:v_lo + vw])
                m_s[gi][...] = m_new
                continue
            st = (m_s[gi][...], l_s[gi][...], acc_s[gi][...])
            m, l, acc = _softmax_step(st, s, vv[:, v_lo:v_lo + vw])
            m_s[gi][...] = m
            l_s[gi][...] = l
            acc_s[gi][...] = acc

    if has_ctx:
        keys_step(ck_ref[0].astype(BF), cv_ref[0].astype(BF))
    if first:
        keys_step(k_ref[0, 0:first, :].astype(BF), v_ref[0, 0:first, :].astype(BF))

    def body(c, carry):
        off = pl.multiple_of(first + c * tk, 128)
        keys_step(k_ref[0, pl.ds(off, tk), :].astype(BF), v_ref[0, pl.ds(off, tk), :].astype(BF))
        return carry

    if n_chunks == 1:
        for c in range(n_chunks):
            body(c, 0)
    else:
        lax.fori_loop(0, n_chunks, body, 0)

    if aug:
        half = _lane_mask(LANES, 0, 64)
        heads = {}
        for gi, (q_lo, k_lo, w, qmask, v_lo, vw, head, omask, sink_idx, sign) in enumerate(groups):
            acc = acc_s[gi][...]
            part = acc * pltpu.roll(1.0 / acc, 64, 1)
            if sign < 0:
                part = -lam * part
            heads[head] = part if head not in heads else heads[head] + part
        for hp in range(len(heads) // 2):
            pair = []
            for o in (heads[2 * hp], heads[2 * hp + 1]):
                if diff:
                    ms = jnp.sum(jnp.where(half, o * o, 0.0), axis=-1, keepdims=True) * (1.0 / 64.0)
                    o = o * lax.rsqrt(ms + NORM_EPS) * subln_ref[...] * (1.0 - lam_init)
                pair.append(o)
            slab = jnp.where(half, pair[0], pltpu.roll(pair[1], 64, 1))
            o_ref[0, :, hp * LANES:(hp + 1) * LANES] = slab.astype(o_ref.dtype)
        return

    out_parts = {}
    for gi, (q_lo, k_lo, w, qmask, v_lo, vw, o_lo, omask, sink_idx, sign) in enumerate(groups):
        m, l, acc = m_s[gi][...], l_s[gi][...], acc_s[gi][...]
        if sink_idx is not None:
            l = l + jnp.exp2(sink_ref[sink_idx] * LOG2E - m)
        part = jnp.where(_lane_mask(vw, *omask), acc * (1.0 / l), 0.0)
        key = (o_lo, vw)
        out_parts[key] = part if key not in out_parts else out_parts[key] + part
    for (o_lo, vw), part in out_parts.items():
        o_ref[0, :, o_lo:o_lo + vw] = part.astype(o_ref.dtype)


def _dense_attn(q_arr, q_blk, k_arr, k_blk, v_arr, v_blk, *, groups, out_w, tq, tk, first=0,
                ctx=None, sink=None, diff=None, aug=False, name):
    B, N, _ = q_arr.shape
    K = k_arr.shape[1]
    n_chunks = (K - first) // tk
    assert first + n_chunks * tk == K
    qw, qi = q_blk
    kw, ki = k_blk
    vw, vi = v_blk
    in_specs = [
        pl.BlockSpec((1, tq, qw), lambda b, i: (b, i, qi)),
        pl.BlockSpec((1, K, kw), lambda b, i: (b, 0, ki)),
        pl.BlockSpec((1, K, vw), lambda b, i: (b, 0, vi)),
    ]
    args = [q_arr, k_arr, v_arr]
    if ctx is not None:
        ck, cv = ctx
        in_specs += [pl.BlockSpec((1,) + ck.shape[1:], lambda b, i: (b, 0, 0)),
                     pl.BlockSpec((1,) + cv.shape[1:], lambda b, i: (b, 0, 0))]
        args += [ck, cv]
    if sink is not None:
        in_specs.append(pl.BlockSpec(memory_space=pltpu.SMEM))
        args.append(sink)
    if diff is not None:
        da_lambda, subln, lam_init = diff
        in_specs += [pl.BlockSpec((4, DA_QK), lambda b, i: (0, 0)),
                     pl.BlockSpec((1, LANES), lambda b, i: (0, 0)),
                     pl.BlockSpec(memory_space=pltpu.SMEM)]
        args += [da_lambda, subln, jnp.full((1,), lam_init, F32)]
    kern = functools.partial(
        _dense_attn_kernel, groups=groups, first=first, tk=tk, n_chunks=n_chunks,
        has_ctx=ctx is not None, has_sink=sink is not None, diff=diff is not None, aug=aug)
    scratch = ([pltpu.VMEM((tq, g[2]), BF) for g in groups] + [pltpu.VMEM((tq, 1), F32) for _ in groups] * 2
               + [pltpu.VMEM((tq, g[5]), F32) for g in groups])
    return pl.pallas_call(
        kern,
        grid=(B, N // tq),
        in_specs=in_specs,
        out_specs=pl.BlockSpec((1, tq, out_w), lambda b, i: (b, i, 0)),
        out_shape=jax.ShapeDtypeStruct((B, N, out_w), BF),
        scratch_shapes=scratch,
        compiler_params=_cparams(("parallel", "parallel"), VMEM_BIG),
        name=name,
    )(*args)


GROUPS_A = tuple((0, 0, 256, (32 * g, 32), 128 * (g // 2), 128, g // 2, None, None, 1 if g % 2 == 0 else -1)
                 for g in range(8))
GROUPS_B = tuple((0, 0, 256, (64 * h, 64), 0, 256, 0, (64 * h, 64), None, 1) for h in range(4))
GROUPS_C = tuple((128 * g, 0, 128, (64 * j, 64), 0, 128, 128 * g, (64 * j, 64), 2 * j + g, 1)
                 for g in range(2) for j in range(2))
GROUPS_D = tuple((128 * h, 128 * h, 128, None, 128 * h, 128, h, None, None, 1) for h in range(4))


def _window_attn_kernel(q_ref, k_ref, v_ref, ck_ref, cv_ref, sink_ref, o_ref, *, n, band):
    tq = q_ref.shape[1]
    i = pl.program_id(1)
    start = jnp.clip(i * tq - SW_WINDOW, 0, n - band)
    start = pl.multiple_of(start, 128)
    kb = k_ref[0, pl.ds(start, band), :]
    vb = v_ref[0, pl.ds(start, band), :]
    ck = ck_ref[0].astype(BF)
    cv = cv_ref[0].astype(BF)
    qpos = i * tq + lax.broadcasted_iota(jnp.int32, (tq, band), 0)
    kpos = start + lax.broadcasted_iota(jnp.int32, (tq, band), 1)
    valid = jnp.abs(qpos - kpos) <= SW_WINDOW
    for g in range(2):
        part = jnp.zeros((tq, 128), F32)
        for j in range(2):
            hm = _lane_mask(128, 64 * j, 64)
            qg = q_ref[0, :, 128 * g:128 * (g + 1)]
            qg = jnp.where(hm, qg, jnp.zeros_like(qg))
            st = _softmax_init(tq, 128)
            st = _softmax_step(st, _dot_nt(qg, ck), cv)
            st = _softmax_step(st, jnp.where(valid, _dot_nt(qg, kb), NEG_INF), vb)
            m, l, acc = st
            l = l + jnp.exp2(sink_ref[2 * j + g] * LOG2E - m)
            part = jnp.where(hm, acc * (1.0 / l), part)
        o_ref[0, :, 128 * g:128 * (g + 1)] = part.astype(o_ref.dtype)


def _window_attn(qkv_c, ck, cv, sink, tq):
    B, N, _ = qkv_c.shape
    band = tq + 2 * SW_WINDOW
    P = ck.shape[1]
    return pl.pallas_call(
        functools.partial(_window_attn_kernel, n=N, band=band),
        grid=(B, N // tq),
        in_specs=[
            pl.BlockSpec((1, tq, 256), lambda b, i: (b, i, 0)),
            pl.BlockSpec((1, N, 128), lambda b, i: (b, 0, 2)),
            pl.BlockSpec((1, N, 128), lambda b, i: (b, 0, 3)),
            pl.BlockSpec((1, P, 128), lambda b, i: (b, 0, 0)),
            pl.BlockSpec((1, P, 128), lambda b, i: (b, 0, 0)),
            pl.BlockSpec(memory_space=pltpu.SMEM),
        ],
        out_specs=pl.BlockSpec((1, tq, 256), lambda b, i: (b, i, 0)),
        out_shape=jax.ShapeDtypeStruct((B, N, 256), BF),
        compiler_params=_cparams(("parallel", "parallel")),
        name="window_attn",
    )(qkv_c, qkv_c, qkv_c, ck, cv, sink)


NA_TILE_ROWS = 4
NA_KEY_ROWS = NA_TILE_ROWS + NA_WIN_R


def _nbr_attn_kernel(q_ref, k_ref, v_ref, ck_ref, cv_ref, bias_ref, o_ref, *, rows):
    tq = q_ref.shape[1]
    i = pl.program_id(1)
    krow = jnp.clip(i * NA_TILE_ROWS - NA_WIN_R // 2, 0, rows - NA_KEY_ROWS)
    start = pl.multiple_of(krow * GRID_W, GRID_W)
    nk = NA_KEY_ROWS * GRID_W
    kb = k_ref[0, pl.ds(start, nk), :]
    vb = v_ref[0, pl.ds(start, nk), :]
    ck = ck_ref[0].astype(BF)
    cv = cv_ref[0].astype(BF)
    q = q_ref[0]
    o = jnp.zeros((tq, 256), F32)
    for h in range(4):
        hm = _lane_mask(256, 64 * h, 64)
        qg = jnp.where(hm, q, jnp.zeros_like(q))
        st = _softmax_init(tq, 256)
        st = _softmax_step(st, _dot_nt(qg, ck), cv)
        st = _softmax_step(st, _dot_nt(qg, kb) + bias_ref[0, h], vb)
        m, l, acc = st
        o = jnp.where(hm, acc * (1.0 / l), o)
    o_ref[0] = o.astype(o_ref.dtype)


def _nbr_attn(qkv_b, ck, cv, bias):
    B, N, _ = qkv_b.shape
    rows = N // GRID_W
    tq = NA_TILE_ROWS * GRID_W
    nt = N // tq
    nk = NA_KEY_ROWS * GRID_W
    P = ck.shape[1]

    def bias_map(b, i):
        return (jnp.where(i == 0, 0, jnp.where(i == nt - 1, 2, 1)), 0, 0, 0)

    return pl.pallas_call(
        functools.partial(_nbr_attn_kernel, rows=rows),
        grid=(B, nt),
        in_specs=[
            pl.BlockSpec((1, tq, 256), lambda b, i: (b, i, 0)),
            pl.BlockSpec((1, N, 256), lambda b, i: (b, 0, 1)),
            pl.BlockSpec((1, N, 256), lambda b, i: (b, 0, 2)),
            pl.BlockSpec((1, P, 256), lambda b, i: (b, 0, 0)),
            pl.BlockSpec((1, P, 256), lambda b, i: (b, 0, 0)),
            pl.BlockSpec((1, 4, tq, nk), bias_map),
        ],
        out_specs=pl.BlockSpec((1, tq, 256), lambda b, i: (b, i, 0)),
        out_shape=jax.ShapeDtypeStruct((B, N, 256), BF),
        compiler_params=_cparams(("parallel", "parallel"), VMEM_MID),
        name="nbr_attn",
    )(qkv_b, qkv_b, qkv_b, ck, cv, bias)


def _nbr_bias_tables(rpb, rows):
    H = rpb.shape[0]
    rp = jnp.pad(rpb.astype(F32), ((0, 0), (NA_KEY_ROWS, NA_KEY_ROWS), (GRID_W, GRID_W)))
    c0 = NA_WIN_C - 1 + GRID_W
    tcol = jnp.stack([rp[:, :, c0 - qc:c0 - qc + GRID_W] for qc in range(GRID_W)], axis=2)
    tabs = []
    nt = rows // NA_TILE_ROWS
    for r0 in (0, NA_TILE_ROWS * min(1, nt - 1) + NA_TILE_ROWS, rows - NA_TILE_ROWS):
        ks = int(np.clip(r0 - NA_WIN_R // 2, 0, rows - NA_KEY_ROWS))
        per_row = []
        for a in range(NA_TILE_ROWS):
            s0 = ks - (r0 + a) + NA_WIN_R - 1 + NA_KEY_ROWS
            per_row.append(tcol[:, s0:s0 + NA_KEY_ROWS])
        t = jnp.stack(per_row, axis=1).transpose(0, 1, 3, 2, 4)
        r = r0 + np.arange(NA_TILE_ROWS)[:, None, None, None]
        qc = np.arange(GRID_W)[None, :, None, None]
        kr = ks + np.arange(NA_KEY_ROWS)[None, None, :, None]
        kc = np.arange(GRID_W)[None, None, None, :]
        rs = np.clip(r - NA_WIN_R // 2, 0, rows - NA_WIN_R)
        cs = np.clip(qc - NA_WIN_C // 2, 0, GRID_W - NA_WIN_C)
        ok = (kr >= rs) & (kr < rs + NA_WIN_R) & (kc >= cs) & (kc < cs + NA_WIN_C)
        t = jnp.where(ok[None], t * LOG2E, NEG_INF)
        tabs.append(t.reshape(H, NA_TILE_ROWS * GRID_W, NA_KEY_ROWS * GRID_W))
    return jnp.stack(tabs)


def _merge_kernel(x_ref, mod_ref, g1_ref, g2_ref, oa, ob, oc, od, wg_ref, wb_ref, wo_ref, wr_ref,
                  xo_ref, h2_ref, lg_ref):
    D = x_ref.shape[2]
    x = x_ref[0]
    h = (_rms(x, g1_ref[...]) * (1.0 + mod_ref[0, 1:2, :]) + mod_ref[0, 0:1, :]).astype(BF)
    merged = None
    for i, o_ref in enumerate((oa, ob, oc, od)):
        z = _dot(h, wg_ref[:, i * D:(i + 1) * D])
        gate = 1.0 / (1.0 + jnp.exp(-z))
        term = gate * _dot(o_ref[0], wb_ref[i])
        merged = term if merged is None else merged + term
    mixed = _dot(merged.astype(BF), wo_ref[...])
    xn = x + mod_ref[0, 2:3, :] * mixed
    xo_ref[0] = xn
    h2 = (_rms(xn, g2_ref[...]) * (1.0 + mod_ref[0, 4:5, :]) + mod_ref[0, 3:4, :]).astype(BF)
    h2_ref[0] = h2
    lg_ref[0] = _dot_nt(wr_ref[...], h2)


def _merge(x, mod, g1, g2, o_a, o_b, o_c, o_d, w_gate, w_branch, w_out, w_router_p, tm):
    B, N, D = x.shape
    nmod = mod.shape[0]
    mod_map = (lambda b, j: (b, 0, 0)) if nmod > 1 else (lambda b, j: (0, 0, 0))
    tok = lambda w: pl.BlockSpec((1, tm, w), lambda b, j: (b, j, 0))
    c2 = lambda b, j: (0, 0)
    return pl.pallas_call(
        _merge_kernel,
        grid=(B, N // tm),
        in_specs=[
            tok(D), pl.BlockSpec((1, 6, D), mod_map), pl.BlockSpec((1, D), c2), pl.BlockSpec((1, D), c2),
            tok(256), tok(256), tok(256), tok(256),
            pl.BlockSpec((D, 4 * D), c2),
            pl.BlockSpec((4, 256, D), lambda b, j: (0, 0, 0)),
            pl.BlockSpec((D, D), c2),
            pl.BlockSpec((N_EXPERTS, D), c2),
        ],
        out_specs=[tok(D), tok(D), pl.BlockSpec((1, N_EXPERTS, tm), lambda b, j: (b, 0, j))],
        out_shape=[jax.ShapeDtypeStruct((B, N, D), F32), jax.ShapeDtypeStruct((B, N, D), BF),
                   jax.ShapeDtypeStruct((B, N_EXPERTS, N), F32)],
        compiler_params=_cparams(("parallel", "parallel"), VMEM_BIG),
        name="merge",
    )(x, mod, g1, g2, o_a, o_b, o_c, o_d, w_gate, w_branch, w_out, w_router_p)


MOE_TT = 256
MOE_RC = 64
SLOT_ALIGN = 16


def _excl_cumsum(x, tri, reset_every):
    R, N = x.shape
    outs = []
    carry = jnp.zeros((R, 1), F32)
    for c in range(N // LANES):
        if reset_every and (c * LANES) % reset_every == 0:
            carry = jnp.zeros((R, 1), F32)
        xc = x[:, c * LANES:(c + 1) * LANES]
        inc = _dot(xc.astype(BF), tri)
        outs.append(inc - xc + carry)
        carry = carry + inc[:, LANES - 1:LANES]
    return (outs[0] if len(outs) == 1 else jnp.concatenate(outs, axis=1)), carry


def _route_kernel(lg_ref, aff_ref, rel_ref, base_ref, *, cap, tt):
    B, E, N = lg_ref.shape
    R = B * E
    nt = N // tt
    lg = lg_ref[...]
    ex = jnp.exp(lg - jnp.max(lg, axis=1, keepdims=True))
    aff = ex / jnp.sum(ex, axis=1, keepdims=True)
    aff_ref[...] = aff
    bits = pltpu.bitcast(aff.reshape(R, N), jnp.int32)

    def per_expert(v):
        t = v[0:E]
        for b in range(1, B):
            t = t + v[b * E:(b + 1) * E]
        return t

    def per_row(v):
        return jnp.concatenate([v] * B, axis=0) if B > 1 else v

    def batch_prefix(v):
        parts, run = [], jnp.zeros((E, 1), F32)
        for b in range(B):
            parts.append(run)
            run = run + v[b * E:(b + 1) * E]
        return jnp.concatenate(parts, axis=0) if B > 1 else parts[0]

    def count(mask):
        return per_expert(jnp.sum(jnp.where(mask, 1.0, 0.0), axis=1, keepdims=True))

    def search(i, thr):
        cand = thr | jnp.left_shift(jnp.int32(1), 30 - i)
        return jnp.where(count(bits >= per_row(cand)) >= float(cap), cand, thr)

    thr = per_row(lax.fori_loop(0, 31, search, jnp.zeros((E, 1), jnp.int32)))
    gt = bits > thr
    eq = bits == thr
    need = float(cap) - count(gt)
    ri = lax.broadcasted_iota(jnp.int32, (LANES, LANES), 0)
    ci = lax.broadcasted_iota(jnp.int32, (LANES, LANES), 1)
    tri = jnp.where(ri <= ci, 1.0, 0.0).astype(BF)
    rank_eq, eq_tot = _excl_cumsum(jnp.where(eq, 1.0, 0.0), tri, None)
    rank_eq = rank_eq + batch_prefix(eq_tot)
    sel = gt | (eq & (rank_eq < per_row(need)))
    self = jnp.where(sel, 1.0, 0.0)
    local, _ = _excl_cumsum(self, tri, tt)
    rel_ref[...] = jnp.where(sel, local, -1.0).astype(jnp.int32).reshape(B, E, N)
    tn = jnp.right_shift(lax.broadcasted_iota(jnp.int32, (N, LANES), 0), tt.bit_length() - 1)
    tj = lax.broadcasted_iota(jnp.int32, (N, LANES), 1)
    cnt = _dot(self.astype(BF), jnp.where(tn == tj, 1.0, 0.0).astype(BF)).astype(jnp.int32)
    sh = SLOT_ALIGN.bit_length() - 1
    ru = jnp.left_shift(jnp.right_shift(cnt + (SLOT_ALIGN - 1), sh), sh).astype(F32)
    base_b = _dot(ru.astype(BF), jnp.where(ri < ci, 1.0, 0.0).astype(BF))
    base = base_b + batch_prefix(base_b[:, nt:nt + 1])
    base_ref[...] = base.astype(jnp.int32).reshape(B, E, LANES)


def _route(lg_t, cap):
    B, E, N = lg_t.shape
    full = lambda s: pl.BlockSpec(s, lambda i: (0,) * len(s))
    return pl.pallas_call(
        functools.partial(_route_kernel, cap=cap, tt=MOE_TT),
        grid=(1,),
        in_specs=[full((B, E, N))],
        out_specs=[full((B, E, N)), full((B, E, N)), full((B, E, LANES))],
        out_shape=[jax.ShapeDtypeStruct((B, E, N), F32), jax.ShapeDtypeStruct((B, E, N), jnp.int32),
                   jax.ShapeDtypeStruct((B, E, LANES), jnp.int32)],
        compiler_params=_cparams(("arbitrary",), VMEM_BIG),
        name="route",
    )(lg_t)


def _slot_onehot(rel, values, c, rc):
    E, tt = rel.shape
    r = lax.broadcasted_iota(jnp.int32, (E, rc, tt), 1) + c * rc
    return jnp.where(rel[:, None, :] == r, values[:, None, :], 0.0).astype(BF).reshape(E * rc, tt)


def _dispatch_kernel(base_sm, x_ref, rel_ref, xe_in, xe_hbm, stage, stage2, sem, sem2, *, nt, rc):
    del xe_in
    b, j = pl.program_id(0), pl.program_id(1)
    E = rel_ref.shape[1]
    step = b * nt + j
    last = pl.num_programs(0) * nt - 1
    base = lambda e, jj: base_sm[(b * E + e) * (nt + 1) + jj]
    rel = rel_ref[0]
    x = x_ref[0]
    ones = jnp.ones(rel.shape, F32)

    def copy(buf, s, e, c):
        start = pl.multiple_of(base(e, j) + c * rc, SLOT_ALIGN)
        return pltpu.make_async_copy(buf.at[pl.ds(e * rc, rc)], xe_hbm.at[e, pl.ds(start, rc)], s.at[e])

    strip = _dot(_slot_onehot(rel, ones, 0, rc), x).astype(BF)

    @pl.when(step > 0)
    def _():
        for e in range(E):
            copy(stage, sem, e, 0).wait()

    stage[...] = strip
    for e in range(E):
        copy(stage, sem, e, 0).start()

    span = [base(e, j + 1) - base(e, j) for e in range(E)]
    widest = functools.reduce(jnp.maximum, span)

    def extra(c, carry):
        stage2[...] = _dot(_slot_onehot(rel, ones, c, rc), x).astype(BF)
        for e in range(E):
            @pl.when(span[e] > c * rc)
            def _():
                cp = copy(stage2, sem2, e, c)
                cp.start()
                cp.wait()
        return carry

    lax.fori_loop(1, (widest + rc - 1) // rc, extra, 0)

    @pl.when(step == last)
    def _():
        for e in range(E):
            copy(stage, sem, e, 0).wait()


def _dispatch(h2, rel, base_flat, rows):
    B, N, D = h2.shape
    E = rel.shape[1]
    nt = N // MOE_TT
    xe0 = jnp.zeros((E, rows, D), BF)
    return pl.pallas_call(
        functools.partial(_dispatch_kernel, nt=nt, rc=MOE_RC),
        grid_spec=pltpu.PrefetchScalarGridSpec(
            num_scalar_prefetch=1,
            grid=(B, nt),
            in_specs=[
                pl.BlockSpec((1, MOE_TT, D), lambda b, j, s: (b, j, 0)),
                pl.BlockSpec((1, E, MOE_TT), lambda b, j, s: (b, 0, j)),
                pl.BlockSpec(memory_space=pl.ANY),
            ],
            out_specs=pl.BlockSpec(memory_space=pl.ANY),
            scratch_shapes=[pltpu.VMEM((E * MOE_RC, D), BF), pltpu.VMEM((E * MOE_RC, D), BF),
                            pltpu.SemaphoreType.DMA((E,)), pltpu.SemaphoreType.DMA((E,))],
        ),
        out_shape=jax.ShapeDtypeStruct((E, rows, D), BF),
        input_output_aliases={3: 0},
        compiler_params=_cparams(("arbitrary", "arbitrary")),
        name="dispatch",
    )(base_flat, h2, rel, xe0)


def _ffn_kernel(tot_sm, x_ref, wg_ref, wu_ref, wd_ref, y_ref, wg_s, wu_s, wd_s, *, tm, rc):
    e, j = pl.program_id(0), pl.program_id(1)

    @pl.when(j == 0)
    def _():
        wg_s[...] = wg_ref[0].astype(BF)
        wu_s[...] = wu_ref[0].astype(BF)
        wd_s[...] = wd_ref[0].astype(BF)

    used = j * tm < tot_sm[e] + rc

    @pl.when(used)
    def _():
        x = x_ref[0]
        hg = _dot(x, wg_s[...])
        hu = _dot(x, wu_s[...])
        act = (hg * (1.0 / (1.0 + jnp.exp(-hg))) * hu).astype(BF)
        y_ref[0] = _dot(act, wd_s[...]).astype(y_ref.dtype)

    @pl.when(jnp.logical_not(used))
    def _():
        y_ref[...] = jnp.zeros_like(y_ref)


def _expert_ffn(xe, total, w_g, w_u, w_d, tm):
    E, rows, D = xe.shape
    DE = w_g.shape[2]
    tok = lambda e, j, tot: (e, jnp.minimum(j, (tot[e] + MOE_RC - 1) // tm), 0)
    wmap = lambda e, j, tot: (e, 0, 0)
    return pl.pallas_call(
        functools.partial(_ffn_kernel, tm=tm, rc=MOE_RC),
        grid_spec=pltpu.PrefetchScalarGridSpec(
            num_scalar_prefetch=1,
            grid=(E, rows // tm),
            in_specs=[pl.BlockSpec((1, tm, D), tok), pl.BlockSpec((1, D, DE), wmap),
                      pl.BlockSpec((1, D, DE), wmap), pl.BlockSpec((1, DE, D), wmap)],
            out_specs=pl.BlockSpec((1, tm, D), lambda e, j, tot: (e, j, 0)),
            scratch_shapes=[pltpu.VMEM((D, DE), BF), pltpu.VMEM((D, DE), BF), pltpu.VMEM((DE, D), BF)],
        ),
        out_shape=jax.ShapeDtypeStruct((E, rows, D), BF),
        compiler_params=_cparams(("arbitrary", "arbitrary"), VMEM_BIG),
        name="expert_ffn",
    )(total, xe, w_g, w_u, w_d)


def _combine_kernel(base_sm, x_ref, rel_ref, aff_ref, mod_ref, gf_ref, ye_hbm, o_ref, stage, sem, *,
                    nt, rc, final):
    b, j = pl.program_id(0), pl.program_id(1)
    E = rel_ref.shape[1]
    step = b * nt + j
    nsteps = pl.num_programs(0) * nt
    slot = step % 2
    base = lambda e, jj: base_sm[(b * E + e) * (nt + 1) + jj]
    rel = rel_ref[0]
    aff = aff_ref[0]

    def copy(st, sl, e, c):
        first = base_sm[(st // nt * E + e) * (nt + 1) + st % nt]
        start = pl.multiple_of(first + c * rc, SLOT_ALIGN)
        return pltpu.make_async_copy(ye_hbm.at[e, pl.ds(start, rc)], stage.at[sl, pl.ds(e * rc, rc)],
                                     sem.at[sl, e])

    def scatter(c):
        w = _slot_onehot(rel, aff, c, rc)
        return lax.dot_general(w, stage[slot], (((0,), (0,)), ((), ())), preferred_element_type=F32)

    @pl.when(step == 0)
    def _():
        for e in range(E):
            copy(step, slot, e, 0).start()

    @pl.when(step + 1 < nsteps)
    def _():
        for e in range(E):
            copy(step + 1, 1 - slot, e, 0).start()

    for e in range(E):
        copy(step, slot, e, 0).wait()
    acc = scatter(0)
    span = [base(e, j + 1) - base(e, j) for e in range(E)]
    widest = functools.reduce(jnp.maximum, span)

    def extra(c, acc):
        for e in range(E):
            @pl.when(span[e] > c * rc)
            def _():
                cp = copy(step, slot, e, c)
                cp.start()
                cp.wait()
        return acc + scatter(c)

    acc = lax.fori_loop(1, (widest + rc - 1) // rc, extra, acc)
    xn = x_ref[0] + mod_ref[0, 5:6, :] * acc
    if final:
        xn = _rms(xn, gf_ref[...])
    o_ref[0] = xn


def _combine(x, rel, aff, mod, g_final, ye, base_flat, final):
    B, N, D = x.shape
    E = rel.shape[1]
    nt = N // MOE_TT
    nmod = mod.shape[0]
    mod_map = (lambda b, j, s: (b, 0, 0)) if nmod > 1 else (lambda b, j, s: (0, 0, 0))
    tok = pl.BlockSpec((1, MOE_TT, D), lambda b, j, s: (b, j, 0))
    slots = pl.BlockSpec((1, E, MOE_TT), lambda b, j, s: (b, 0, j))
    return pl.pallas_call(
        functools.partial(_combine_kernel, nt=nt, rc=MOE_RC, final=final),
        grid_spec=pltpu.PrefetchScalarGridSpec(
            num_scalar_prefetch=1,
            grid=(B, nt),
            in_specs=[tok, slots, slots, pl.BlockSpec((1, 6, D), mod_map),
                      pl.BlockSpec((1, D), lambda b, j, s: (0, 0)), pl.BlockSpec(memory_space=pl.ANY)],
            out_specs=tok,
            scratch_shapes=[pltpu.VMEM((2, E * MOE_RC, D), BF), pltpu.SemaphoreType.DMA((2, E))],
        ),
        out_shape=jax.ShapeDtypeStruct((B, N, D), F32),
        compiler_params=_cparams(("arbitrary", "arbitrary")),
        name="combine_final" if final else "combine",
    )(base_flat, x, rel, aff, mod, g_final, ye)


def _moe(x, h2, lg_t, mod, g_final, w_g, w_u, w_d, final):
    B, N, D = h2.shape
    T = B * N
    nt = N // MOE_TT
    cap = EC_CAPACITY * T // N_EXPERTS
    tm = 512 if cap >= 2048 else 256
    worst = cap + (SLOT_ALIGN - 1) * (T // MOE_TT) + MOE_RC
    rows = -(-worst // tm) * tm
    aff, rel, base = _route(lg_t, cap)
    base_flat = base[:, :, :nt + 1].reshape(-1)
    total = base[B - 1, :, nt]
    xe = _dispatch(h2, rel, base_flat, rows)
    ye = _expert_ffn(xe, total, w_g, w_u, w_d, tm)
    return _combine(x, rel, aff, mod, g_final, ye, base_flat, final)


def _rope_tables(n):
    t = jnp.arange(n)
    pos = {1: (t // GRID_W).astype(F32), 2: (t % GRID_W).astype(F32)}

    def group(quarter):
        freqs = jnp.float32(ROPE_BASE) ** (-jnp.arange(quarter, dtype=F32) / quarter)
        cs, ss = [], []
        for kind in (1, 2):
            ang = pos[kind][:, None] * freqs
            c, s = jnp.cos(ang), jnp.sin(ang)
            cs += [c, c]
            ss += [-s, s]
        return jnp.concatenate(cs, axis=1), jnp.concatenate(ss, axis=1)

    c32, s32 = group(8)
    c64, s64 = group(16)
    one = lambda w: jnp.ones((n, w), F32)
    zero = lambda w: jnp.zeros((n, w), F32)
    ta = (jnp.tile(c32, (1, 16)), jnp.tile(s32, (1, 16)))
    tc = (jnp.tile(c64, (1, 6)), jnp.tile(s64, (1, 6)))
    qc = jnp.concatenate([one(64), c32, one(32)], axis=1)
    qs = jnp.concatenate([zero(64), s32, zero(32)], axis=1)
    tq = (jnp.tile(qc, (1, 4)), jnp.tile(qs, (1, 4)))
    tk = (jnp.concatenate([c32, one(96)], axis=1), jnp.concatenate([s32, zero(96)], axis=1))
    return (ta[0], ta[1], tc[0], tc[1], tq[0], tq[1], tk[0], tk[1])


LAT_TQ = 512
C_HEAD_ORDER = (0, 2, 1, 3)


def _prep_layer(l, w_in, mla_w_uq, mla_w_ukv, w_gate, w_branch, w_out, w_router):
    D = D_MODEL
    wi = w_in[l]
    cq = wi[:, 1536:1792].reshape(D, 4, 64)[:, C_HEAD_ORDER, :].reshape(D, 256)
    av = jnp.pad(wi[:, 512:768].reshape(D, 4, 64), ((0, 0), (0, 0), (0, 64))).reshape(D, 512)
    w_in_p = jnp.concatenate(
        [wi[:, :512], av, wi[:, 768:1536], cq, wi[:, 1792:], jnp.zeros((D, LANES - MLA_ROPE), F32)],
        axis=1).astype(BF)
    uq = mla_w_uq[l].reshape(256, 4, MLA_NOPE + MLA_ROPE)
    w_uq_p = jnp.concatenate([uq, jnp.zeros((256, 4, 32), F32)], axis=2).reshape(256, 512).astype(BF)
    ukv = mla_w_ukv[l].reshape(128, 4, 128)
    w_k = jnp.concatenate([ukv[:, :, :64], jnp.zeros((128, 4, 64), F32)], axis=2).reshape(128, 512).astype(BF)
    w_v = jnp.concatenate([ukv[:, :, 64:], jnp.zeros((128, 4, 64), F32)], axis=2).reshape(128, 512).astype(BF)
    wb = w_branch[l]
    wb_c = wb[2].reshape(4, 64, D)[C_HEAD_ORDER, :, :].reshape(256, D)
    w_branch_p = jnp.stack([wb[0], wb[1], wb_c, wb[3]]).astype(BF)
    w_router_p = w_router[l].T.astype(BF)
    return dict(w_in=w_in_p, w_uq=w_uq_p, w_k=w_k, w_v=w_v, w_gate=w_gate[l].astype(BF),
                w_branch=w_branch_p, w_out=w_out[l].astype(BF), w_router=w_router_p)


def _kpe_placement():
    e = np.zeros((128, 512), np.float32)
    for h in range(4):
        for i in range(MLA_ROPE):
            e[i, 128 * h + MLA_NOPE + i] = 1.0
    return jnp.asarray(e, BF)


def _layer(x, mod, lw, lp, lam_init, ctx, tables, final, g_final):
    B, N, D = x.shape
    latent = ctx is not None
    tm = 512 if latent else 256
    out_dtype = BF if latent else F32
    qkv_a, qkv_b, qkv_c, q_d, ckv_n, kpe_r = _in_proj(
        x, mod, lp["g_norm1"], lw["w_in"], lp["mla_q_norm"], lw["w_uq"], lp["mla_kv_norm"],
        tables if latent else None, out_dtype, tm)
    diff = (lp["da_lambda"], jnp.tile(lp["da_subln"], (1, 2)), lam_init)
    if latent:
        ck_a, cv_a, ck_b, cv_b, ck_c, cv_c, ckv_c, kpe_c = ctx
        P = ck_a.shape[1]
        ones = jnp.ones(cv_a.shape[:2] + (4, 64), BF)
        cv_aug = jnp.concatenate([cv_a.reshape(B, P, 4, 64).astype(BF), ones], axis=-1).reshape(B, P, 512)
        k_a = jnp.concatenate([ck_a.astype(BF), qkv_a[:, :, 256:512]], axis=1)
        v_a = jnp.concatenate([cv_aug, qkv_a[:, :, 512:1024]], axis=1)
        o_a = _dense_attn(qkv_a, (256, 0), k_a, (256, 0), v_a, (512, 0), groups=GROUPS_A, out_w=256,
                          tq=LAT_TQ, tk=(N + P) // 2, diff=diff, aug=True, name="diff_attn")
        o_b = _nbr_attn(qkv_b, ck_b, cv_b, lp["na_bias"])
        o_c = _window_attn(qkv_c, ck_c, cv_c, lp["sw_sink"], 256)
        ckv_all = jnp.concatenate([ckv_c.astype(BF), ckv_n], axis=1)
        kpe_all = jnp.concatenate([jnp.pad(kpe_c, ((0, 0), (0, 0), (0, 128 - MLA_ROPE))).astype(BF), kpe_r], axis=1)
        k_d, v_d = _mla_expand(ckv_all, kpe_all, lw["w_k"], lw["w_v"], lp["e_place"], 256)
        o_d = _dense_attn(q_d, (512, 0), k_d, (512, 0), v_d, (512, 0), groups=GROUPS_D, out_w=256,
                          tq=LAT_TQ, tk=(N + P) // 2, aug=True, name="mla_attn")
    else:
        o_a = _dense_attn(qkv_a, (256, 0), qkv_a, (256, 1), qkv_a, (512, 1), groups=GROUPS_A, out_w=256,
                          tq=N, tk=N, diff=diff, aug=True, name="diff_attn_ctx")
        o_b = _dense_attn(qkv_b, (256, 0), qkv_b, (256, 1), qkv_b, (256, 2), groups=GROUPS_B, out_w=256,
                          tq=N, tk=N, name="dense_attn_ctx")
        o_c = _dense_attn(qkv_c, (256, 0), qkv_c, (128, 2), qkv_c, (128, 3), groups=GROUPS_C, out_w=256,
                          tq=N, tk=N, sink=lp["sw_sink"], name="gqa_attn_ctx")
        k_d, v_d = _mla_expand(ckv_n, kpe_r, lw["w_k"], lw["w_v"], lp["e_place"], N)
        o_d = _dense_attn(q_d, (512, 0), k_d, (512, 0), v_d, (512, 0), groups=GROUPS_D, out_w=256,
                          tq=N, tk=N, aug=True, name="mla_attn_ctx")
    x_new, h2, logits = _merge(x, mod, lp["g_norm1"], lp["g_norm2"], o_a, o_b, o_c, o_d,
                               lw["w_gate"], lw["w_branch"], lw["w_out"], lw["w_router"], tm)
    x_out = _moe(x_new, h2, logits, mod, g_final, lp["w_e_gate"], lp["w_e_up"], lp["w_e_down"], final)
    cache = None
    if not latent:
        cache = (qkv_a[:, :, 256:512].reshape(B, N, 4, 64),
                 qkv_a[:, :, 512:1024].reshape(B, N, 4, 128)[..., :64],
                 qkv_b[:, :, 256:512].reshape(B, N, 4, 64), qkv_b[:, :, 512:768].reshape(B, N, 4, 64),
                 qkv_c[:, :, 256:384].reshape(B, N, 2, 64), qkv_c[:, :, 384:512].reshape(B, N, 2, 64),
                 ckv_n, kpe_r[:, :, :MLA_ROPE])
    return x_out, cache


def kernel(x_prompt, x_sample, cache_diff_k, cache_diff_v, cache_na_k, cache_na_v, cache_swa_k, cache_swa_v, cache_mla_ckv, cache_mla_kpe, c, c_ctx, w_mod, b_mod, g_norm1, g_norm2, w_in, da_lambda, da_subln, na_rpb, sw_sink, mla_q_norm, mla_w_uq, mla_kv_norm, mla_w_ukv, w_gate, w_branch, w_out, w_router, w_e_gate, w_e_up, w_e_down, g_final):
    D = D_MODEL
    BS, NS, _ = x_sample.shape
    P = cache_diff_k.shape[2]
    cond = jnp.concatenate([c_ctx[None], c, jnp.zeros((16 - 1 - BS, D), F32)], axis=0)
    mod_all = _modulation(cond, w_mod, b_mod).reshape(DEPTH, 16, 6, D)
    tables = _rope_tables(NS)
    e_place = _kpe_placement()
    gf = g_final.reshape(1, D)
    xp, xs = x_prompt, x_sample
    caches = []
    for l in range(DEPTH):
        lw = _prep_layer(l, w_in, mla_w_uq, mla_w_ukv, w_gate, w_branch, w_out, w_router)
        lp = dict(
            g_norm1=g_norm1[l].reshape(1, D), g_norm2=g_norm2[l].reshape(1, D),
            mla_q_norm=mla_q_norm[l].reshape(1, 256), mla_kv_norm=mla_kv_norm[l].reshape(1, 128),
            da_lambda=da_lambda[l], da_subln=da_subln[l].reshape(1, 64), sw_sink=sw_sink[l],
            w_e_gate=w_e_gate[l], w_e_up=w_e_up[l], w_e_down=w_e_down[l], e_place=e_place,
            na_bias=_nbr_bias_tables(na_rpb[l], NS // GRID_W))
        lam_init = 0.8 - 0.6 * math.exp(-0.3 * l)
        final = l == DEPTH - 1
        xp, cache_l = _layer(xp, mod_all[l, 0:1], lw, lp, lam_init, None, None, final, gf)
        caches.append(cache_l)
        ctx = (cache_diff_k[:, l].reshape(BS, P, 256), cache_diff_v[:, l].reshape(BS, P, 256),
               cache_na_k[:, l].reshape(BS, P, 256), cache_na_v[:, l].reshape(BS, P, 256),
               cache_swa_k[:, l].reshape(BS, P, 128), cache_swa_v[:, l].reshape(BS, P, 128),
               cache_mla_ckv[:, l], cache_mla_kpe[:, l])
        xs, _ = _layer(xs, mod_all[l, 1:1 + BS], lw, lp, lam_init, ctx, tables, final, gf)
    outs = tuple(jnp.stack([cl[i] for cl in caches], axis=1) for i in range(8))
    return (xp, xs) + outs
```

```python
import functools
import math

import numpy as np
import jax
import jax.numpy as jnp
from jax import lax
from jax.experimental import pallas as pl
from jax.experimental.pallas import tpu as pltpu

BF = jnp.bfloat16
F32 = jnp.float32

D_MODEL = 1024
DEPTH = 4
GRID_W = 64
ROPE_BASE = 10000.0
NORM_EPS = 1e-6
NEG_INF = -1e30
LOG2E = math.log2(math.e)

DA_QK = 32
NA_WIN_R = 8
NA_WIN_C = 16
SW_WINDOW = 128
MLA_NOPE = 64
MLA_ROPE = 32
N_EXPERTS = 16
EC_CAPACITY = 2

IN_PAD = 3200
LANES = 128

VMEM_BIG = 56 * 1024 * 1024
VMEM_MID = 40 * 1024 * 1024


def _cparams(sem, vmem=None):
    return pltpu.CompilerParams(dimension_semantics=sem, vmem_limit_bytes=vmem)


def _dot(a, b):
    return jnp.dot(a, b, preferred_element_type=F32)


def _dot_nt(a, b):
    return lax.dot_general(a, b, (((1,), (1,)), ((), ())), preferred_element_type=F32)


def _rms(x, g):
    var = jnp.mean(x * x, axis=-1, keepdims=True)
    return x * lax.rsqrt(var + NORM_EPS) * g


def _lane_mask(width, lo, length):
    lane = lax.broadcasted_iota(jnp.int32, (1, width), 1)
    return (lane >= lo) & (lane < lo + length)


def _mod_kernel(c_ref, w_ref, b_ref, o_ref):
    c = c_ref[...]
    s = c * (1.0 / (1.0 + jnp.exp(-c)))
    o_ref[0] = jnp.dot(s, w_ref[0], preferred_element_type=F32, precision=lax.Precision.HIGHEST) + b_ref[0]


def _modulation(cond, w_mod, b_mod):
    R, D = cond.shape
    L = w_mod.shape[0]
    nj = w_mod.shape[2] // D
    return pl.pallas_call(
        _mod_kernel,
        grid=(L, nj),
        in_specs=[
            pl.BlockSpec((R, D), lambda l, j: (0, 0)),
            pl.BlockSpec((1, D, D), lambda l, j: (l, 0, j)),
            pl.BlockSpec((1, 1, D), lambda l, j: (l, 0, j)),
        ],
        out_specs=pl.BlockSpec((1, R, D), lambda l, j: (l, 0, j)),
        out_shape=jax.ShapeDtypeStruct((L, R, nj * D), F32),
        compiler_params=_cparams(("parallel", "parallel")),
        name="modulation",
    )(cond, w_mod, b_mod.reshape(L, 1, nj * D))


def _rope(x, c, s, d):
    w = x.shape[-1]
    lane = lax.broadcasted_iota(jnp.int32, (1, LANES), 1)
    first = (lane % (2 * d)) < d
    outs = []
    for b in range(w // LANES):
        sl = slice(b * LANES, (b + 1) * LANES)
        xb = x[:, sl]
        partner = jnp.where(first, pltpu.roll(xb, LANES - d, 1), pltpu.roll(xb, d, 1))
        outs.append(xb * c[:, sl] + partner * s[:, sl])
    return outs[0] if len(outs) == 1 else jnp.concatenate(outs, axis=-1)


def _inproj_kernel(*refs, rope):
    if rope:
        (x_ref, mod_ref, g1_ref, w_ref, qn_ref, wuq_ref, kvn_ref,
         tac, tas, tcc, tcs, tqc, tqs, tkc, tks,
         oa, ob, oc, oq, ockv, okpe) = refs
    else:
        (x_ref, mod_ref, g1_ref, w_ref, qn_ref, wuq_ref, kvn_ref,
         oa, ob, oc, oq, ockv, okpe) = refs
    dt = oa.dtype
    x = x_ref[0]
    h = (_rms(x, g1_ref[...]) * (1.0 + mod_ref[0, 1:2, :]) + mod_ref[0, 0:1, :]).astype(BF)

    def seg(lo, w):
        return _dot(h, w_ref[:, lo:lo + w])

    ua = seg(0, 1024)
    qk = ua[:, :512]
    if rope:
        qk = _rope(qk, tac[...], tas[...], 8)
    oa[0, :, 0:256] = (qk[:, :256] * (DA_QK ** -0.5 * LOG2E)).astype(dt)
    oa[0, :, 256:512] = qk[:, 256:].astype(dt)
    ones_lane = lax.broadcasted_iota(jnp.int32, (1, 512), 1) % LANES >= 64
    oa[0, :, 512:1024] = jnp.where(ones_lane, 1.0, ua[:, 512:]).astype(dt)
    ub = seg(1024, 1024)
    ob[0, :, 0:256] = (ub[:, :256] * (0.125 * LOG2E)).astype(dt)
    ob[0, :, 256:512] = ub[:, 256:512].astype(dt)
    ob[0, :, 512:1024] = jnp.where(ones_lane, 1.0, ub[:, 512:]).astype(dt)
    uc = seg(2048, 640)
    qk = jnp.concatenate([uc[:, :256], uc[:, 512:]], axis=-1)
    if rope:
        qk = _rope(qk, tcc[...], tcs[...], 16)
    oc[0, :, 0:256] = (qk[:, :256] * (0.125 * LOG2E)).astype(dt)
    oc[0, :, 256:512] = jnp.where(ones_lane[:, :256], 1.0, uc[:, 256:512]).astype(dt)
    oc[0, :, 512:640] = qk[:, 256:].astype(dt)
    cq = _rms(seg(2688, 256), qn_ref[...]).astype(BF)
    qd = _dot(cq, wuq_ref[...])
    if rope:
        qd = _rope(qd, tqc[...], tqs[...], 8)
    oq[0] = (qd * ((MLA_NOPE + MLA_ROPE) ** -0.5 * LOG2E)).astype(dt)
    ockv[0] = _rms(seg(2944, 128), kvn_ref[...]).astype(dt)
    kpe = seg(3072, 128)
    if rope:
        kpe = _rope(kpe, tkc[...], tks[...], 8)
    okpe[0] = kpe.astype(dt)


def _in_proj(x, mod, g1, w_in_p, q_norm, w_uq_p, kv_norm, tables, out_dtype, tm):
    B, N, D = x.shape
    rope = tables is not None
    nmod = mod.shape[0]
    mod_map = (lambda j, b: (b, 0, 0)) if nmod > 1 else (lambda j, b: (0, 0, 0))
    const2 = lambda j, b: (0, 0)
    in_specs = [
        pl.BlockSpec((1, tm, D), lambda j, b: (b, j, 0)),
        pl.BlockSpec((1, 6, D), mod_map),
        pl.BlockSpec((1, D), const2),
        pl.BlockSpec((D, IN_PAD), const2),
        pl.BlockSpec((1, 256), const2),
        pl.BlockSpec((256, 512), const2),
        pl.BlockSpec((1, 128), const2),
    ]
    args = [x, mod, g1, w_in_p, q_norm, w_uq_p, kv_norm]
    if rope:
        for t in tables:
            in_specs.append(pl.BlockSpec((tm, t.shape[1]), lambda j, b: (j, 0)))
            args.append(t)
    widths = (1024, 1024, 640, 512, 128, 128)
    out_specs = [pl.BlockSpec((1, tm, w), lambda j, b: (b, j, 0)) for w in widths]
    out_shape = [jax.ShapeDtypeStruct((B, N, w), out_dtype) for w in widths]
    return pl.pallas_call(
        functools.partial(_inproj_kernel, rope=rope),
        grid=(N // tm, B),
        in_specs=in_specs,
        out_specs=out_specs,
        out_shape=out_shape,
        compiler_params=_cparams(("parallel", "parallel"), VMEM_BIG),
        name="in_proj_rope" if rope else "in_proj",
    )(*args)


def _mla_expand_kernel(ckv_ref, kpe_ref, wk_ref, wv_ref, e_ref, ok, ov):
    ckv = ckv_ref[0].astype(BF)
    kpe = kpe_ref[0].astype(BF)
    ok[0] = (_dot(ckv, wk_ref[...]) + _dot(kpe, e_ref[...])).astype(ok.dtype)
    ones_lane = lax.broadcasted_iota(jnp.int32, (1, 512), 1) % LANES >= 64
    ov[0] = jnp.where(ones_lane, 1.0, _dot(ckv, wv_ref[...])).astype(ov.dtype)


def _mla_expand(ckv, kpe, w_k, w_v, e_place, tk):
    B, K, _ = ckv.shape
    const2 = lambda b, j: (0, 0)
    return pl.pallas_call(
        _mla_expand_kernel,
        grid=(B, K // tk),
        in_specs=[
            pl.BlockSpec((1, tk, 128), lambda b, j: (b, j, 0)),
            pl.BlockSpec((1, tk, 128), lambda b, j: (b, j, 0)),
            pl.BlockSpec((128, 512), const2),
            pl.BlockSpec((128, 512), const2),
            pl.BlockSpec((128, 512), const2),
        ],
        out_specs=[pl.BlockSpec((1, tk, 512), lambda b, j: (b, j, 0)),
                   pl.BlockSpec((1, tk, 512), lambda b, j: (b, j, 0))],
        out_shape=[jax.ShapeDtypeStruct((B, K, 512), BF), jax.ShapeDtypeStruct((B, K, 512), BF)],
        compiler_params=_cparams(("parallel", "parallel")),
        name="mla_expand",
    )(ckv, kpe, w_k, w_v, e_place)


def _head_softmax(scores, values):
    m = functools.reduce(jnp.maximum, [jnp.max(s, axis=-1, keepdims=True) for s in scores])
    acc = None
    for s, v in zip(scores, values):
        part = _dot(jnp.exp2(s - m).astype(BF), v)
        acc = part if acc is None else acc + part
    return acc, m


def _normalise(acc):
    return acc * pltpu.roll(1.0 / acc, 64, 1)


def _store_heads(o_ref, heads):
    half = _lane_mask(LANES, 0, 64)
    for hp in range(len(heads) // 2):
        slab = jnp.where(half, heads[2 * hp], pltpu.roll(heads[2 * hp + 1], 64, 1))
        o_ref[0, :, hp * LANES:(hp + 1) * LANES] = slab.astype(o_ref.dtype)


def _dense_attn_kernel(*refs, groups, tk, n_chunks, has_sink, diff):
    it = iter(refs)
    q_ref, k_ref, v_ref = next(it), next(it), next(it)
    sink_ref = lam_ref = subln_ref = None
    if has_sink:
        sink_ref = next(it)
    if diff:
        lam_ref, subln_ref, lam_init_ref = next(it), next(it), next(it)
    o_ref = next(it)
    tq = q_ref.shape[1]

    lam = lam_init = None
    if diff:
        lam_init = lam_init_ref[0]
        dl = lam_ref[...]
        lam = (jnp.exp(jnp.sum(dl[0:1] * dl[1:2], axis=-1, keepdims=True))
               - jnp.exp(jnp.sum(dl[2:3] * dl[3:4], axis=-1, keepdims=True)) + lam_init)

    ng = len(groups)
    scratch = list(it)
    qg_s, m_s, acc_s = (scratch[j * ng:(j + 1) * ng] for j in range(3))
    for gi, (q_lo, k_lo, w, qmask, v_lo, head, sink_idx, sign) in enumerate(groups):
        qg = q_ref[0, :, q_lo:q_lo + w].astype(BF)
        if qmask is not None:
            qg = jnp.where(_lane_mask(w, *qmask), qg, jnp.zeros_like(qg))
        qg_s[gi][...] = qg
        m_s[gi][...] = jnp.full((tq, 1), NEG_INF, F32)
        acc_s[gi][...] = jnp.zeros((tq, LANES), F32)

    def keys_step(c):
        off = c * tk
        for gi, (q_lo, k_lo, w, qmask, v_lo, head, sink_idx, sign) in enumerate(groups):
            s = _dot_nt(qg_s[gi][...], k_ref[0, pl.ds(off, tk), k_lo:k_lo + w].astype(BF))
            m = m_s[gi][...]
            m_new = jnp.maximum(m, jnp.max(s, axis=-1, keepdims=True))
            p = jnp.exp2(s - m_new).astype(BF)
            pv = _dot(p, v_ref[0, pl.ds(off, tk), v_lo:v_lo + LANES].astype(BF))
            acc_s[gi][...] = jnp.exp2(m - m_new) * acc_s[gi][...] + pv
            m_s[gi][...] = m_new

    for c in range(n_chunks):
        keys_step(c)

    heads = {}
    ones_half = _lane_mask(LANES, 64, 64)
    for gi, (q_lo, k_lo, w, qmask, v_lo, head, sink_idx, sign) in enumerate(groups):
        acc = acc_s[gi][...]
        if sink_idx is not None:
            acc = acc + jnp.where(ones_half, jnp.exp2(sink_ref[sink_idx] * LOG2E - m_s[gi][...]), 0.0)
        part = _normalise(acc)
        if sign < 0:
            part = -lam * part
        heads[head] = part if head not in heads else heads[head] + part
    if diff:
        half = _lane_mask(LANES, 0, 64)
        for hd, o in heads.items():
            ms = jnp.sum(jnp.where(half, o * o, 0.0), axis=-1, keepdims=True) * (1.0 / 64.0)
            heads[hd] = o * lax.rsqrt(ms + NORM_EPS) * subln_ref[...] * (1.0 - lam_init)
    _store_heads(o_ref, heads)


def _dense_attn(q_arr, q_blk, k_arr, k_blk, v_arr, v_blk, *, groups, tq, tk, sink=None, diff=None, name):
    B, N, _ = q_arr.shape
    K = k_arr.shape[1]
    n_chunks = K // tk
    assert n_chunks * tk == K and n_chunks <= 2
    qw, qi = q_blk
    kw, ki = k_blk
    vw, vi = v_blk
    in_specs = [
        pl.BlockSpec((1, tq, qw), lambda b, i: (b, i, qi)),
        pl.BlockSpec((1, K, kw), lambda b, i: (b, 0, ki)),
        pl.BlockSpec((1, K, vw), lambda b, i: (b, 0, vi)),
    ]
    args = [q_arr, k_arr, v_arr]
    if sink is not None:
        in_specs.append(pl.BlockSpec(memory_space=pltpu.SMEM))
        args.append(sink)
    if diff is not None:
        da_lambda, subln, lam_init = diff
        in_specs += [pl.BlockSpec((4, DA_QK), lambda b, i: (0, 0)),
                     pl.BlockSpec((1, LANES), lambda b, i: (0, 0)),
                     pl.BlockSpec(memory_space=pltpu.SMEM)]
        args += [da_lambda, subln, jnp.full((1,), lam_init, F32)]
    kern = functools.partial(_dense_attn_kernel, groups=groups, tk=tk, n_chunks=n_chunks,
                             has_sink=sink is not None, diff=diff is not None)
    scratch = ([pltpu.VMEM((tq, g[2]), BF) for g in groups] + [pltpu.VMEM((tq, 1), F32) for _ in groups]
               + [pltpu.VMEM((tq, LANES), F32) for _ in groups])
    return pl.pallas_call(
        kern,
        grid=(B, N // tq),
        in_specs=in_specs,
        out_specs=pl.BlockSpec((1, tq, 256), lambda b, i: (b, i, 0)),
        out_shape=jax.ShapeDtypeStruct((B, N, 256), BF),
        scratch_shapes=scratch,
        compiler_params=_cparams(("parallel", "parallel"), VMEM_BIG),
        name=name,
    )(*args)


GROUPS_A = tuple((0, 0, 256, (32 * g, 32), 128 * (g // 2), g // 2, None, 1 if g % 2 == 0 else -1) for g in range(8))
GROUPS_B = tuple((0, 0, 256, (64 * h, 64), 128 * h, h, None, 1) for h in range(4))
GROUPS_C = tuple((128 * g, 0, 128, (64 * j, 64), 128 * j, 2 * g + j, 2 * j + g, 1) for g in range(2) for j in range(2))
GROUPS_D = tuple((128 * h, 128 * h, 128, None, 128 * h, h, None, 1) for h in range(4))


def _window_attn_kernel(q_ref, k_ref, v_ref, ck_ref, cv_ref, sink_ref, o_ref, *, n, band):
    tq = q_ref.shape[1]
    i = pl.program_id(1)
    start = jnp.clip(i * tq - SW_WINDOW, 0, n - band)
    start = pl.multiple_of(start, 128)
    kb = k_ref[0, pl.ds(start, band), :]
    vb = v_ref[0, pl.ds(start, band), :]
    ck = ck_ref[0]
    cv = cv_ref[0]
    qpos = i * tq + lax.broadcasted_iota(jnp.int32, (tq, band), 0)
    kpos = start + lax.broadcasted_iota(jnp.int32, (tq, band), 1)
    valid = jnp.abs(qpos - kpos) <= SW_WINDOW
    ones_half = _lane_mask(LANES, 64, 64)
    heads = {}
    for g in range(2):
        for j in range(2):
            qg = q_ref[0, :, LANES * g:LANES * (g + 1)]
            qg = jnp.where(_lane_mask(LANES, 64 * j, 64), qg, jnp.zeros_like(qg))
            vs = slice(LANES * j, LANES * (j + 1))
            acc, m = _head_softmax([_dot_nt(qg, ck), jnp.where(valid, _dot_nt(qg, kb), NEG_INF)],
                                   [cv[:, vs], vb[:, vs]])
            acc = acc + jnp.where(ones_half, jnp.exp2(sink_ref[2 * j + g] * LOG2E - m), 0.0)
            heads[2 * g + j] = _normalise(acc)
    _store_heads(o_ref, heads)


def _window_attn(qkv_c, ck, cv, sink, tq):
    B, N, _ = qkv_c.shape
    band = tq + 2 * SW_WINDOW
    P = ck.shape[1]
    return pl.pallas_call(
        functools.partial(_window_attn_kernel, n=N, band=band),
        grid=(B, N // tq),
        in_specs=[
            pl.BlockSpec((1, tq, 256), lambda b, i: (b, i, 0)),
            pl.BlockSpec((1, N, 128), lambda b, i: (b, 0, 4)),
            pl.BlockSpec((1, N, 256), lambda b, i: (b, 0, 1)),
            pl.BlockSpec((1, P, 128), lambda b, i: (b, 0, 0)),
            pl.BlockSpec((1, P, 256), lambda b, i: (b, 0, 0)),
            pl.BlockSpec(memory_space=pltpu.SMEM),
        ],
        out_specs=pl.BlockSpec((1, tq, 256), lambda b, i: (b, i, 0)),
        out_shape=jax.ShapeDtypeStruct((B, N, 256), BF),
        compiler_params=_cparams(("parallel", "parallel")),
        name="window_attn",
    )(qkv_c, qkv_c, qkv_c, ck, cv, sink)


NA_TILE_ROWS = 4
NA_KEY_ROWS = NA_TILE_ROWS + NA_WIN_R


def _nbr_attn_kernel(q_ref, k_ref, v_ref, ck_ref, cv_ref, bias_ref, o_ref, *, rows):
    i = pl.program_id(1)
    krow = jnp.clip(i * NA_TILE_ROWS - NA_WIN_R // 2, 0, rows - NA_KEY_ROWS)
    start = pl.multiple_of(krow * GRID_W, GRID_W)
    nk = NA_KEY_ROWS * GRID_W
    kb = k_ref[0, pl.ds(start, nk), :]
    vb = v_ref[0, pl.ds(start, nk), :]
    ck = ck_ref[0]
    cv = cv_ref[0]
    q = q_ref[0]
    heads = {}
    for h in range(4):
        qg = jnp.where(_lane_mask(256, 64 * h, 64), q, jnp.zeros_like(q))
        vs = slice(LANES * h, LANES * (h + 1))
        acc, _ = _head_softmax([_dot_nt(qg, ck), _dot_nt(qg, kb) + bias_ref[0, h]], [cv[:, vs], vb[:, vs]])
        heads[h] = _normalise(acc)
    _store_heads(o_ref, heads)


def _nbr_attn(qkv_b, ck, cv, bias):
    B, N, _ = qkv_b.shape
    rows = N // GRID_W
    tq = NA_TILE_ROWS * GRID_W
    nt = N // tq
    nk = NA_KEY_ROWS * GRID_W
    P = ck.shape[1]

    def bias_map(b, i):
        return (jnp.where(i == 0, 0, jnp.where(i == nt - 1, 2, 1)), 0, 0, 0)

    return pl.pallas_call(
        functools.partial(_nbr_attn_kernel, rows=rows),
        grid=(B, nt),
        in_specs=[
            pl.BlockSpec((1, tq, 256), lambda b, i: (b, i, 0)),
            pl.BlockSpec((1, N, 256), lambda b, i: (b, 0, 1)),
            pl.BlockSpec((1, N, 512), lambda b, i: (b, 0, 1)),
            pl.BlockSpec((1, P, 256), lambda b, i: (b, 0, 0)),
            pl.BlockSpec((1, P, 512), lambda b, i: (b, 0, 0)),
            pl.BlockSpec((1, 4, tq, nk), bias_map),
        ],
        out_specs=pl.BlockSpec((1, tq, 256), lambda b, i: (b, i, 0)),
        out_shape=jax.ShapeDtypeStruct((B, N, 256), BF),
        compiler_params=_cparams(("parallel", "parallel"), VMEM_MID),
        name="nbr_attn",
    )(qkv_b, qkv_b, qkv_b, ck, cv, bias)


def _nbr_bias_tables(rpb, rows):
    H = rpb.shape[0]
    rp = jnp.pad(rpb.astype(F32), ((0, 0), (NA_KEY_ROWS, NA_KEY_ROWS), (GRID_W, GRID_W)))
    c0 = NA_WIN_C - 1 + GRID_W
    tcol = jnp.stack([rp[:, :, c0 - qc:c0 - qc + GRID_W] for qc in range(GRID_W)], axis=2)
    tabs = []
    nt = rows // NA_TILE_ROWS
    for r0 in (0, NA_TILE_ROWS * min(1, nt - 1) + NA_TILE_ROWS, rows - NA_TILE_ROWS):
        ks = int(np.clip(r0 - NA_WIN_R // 2, 0, rows - NA_KEY_ROWS))
        per_row = []
        for a in range(NA_TILE_ROWS):
            s0 = ks - (r0 + a) + NA_WIN_R - 1 + NA_KEY_ROWS
            per_row.append(tcol[:, s0:s0 + NA_KEY_ROWS])
        t = jnp.stack(per_row, axis=1).transpose(0, 1, 3, 2, 4)
        r = r0 + np.arange(NA_TILE_ROWS)[:, None, None, None]
        qc = np.arange(GRID_W)[None, :, None, None]
        kr = ks + np.arange(NA_KEY_ROWS)[None, None, :, None]
        kc = np.arange(GRID_W)[None, None, None, :]
        rs = np.clip(r - NA_WIN_R // 2, 0, rows - NA_WIN_R)
        cs = np.clip(qc - NA_WIN_C // 2, 0, GRID_W - NA_WIN_C)
        ok = (kr >= rs) & (kr < rs + NA_WIN_R) & (kc >= cs) & (kc < cs + NA_WIN_C)
        t = jnp.where(ok[None], t * LOG2E, NEG_INF)
        tabs.append(t.reshape(H, NA_TILE_ROWS * GRID_W, NA_KEY_ROWS * GRID_W))
    return jnp.stack(tabs)


def _merge_kernel(x_ref, mod_ref, g1_ref, g2_ref, oa, ob, oc, od, wg_ref, wb_ref, wo_ref, wr_ref,
                  xo_ref, h2_ref, lg_ref):
    D = x_ref.shape[2]
    x = x_ref[0]
    h = (_rms(x, g1_ref[...]) * (1.0 + mod_ref[0, 1:2, :]) + mod_ref[0, 0:1, :]).astype(BF)
    merged = None
    for i, o_ref in enumerate((oa, ob, oc, od)):
        z = _dot(h, wg_ref[:, i * D:(i + 1) * D])
        gate = 1.0 / (1.0 + jnp.exp(-z))
        term = gate * _dot(o_ref[0], wb_ref[i])
        merged = term if merged is None else merged + term
    mixed = _dot(merged.astype(BF), wo_ref[...])
    xn = x + mod_ref[0, 2:3, :] * mixed
    xo_ref[0] = xn
    h2 = (_rms(xn, g2_ref[...]) * (1.0 + mod_ref[0, 4:5, :]) + mod_ref[0, 3:4, :]).astype(BF)
    h2_ref[0] = h2
    lg_ref[0] = _dot_nt(wr_ref[...], h2)


def _merge(x, mod, g1, g2, o_a, o_b, o_c, o_d, w_gate, w_branch, w_out, w_router_p, tm):
    B, N, D = x.shape
    nmod = mod.shape[0]
    mod_map = (lambda b, j: (b, 0, 0)) if nmod > 1 else (lambda b, j: (0, 0, 0))
    tok = lambda w: pl.BlockSpec((1, tm, w), lambda b, j: (b, j, 0))
    c2 = lambda b, j: (0, 0)
    return pl.pallas_call(
        _merge_kernel,
        grid=(B, N // tm),
        in_specs=[
            tok(D), pl.BlockSpec((1, 6, D), mod_map), pl.BlockSpec((1, D), c2), pl.BlockSpec((1, D), c2),
            tok(256), tok(256), tok(256), tok(256),
            pl.BlockSpec((D, 4 * D), c2),
            pl.BlockSpec((4, 256, D), lambda b, j: (0, 0, 0)),
            pl.BlockSpec((D, D), c2),
            pl.BlockSpec((N_EXPERTS, D), c2),
        ],
        out_specs=[tok(D), tok(D), pl.BlockSpec((1, N_EXPERTS, tm), lambda b, j: (b, 0, j))],
        out_shape=[jax.ShapeDtypeStruct((B, N, D), F32), jax.ShapeDtypeStruct((B, N, D), BF),
                   jax.ShapeDtypeStruct((B, N_EXPERTS, N), F32)],
        compiler_params=_cparams(("parallel", "parallel"), VMEM_BIG),
        name="merge",
    )(x, mod, g1, g2, o_a, o_b, o_c, o_d, w_gate, w_branch, w_out, w_router_p)


MOE_TT = 256
MOE_RC = 64
SLOT_ALIGN = 16


def _excl_cumsum(x, tri, reset_every):
    R, N = x.shape
    outs = []
    carry = jnp.zeros((R, 1), F32)
    for c in range(N // LANES):
        if reset_every and (c * LANES) % reset_every == 0:
            carry = jnp.zeros((R, 1), F32)
        xc = x[:, c * LANES:(c + 1) * LANES]
        inc = _dot(xc.astype(BF), tri)
        outs.append(inc - xc + carry)
        carry = carry + inc[:, LANES - 1:LANES]
    return (outs[0] if len(outs) == 1 else jnp.concatenate(outs, axis=1)), carry


def _route_kernel(lg_ref, aff_ref, rel_ref, base_ref, *, cap, tt):
    B, E, N = lg_ref.shape
    R = B * E
    nt = N // tt
    lg = lg_ref[...]
    ex = jnp.exp(lg - jnp.max(lg, axis=1, keepdims=True))
    aff = ex / jnp.sum(ex, axis=1, keepdims=True)
    aff_ref[...] = aff
    bits = pltpu.bitcast(aff.reshape(R, N), jnp.int32)

    def per_expert(v):
        t = v[0:E]
        for b in range(1, B):
            t = t + v[b * E:(b + 1) * E]
        return t

    def per_row(v):
        return jnp.concatenate([v] * B, axis=0) if B > 1 else v

    def batch_prefix(v):
        parts, run = [], jnp.zeros((E, 1), F32)
        for b in range(B):
            parts.append(run)
            run = run + v[b * E:(b + 1) * E]
        return jnp.concatenate(parts, axis=0) if B > 1 else parts[0]

    def count(mask):
        return per_expert(jnp.sum(jnp.where(mask, 1.0, 0.0), axis=1, keepdims=True))

    def search(i, thr):
        cand = thr | jnp.left_shift(jnp.int32(1), 30 - i)
        return jnp.where(count(bits >= per_row(cand)) >= float(cap), cand, thr)

    thr = per_row(lax.fori_loop(0, 31, search, jnp.zeros((E, 1), jnp.int32)))
    gt = bits > thr
    eq = bits == thr
    need = float(cap) - count(gt)
    ri = lax.broadcasted_iota(jnp.int32, (LANES, LANES), 0)
    ci = lax.broadcasted_iota(jnp.int32, (LANES, LANES), 1)
    tri = jnp.where(ri <= ci, 1.0, 0.0).astype(BF)
    rank_eq, eq_tot = _excl_cumsum(jnp.where(eq, 1.0, 0.0), tri, None)
    rank_eq = rank_eq + batch_prefix(eq_tot)
    sel = gt | (eq & (rank_eq < per_row(need)))
    self = jnp.where(sel, 1.0, 0.0)
    local, _ = _excl_cumsum(self, tri, tt)
    rel_ref[...] = jnp.where(sel, local, -1.0).astype(jnp.int32).reshape(B, E, N)
    tn = jnp.right_shift(lax.broadcasted_iota(jnp.int32, (N, LANES), 0), tt.bit_length() - 1)
    tj = lax.broadcasted_iota(jnp.int32, (N, LANES), 1)
    cnt = _dot(self.astype(BF), jnp.where(tn == tj, 1.0, 0.0).astype(BF)).astype(jnp.int32)
    sh = SLOT_ALIGN.bit_length() - 1
    ru = jnp.left_shift(jnp.right_shift(cnt + (SLOT_ALIGN - 1), sh), sh).astype(F32)
    base_b = _dot(ru.astype(BF), jnp.where(ri < ci, 1.0, 0.0).astype(BF))
    base = base_b + batch_prefix(base_b[:, nt:nt + 1])
    base_ref[...] = base.astype(jnp.int32).reshape(B, E, LANES)


def _route(lg_t, cap):
    B, E, N = lg_t.shape
    full = lambda s: pl.BlockSpec(s, lambda i: (0,) * len(s))
    return pl.pallas_call(
        functools.partial(_route_kernel, cap=cap, tt=MOE_TT),
        grid=(1,),
        in_specs=[full((B, E, N))],
        out_specs=[full((B, E, N)), full((B, E, N)), full((B, E, LANES))],
        out_shape=[jax.ShapeDtypeStruct((B, E, N), F32), jax.ShapeDtypeStruct((B, E, N), jnp.int32),
                   jax.ShapeDtypeStruct((B, E, LANES), jnp.int32)],
        compiler_params=_cparams(("arbitrary",), VMEM_BIG),
        name="route",
    )(lg_t)


def _slot_onehot(rel, values, c, rc):
    E, tt = rel.shape
    r = lax.broadcasted_iota(jnp.int32, (E, rc, tt), 1) + c * rc
    return jnp.where(rel[:, None, :] == r, values[:, None, :], 0.0).astype(BF).reshape(E * rc, tt)


def _dispatch_kernel(base_sm, x_ref, rel_ref, xe_in, xe_hbm, stage, stage2, sem, sem2, *, nt, rc):
    del xe_in
    b, j = pl.program_id(0), pl.program_id(1)
    E = rel_ref.shape[1]
    step = b * nt + j
    last = pl.num_programs(0) * nt - 1
    base = lambda e, jj: base_sm[(b * E + e) * (nt + 1) + jj]
    rel = rel_ref[0]
    x = x_ref[0]
    ones = jnp.ones(rel.shape, F32)

    def copy(buf, s, e, c):
        start = pl.multiple_of(base(e, j) + c * rc, SLOT_ALIGN)
        return pltpu.make_async_copy(buf.at[pl.ds(e * rc, rc)], xe_hbm.at[e, pl.ds(start, rc)], s.at[e])

    strip = _dot(_slot_onehot(rel, ones, 0, rc), x).astype(BF)

    @pl.when(step > 0)
    def _():
        for e in range(E):
            copy(stage, sem, e, 0).wait()

    stage[...] = strip
    for e in range(E):
        copy(stage, sem, e, 0).start()

    span = [base(e, j + 1) - base(e, j) for e in range(E)]
    widest = functools.reduce(jnp.maximum, span)

    def extra(c, carry):
        stage2[...] = _dot(_slot_onehot(rel, ones, c, rc), x).astype(BF)
        for e in range(E):
            @pl.when(span[e] > c * rc)
            def _():
                cp = copy(stage2, sem2, e, c)
                cp.start()
                cp.wait()
        return carry

    lax.fori_loop(1, (widest + rc - 1) // rc, extra, 0)

    @pl.when(step == last)
    def _():
        for e in range(E):
            copy(stage, sem, e, 0).wait()


def _dispatch(h2, rel, base_flat, rows):
    B, N, D = h2.shape
    E = rel.shape[1]
    nt = N // MOE_TT
    xe0 = jnp.zeros((E, rows, D), BF)
    return pl.pallas_call(
        functools.partial(_dispatch_kernel, nt=nt, rc=MOE_RC),
        grid_spec=pltpu.PrefetchScalarGridSpec(
            num_scalar_prefetch=1,
            grid=(B, nt),
            in_specs=[
                pl.BlockSpec((1, MOE_TT, D), lambda b, j, s: (b, j, 0)),
                pl.BlockSpec((1, E, MOE_TT), lambda b, j, s: (b, 0, j)),
                pl.BlockSpec(memory_space=pl.ANY),
            ],
            out_specs=pl.BlockSpec(memory_space=pl.ANY),
            scratch_shapes=[pltpu.VMEM((E * MOE_RC, D), BF), pltpu.VMEM((E * MOE_RC, D), BF),
                            pltpu.SemaphoreType.DMA((E,)), pltpu.SemaphoreType.DMA((E,))],
        ),
        out_shape=jax.ShapeDtypeStruct((E, rows, D), BF),
        input_output_aliases={3: 0},
        compiler_params=_cparams(("arbitrary", "arbitrary")),
        name="dispatch",
    )(base_flat, h2, rel, xe0)


def _ffn_kernel(tot_sm, x_ref, wg_ref, wu_ref, wd_ref, y_ref, wg_s, wu_s, wd_s, *, tm, rc):
    e, j = pl.program_id(0), pl.program_id(1)

    @pl.when(j == 0)
    def _():
        wg_s[...] = wg_ref[0].astype(BF)
        wu_s[...] = wu_ref[0].astype(BF)
        wd_s[...] = wd_ref[0].astype(BF)

    used = j * tm < tot_sm[e] + rc

    @pl.when(used)
    def _():
        x = x_ref[0]
        hg = _dot(x, wg_s[...])
        hu = _dot(x, wu_s[...])
        act = (hg * (1.0 / (1.0 + jnp.exp(-hg))) * hu).astype(BF)
        y_ref[0] = _dot(act, wd_s[...]).astype(y_ref.dtype)

    @pl.when(jnp.logical_not(used))
    def _():
        y_ref[...] = jnp.zeros_like(y_ref)


def _expert_ffn(xe, total, w_g, w_u, w_d, tm):
    E, rows, D = xe.shape
    DE = w_g.shape[2]
    tok = lambda e, j, tot: (e, jnp.minimum(j, (tot[e] + MOE_RC - 1) // tm), 0)
    wmap = lambda e, j, tot: (e, 0, 0)
    return pl.pallas_call(
        functools.partial(_ffn_kernel, tm=tm, rc=MOE_RC),
        grid_spec=pltpu.PrefetchScalarGridSpec(
            num_scalar_prefetch=1,
            grid=(E, rows // tm),
            in_specs=[pl.BlockSpec((1, tm, D), tok), pl.BlockSpec((1, D, DE), wmap),
                      pl.BlockSpec((1, D, DE), wmap), pl.BlockSpec((1, DE, D), wmap)],
            out_specs=pl.BlockSpec((1, tm, D), lambda e, j, tot: (e, j, 0)),
            scratch_shapes=[pltpu.VMEM((D, DE), BF), pltpu.VMEM((D, DE), BF), pltpu.VMEM((DE, D), BF)],
        ),
        out_shape=jax.ShapeDtypeStruct((E, rows, D), BF),
        compiler_params=_cparams(("arbitrary", "arbitrary"), VMEM_BIG),
        name="expert_ffn",
    )(total, xe, w_g, w_u, w_d)


def _combine_kernel(base_sm, x_ref, rel_ref, aff_ref, mod_ref, gf_ref, ye_hbm, o_ref, stage, sem, *,
                    nt, rc, final):
    b, j = pl.program_id(0), pl.program_id(1)
    E = rel_ref.shape[1]
    step = b * nt + j
    nsteps = pl.num_programs(0) * nt
    slot = step % 2
    base = lambda e, jj: base_sm[(b * E + e) * (nt + 1) + jj]
    rel = rel_ref[0]
    aff = aff_ref[0]

    def copy(st, sl, e, c):
        first = base_sm[(st // nt * E + e) * (nt + 1) + st % nt]
        start = pl.multiple_of(first + c * rc, SLOT_ALIGN)
        return pltpu.make_async_copy(ye_hbm.at[e, pl.ds(start, rc)], stage.at[sl, pl.ds(e * rc, rc)],
                                     sem.at[sl, e])

    def scatter(c):
        w = _slot_onehot(rel, aff, c, rc)
        return lax.dot_general(w, stage[slot], (((0,), (0,)), ((), ())), preferred_element_type=F32)

    @pl.when(step == 0)
    def _():
        for e in range(E):
            copy(step, slot, e, 0).start()

    @pl.when(step + 1 < nsteps)
    def _():
        for e in range(E):
            copy(step + 1, 1 - slot, e, 0).start()

    for e in range(E):
        copy(step, slot, e, 0).wait()
    acc = scatter(0)
    span = [base(e, j + 1) - base(e, j) for e in range(E)]
    widest = functools.reduce(jnp.maximum, span)

    def extra(c, acc):
        for e in range(E):
            @pl.when(span[e] > c * rc)
            def _():
                cp = copy(step, slot, e, c)
                cp.start()
                cp.wait()
        return acc + scatter(c)

    acc = lax.fori_loop(1, (widest + rc - 1) // rc, extra, acc)
    xn = x_ref[0] + mod_ref[0, 5:6, :] * acc
    if final:
        xn = _rms(xn, gf_ref[...])
    o_ref[0] = xn


def _combine(x, rel, aff, mod, g_final, ye, base_flat, final):
    B, N, D = x.shape
    E = rel.shape[1]
    nt = N // MOE_TT
    nmod = mod.shape[0]
    mod_map = (lambda b, j, s: (b, 0, 0)) if nmod > 1 else (lambda b, j, s: (0, 0, 0))
    tok = pl.BlockSpec((1, MOE_TT, D), lambda b, j, s: (b, j, 0))
    slots = pl.BlockSpec((1, E, MOE_TT), lambda b, j, s: (b, 0, j))
    return pl.pallas_call(
        functools.partial(_combine_kernel, nt=nt, rc=MOE_RC, final=final),
        grid_spec=pltpu.PrefetchScalarGridSpec(
            num_scalar_prefetch=1,
            grid=(B, nt),
            in_specs=[tok, slots, slots, pl.BlockSpec((1, 6, D), mod_map),
                      pl.BlockSpec((1, D), lambda b, j, s: (0, 0)), pl.BlockSpec(memory_space=pl.ANY)],
            out_specs=tok,
            scratch_shapes=[pltpu.VMEM((2, E * MOE_RC, D), BF), pltpu.SemaphoreType.DMA((2, E))],
        ),
        out_shape=jax.ShapeDtypeStruct((B, N, D), F32),
        compiler_params=_cparams(("arbitrary", "arbitrary")),
        name="combine_final" if final else "combine",
    )(base_flat, x, rel, aff, mod, g_final, ye)


def _moe(x, h2, lg_t, mod, g_final, w_g, w_u, w_d, final):
    B, N, D = h2.shape
    T = B * N
    nt = N // MOE_TT
    cap = EC_CAPACITY * T // N_EXPERTS
    tm = 512 if cap >= 2048 else 256
    worst = cap + (SLOT_ALIGN - 1) * (T // MOE_TT) + MOE_RC
    rows = -(-worst // tm) * tm
    aff, rel, base = _route(lg_t, cap)
    base_flat = base[:, :, :nt + 1].reshape(-1)
    total = base[B - 1, :, nt]
    xe = _dispatch(h2, rel, base_flat, rows)
    ye = _expert_ffn(xe, total, w_g, w_u, w_d, tm)
    return _combine(x, rel, aff, mod, g_final, ye, base_flat, final)


def _rope_tables(n):
    t = jnp.arange(n)
    pos = {1: (t // GRID_W).astype(F32), 2: (t % GRID_W).astype(F32)}

    def group(quarter):
        freqs = jnp.float32(ROPE_BASE) ** (-jnp.arange(quarter, dtype=F32) / quarter)
        cs, ss = [], []
        for kind in (1, 2):
            ang = pos[kind][:, None] * freqs
            c, s = jnp.cos(ang), jnp.sin(ang)
            cs += [c, c]
            ss += [-s, s]
        return jnp.concatenate(cs, axis=1), jnp.concatenate(ss, axis=1)

    c32, s32 = group(8)
    c64, s64 = group(16)
    one = lambda w: jnp.ones((n, w), F32)
    zero = lambda w: jnp.zeros((n, w), F32)
    ta = (jnp.tile(c32, (1, 16)), jnp.tile(s32, (1, 16)))
    tc = (jnp.tile(c64, (1, 6)), jnp.tile(s64, (1, 6)))
    qc = jnp.concatenate([one(64), c32, one(32)], axis=1)
    qs = jnp.concatenate([zero(64), s32, zero(32)], axis=1)
    tq = (jnp.tile(qc, (1, 4)), jnp.tile(qs, (1, 4)))
    tk = (jnp.concatenate([c32, one(96)], axis=1), jnp.concatenate([s32, zero(96)], axis=1))
    return (ta[0], ta[1], tc[0], tc[1], tq[0], tq[1], tk[0], tk[1])


LAT_TQ = 512
C_HEAD_ORDER = (0, 2, 1, 3)


def _prep_layer(l, w_in, mla_w_uq, mla_w_ukv, w_gate, w_branch, w_out, w_router):
    D = D_MODEL
    wi = w_in[l]
    cq = wi[:, 1536:1792].reshape(D, 4, 64)[:, C_HEAD_ORDER, :].reshape(D, 256)
    def spread(cols, heads):
        return jnp.pad(cols.reshape(D, heads, 64), ((0, 0), (0, 0), (0, 64))).reshape(D, heads * LANES)

    w_in_p = jnp.concatenate(
        [wi[:, :512], spread(wi[:, 512:768], 4), wi[:, 768:1280], spread(wi[:, 1280:1536], 4),
         cq, spread(wi[:, 1920:2048], 2), wi[:, 1792:1920], wi[:, 2048:],
         jnp.zeros((D, LANES - MLA_ROPE), F32)], axis=1).astype(BF)
    uq = mla_w_uq[l].reshape(256, 4, MLA_NOPE + MLA_ROPE)
    w_uq_p = jnp.concatenate([uq, jnp.zeros((256, 4, 32), F32)], axis=2).reshape(256, 512).astype(BF)
    ukv = mla_w_ukv[l].reshape(128, 4, 128)
    w_k = jnp.concatenate([ukv[:, :, :64], jnp.zeros((128, 4, 64), F32)], axis=2).reshape(128, 512).astype(BF)
    w_v = jnp.concatenate([ukv[:, :, 64:], jnp.zeros((128, 4, 64), F32)], axis=2).reshape(128, 512).astype(BF)
    wb = w_branch[l]
    wb_c = wb[2].reshape(4, 64, D)[C_HEAD_ORDER, :, :].reshape(256, D)
    w_branch_p = jnp.stack([wb[0], wb[1], wb_c, wb[3]]).astype(BF)
    w_router_p = w_router[l].T.astype(BF)
    return dict(w_in=w_in_p, w_uq=w_uq_p, w_k=w_k, w_v=w_v, w_gate=w_gate[l].astype(BF),
                w_branch=w_branch_p, w_out=w_out[l].astype(BF), w_router=w_router_p)


def _kpe_placement():
    e = np.zeros((128, 512), np.float32)
    for h in range(4):
        for i in range(MLA_ROPE):
            e[i, 128 * h + MLA_NOPE + i] = 1.0
    return jnp.asarray(e, BF)


def _layer(x, mod, lw, lp, lam_init, ctx, tables, final, g_final):
    B, N, D = x.shape
    latent = ctx is not None
    tm = 512 if latent else 256
    out_dtype = BF if latent else F32
    qkv_a, qkv_b, qkv_c, q_d, ckv_n, kpe_r = _in_proj(
        x, mod, lp["g_norm1"], lw["w_in"], lp["mla_q_norm"], lw["w_uq"], lp["mla_kv_norm"],
        tables if latent else None, out_dtype, tm)
    diff = (lp["da_lambda"], jnp.tile(lp["da_subln"], (1, 2)), lam_init)
    if latent:
        ck_a, cv_a, ck_b, cv_b, ck_c, cv_c, ckv_c, kpe_c = ctx
        P = ck_a.shape[1]

        def with_ones(v, heads):
            v = v.reshape(B, P, heads, 64).astype(BF)
            return jnp.concatenate([v, jnp.ones_like(v)], axis=-1).reshape(B, P, heads * LANES)

        k_a = jnp.concatenate([ck_a.astype(BF), qkv_a[:, :, 256:512]], axis=1)
        v_a = jnp.concatenate([with_ones(cv_a, 4), qkv_a[:, :, 512:1024]], axis=1)
        o_a = _dense_attn(qkv_a, (256, 0), k_a, (256, 0), v_a, (512, 0), groups=GROUPS_A,
                          tq=LAT_TQ, tk=(N + P) // 2, diff=diff, name="diff_attn")
        o_b = _nbr_attn(qkv_b, ck_b.astype(BF), with_ones(cv_b, 4), lp["na_bias"])
        o_c = _window_attn(qkv_c, ck_c.astype(BF), with_ones(cv_c, 2), lp["sw_sink"], 256)
        ckv_all = jnp.concatenate([ckv_c.astype(BF), ckv_n], axis=1)
        kpe_all = jnp.concatenate([jnp.pad(kpe_c, ((0, 0), (0, 0), (0, 128 - MLA_ROPE))).astype(BF), kpe_r], axis=1)
        k_d, v_d = _mla_expand(ckv_all, kpe_all, lw["w_k"], lw["w_v"], lp["e_place"], 256)
        o_d = _dense_attn(q_d, (512, 0), k_d, (512, 0), v_d, (512, 0), groups=GROUPS_D,
                          tq=LAT_TQ, tk=(N + P) // 2, name="mla_attn")
    else:
        o_a = _dense_attn(qkv_a, (256, 0), qkv_a, (256, 1), qkv_a, (512, 1), groups=GROUPS_A,
                          tq=N, tk=N, diff=diff, name="diff_attn_ctx")
        o_b = _dense_attn(qkv_b, (256, 0), qkv_b, (256, 1), qkv_b, (512, 1), groups=GROUPS_B,
                          tq=N, tk=N, name="dense_attn_ctx")
        o_c = _dense_attn(qkv_c, (256, 0), qkv_c, (128, 4), qkv_c, (256, 1), groups=GROUPS_C,
                          tq=N, tk=N, sink=lp["sw_sink"], name="gqa_attn_ctx")
        k_d, v_d = _mla_expand(ckv_n, kpe_r, lw["w_k"], lw["w_v"], lp["e_place"], N)
        o_d = _dense_attn(q_d, (512, 0), k_d, (512, 0), v_d, (512, 0), groups=GROUPS_D,
                          tq=N, tk=N, name="mla_attn_ctx")
    x_new, h2, logits = _merge(x, mod, lp["g_norm1"], lp["g_norm2"], o_a, o_b, o_c, o_d,
                               lw["w_gate"], lw["w_branch"], lw["w_out"], lw["w_router"], tm)
    x_out = _moe(x_new, h2, logits, mod, g_final, lp["w_e_gate"], lp["w_e_up"], lp["w_e_down"], final)
    cache = None
    if not latent:
        strip = lambda v, heads: v.reshape(B, N, heads, LANES)[..., :64]
        cache = (qkv_a[:, :, 256:512].reshape(B, N, 4, 64), strip(qkv_a[:, :, 512:1024], 4),
                 qkv_b[:, :, 256:512].reshape(B, N, 4, 64), strip(qkv_b[:, :, 512:1024], 4),
                 qkv_c[:, :, 512:640].reshape(B, N, 2, 64), strip(qkv_c[:, :, 256:512], 2),
                 ckv_n, kpe_r[:, :, :MLA_ROPE])
    return x_out, cache


def kernel(x_prompt, x_sample, cache_diff_k, cache_diff_v, cache_na_k, cache_na_v, cache_swa_k, cache_swa_v, cache_mla_ckv, cache_mla_kpe, c, c_ctx, w_mod, b_mod, g_norm1, g_norm2, w_in, da_lambda, da_subln, na_rpb, sw_sink, mla_q_norm, mla_w_uq, mla_kv_norm, mla_w_ukv, w_gate, w_branch, w_out, w_router, w_e_gate, w_e_up, w_e_down, g_final):
    D = D_MODEL
    BS, NS, _ = x_sample.shape
    P = cache_diff_k.shape[2]
    cond = jnp.concatenate([c_ctx[None], c, jnp.zeros((16 - 1 - BS, D), F32)], axis=0)
    mod_all = _modulation(cond, w_mod, b_mod).reshape(DEPTH, 16, 6, D)
    tables = _rope_tables(NS)
    e_place = _kpe_placement()
    gf = g_final.reshape(1, D)
    xp, xs = x_prompt, x_sample
    caches = []
    for l in range(DEPTH):
        lw = _prep_layer(l, w_in, mla_w_uq, mla_w_ukv, w_gate, w_branch, w_out, w_router)
        lp = dict(
            g_norm1=g_norm1[l].reshape(1, D), g_norm2=g_norm2[l].reshape(1, D),
            mla_q_norm=mla_q_norm[l].reshape(1, 256), mla_kv_norm=mla_kv_norm[l].reshape(1, 128),
            da_lambda=da_lambda[l], da_subln=da_subln[l].reshape(1, 64), sw_sink=sw_sink[l],
            w_e_gate=w_e_gate[l], w_e_up=w_e_up[l], w_e_down=w_e_down[l], e_place=e_place,
            na_bias=_nbr_bias_tables(na_rpb[l], NS // GRID_W))
        lam_init = 0.8 - 0.6 * math.exp(-0.3 * l)
        final = l == DEPTH - 1
        xp, cache_l = _layer(xp, mod_all[l, 0:1], lw, lp, lam_init, None, None, final, gf)
        caches.append(cache_l)
        ctx = (cache_diff_k[:, l].reshape(BS, P, 256), cache_diff_v[:, l].reshape(BS, P, 256),
               cache_na_k[:, l].reshape(BS, P, 256), cache_na_v[:, l].reshape(BS, P, 256),
               cache_swa_k[:, l].reshape(BS, P, 128), cache_swa_v[:, l].reshape(BS, P, 128),
               cache_mla_ckv[:, l], cache_mla_kpe[:, l])
        xs, _ = _layer(xs, mod_all[l, 1:1 + BS], lw, lp, lam_init, ctx, tables, final, gf)
    outs = tuple(jnp.stack([cl[i] for cl in caches], axis=1) for i in range(8))
    return (xp, xs) + outs
```

```python
import functools
import math

import numpy as np
import jax
import jax.numpy as jnp
from jax import lax
from jax.experimental import pallas as pl
from jax.experimental.pallas import tpu as pltpu

BF = jnp.bfloat16
F32 = jnp.float32

D_MODEL = 1024
DEPTH = 4
GRID_W = 64
ROPE_BASE = 10000.0
NORM_EPS = 1e-6
NEG_INF = -1e30
LOG2E = math.log2(math.e)

DA_QK = 32
NA_WIN_R = 8
NA_WIN_C = 16
SW_WINDOW = 128
MLA_NOPE = 64
MLA_ROPE = 32
N_EXPERTS = 16
EC_CAPACITY = 2

IN_PAD = 3200
LANES = 128

VMEM_BIG = 56 * 1024 * 1024
VMEM_MID = 40 * 1024 * 1024


def _cparams(sem, vmem=None):
    return pltpu.CompilerParams(dimension_semantics=sem, vmem_limit_bytes=vmem)


def _dot(a, b):
    return jnp.dot(a, b, preferred_element_type=F32)


def _dot_nt(a, b):
    return lax.dot_general(a, b, (((1,), (1,)), ((), ())), preferred_element_type=F32)


def _rms(x, g):
    var = jnp.mean(x * x, axis=-1, keepdims=True)
    return x * lax.rsqrt(var + NORM_EPS) * g


def _lane_mask(width, lo, length):
    lane = lax.broadcasted_iota(jnp.int32, (1, width), 1)
    return (lane >= lo) & (lane < lo + length)


def _mod_kernel(c_ref, w_ref, b_ref, o_ref):
    c = c_ref[...]
    s = c * (1.0 / (1.0 + jnp.exp(-c)))
    o_ref[0] = jnp.dot(s, w_ref[0], preferred_element_type=F32, precision=lax.Precision.HIGHEST) + b_ref[0]


def _modulation(cond, w_mod, b_mod):
    R, D = cond.shape
    L = w_mod.shape[0]
    nj = w_mod.shape[2] // D
    return pl.pallas_call(
        _mod_kernel,
        grid=(L, nj),
        in_specs=[
            pl.BlockSpec((R, D), lambda l, j: (0, 0)),
            pl.BlockSpec((1, D, D), lambda l, j: (l, 0, j)),
            pl.BlockSpec((1, 1, D), lambda l, j: (l, 0, j)),
        ],
        out_specs=pl.BlockSpec((1, R, D), lambda l, j: (l, 0, j)),
        out_shape=jax.ShapeDtypeStruct((L, R, nj * D), F32),
        compiler_params=_cparams(("parallel", "parallel")),
        name="modulation",
    )(cond, w_mod, b_mod.reshape(L, 1, nj * D))


def _rope(x, c, s, d):
    w = x.shape[-1]
    lane = lax.broadcasted_iota(jnp.int32, (1, LANES), 1)
    first = (lane % (2 * d)) < d
    outs = []
    for b in range(w // LANES):
        sl = slice(b * LANES, (b + 1) * LANES)
        xb = x[:, sl]
        partner = jnp.where(first, pltpu.roll(xb, LANES - d, 1), pltpu.roll(xb, d, 1))
        outs.append(xb * c[:, sl] + partner * s[:, sl])
    return outs[0] if len(outs) == 1 else jnp.concatenate(outs, axis=-1)


def _inproj_kernel(*refs, rope):
    if rope:
        (x_ref, mod_ref, g1_ref, w_ref, qn_ref, wuq_ref, kvn_ref,
         tac, tas, tcc, tcs, tqc, tqs, tkc, tks,
         oa, ob, oc, oq, ockv, okpe) = refs
    else:
        (x_ref, mod_ref, g1_ref, w_ref, qn_ref, wuq_ref, kvn_ref,
         oa, ob, oc, oq, ockv, okpe) = refs
    dt = oa.dtype
    x = x_ref[0]
    h = (_rms(x, g1_ref[...]) * (1.0 + mod_ref[0, 1:2, :]) + mod_ref[0, 0:1, :]).astype(BF)

    def seg(lo, w):
        return _dot(h, w_ref[:, lo:lo + w])

    ua = seg(0, 1024)
    qk = ua[:, :512]
    if rope:
        qk = _rope(qk, tac[...], tas[...], 8)
    oa[0, :, 0:256] = (qk[:, :256] * (DA_QK ** -0.5 * LOG2E)).astype(dt)
    oa[0, :, 256:512] = qk[:, 256:].astype(dt)
    ones_lane = lax.broadcasted_iota(jnp.int32, (1, 512), 1) % LANES >= 64
    oa[0, :, 512:1024] = jnp.where(ones_lane, 1.0, ua[:, 512:]).astype(dt)
    ub = seg(1024, 1024)
    ob[0, :, 0:256] = (ub[:, :256] * (0.125 * LOG2E)).astype(dt)
    ob[0, :, 256:512] = ub[:, 256:512].astype(dt)
    ob[0, :, 512:1024] = jnp.where(ones_lane, 1.0, ub[:, 512:]).astype(dt)
    uc = seg(2048, 640)
    qk = jnp.concatenate([uc[:, :256], uc[:, 512:]], axis=-1)
    if rope:
        qk = _rope(qk, tcc[...], tcs[...], 16)
    oc[0, :, 0:256] = (qk[:, :256] * (0.125 * LOG2E)).astype(dt)
    oc[0, :, 256:512] = jnp.where(ones_lane[:, :256], 1.0, uc[:, 256:512]).astype(dt)
    oc[0, :, 512:640] = qk[:, 256:].astype(dt)
    cq = _rms(seg(2688, 256), qn_ref[...]).astype(BF)
    qd = _dot(cq, wuq_ref[...])
    if rope:
        qd = _rope(qd, tqc[...], tqs[...], 8)
    oq[0] = (qd * ((MLA_NOPE + MLA_ROPE) ** -0.5 * LOG2E)).astype(dt)
    ockv[0] = _rms(seg(2944, 128), kvn_ref[...]).astype(dt)
    kpe = seg(3072, 128)
    if rope:
        kpe = _rope(kpe, tkc[...], tks[...], 8)
    okpe[0] = kpe.astype(dt)


def _in_proj(x, mod, g1, w_in_p, q_norm, w_uq_p, kv_norm, tables, out_dtype, tm):
    B, N, D = x.shape
    rope = tables is not None
    nmod = mod.shape[0]
    mod_map = (lambda j, b: (b, 0, 0)) if nmod > 1 else (lambda j, b: (0, 0, 0))
    const2 = lambda j, b: (0, 0)
    in_specs = [
        pl.BlockSpec((1, tm, D), lambda j, b: (b, j, 0)),
        pl.BlockSpec((1, 6, D), mod_map),
        pl.BlockSpec((1, D), const2),
        pl.BlockSpec((D, IN_PAD), const2),
        pl.BlockSpec((1, 256), const2),
        pl.BlockSpec((256, 512), const2),
        pl.BlockSpec((1, 128), const2),
    ]
    args = [x, mod, g1, w_in_p, q_norm, w_uq_p, kv_norm]
    if rope:
        for t in tables:
            in_specs.append(pl.BlockSpec((tm, t.shape[1]), lambda j, b: (j, 0)))
            args.append(t)
    widths = (1024, 1024, 640, 512, 128, 128)
    out_specs = [pl.BlockSpec((1, tm, w), lambda j, b: (b, j, 0)) for w in widths]
    out_shape = [jax.ShapeDtypeStruct((B, N, w), out_dtype) for w in widths]
    return pl.pallas_call(
        functools.partial(_inproj_kernel, rope=rope),
        grid=(N // tm, B),
        in_specs=in_specs,
        out_specs=out_specs,
        out_shape=out_shape,
        compiler_params=_cparams(("parallel", "parallel"), VMEM_BIG),
        name="in_proj_rope" if rope else "in_proj",
    )(*args)


def _mla_expand_kernel(ckv_ref, kpe_ref, wk_ref, wv_ref, e_ref, ok, ov):
    ckv = ckv_ref[0].astype(BF)
    kpe = kpe_ref[0].astype(BF)
    ok[0] = (_dot(ckv, wk_ref[...]) + _dot(kpe, e_ref[...])).astype(ok.dtype)
    ones_lane = lax.broadcasted_iota(jnp.int32, (1, 512), 1) % LANES >= 64
    ov[0] = jnp.where(ones_lane, 1.0, _dot(ckv, wv_ref[...])).astype(ov.dtype)


def _mla_expand(ckv, kpe, w_k, w_v, e_place, tk):
    B, K, _ = ckv.shape
    const2 = lambda b, j: (0, 0)
    return pl.pallas_call(
        _mla_expand_kernel,
        grid=(B, K // tk),
        in_specs=[
            pl.BlockSpec((1, tk, 128), lambda b, j: (b, j, 0)),
            pl.BlockSpec((1, tk, 128), lambda b, j: (b, j, 0)),
            pl.BlockSpec((128, 512), const2),
            pl.BlockSpec((128, 512), const2),
            pl.BlockSpec((128, 512), const2),
        ],
        out_specs=[pl.BlockSpec((1, tk, 512), lambda b, j: (b, j, 0)),
                   pl.BlockSpec((1, tk, 512), lambda b, j: (b, j, 0))],
        out_shape=[jax.ShapeDtypeStruct((B, K, 512), BF), jax.ShapeDtypeStruct((B, K, 512), BF)],
        compiler_params=_cparams(("parallel", "parallel")),
        name="mla_expand",
    )(ckv, kpe, w_k, w_v, e_place)


def _head_softmax(scores, values):
    m = functools.reduce(jnp.maximum, [jnp.max(s, axis=-1, keepdims=True) for s in scores])
    acc = None
    for s, v in zip(scores, values):
        part = _dot(jnp.exp2(s - m).astype(BF), v)
        acc = part if acc is None else acc + part
    return acc, m


def _normalise(acc):
    return acc * pltpu.roll(1.0 / acc, 64, 1)


def _store_heads(o_ref, heads):
    half = _lane_mask(LANES, 0, 64)
    for hp in range(len(heads) // 2):
        slab = jnp.where(half, heads[2 * hp], pltpu.roll(heads[2 * hp + 1], 64, 1))
        o_ref[0, :, hp * LANES:(hp + 1) * LANES] = slab.astype(o_ref.dtype)


def _dense_attn_kernel(*refs, groups, tk, n_chunks, has_sink, diff):
    it = iter(refs)
    q_ref, k_ref, v_ref = next(it), next(it), next(it)
    sink_ref = lam_ref = subln_ref = None
    if has_sink:
        sink_ref = next(it)
    if diff:
        lam_ref, subln_ref, lam_init_ref = next(it), next(it), next(it)
    o_ref = next(it)
    tq = q_ref.shape[1]

    lam = lam_init = None
    if diff:
        lam_init = lam_init_ref[0]
        dl = lam_ref[...]
        lam = (jnp.exp(jnp.sum(dl[0:1] * dl[1:2], axis=-1, keepdims=True))
               - jnp.exp(jnp.sum(dl[2:3] * dl[3:4], axis=-1, keepdims=True)) + lam_init)

    ng = len(groups)
    scratch = list(it)
    qg_s, m_s, acc_s = (scratch[j * ng:(j + 1) * ng] for j in range(3))
    for gi, (q_lo, k_lo, w, qmask, v_lo, head, sink_idx, sign) in enumerate(groups):
        qg = q_ref[0, :, q_lo:q_lo + w].astype(BF)
        if qmask is not None:
            qg = jnp.where(_lane_mask(w, *qmask), qg, jnp.zeros_like(qg))
        qg_s[gi][...] = qg
        m_s[gi][...] = jnp.full((tq, 1), NEG_INF, F32)
        acc_s[gi][...] = jnp.zeros((tq, LANES), F32)

    def keys_step(c):
        off = c * tk
        for gi, (q_lo, k_lo, w, qmask, v_lo, head, sink_idx, sign) in enumerate(groups):
            s = _dot_nt(qg_s[gi][...], k_ref[0, pl.ds(off, tk), k_lo:k_lo + w].astype(BF))
            m = m_s[gi][...]
            m_new = jnp.maximum(m, jnp.max(s, axis=-1, keepdims=True))
            p = jnp.exp2(s - m_new).astype(BF)
            pv = _dot(p, v_ref[0, pl.ds(off, tk), v_lo:v_lo + LANES].astype(BF))
            acc_s[gi][...] = jnp.exp2(m - m_new) * acc_s[gi][...] + pv
            m_s[gi][...] = m_new

    for c in range(n_chunks):
        keys_step(c)

    heads = {}
    ones_half = _lane_mask(LANES, 64, 64)
    for gi, (q_lo, k_lo, w, qmask, v_lo, head, sink_idx, sign) in enumerate(groups):
        acc = acc_s[gi][...]
        if sink_idx is not None:
            acc = acc + jnp.where(ones_half, jnp.exp2(sink_ref[sink_idx] * LOG2E - m_s[gi][...]), 0.0)
        part = _normalise(acc)
        if sign < 0:
            part = -lam * part
        heads[head] = part if head not in heads else heads[head] + part
    if diff:
        half = _lane_mask(LANES, 0, 64)
        for hd, o in heads.items():
            ms = jnp.sum(jnp.where(half, o * o, 0.0), axis=-1, keepdims=True) * (1.0 / 64.0)
            heads[hd] = o * lax.rsqrt(ms + NORM_EPS) * subln_ref[...] * (1.0 - lam_init)
    _store_heads(o_ref, heads)


def _dense_attn(q_arr, q_blk, k_arr, k_blk, v_arr, v_blk, *, groups, tq, tk, sink=None, diff=None, name):
    B, N, _ = q_arr.shape
    K = k_arr.shape[1]
    n_chunks = K // tk
    assert n_chunks * tk == K and n_chunks <= 2
    qw, qi = q_blk
    kw, ki = k_blk
    vw, vi = v_blk
    in_specs = [
        pl.BlockSpec((1, tq, qw), lambda b, i: (b, i, qi)),
        pl.BlockSpec((1, K, kw), lambda b, i: (b, 0, ki)),
        pl.BlockSpec((1, K, vw), lambda b, i: (b, 0, vi)),
    ]
    args = [q_arr, k_arr, v_arr]
    if sink is not None:
        in_specs.append(pl.BlockSpec(memory_space=pltpu.SMEM))
        args.append(sink)
    if diff is not None:
        da_lambda, subln, lam_init = diff
        in_specs += [pl.BlockSpec((4, DA_QK), lambda b, i: (0, 0)),
                     pl.BlockSpec((1, LANES), lambda b, i: (0, 0)),
                     pl.BlockSpec(memory_space=pltpu.SMEM)]
        args += [da_lambda, subln, jnp.full((1,), lam_init, F32)]
    kern = functools.partial(_dense_attn_kernel, groups=groups, tk=tk, n_chunks=n_chunks,
                             has_sink=sink is not None, diff=diff is not None)
    scratch = ([pltpu.VMEM((tq, g[2]), BF) for g in groups] + [pltpu.VMEM((tq, 1), F32) for _ in groups]
               + [pltpu.VMEM((tq, LANES), F32) for _ in groups])
    return pl.pallas_call(
        kern,
        grid=(B, N // tq),
        in_specs=in_specs,
        out_specs=pl.BlockSpec((1, tq, 256), lambda b, i: (b, i, 0)),
        out_shape=jax.ShapeDtypeStruct((B, N, 256), BF),
        scratch_shapes=scratch,
        compiler_params=_cparams(("parallel", "parallel"), VMEM_BIG),
        name=name,
    )(*args)


GROUPS_A = tuple((0, 0, 256, (32 * g, 32), 128 * (g // 2), g // 2, None, 1 if g % 2 == 0 else -1) for g in range(8))
GROUPS_B = tuple((0, 0, 256, (64 * h, 64), 128 * h, h, None, 1) for h in range(4))
GROUPS_C = tuple((128 * g, 0, 128, (64 * j, 64), 128 * j, 2 * g + j, 2 * j + g, 1) for g in range(2) for j in range(2))
GROUPS_D = tuple((128 * h, 128 * h, 128, None, 128 * h, h, None, 1) for h in range(4))


def _window_attn_kernel(q_ref, k_ref, v_ref, ck_ref, cv_ref, sink_ref, o_ref, *, n, band):
    tq = q_ref.shape[1]
    i = pl.program_id(1)
    start = jnp.clip(i * tq - SW_WINDOW, 0, n - band)
    start = pl.multiple_of(start, 128)
    kb = k_ref[0, pl.ds(start, band), :]
    vb = v_ref[0, pl.ds(start, band), :]
    ck = ck_ref[0]
    cv = cv_ref[0]
    qpos = i * tq + lax.broadcasted_iota(jnp.int32, (tq, band), 0)
    kpos = start + lax.broadcasted_iota(jnp.int32, (tq, band), 1)
    valid = jnp.abs(qpos - kpos) <= SW_WINDOW
    ones_half = _lane_mask(LANES, 64, 64)
    heads = {}
    for g in range(2):
        for j in range(2):
            qg = q_ref[0, :, LANES * g:LANES * (g + 1)]
            qg = jnp.where(_lane_mask(LANES, 64 * j, 64), qg, jnp.zeros_like(qg))
            vs = slice(LANES * j, LANES * (j + 1))
            acc, m = _head_softmax([_dot_nt(qg, ck), jnp.where(valid, _dot_nt(qg, kb), NEG_INF)],
                                   [cv[:, vs], vb[:, vs]])
            acc = acc + jnp.where(ones_half, jnp.exp2(sink_ref[2 * j + g] * LOG2E - m), 0.0)
            heads[2 * g + j] = _normalise(acc)
    _store_heads(o_ref, heads)


def _window_attn(qkv_c, ck, cv, sink, tq):
    B, N, _ = qkv_c.shape
    band = tq + 2 * SW_WINDOW
    P = ck.shape[1]
    return pl.pallas_call(
        functools.partial(_window_attn_kernel, n=N, band=band),
        grid=(B, N // tq),
        in_specs=[
            pl.BlockSpec((1, tq, 256), lambda b, i: (b, i, 0)),
            pl.BlockSpec((1, N, 128), lambda b, i: (b, 0, 4)),
            pl.BlockSpec((1, N, 256), lambda b, i: (b, 0, 1)),
            pl.BlockSpec((1, P, 128), lambda b, i: (b, 0, 0)),
            pl.BlockSpec((1, P, 256), lambda b, i: (b, 0, 0)),
            pl.BlockSpec(memory_space=pltpu.SMEM),
        ],
        out_specs=pl.BlockSpec((1, tq, 256), lambda b, i: (b, i, 0)),
        out_shape=jax.ShapeDtypeStruct((B, N, 256), BF),
        compiler_params=_cparams(("parallel", "parallel")),
        name="window_attn",
    )(qkv_c, qkv_c, qkv_c, ck, cv, sink)


NA_TILE_ROWS = 4
NA_KEY_ROWS = NA_TILE_ROWS + NA_WIN_R


def _nbr_attn_kernel(q_ref, k_ref, v_ref, ck_ref, cv_ref, bias_ref, o_ref, *, rows):
    i = pl.program_id(1)
    krow = jnp.clip(i * NA_TILE_ROWS - NA_WIN_R // 2, 0, rows - NA_KEY_ROWS)
    start = pl.multiple_of(krow * GRID_W, GRID_W)
    nk = NA_KEY_ROWS * GRID_W
    kb = k_ref[0, pl.ds(start, nk), :]
    vb = v_ref[0, pl.ds(start, nk), :]
    ck = ck_ref[0]
    cv = cv_ref[0]
    q = q_ref[0]
    heads = {}
    for h in range(4):
        qg = jnp.where(_lane_mask(256, 64 * h, 64), q, jnp.zeros_like(q))
        vs = slice(LANES * h, LANES * (h + 1))
        acc, _ = _head_softmax([_dot_nt(qg, ck), _dot_nt(qg, kb) + bias_ref[0, h]], [cv[:, vs], vb[:, vs]])
        heads[h] = _normalise(acc)
    _store_heads(o_ref, heads)


def _nbr_attn(qkv_b, ck, cv, bias):
    B, N, _ = qkv_b.shape
    rows = N // GRID_W
    tq = NA_TILE_ROWS * GRID_W
    nt = N // tq
    nk = NA_KEY_ROWS * GRID_W
    P = ck.shape[1]

    def bias_map(b, i):
        return (jnp.where(i == 0, 0, jnp.where(i == nt - 1, 2, 1)), 0, 0, 0)

    return pl.pallas_call(
        functools.partial(_nbr_attn_kernel, rows=rows),
        grid=(B, nt),
        in_specs=[
            pl.BlockSpec((1, tq, 256), lambda b, i: (b, i, 0)),
            pl.BlockSpec((1, N, 256), lambda b, i: (b, 0, 1)),
            pl.BlockSpec((1, N, 512), lambda b, i: (b, 0, 1)),
            pl.BlockSpec((1, P, 256), lambda b, i: (b, 0, 0)),
            pl.BlockSpec((1, P, 512), lambda b, i: (b, 0, 0)),
            pl.BlockSpec((1, 4, tq, nk), bias_map),
        ],
        out_specs=pl.BlockSpec((1, tq, 256), lambda b, i: (b, i, 0)),
        out_shape=jax.ShapeDtypeStruct((B, N, 256), BF),
        compiler_params=_cparams(("parallel", "parallel"), VMEM_MID),
        name="nbr_attn",
    )(qkv_b, qkv_b, qkv_b, ck, cv, bias)


def _nbr_bias_tables(rpb, rows):
    H = rpb.shape[0]
    rp = jnp.pad(rpb.astype(F32), ((0, 0), (NA_KEY_ROWS, NA_KEY_ROWS), (GRID_W, GRID_W)))
    c0 = NA_WIN_C - 1 + GRID_W
    tcol = jnp.stack([rp[:, :, c0 - qc:c0 - qc + GRID_W] for qc in range(GRID_W)], axis=2)
    tabs = []
    nt = rows // NA_TILE_ROWS
    for r0 in (0, NA_TILE_ROWS * min(1, nt - 1) + NA_TILE_ROWS, rows - NA_TILE_ROWS):
        ks = int(np.clip(r0 - NA_WIN_R // 2, 0, rows - NA_KEY_ROWS))
        per_row = []
        for a in range(NA_TILE_ROWS):
            s0 = ks - (r0 + a) + NA_WIN_R - 1 + NA_KEY_ROWS
            per_row.append(tcol[:, s0:s0 + NA_KEY_ROWS])
        t = jnp.stack(per_row, axis=1).transpose(0, 1, 3, 2, 4)
        r = r0 + np.arange(NA_TILE_ROWS)[:, None, None, None]
        qc = np.arange(GRID_W)[None, :, None, None]
        kr = ks + np.arange(NA_KEY_ROWS)[None, None, :, None]
        kc = np.arange(GRID_W)[None, None, None, :]
        rs = np.clip(r - NA_WIN_R // 2, 0, rows - NA_WIN_R)
        cs = np.clip(qc - NA_WIN_C // 2, 0, GRID_W - NA_WIN_C)
        ok = (kr >= rs) & (kr < rs + NA_WIN_R) & (kc >= cs) & (kc < cs + NA_WIN_C)
        t = jnp.where(ok[None], t * LOG2E, NEG_INF)
        tabs.append(t.reshape(H, NA_TILE_ROWS * GRID_W, NA_KEY_ROWS * GRID_W))
    return jnp.stack(tabs)


def _merge_kernel(x_ref, mod_ref, g1_ref, g2_ref, oa, ob, oc, od, wg_ref, wb_ref, wo_ref, wr_ref,
                  xo_ref, h2_ref, lg_ref):
    D = x_ref.shape[2]
    x = x_ref[0]
    h = (_rms(x, g1_ref[...]) * (1.0 + mod_ref[0, 1:2, :]) + mod_ref[0, 0:1, :]).astype(BF)
    merged = None
    for i, o_ref in enumerate((oa, ob, oc, od)):
        z = _dot(h, wg_ref[:, i * D:(i + 1) * D])
        gate = 1.0 / (1.0 + jnp.exp(-z))
        term = gate * _dot(o_ref[0], wb_ref[i])
        merged = term if merged is None else merged + term
    mixed = _dot(merged.astype(BF), wo_ref[...])
    xn = x + mod_ref[0, 2:3, :] * mixed
    xo_ref[0] = xn
    h2 = (_rms(xn, g2_ref[...]) * (1.0 + mod_ref[0, 4:5, :]) + mod_ref[0, 3:4, :]).astype(BF)
    h2_ref[0] = h2
    lg_ref[0] = _dot_nt(wr_ref[...], h2)


def _merge(x, mod, g1, g2, o_a, o_b, o_c, o_d, w_gate, w_branch, w_out, w_router_p, tm):
    B, N, D = x.shape
    nmod = mod.shape[0]
    mod_map = (lambda b, j: (b, 0, 0)) if nmod > 1 else (lambda b, j: (0, 0, 0))
    tok = lambda w: pl.BlockSpec((1, tm, w), lambda b, j: (b, j, 0))
    c2 = lambda b, j: (0, 0)
    return pl.pallas_call(
        _merge_kernel,
        grid=(B, N // tm),
        in_specs=[
            tok(D), pl.BlockSpec((1, 6, D), mod_map), pl.BlockSpec((1, D), c2), pl.BlockSpec((1, D), c2),
            tok(256), tok(256), tok(256), tok(256),
            pl.BlockSpec((D, 4 * D), c2),
            pl.BlockSpec((4, 256, D), lambda b, j: (0, 0, 0)),
            pl.BlockSpec((D, D), c2),
            pl.BlockSpec((N_EXPERTS, D), c2),
        ],
        out_specs=[tok(D), tok(D), pl.BlockSpec((1, N_EXPERTS, tm), lambda b, j: (b, 0, j))],
        out_shape=[jax.ShapeDtypeStruct((B, N, D), F32), jax.ShapeDtypeStruct((B, N, D), BF),
                   jax.ShapeDtypeStruct((B, N_EXPERTS, N), F32)],
        compiler_params=_cparams(("parallel", "parallel"), VMEM_BIG),
        name="merge",
    )(x, mod, g1, g2, o_a, o_b, o_c, o_d, w_gate, w_branch, w_out, w_router_p)


MOE_TT = 256
MOE_RC = 64
SLOT_ALIGN = 16


def _excl_cumsum(x, tri, reset_every):
    R, N = x.shape
    outs = []
    carry = jnp.zeros((R, 1), F32)
    for c in range(N // LANES):
        if reset_every and (c * LANES) % reset_every == 0:
            carry = jnp.zeros((R, 1), F32)
        xc = x[:, c * LANES:(c + 1) * LANES]
        inc = _dot(xc.astype(BF), tri)
        outs.append(inc - xc + carry)
        carry = carry + inc[:, LANES - 1:LANES]
    return (outs[0] if len(outs) == 1 else jnp.concatenate(outs, axis=1)), carry


def _route_kernel(lg_ref, aff_ref, rel_ref, base_ref, *, cap, tt):
    B, E, N = lg_ref.shape
    R = B * E
    nt = N // tt
    lg = lg_ref[...]
    ex = jnp.exp(lg - jnp.max(lg, axis=1, keepdims=True))
    aff = ex / jnp.sum(ex, axis=1, keepdims=True)
    aff_ref[...] = aff
    bits = pltpu.bitcast(aff.reshape(R, N), jnp.int32)

    def per_expert(v):
        t = v[0:E]
        for b in range(1, B):
            t = t + v[b * E:(b + 1) * E]
        return t

    def per_row(v):
        return jnp.concatenate([v] * B, axis=0) if B > 1 else v

    def batch_prefix(v):
        parts, run = [], jnp.zeros((E, 1), F32)
        for b in range(B):
            parts.append(run)
            run = run + v[b * E:(b + 1) * E]
        return jnp.concatenate(parts, axis=0) if B > 1 else parts[0]

    def count(mask):
        return per_expert(jnp.sum(jnp.where(mask, 1.0, 0.0), axis=1, keepdims=True))

    def search(i, thr):
        cand = thr | jnp.left_shift(jnp.int32(1), 30 - i)
        return jnp.where(count(bits >= per_row(cand)) >= float(cap), cand, thr)

    thr = per_row(lax.fori_loop(0, 31, search, jnp.zeros((E, 1), jnp.int32)))
    gt = bits > thr
    eq = bits == thr
    need = float(cap) - count(gt)
    ri = lax.broadcasted_iota(jnp.int32, (LANES, LANES), 0)
    ci = lax.broadcasted_iota(jnp.int32, (LANES, LANES), 1)
    tri = jnp.where(ri <= ci, 1.0, 0.0).astype(BF)
    rank_eq, eq_tot = _excl_cumsum(jnp.where(eq, 1.0, 0.0), tri, None)
    rank_eq = rank_eq + batch_prefix(eq_tot)
    sel = gt | (eq & (rank_eq < per_row(need)))
    self = jnp.where(sel, 1.0, 0.0)
    local, _ = _excl_cumsum(self, tri, tt)
    rel_ref[...] = jnp.where(sel, local, -1.0).astype(jnp.int32).reshape(B, E, N)
    tn = jnp.right_shift(lax.broadcasted_iota(jnp.int32, (N, LANES), 0), tt.bit_length() - 1)
    tj = lax.broadcasted_iota(jnp.int32, (N, LANES), 1)
    cnt = _dot(self.astype(BF), jnp.where(tn == tj, 1.0, 0.0).astype(BF)).astype(jnp.int32)
    sh = SLOT_ALIGN.bit_length() - 1
    ru = jnp.left_shift(jnp.right_shift(cnt + (SLOT_ALIGN - 1), sh), sh).astype(F32)
    base_b = _dot(ru.astype(BF), jnp.where(ri < ci, 1.0, 0.0).astype(BF))
    base = base_b + batch_prefix(base_b[:, nt:nt + 1])
    base_ref[...] = base.astype(jnp.int32).reshape(B, E, LANES)


def _route(lg_t, cap):
    B, E, N = lg_t.shape
    full = lambda s: pl.BlockSpec(s, lambda i: (0,) * len(s))
    return pl.pallas_call(
        functools.partial(_route_kernel, cap=cap, tt=MOE_TT),
        grid=(1,),
        in_specs=[full((B, E, N))],
        out_specs=[full((B, E, N)), full((B, E, N)), full((B, E, LANES))],
        out_shape=[jax.ShapeDtypeStruct((B, E, N), F32), jax.ShapeDtypeStruct((B, E, N), jnp.int32),
                   jax.ShapeDtypeStruct((B, E, LANES), jnp.int32)],
        compiler_params=_cparams(("arbitrary",), VMEM_BIG),
        name="route",
    )(lg_t)


def _slot_onehot(rel, values, c, rc):
    E, tt = rel.shape
    r = lax.broadcasted_iota(jnp.int32, (E, rc, tt), 1) + c * rc
    return jnp.where(rel[:, None, :] == r, values[:, None, :], 0.0).astype(BF).reshape(E * rc, tt)


def _dispatch_kernel(base_sm, x_ref, rel_ref, xe_in, xe_hbm, stage, stage2, sem, sem2, *, nt, rc):
    del xe_in
    b, j = pl.program_id(0), pl.program_id(1)
    E = rel_ref.shape[1]
    step = b * nt + j
    last = pl.num_programs(0) * nt - 1
    base = lambda e, jj: base_sm[(b * E + e) * (nt + 1) + jj]
    rel = rel_ref[0]
    x = x_ref[0]
    ones = jnp.ones(rel.shape, F32)

    def copy(buf, s, e, c):
        start = pl.multiple_of(base(e, j) + c * rc, SLOT_ALIGN)
        return pltpu.make_async_copy(buf.at[pl.ds(e * rc, rc)], xe_hbm.at[e, pl.ds(start, rc)], s.at[e])

    strip = _dot(_slot_onehot(rel, ones, 0, rc), x).astype(BF)

    @pl.when(step > 0)
    def _():
        for e in range(E):
            copy(stage, sem, e, 0).wait()

    stage[...] = strip
    for e in range(E):
        copy(stage, sem, e, 0).start()

    span = [base(e, j + 1) - base(e, j) for e in range(E)]
    widest = functools.reduce(jnp.maximum, span)

    def extra(c, carry):
        stage2[...] = _dot(_slot_onehot(rel, ones, c, rc), x).astype(BF)
        for e in range(E):
            @pl.when(span[e] > c * rc)
            def _():
                cp = copy(stage2, sem2, e, c)
                cp.start()
                cp.wait()
        return carry

    lax.fori_loop(1, (widest + rc - 1) // rc, extra, 0)

    @pl.when(step == last)
    def _():
        for e in range(E):
            copy(stage, sem, e, 0).wait()


def _dispatch(h2, rel, base_flat, rows):
    B, N, D = h2.shape
    E = rel.shape[1]
    nt = N // MOE_TT
    xe0 = jnp.zeros((E, rows, D), BF)
    return pl.pallas_call(
        functools.partial(_dispatch_kernel, nt=nt, rc=MOE_RC),
        grid_spec=pltpu.PrefetchScalarGridSpec(
            num_scalar_prefetch=1,
            grid=(B, nt),
            in_specs=[
                pl.BlockSpec((1, MOE_TT, D), lambda b, j, s: (b, j, 0)),
                pl.BlockSpec((1, E, MOE_TT), lambda b, j, s: (b, 0, j)),
                pl.BlockSpec(memory_space=pl.ANY),
            ],
            out_specs=pl.BlockSpec(memory_space=pl.ANY),
            scratch_shapes=[pltpu.VMEM((E * MOE_RC, D), BF), pltpu.VMEM((E * MOE_RC, D), BF),
                            pltpu.SemaphoreType.DMA((E,)), pltpu.SemaphoreType.DMA((E,))],
        ),
        out_shape=jax.ShapeDtypeStruct((E, rows, D), BF),
        input_output_aliases={3: 0},
        compiler_params=_cparams(("arbitrary", "arbitrary")),
        name="dispatch",
    )(base_flat, h2, rel, xe0)


def _ffn_kernel(tot_sm, x_ref, wg_hbm, wu_hbm, wd_hbm, y_ref, wbuf, wg_s, wu_s, wd_s, sem, *, tm, rc):
    e, j = pl.program_id(0), pl.program_id(1)
    slot = e % 2

    def copy(ee, sl, k):
        return pltpu.make_async_copy((wg_hbm, wu_hbm, wd_hbm)[k].at[ee], wbuf.at[sl, k], sem.at[sl, k])

    @pl.when(j == 0)
    def _():
        @pl.when(e == 0)
        def _():
            for k in range(3):
                copy(e, slot, k).start()

        @pl.when(e + 1 < pl.num_programs(0))
        def _():
            for k in range(3):
                copy(e + 1, 1 - slot, k).start()

        for k in range(3):
            copy(e, slot, k).wait()
        wg_s[...] = wbuf[slot, 0].astype(BF)
        wu_s[...] = wbuf[slot, 1].astype(BF)
        wd_s[...] = wbuf[slot, 2].astype(BF)

    used = j * tm < tot_sm[e] + rc

    @pl.when(used)
    def _():
        x = x_ref[0]
        hg = _dot(x, wg_s[...])
        hu = _dot(x, wu_s[...])
        act = (hg * (1.0 / (1.0 + jnp.exp(-hg))) * hu).astype(BF)
        y_ref[0] = _dot(act, wd_s[...]).astype(y_ref.dtype)

    @pl.when(jnp.logical_not(used))
    def _():
        y_ref[...] = jnp.zeros_like(y_ref)


def _expert_ffn(xe, total, w_g, w_u, w_d, tm):
    E, rows, D = xe.shape
    DE = w_g.shape[2]
    tok = lambda e, j, tot: (e, jnp.minimum(j, (tot[e] + MOE_RC - 1) // tm), 0)
    assert D == DE
    hbm = pl.BlockSpec(memory_space=pl.ANY)
    return pl.pallas_call(
        functools.partial(_ffn_kernel, tm=tm, rc=MOE_RC),
        grid_spec=pltpu.PrefetchScalarGridSpec(
            num_scalar_prefetch=1,
            grid=(E, rows // tm),
            in_specs=[pl.BlockSpec((1, tm, D), tok), hbm, hbm, hbm],
            out_specs=pl.BlockSpec((1, tm, D), lambda e, j, tot: (e, j, 0)),
            scratch_shapes=[pltpu.VMEM((2, 3, D, DE), F32), pltpu.VMEM((D, DE), BF), pltpu.VMEM((D, DE), BF),
                            pltpu.VMEM((DE, D), BF), pltpu.SemaphoreType.DMA((2, 3))],
        ),
        out_shape=jax.ShapeDtypeStruct((E, rows, D), BF),
        compiler_params=_cparams(("arbitrary", "arbitrary"), VMEM_BIG),
        name="expert_ffn",
    )(total, xe, w_g, w_u, w_d)


def _combine_kernel(base_sm, x_ref, rel_ref, aff_ref, mod_ref, gf_ref, ye_hbm, o_ref, stage, sem, *,
                    nt, rc, final):
    b, j = pl.program_id(0), pl.program_id(1)
    E = rel_ref.shape[1]
    step = b * nt + j
    nsteps = pl.num_programs(0) * nt
    slot = step % 2
    base = lambda e, jj: base_sm[(b * E + e) * (nt + 1) + jj]
    rel = rel_ref[0]
    aff = aff_ref[0]

    def copy(st, sl, e, c):
        first = base_sm[(st // nt * E + e) * (nt + 1) + st % nt]
        start = pl.multiple_of(first + c * rc, SLOT_ALIGN)
        return pltpu.make_async_copy(ye_hbm.at[e, pl.ds(start, rc)], stage.at[sl, pl.ds(e * rc, rc)],
                                     sem.at[sl, e])

    def scatter(c):
        w = _slot_onehot(rel, aff, c, rc)
        return lax.dot_general(w, stage[slot], (((0,), (0,)), ((), ())), preferred_element_type=F32)

    @pl.when(step == 0)
    def _():
        for e in range(E):
            copy(step, slot, e, 0).start()

    @pl.when(step + 1 < nsteps)
    def _():
        for e in range(E):
            copy(step + 1, 1 - slot, e, 0).start()

    for e in range(E):
        copy(step, slot, e, 0).wait()
    acc = scatter(0)
    span = [base(e, j + 1) - base(e, j) for e in range(E)]
    widest = functools.reduce(jnp.maximum, span)

    def extra(c, acc):
        for e in range(E):
            @pl.when(span[e] > c * rc)
            def _():
                cp = copy(step, slot, e, c)
                cp.start()
                cp.wait()
        return acc + scatter(c)

    acc = lax.fori_loop(1, (widest + rc - 1) // rc, extra, acc)
    xn = x_ref[0] + mod_ref[0, 5:6, :] * acc
    if final:
        xn = _rms(xn, gf_ref[...])
    o_ref[0] = xn


def _combine(x, rel, aff, mod, g_final, ye, base_flat, final):
    B, N, D = x.shape
    E = rel.shape[1]
    nt = N // MOE_TT
    nmod = mod.shape[0]
    mod_map = (lambda b, j, s: (b, 0, 0)) if nmod > 1 else (lambda b, j, s: (0, 0, 0))
    tok = pl.BlockSpec((1, MOE_TT, D), lambda b, j, s: (b, j, 0))
    slots = pl.BlockSpec((1, E, MOE_TT), lambda b, j, s: (b, 0, j))
    return pl.pallas_call(
        functools.partial(_combine_kernel, nt=nt, rc=MOE_RC, final=final),
        grid_spec=pltpu.PrefetchScalarGridSpec(
            num_scalar_prefetch=1,
            grid=(B, nt),
            in_specs=[tok, slots, slots, pl.BlockSpec((1, 6, D), mod_map),
                      pl.BlockSpec((1, D), lambda b, j, s: (0, 0)), pl.BlockSpec(memory_space=pl.ANY)],
            out_specs=tok,
            scratch_shapes=[pltpu.VMEM((2, E * MOE_RC, D), BF), pltpu.SemaphoreType.DMA((2, E))],
        ),
        out_shape=jax.ShapeDtypeStruct((B, N, D), F32),
        compiler_params=_cparams(("arbitrary", "arbitrary")),
        name="combine_final" if final else "combine",
    )(base_flat, x, rel, aff, mod, g_final, ye)


def _moe(x, h2, lg_t, mod, g_final, w_g, w_u, w_d, final):
    B, N, D = h2.shape
    T = B * N
    nt = N // MOE_TT
    cap = EC_CAPACITY * T // N_EXPERTS
    tm = 512 if cap >= 2048 else 256
    worst = cap + (SLOT_ALIGN - 1) * (T // MOE_TT) + MOE_RC
    rows = -(-worst // tm) * tm
    aff, rel, base = _route(lg_t, cap)
    base_flat = base[:, :, :nt + 1].reshape(-1)
    total = base[B - 1, :, nt]
    xe = _dispatch(h2, rel, base_flat, rows)
    ye = _expert_ffn(xe, total, w_g, w_u, w_d, tm)
    return _combine(x, rel, aff, mod, g_final, ye, base_flat, final)


def _rope_tables(n):
    t = jnp.arange(n)
    pos = {1: (t // GRID_W).astype(F32), 2: (t % GRID_W).astype(F32)}

    def group(quarter):
        freqs = jnp.float32(ROPE_BASE) ** (-jnp.arange(quarter, dtype=F32) / quarter)
        cs, ss = [], []
        for kind in (1, 2):
            ang = pos[kind][:, None] * freqs
            c, s = jnp.cos(ang), jnp.sin(ang)
            cs += [c, c]
            ss += [-s, s]
        return jnp.concatenate(cs, axis=1), jnp.concatenate(ss, axis=1)

    c32, s32 = group(8)
    c64, s64 = group(16)
    one = lambda w: jnp.ones((n, w), F32)
    zero = lambda w: jnp.zeros((n, w), F32)
    ta = (jnp.tile(c32, (1, 16)), jnp.tile(s32, (1, 16)))
    tc = (jnp.tile(c64, (1, 6)), jnp.tile(s64, (1, 6)))
    qc = jnp.concatenate([one(64), c32, one(32)], axis=1)
    qs = jnp.concatenate([zero(64), s32, zero(32)], axis=1)
    tq = (jnp.tile(qc, (1, 4)), jnp.tile(qs, (1, 4)))
    tk = (jnp.concatenate([c32, one(96)], axis=1), jnp.concatenate([s32, zero(96)], axis=1))
    return (ta[0], ta[1], tc[0], tc[1], tq[0], tq[1], tk[0], tk[1])


LAT_TQ = 512
C_HEAD_ORDER = (0, 2, 1, 3)


def _prep_layer(l, w_in, mla_w_uq, mla_w_ukv, w_gate, w_branch, w_out, w_router):
    D = D_MODEL
    wi = w_in[l]
    cq = wi[:, 1536:1792].reshape(D, 4, 64)[:, C_HEAD_ORDER, :].reshape(D, 256)
    def spread(cols, heads):
        return jnp.pad(cols.reshape(D, heads, 64), ((0, 0), (0, 0), (0, 64))).reshape(D, heads * LANES)

    w_in_p = jnp.concatenate(
        [wi[:, :512], spread(wi[:, 512:768], 4), wi[:, 768:1280], spread(wi[:, 1280:1536], 4),
         cq, spread(wi[:, 1920:2048], 2), wi[:, 1792:1920], wi[:, 2048:],
         jnp.zeros((D, LANES - MLA_ROPE), F32)], axis=1).astype(BF)
    uq = mla_w_uq[l].reshape(256, 4, MLA_NOPE + MLA_ROPE)
    w_uq_p = jnp.concatenate([uq, jnp.zeros((256, 4, 32), F32)], axis=2).reshape(256, 512).astype(BF)
    ukv = mla_w_ukv[l].reshape(128, 4, 128)
    w_k = jnp.concatenate([ukv[:, :, :64], jnp.zeros((128, 4, 64), F32)], axis=2).reshape(128, 512).astype(BF)
    w_v = jnp.concatenate([ukv[:, :, 64:], jnp.zeros((128, 4, 64), F32)], axis=2).reshape(128, 512).astype(BF)
    wb = w_branch[l]
    wb_c = wb[2].reshape(4, 64, D)[C_HEAD_ORDER, :, :].reshape(256, D)
    w_branch_p = jnp.stack([wb[0], wb[1], wb_c, wb[3]]).astype(BF)
    w_router_p = w_router[l].T.astype(BF)
    return dict(w_in=w_in_p, w_uq=w_uq_p, w_k=w_k, w_v=w_v, w_gate=w_gate[l].astype(BF),
                w_branch=w_branch_p, w_out=w_out[l].astype(BF), w_router=w_router_p)


def _kpe_placement():
    e = np.zeros((128, 512), np.float32)
    for h in range(4):
        for i in range(MLA_ROPE):
            e[i, 128 * h + MLA_NOPE + i] = 1.0
    return jnp.asarray(e, BF)


def _layer(x, mod, lw, lp, lam_init, ctx, tables, final, g_final):
    B, N, D = x.shape
    latent = ctx is not None
    tm = 512 if latent else 256
    out_dtype = BF if latent else F32
    qkv_a, qkv_b, qkv_c, q_d, ckv_n, kpe_r = _in_proj(
        x, mod, lp["g_norm1"], lw["w_in"], lp["mla_q_norm"], lw["w_uq"], lp["mla_kv_norm"],
        tables if latent else None, out_dtype, tm)
    diff = (lp["da_lambda"], jnp.tile(lp["da_subln"], (1, 2)), lam_init)
    if latent:
        ck_a, cv_a, ck_b, cv_b, ck_c, cv_c, ckv_c, kpe_c = ctx
        P = ck_a.shape[1]

        def with_ones(v, heads):
            v = v.reshape(B, P, heads, 64).astype(BF)
            return jnp.concatenate([v, jnp.ones_like(v)], axis=-1).reshape(B, P, heads * LANES)

        k_a = jnp.concatenate([ck_a.astype(BF), qkv_a[:, :, 256:512]], axis=1)
        v_a = jnp.concatenate([with_ones(cv_a, 4), qkv_a[:, :, 512:1024]], axis=1)
        o_a = _dense_attn(qkv_a, (256, 0), k_a, (256, 0), v_a, (512, 0), groups=GROUPS_A,
                          tq=LAT_TQ, tk=(N + P) // 2, diff=diff, name="diff_attn")
        o_b = _nbr_attn(qkv_b, ck_b.astype(BF), with_ones(cv_b, 4), lp["na_bias"])
        o_c = _window_attn(qkv_c, ck_c.astype(BF), with_ones(cv_c, 2), lp["sw_sink"], 256)
        ckv_all = jnp.concatenate([ckv_c.astype(BF), ckv_n], axis=1)
        kpe_all = jnp.concatenate([jnp.pad(kpe_c, ((0, 0), (0, 0), (0, 128 - MLA_ROPE))).astype(BF), kpe_r], axis=1)
        k_d, v_d = _mla_expand(ckv_all, kpe_all, lw["w_k"], lw["w_v"], lp["e_place"], 256)
        o_d = _dense_attn(q_d, (512, 0), k_d, (512, 0), v_d, (512, 0), groups=GROUPS_D,
                          tq=LAT_TQ, tk=(N + P) // 2, name="mla_attn")
    else:
        o_a = _dense_attn(qkv_a, (256, 0), qkv_a, (256, 1), qkv_a, (512, 1), groups=GROUPS_A,
                          tq=N, tk=N, diff=diff, name="diff_attn_ctx")
        o_b = _dense_attn(qkv_b, (256, 0), qkv_b, (256, 1), qkv_b, (512, 1), groups=GROUPS_B,
                          tq=N, tk=N, name="dense_attn_ctx")
        o_c = _dense_attn(qkv_c, (256, 0), qkv_c, (128, 4), qkv_c, (256, 1), groups=GROUPS_C,
                          tq=N, tk=N, sink=lp["sw_sink"], name="gqa_attn_ctx")
        k_d, v_d = _mla_expand(ckv_n, kpe_r, lw["w_k"], lw["w_v"], lp["e_place"], N)
        o_d = _dense_attn(q_d, (512, 0), k_d, (512, 0), v_d, (512, 0), groups=GROUPS_D,
                          tq=N, tk=N, name="mla_attn_ctx")
    x_new, h2, logits = _merge(x, mod, lp["g_norm1"], lp["g_norm2"], o_a, o_b, o_c, o_d,
                               lw["w_gate"], lw["w_branch"], lw["w_out"], lw["w_router"], tm)
    x_out = _moe(x_new, h2, logits, mod, g_final, lp["w_e_gate"], lp["w_e_up"], lp["w_e_down"], final)
    cache = None
    if not latent:
        strip = lambda v, heads: v.reshape(B, N, heads, LANES)[..., :64]
        cache = (qkv_a[:, :, 256:512].reshape(B, N, 4, 64), strip(qkv_a[:, :, 512:1024], 4),
                 qkv_b[:, :, 256:512].reshape(B, N, 4, 64), strip(qkv_b[:, :, 512:1024], 4),
                 qkv_c[:, :, 512:640].reshape(B, N, 2, 64), strip(qkv_c[:, :, 256:512], 2),
                 ckv_n, kpe_r[:, :, :MLA_ROPE])
    return x_out, cache


def kernel(x_prompt, x_sample, cache_diff_k, cache_diff_v, cache_na_k, cache_na_v, cache_swa_k, cache_swa_v, cache_mla_ckv, cache_mla_kpe, c, c_ctx, w_mod, b_mod, g_norm1, g_norm2, w_in, da_lambda, da_subln, na_rpb, sw_sink, mla_q_norm, mla_w_uq, mla_kv_norm, mla_w_ukv, w_gate, w_branch, w_out, w_router, w_e_gate, w_e_up, w_e_down, g_final):
    D = D_MODEL
    BS, NS, _ = x_sample.shape
    P = cache_diff_k.shape[2]
    cond = jnp.concatenate([c_ctx[None], c, jnp.zeros((16 - 1 - BS, D), F32)], axis=0)
    mod_all = _modulation(cond, w_mod, b_mod).reshape(DEPTH, 16, 6, D)
    tables = _rope_tables(NS)
    e_place = _kpe_placement()
    gf = g_final.reshape(1, D)
    xp, xs = x_prompt, x_sample
    caches = []
    for l in range(DEPTH):
        lw = _prep_layer(l, w_in, mla_w_uq, mla_w_ukv, w_gate, w_branch, w_out, w_router)
        lp = dict(
            g_norm1=g_norm1[l].reshape(1, D), g_norm2=g_norm2[l].reshape(1, D),
            mla_q_norm=mla_q_norm[l].reshape(1, 256), mla_kv_norm=mla_kv_norm[l].reshape(1, 128),
            da_lambda=da_lambda[l], da_subln=da_subln[l].reshape(1, 64), sw_sink=sw_sink[l],
            w_e_gate=w_e_gate[l], w_e_up=w_e_up[l], w_e_down=w_e_down[l], e_place=e_place,
            na_bias=_nbr_bias_tables(na_rpb[l], NS // GRID_W))
        lam_init = 0.8 - 0.6 * math.exp(-0.3 * l)
        final = l == DEPTH - 1
        xp, cache_l = _layer(xp, mod_all[l, 0:1], lw, lp, lam_init, None, None, final, gf)
        caches.append(cache_l)
        ctx = (cache_diff_k[:, l].reshape(BS, P, 256), cache_diff_v[:, l].reshape(BS, P, 256),
               cache_na_k[:, l].reshape(BS, P, 256), cache_na_v[:, l].reshape(BS, P, 256),
               cache_swa_k[:, l].reshape(BS, P, 128), cache_swa_v[:, l].reshape(BS, P, 128),
               cache_mla_ckv[:, l], cache_mla_kpe[:, l])
        xs, _ = _layer(xs, mod_all[l, 1:1 + BS], lw, lp, lam_init, ctx, tables, final, gf)
    outs = tuple(jnp.stack([cl[i] for cl in caches], axis=1) for i in range(8))
    return (xp, xs) + outs
```

```python
import functools
import math

import numpy as np
import jax
import jax.numpy as jnp
from jax import lax
from jax.experimental import pallas as pl
from jax.experimental.pallas import tpu as pltpu

BF = jnp.bfloat16
F32 = jnp.float32

D_MODEL = 1024
DEPTH = 4
GRID_W = 64
ROPE_BASE = 10000.0
NORM_EPS = 1e-6
NEG_INF = -1e30
LOG2E = math.log2(math.e)

DA_QK = 32
NA_WIN_R = 8
NA_WIN_C = 16
SW_WINDOW = 128
MLA_NOPE = 64
MLA_ROPE = 32
N_EXPERTS = 16
EC_CAPACITY = 2

IN_PAD = 3200
LANES = 128

VMEM_BIG = 56 * 1024 * 1024
VMEM_MID = 40 * 1024 * 1024


def _cparams(sem, vmem=None):
    return pltpu.CompilerParams(dimension_semantics=sem, vmem_limit_bytes=vmem)


def _dot(a, b):
    return jnp.dot(a, b, preferred_element_type=F32)


def _dot_nt(a, b):
    return lax.dot_general(a, b, (((1,), (1,)), ((), ())), preferred_element_type=F32)


def _rms(x, g):
    var = jnp.mean(x * x, axis=-1, keepdims=True)
    return x * lax.rsqrt(var + NORM_EPS) * g


def _lane_mask(width, lo, length):
    lane = lax.broadcasted_iota(jnp.int32, (1, width), 1)
    return (lane >= lo) & (lane < lo + length)


def _mod_kernel(c_ref, w_ref, b_ref, o_ref):
    c = c_ref[...]
    s = c * (1.0 / (1.0 + jnp.exp(-c)))
    o_ref[0] = jnp.dot(s, w_ref[0], preferred_element_type=F32, precision=lax.Precision.HIGHEST) + b_ref[0]


def _modulation(cond, w_mod, b_mod):
    R, D = cond.shape
    L = w_mod.shape[0]
    nj = w_mod.shape[2] // D
    return pl.pallas_call(
        _mod_kernel,
        grid=(L, nj),
        in_specs=[
            pl.BlockSpec((R, D), lambda l, j: (0, 0)),
            pl.BlockSpec((1, D, D), lambda l, j: (l, 0, j)),
            pl.BlockSpec((1, 1, D), lambda l, j: (l, 0, j)),
        ],
        out_specs=pl.BlockSpec((1, R, D), lambda l, j: (l, 0, j)),
        out_shape=jax.ShapeDtypeStruct((L, R, nj * D), F32),
        compiler_params=_cparams(("parallel", "parallel")),
        name="modulation",
    )(cond, w_mod, b_mod.reshape(L, 1, nj * D))


def _rope(x, c, s, d):
    w = x.shape[-1]
    lane = lax.broadcasted_iota(jnp.int32, (1, LANES), 1)
    first = (lane % (2 * d)) < d
    outs = []
    for b in range(w // LANES):
        sl = slice(b * LANES, (b + 1) * LANES)
        xb = x[:, sl]
        partner = jnp.where(first, pltpu.roll(xb, LANES - d, 1), pltpu.roll(xb, d, 1))
        outs.append(xb * c[:, sl] + partner * s[:, sl])
    return outs[0] if len(outs) == 1 else jnp.concatenate(outs, axis=-1)


def _inproj_kernel(*refs, rope):
    if rope:
        (x_ref, mod_ref, g1_ref, w_ref, qn_ref, wuq_ref, kvn_ref,
         tac, tas, tcc, tcs, tqc, tqs, tkc, tks,
         oa, ob, oc, oq, ockv, okpe) = refs
    else:
        (x_ref, mod_ref, g1_ref, w_ref, qn_ref, wuq_ref, kvn_ref,
         oa, ob, oc, oq, ockv, okpe) = refs
    dt = oa.dtype
    x = x_ref[0]
    h = (_rms(x, g1_ref[...]) * (1.0 + mod_ref[0, 1:2, :]) + mod_ref[0, 0:1, :]).astype(BF)

    def seg(lo, w):
        return _dot(h, w_ref[:, lo:lo + w])

    ua = seg(0, 1024)
    qk = ua[:, :512]
    if rope:
        qk = _rope(qk, tac[...], tas[...], 8)
    oa[0, :, 0:256] = (qk[:, :256] * (DA_QK ** -0.5 * LOG2E)).astype(dt)
    oa[0, :, 256:512] = qk[:, 256:].astype(dt)
    ones_lane = lax.broadcasted_iota(jnp.int32, (1, 512), 1) % LANES >= 64
    oa[0, :, 512:1024] = jnp.where(ones_lane, 1.0, ua[:, 512:]).astype(dt)
    ub = seg(1024, 1024)
    ob[0, :, 0:256] = (ub[:, :256] * (0.125 * LOG2E)).astype(dt)
    ob[0, :, 256:512] = ub[:, 256:512].astype(dt)
    ob[0, :, 512:1024] = jnp.where(ones_lane, 1.0, ub[:, 512:]).astype(dt)
    uc = seg(2048, 640)
    qk = jnp.concatenate([uc[:, :256], uc[:, 512:]], axis=-1)
    if rope:
        qk = _rope(qk, tcc[...], tcs[...], 16)
    oc[0, :, 0:256] = (qk[:, :256] * (0.125 * LOG2E)).astype(dt)
    oc[0, :, 256:512] = jnp.where(ones_lane[:, :256], 1.0, uc[:, 256:512]).astype(dt)
    oc[0, :, 512:640] = qk[:, 256:].astype(dt)
    cq = _rms(seg(2688, 256), qn_ref[...]).astype(BF)
    qd = _dot(cq, wuq_ref[...])
    if rope:
        qd = _rope(qd, tqc[...], tqs[...], 8)
    oq[0] = (qd * ((MLA_NOPE + MLA_ROPE) ** -0.5 * LOG2E)).astype(dt)
    ockv[0] = _rms(seg(2944, 128), kvn_ref[...]).astype(dt)
    kpe = seg(3072, 128)
    if rope:
        kpe = _rope(kpe, tkc[...], tks[...], 8)
    okpe[0] = kpe.astype(dt)


def _in_proj(x, mod, g1, w_in_p, q_norm, w_uq_p, kv_norm, tables, out_dtype, tm):
    B, N, D = x.shape
    rope = tables is not None
    nmod = mod.shape[0]
    mod_map = (lambda j, b: (b, 0, 0)) if nmod > 1 else (lambda j, b: (0, 0, 0))
    const2 = lambda j, b: (0, 0)
    in_specs = [
        pl.BlockSpec((1, tm, D), lambda j, b: (b, j, 0)),
        pl.BlockSpec((1, 6, D), mod_map),
        pl.BlockSpec((1, D), const2),
        pl.BlockSpec((D, IN_PAD), const2),
        pl.BlockSpec((1, 256), const2),
        pl.BlockSpec((256, 512), const2),
        pl.BlockSpec((1, 128), const2),
    ]
    args = [x, mod, g1, w_in_p, q_norm, w_uq_p, kv_norm]
    if rope:
        for t in tables:
            in_specs.append(pl.BlockSpec((tm, t.shape[1]), lambda j, b: (j, 0)))
            args.append(t)
    widths = (1024, 1024, 640, 512, 128, 128)
    out_specs = [pl.BlockSpec((1, tm, w), lambda j, b: (b, j, 0)) for w in widths]
    out_shape = [jax.ShapeDtypeStruct((B, N, w), out_dtype) for w in widths]
    return pl.pallas_call(
        functools.partial(_inproj_kernel, rope=rope),
        grid=(N // tm, B),
        in_specs=in_specs,
        out_specs=out_specs,
        out_shape=out_shape,
        compiler_params=_cparams(("parallel", "parallel"), VMEM_BIG),
        name="in_proj_rope" if rope else "in_proj",
    )(*args)


def _mla_expand_kernel(ckv_ref, kpe_ref, wk_ref, wv_ref, e_ref, ok, ov):
    ckv = ckv_ref[0].astype(BF)
    kpe = kpe_ref[0].astype(BF)
    ok[0] = (_dot(ckv, wk_ref[...]) + _dot(kpe, e_ref[...])).astype(ok.dtype)
    ones_lane = lax.broadcasted_iota(jnp.int32, (1, 512), 1) % LANES >= 64
    ov[0] = jnp.where(ones_lane, 1.0, _dot(ckv, wv_ref[...])).astype(ov.dtype)


def _mla_expand(ckv, kpe, w_k, w_v, e_place, tk):
    B, K, _ = ckv.shape
    const2 = lambda b, j: (0, 0)
    return pl.pallas_call(
        _mla_expand_kernel,
        grid=(B, K // tk),
        in_specs=[
            pl.BlockSpec((1, tk, 128), lambda b, j: (b, j, 0)),
            pl.BlockSpec((1, tk, 128), lambda b, j: (b, j, 0)),
            pl.BlockSpec((128, 512), const2),
            pl.BlockSpec((128, 512), const2),
            pl.BlockSpec((128, 512), const2),
        ],
        out_specs=[pl.BlockSpec((1, tk, 512), lambda b, j: (b, j, 0)),
                   pl.BlockSpec((1, tk, 512), lambda b, j: (b, j, 0))],
        out_shape=[jax.ShapeDtypeStruct((B, K, 512), BF), jax.ShapeDtypeStruct((B, K, 512), BF)],
        compiler_params=_cparams(("parallel", "parallel")),
        name="mla_expand",
    )(ckv, kpe, w_k, w_v, e_place)


def _head_softmax(scores, values):
    m = functools.reduce(jnp.maximum, [jnp.max(s, axis=-1, keepdims=True) for s in scores])
    acc = None
    for s, v in zip(scores, values):
        part = _dot(jnp.exp2(s - m).astype(BF), v)
        acc = part if acc is None else acc + part
    return acc, m


def _normalise(acc):
    return acc * pltpu.roll(1.0 / acc, 64, 1)


def _store_heads(o_ref, heads):
    half = _lane_mask(LANES, 0, 64)
    for hp in range(len(heads) // 2):
        slab = jnp.where(half, heads[2 * hp], pltpu.roll(heads[2 * hp + 1], 64, 1))
        o_ref[0, :, hp * LANES:(hp + 1) * LANES] = slab.astype(o_ref.dtype)


def _dense_attn_kernel(*refs, groups, tk, n_chunks, has_sink, diff):
    it = iter(refs)
    q_ref, k_ref, v_ref = next(it), next(it), next(it)
    sink_ref = lam_ref = subln_ref = None
    if has_sink:
        sink_ref = next(it)
    if diff:
        lam_ref, subln_ref, lam_init_ref = next(it), next(it), next(it)
    o_ref = next(it)
    tq = q_ref.shape[1]

    lam = lam_init = None
    if diff:
        lam_init = lam_init_ref[0]
        dl = lam_ref[...]
        lam = (jnp.exp(jnp.sum(dl[0:1] * dl[1:2], axis=-1, keepdims=True))
               - jnp.exp(jnp.sum(dl[2:3] * dl[3:4], axis=-1, keepdims=True)) + lam_init)

    ng = len(groups)
    scratch = list(it)
    qg_s, m_s, acc_s = (scratch[j * ng:(j + 1) * ng] for j in range(3))
    for gi, (q_lo, k_lo, w, qmask, v_lo, head, sink_idx, sign) in enumerate(groups):
        qg = q_ref[0, :, q_lo:q_lo + w].astype(BF)
        if qmask is not None:
            qg = jnp.where(_lane_mask(w, *qmask), qg, jnp.zeros_like(qg))
        qg_s[gi][...] = qg
        m_s[gi][...] = jnp.full((tq, 1), NEG_INF, F32)
        acc_s[gi][...] = jnp.zeros((tq, LANES), F32)

    def keys_step(c):
        off = c * tk
        for gi, (q_lo, k_lo, w, qmask, v_lo, head, sink_idx, sign) in enumerate(groups):
            s = _dot_nt(qg_s[gi][...], k_ref[0, pl.ds(off, tk), k_lo:k_lo + w].astype(BF))
            m = m_s[gi][...]
            m_new = jnp.maximum(m, jnp.max(s, axis=-1, keepdims=True))
            p = jnp.exp2(s - m_new).astype(BF)
            pv = _dot(p, v_ref[0, pl.ds(off, tk), v_lo:v_lo + LANES].astype(BF))
            acc_s[gi][...] = jnp.exp2(m - m_new) * acc_s[gi][...] + pv
            m_s[gi][...] = m_new

    for c in range(n_chunks):
        keys_step(c)

    heads = {}
    ones_half = _lane_mask(LANES, 64, 64)
    for gi, (q_lo, k_lo, w, qmask, v_lo, head, sink_idx, sign) in enumerate(groups):
        acc = acc_s[gi][...]
        if sink_idx is not None:
            acc = acc + jnp.where(ones_half, jnp.exp2(sink_ref[sink_idx] * LOG2E - m_s[gi][...]), 0.0)
        part = _normalise(acc)
        if sign < 0:
            part = -lam * part
        heads[head] = part if head not in heads else heads[head] + part
    if diff:
        half = _lane_mask(LANES, 0, 64)
        for hd, o in heads.items():
            ms = jnp.sum(jnp.where(half, o * o, 0.0), axis=-1, keepdims=True) * (1.0 / 64.0)
            heads[hd] = o * lax.rsqrt(ms + NORM_EPS) * subln_ref[...] * (1.0 - lam_init)
    _store_heads(o_ref, heads)


def _dense_attn(q_arr, q_blk, k_arr, k_blk, v_arr, v_blk, *, groups, tq, tk, sink=None, diff=None, name):
    B, N, _ = q_arr.shape
    K = k_arr.shape[1]
    n_chunks = K // tk
    assert n_chunks * tk == K and n_chunks <= 2
    qw, qi = q_blk
    kw, ki = k_blk
    vw, vi = v_blk
    in_specs = [
        pl.BlockSpec((1, tq, qw), lambda b, i: (b, i, qi)),
        pl.BlockSpec((1, K, kw), lambda b, i: (b, 0, ki)),
        pl.BlockSpec((1, K, vw), lambda b, i: (b, 0, vi)),
    ]
    args = [q_arr, k_arr, v_arr]
    if sink is not None:
        in_specs.append(pl.BlockSpec(memory_space=pltpu.SMEM))
        args.append(sink)
    if diff is not None:
        da_lambda, subln, lam_init = diff
        in_specs += [pl.BlockSpec((4, DA_QK), lambda b, i: (0, 0)),
                     pl.BlockSpec((1, LANES), lambda b, i: (0, 0)),
                     pl.BlockSpec(memory_space=pltpu.SMEM)]
        args += [da_lambda, subln, jnp.full((1,), lam_init, F32)]
    kern = functools.partial(_dense_attn_kernel, groups=groups, tk=tk, n_chunks=n_chunks,
                             has_sink=sink is not None, diff=diff is not None)
    scratch = ([pltpu.VMEM((tq, g[2]), BF) for g in groups] + [pltpu.VMEM((tq, 1), F32) for _ in groups]
               + [pltpu.VMEM((tq, LANES), F32) for _ in groups])
    return pl.pallas_call(
        kern,
        grid=(B, N // tq),
        in_specs=in_specs,
        out_specs=pl.BlockSpec((1, tq, 256), lambda b, i: (b, i, 0)),
        out_shape=jax.ShapeDtypeStruct((B, N, 256), BF),
        scratch_shapes=scratch,
        compiler_params=_cparams(("parallel", "parallel"), VMEM_BIG),
        name=name,
    )(*args)


GROUPS_A = tuple((128 * (g // 4), 128 * (g // 4), 128, (32 * (g % 4), 32), 128 * (g // 2), g // 2, None,
                  1 if g % 2 == 0 else -1) for g in range(8))
GROUPS_B = tuple((128 * (h // 2), 128 * (h // 2), 128, (64 * (h % 2), 64), 128 * h, h, None, 1) for h in range(4))
GROUPS_C = tuple((128 * g, 0, 128, (64 * j, 64), 128 * j, 2 * g + j, 2 * j + g, 1) for g in range(2) for j in range(2))
GROUPS_D = tuple((128 * h, 128 * h, 128, None, 128 * h, h, None, 1) for h in range(4))


def _window_attn_kernel(q_ref, k_ref, v_ref, ck_ref, cv_ref, sink_ref, o_ref, *, n, band):
    tq = q_ref.shape[1]
    i = pl.program_id(1)
    start = jnp.clip(i * tq - SW_WINDOW, 0, n - band)
    start = pl.multiple_of(start, 128)
    kb = k_ref[0, pl.ds(start, band), :]
    vb = v_ref[0, pl.ds(start, band), :]
    ck = ck_ref[0]
    cv = cv_ref[0]
    qpos = i * tq + lax.broadcasted_iota(jnp.int32, (tq, band), 0)
    kpos = start + lax.broadcasted_iota(jnp.int32, (tq, band), 1)
    valid = jnp.abs(qpos - kpos) <= SW_WINDOW
    ones_half = _lane_mask(LANES, 64, 64)
    heads = {}
    for g in range(2):
        for j in range(2):
            qg = q_ref[0, :, LANES * g:LANES * (g + 1)]
            qg = jnp.where(_lane_mask(LANES, 64 * j, 64), qg, jnp.zeros_like(qg))
            vs = slice(LANES * j, LANES * (j + 1))
            acc, m = _head_softmax([_dot_nt(qg, ck), jnp.where(valid, _dot_nt(qg, kb), NEG_INF)],
                                   [cv[:, vs], vb[:, vs]])
            acc = acc + jnp.where(ones_half, jnp.exp2(sink_ref[2 * j + g] * LOG2E - m), 0.0)
            heads[2 * g + j] = _normalise(acc)
    _store_heads(o_ref, heads)


def _window_attn(qkv_c, ck, cv, sink, tq):
    B, N, _ = qkv_c.shape
    band = tq + 2 * SW_WINDOW
    P = ck.shape[1]
    return pl.pallas_call(
        functools.partial(_window_attn_kernel, n=N, band=band),
        grid=(B, N // tq),
        in_specs=[
            pl.BlockSpec((1, tq, 256), lambda b, i: (b, i, 0)),
            pl.BlockSpec((1, N, 128), lambda b, i: (b, 0, 4)),
            pl.BlockSpec((1, N, 256), lambda b, i: (b, 0, 1)),
            pl.BlockSpec((1, P, 128), lambda b, i: (b, 0, 0)),
            pl.BlockSpec((1, P, 256), lambda b, i: (b, 0, 0)),
            pl.BlockSpec(memory_space=pltpu.SMEM),
        ],
        out_specs=pl.BlockSpec((1, tq, 256), lambda b, i: (b, i, 0)),
        out_shape=jax.ShapeDtypeStruct((B, N, 256), BF),
        compiler_params=_cparams(("parallel", "parallel")),
        name="window_attn",
    )(qkv_c, qkv_c, qkv_c, ck, cv, sink)


NA_TILE_ROWS = 4
NA_KEY_ROWS = NA_TILE_ROWS + NA_WIN_R


def _nbr_attn_kernel(q_ref, k_ref, v_ref, ck_ref, cv_ref, bias_ref, o_ref, *, rows):
    i = pl.program_id(1)
    krow = jnp.clip(i * NA_TILE_ROWS - NA_WIN_R // 2, 0, rows - NA_KEY_ROWS)
    start = pl.multiple_of(krow * GRID_W, GRID_W)
    nk = NA_KEY_ROWS * GRID_W
    kb = k_ref[0, pl.ds(start, nk), :]
    vb = v_ref[0, pl.ds(start, nk), :]
    ck = ck_ref[0]
    cv = cv_ref[0]
    q = q_ref[0]
    heads = {}
    for h in range(4):
        ks = slice(LANES * (h // 2), LANES * (h // 2 + 1))
        qg = jnp.where(_lane_mask(LANES, 64 * (h % 2), 64), q[:, ks], jnp.zeros_like(q[:, ks]))
        vs = slice(LANES * h, LANES * (h + 1))
        acc, _ = _head_softmax([_dot_nt(qg, ck[:, ks]), _dot_nt(qg, kb[:, ks]) + bias_ref[0, h]],
                               [cv[:, vs], vb[:, vs]])
        heads[h] = _normalise(acc)
    _store_heads(o_ref, heads)


def _nbr_attn(qkv_b, ck, cv, bias):
    B, N, _ = qkv_b.shape
    rows = N // GRID_W
    tq = NA_TILE_ROWS * GRID_W
    nt = N // tq
    nk = NA_KEY_ROWS * GRID_W
    P = ck.shape[1]

    def bias_map(b, i):
        return (jnp.where(i == 0, 0, jnp.where(i == nt - 1, 2, 1)), 0, 0, 0)

    return pl.pallas_call(
        functools.partial(_nbr_attn_kernel, rows=rows),
        grid=(B, nt),
        in_specs=[
            pl.BlockSpec((1, tq, 256), lambda b, i: (b, i, 0)),
            pl.BlockSpec((1, N, 256), lambda b, i: (b, 0, 1)),
            pl.BlockSpec((1, N, 512), lambda b, i: (b, 0, 1)),
            pl.BlockSpec((1, P, 256), lambda b, i: (b, 0, 0)),
            pl.BlockSpec((1, P, 512), lambda b, i: (b, 0, 0)),
            pl.BlockSpec((1, 4, tq, nk), bias_map),
        ],
        out_specs=pl.BlockSpec((1, tq, 256), lambda b, i: (b, i, 0)),
        out_shape=jax.ShapeDtypeStruct((B, N, 256), BF),
        compiler_params=_cparams(("parallel", "parallel"), VMEM_MID),
        name="nbr_attn",
    )(qkv_b, qkv_b, qkv_b, ck, cv, bias)


def _nbr_bias_tables(rpb, rows):
    H = rpb.shape[0]
    rp = jnp.pad(rpb.astype(F32), ((0, 0), (NA_KEY_ROWS, NA_KEY_ROWS), (GRID_W, GRID_W)))
    c0 = NA_WIN_C - 1 + GRID_W
    tcol = jnp.stack([rp[:, :, c0 - qc:c0 - qc + GRID_W] for qc in range(GRID_W)], axis=2)
    tabs = []
    nt = rows // NA_TILE_ROWS
    for r0 in (0, NA_TILE_ROWS * min(1, nt - 1) + NA_TILE_ROWS, rows - NA_TILE_ROWS):
        ks = int(np.clip(r0 - NA_WIN_R // 2, 0, rows - NA_KEY_ROWS))
        per_row = []
        for a in range(NA_TILE_ROWS):
            s0 = ks - (r0 + a) + NA_WIN_R - 1 + NA_KEY_ROWS
            per_row.append(tcol[:, s0:s0 + NA_KEY_ROWS])
        t = jnp.stack(per_row, axis=1).transpose(0, 1, 3, 2, 4)
        r = r0 + np.arange(NA_TILE_ROWS)[:, None, None, None]
        qc = np.arange(GRID_W)[None, :, None, None]
        kr = ks + np.arange(NA_KEY_ROWS)[None, None, :, None]
        kc = np.arange(GRID_W)[None, None, None, :]
        rs = np.clip(r - NA_WIN_R // 2, 0, rows - NA_WIN_R)
        cs = np.clip(qc - NA_WIN_C // 2, 0, GRID_W - NA_WIN_C)
        ok = (kr >= rs) & (kr < rs + NA_WIN_R) & (kc >= cs) & (kc < cs + NA_WIN_C)
        t = jnp.where(ok[None], t * LOG2E, NEG_INF)
        tabs.append(t.reshape(H, NA_TILE_ROWS * GRID_W, NA_KEY_ROWS * GRID_W))
    return jnp.stack(tabs)


def _merge_kernel(x_ref, mod_ref, g1_ref, g2_ref, oa, ob, oc, od, wg_ref, wb_ref, wo_ref, wr_ref,
                  xo_ref, h2_ref, lg_ref):
    D = x_ref.shape[2]
    x = x_ref[0]
    h = (_rms(x, g1_ref[...]) * (1.0 + mod_ref[0, 1:2, :]) + mod_ref[0, 0:1, :]).astype(BF)
    merged = None
    for i, o_ref in enumerate((oa, ob, oc, od)):
        z = _dot(h, wg_ref[:, i * D:(i + 1) * D])
        gate = 1.0 / (1.0 + jnp.exp(-z))
        term = gate * _dot(o_ref[0], wb_ref[i])
        merged = term if merged is None else merged + term
    mixed = _dot(merged.astype(BF), wo_ref[...])
    xn = x + mod_ref[0, 2:3, :] * mixed
    xo_ref[0] = xn
    h2 = (_rms(xn, g2_ref[...]) * (1.0 + mod_ref[0, 4:5, :]) + mod_ref[0, 3:4, :]).astype(BF)
    h2_ref[0] = h2
    lg_ref[0] = _dot_nt(wr_ref[...], h2)


def _merge(x, mod, g1, g2, o_a, o_b, o_c, o_d, w_gate, w_branch, w_out, w_router_p, tm):
    B, N, D = x.shape
    nmod = mod.shape[0]
    mod_map = (lambda b, j: (b, 0, 0)) if nmod > 1 else (lambda b, j: (0, 0, 0))
    tok = lambda w: pl.BlockSpec((1, tm, w), lambda b, j: (b, j, 0))
    c2 = lambda b, j: (0, 0)
    return pl.pallas_call(
        _merge_kernel,
        grid=(B, N // tm),
        in_specs=[
            tok(D), pl.BlockSpec((1, 6, D), mod_map), pl.BlockSpec((1, D), c2), pl.BlockSpec((1, D), c2),
            tok(256), tok(256), tok(256), tok(256),
            pl.BlockSpec((D, 4 * D), c2),
            pl.BlockSpec((4, 256, D), lambda b, j: (0, 0, 0)),
            pl.BlockSpec((D, D), c2),
            pl.BlockSpec((N_EXPERTS, D), c2),
        ],
        out_specs=[tok(D), tok(D), pl.BlockSpec((1, N_EXPERTS, tm), lambda b, j: (b, 0, j))],
        out_shape=[jax.ShapeDtypeStruct((B, N, D), F32), jax.ShapeDtypeStruct((B, N, D), BF),
                   jax.ShapeDtypeStruct((B, N_EXPERTS, N), F32)],
        compiler_params=_cparams(("parallel", "parallel"), VMEM_BIG),
        name="merge",
    )(x, mod, g1, g2, o_a, o_b, o_c, o_d, w_gate, w_branch, w_out, w_router_p)


MOE_TT = 256
MOE_RC = 64
SLOT_ALIGN = 16


def _excl_cumsum(x, tri, reset_every):
    R, N = x.shape
    outs = []
    carry = jnp.zeros((R, 1), F32)
    for c in range(N // LANES):
        if reset_every and (c * LANES) % reset_every == 0:
            carry = jnp.zeros((R, 1), F32)
        xc = x[:, c * LANES:(c + 1) * LANES]
        inc = _dot(xc.astype(BF), tri)
        outs.append(inc - xc + carry)
        carry = carry + inc[:, LANES - 1:LANES]
    return (outs[0] if len(outs) == 1 else jnp.concatenate(outs, axis=1)), carry


def _route_kernel(lg_ref, aff_ref, rel_ref, base_ref, *, cap, tt):
    B, E, N = lg_ref.shape
    R = B * E
    nt = N // tt
    lg = lg_ref[...]
    ex = jnp.exp(lg - jnp.max(lg, axis=1, keepdims=True))
    aff = ex / jnp.sum(ex, axis=1, keepdims=True)
    aff_ref[...] = aff
    bits = pltpu.bitcast(aff.reshape(R, N), jnp.int32)

    def per_expert(v):
        t = v[0:E]
        for b in range(1, B):
            t = t + v[b * E:(b + 1) * E]
        return t

    def per_row(v):
        return jnp.concatenate([v] * B, axis=0) if B > 1 else v

    def batch_prefix(v):
        parts, run = [], jnp.zeros((E, 1), F32)
        for b in range(B):
            parts.append(run)
            run = run + v[b * E:(b + 1) * E]
        return jnp.concatenate(parts, axis=0) if B > 1 else parts[0]

    def count(mask):
        return per_expert(jnp.sum(jnp.where(mask, 1.0, 0.0), axis=1, keepdims=True))

    def search(i, thr):
        cand = thr | jnp.left_shift(jnp.int32(1), 30 - i)
        return jnp.where(count(bits >= per_row(cand)) >= float(cap), cand, thr)

    thr = per_row(lax.fori_loop(0, 31, search, jnp.zeros((E, 1), jnp.int32)))
    gt = bits > thr
    eq = bits == thr
    need = float(cap) - count(gt)
    ri = lax.broadcasted_iota(jnp.int32, (LANES, LANES), 0)
    ci = lax.broadcasted_iota(jnp.int32, (LANES, LANES), 1)
    tri = jnp.where(ri <= ci, 1.0, 0.0).astype(BF)
    rank_eq, eq_tot = _excl_cumsum(jnp.where(eq, 1.0, 0.0), tri, None)
    rank_eq = rank_eq + batch_prefix(eq_tot)
    sel = gt | (eq & (rank_eq < per_row(need)))
    self = jnp.where(sel, 1.0, 0.0)
    local, _ = _excl_cumsum(self, tri, tt)
    rel_ref[...] = jnp.where(sel, local, -1.0).astype(jnp.int32).reshape(B, E, N)
    tn = jnp.right_shift(lax.broadcasted_iota(jnp.int32, (N, LANES), 0), tt.bit_length() - 1)
    tj = lax.broadcasted_iota(jnp.int32, (N, LANES), 1)
    cnt = _dot(self.astype(BF), jnp.where(tn == tj, 1.0, 0.0).astype(BF)).astype(jnp.int32)
    sh = SLOT_ALIGN.bit_length() - 1
    ru = jnp.left_shift(jnp.right_shift(cnt + (SLOT_ALIGN - 1), sh), sh).astype(F32)
    base_b = _dot(ru.astype(BF), jnp.where(ri < ci, 1.0, 0.0).astype(BF))
    base = base_b + batch_prefix(base_b[:, nt:nt + 1])
    base_ref[...] = base.astype(jnp.int32).reshape(B, E, LANES)


def _route(lg_t, cap):
    B, E, N = lg_t.shape
    full = lambda s: pl.BlockSpec(s, lambda i: (0,) * len(s))
    return pl.pallas_call(
        functools.partial(_route_kernel, cap=cap, tt=MOE_TT),
        grid=(1,),
        in_specs=[full((B, E, N))],
        out_specs=[full((B, E, N)), full((B, E, N)), full((B, E, LANES))],
        out_shape=[jax.ShapeDtypeStruct((B, E, N), F32), jax.ShapeDtypeStruct((B, E, N), jnp.int32),
                   jax.ShapeDtypeStruct((B, E, LANES), jnp.int32)],
        compiler_params=_cparams(("arbitrary",), VMEM_BIG),
        name="route",
    )(lg_t)


def _slot_onehot(rel, values, c, rc):
    E, tt = rel.shape
    r = lax.broadcasted_iota(jnp.int32, (E, rc, tt), 1) + c * rc
    return jnp.where(rel[:, None, :] == r, values[:, None, :], 0.0).astype(BF).reshape(E * rc, tt)


def _dispatch_kernel(base_sm, x_ref, rel_ref, xe_in, xe_hbm, stage, stage2, sem, sem2, *, nt, rc):
    del xe_in
    b, j = pl.program_id(0), pl.program_id(1)
    E = rel_ref.shape[1]
    step = b * nt + j
    last = pl.num_programs(0) * nt - 1
    base = lambda e, jj: base_sm[(b * E + e) * (nt + 1) + jj]
    rel = rel_ref[0]
    x = x_ref[0]
    ones = jnp.ones(rel.shape, F32)

    def copy(buf, s, e, c):
        start = pl.multiple_of(base(e, j) + c * rc, SLOT_ALIGN)
        return pltpu.make_async_copy(buf.at[pl.ds(e * rc, rc)], xe_hbm.at[e, pl.ds(start, rc)], s.at[e])

    strip = _dot(_slot_onehot(rel, ones, 0, rc), x).astype(BF)

    @pl.when(step > 0)
    def _():
        for e in range(E):
            copy(stage, sem, e, 0).wait()

    stage[...] = strip
    for e in range(E):
        copy(stage, sem, e, 0).start()

    span = [base(e, j + 1) - base(e, j) for e in range(E)]
    widest = functools.reduce(jnp.maximum, span)

    def extra(c, carry):
        stage2[...] = _dot(_slot_onehot(rel, ones, c, rc), x).astype(BF)
        for e in range(E):
            @pl.when(span[e] > c * rc)
            def _():
                cp = copy(stage2, sem2, e, c)
                cp.start()
                cp.wait()
        return carry

    lax.fori_loop(1, (widest + rc - 1) // rc, extra, 0)

    @pl.when(step == last)
    def _():
        for e in range(E):
            copy(stage, sem, e, 0).wait()


def _dispatch(h2, rel, base_flat, rows):
    B, N, D = h2.shape
    E = rel.shape[1]
    nt = N // MOE_TT
    xe0 = jnp.zeros((E, rows, D), BF)
    return pl.pallas_call(
        functools.partial(_dispatch_kernel, nt=nt, rc=MOE_RC),
        grid_spec=pltpu.PrefetchScalarGridSpec(
            num_scalar_prefetch=1,
            grid=(B, nt),
            in_specs=[
                pl.BlockSpec((1, MOE_TT, D), lambda b, j, s: (b, j, 0)),
                pl.BlockSpec((1, E, MOE_TT), lambda b, j, s: (b, 0, j)),
                pl.BlockSpec(memory_space=pl.ANY),
            ],
            out_specs=pl.BlockSpec(memory_space=pl.ANY),
            scratch_shapes=[pltpu.VMEM((E * MOE_RC, D), BF), pltpu.VMEM((E * MOE_RC, D), BF),
                            pltpu.SemaphoreType.DMA((E,)), pltpu.SemaphoreType.DMA((E,))],
        ),
        out_shape=jax.ShapeDtypeStruct((E, rows, D), BF),
        input_output_aliases={3: 0},
        compiler_params=_cparams(("arbitrary", "arbitrary")),
        name="dispatch",
    )(base_flat, h2, rel, xe0)


def _ffn_kernel(tot_sm, x_ref, wg_hbm, wu_hbm, wd_hbm, y_ref, wbuf, wg_s, wu_s, wd_s, sem, *, tm, rc):
    e, j = pl.program_id(0), pl.program_id(1)
    slot = e % 2

    def copy(ee, sl, k):
        return pltpu.make_async_copy((wg_hbm, wu_hbm, wd_hbm)[k].at[ee], wbuf.at[sl, k], sem.at[sl, k])

    @pl.when(j == 0)
    def _():
        @pl.when(e == 0)
        def _():
            for k in range(3):
                copy(e, slot, k).start()

        @pl.when(e + 1 < pl.num_programs(0))
        def _():
            for k in range(3):
                copy(e + 1, 1 - slot, k).start()

        for k in range(3):
            copy(e, slot, k).wait()
        wg_s[...] = wbuf[slot, 0].astype(BF)
        wu_s[...] = wbuf[slot, 1].astype(BF)
        wd_s[...] = wbuf[slot, 2].astype(BF)

    used = j * tm < tot_sm[e] + rc

    @pl.when(used)
    def _():
        x = x_ref[0]
        hg = _dot(x, wg_s[...])
        hu = _dot(x, wu_s[...])
        act = (hg * (1.0 / (1.0 + jnp.exp(-hg))) * hu).astype(BF)
        y_ref[0] = _dot(act, wd_s[...]).astype(y_ref.dtype)

    @pl.when(jnp.logical_not(used))
    def _():
        y_ref[...] = jnp.zeros_like(y_ref)


def _expert_ffn(xe, total, w_g, w_u, w_d, tm):
    E, rows, D = xe.shape
    DE = w_g.shape[2]
    tok = lambda e, j, tot: (e, jnp.minimum(j, (tot[e] + MOE_RC - 1) // tm), 0)
    assert D == DE
    hbm = pl.BlockSpec(memory_space=pl.ANY)
    return pl.pallas_call(
        functools.partial(_ffn_kernel, tm=tm, rc=MOE_RC),
        grid_spec=pltpu.PrefetchScalarGridSpec(
            num_scalar_prefetch=1,
            grid=(E, rows // tm),
            in_specs=[pl.BlockSpec((1, tm, D), tok), hbm, hbm, hbm],
            out_specs=pl.BlockSpec((1, tm, D), lambda e, j, tot: (e, j, 0)),
            scratch_shapes=[pltpu.VMEM((2, 3, D, DE), F32), pltpu.VMEM((D, DE), BF), pltpu.VMEM((D, DE), BF),
                            pltpu.VMEM((DE, D), BF), pltpu.SemaphoreType.DMA((2, 3))],
        ),
        out_shape=jax.ShapeDtypeStruct((E, rows, D), BF),
        compiler_params=_cparams(("arbitrary", "arbitrary"), VMEM_BIG),
        name="expert_ffn",
    )(total, xe, w_g, w_u, w_d)


def _combine_kernel(base_sm, x_ref, rel_ref, aff_ref, mod_ref, gf_ref, ye_hbm, o_ref, stage, sem, *,
                    nt, rc, final):
    b, j = pl.program_id(0), pl.program_id(1)
    E = rel_ref.shape[1]
    step = b * nt + j
    nsteps = pl.num_programs(0) * nt
    slot = step % 2
    base = lambda e, jj: base_sm[(b * E + e) * (nt + 1) + jj]
    rel = rel_ref[0]
    aff = aff_ref[0]

    def copy(st, sl, e, c):
        first = base_sm[(st // nt * E + e) * (nt + 1) + st % nt]
        start = pl.multiple_of(first + c * rc, SLOT_ALIGN)
        return pltpu.make_async_copy(ye_hbm.at[e, pl.ds(start, rc)], stage.at[sl, pl.ds(e * rc, rc)],
                                     sem.at[sl, e])

    def scatter(c):
        w = _slot_onehot(rel, aff, c, rc)
        return lax.dot_general(w, stage[slot], (((0,), (0,)), ((), ())), preferred_element_type=F32)

    @pl.when(step == 0)
    def _():
        for e in range(E):
            copy(step, slot, e, 0).start()

    @pl.when(step + 1 < nsteps)
    def _():
        for e in range(E):
            copy(step + 1, 1 - slot, e, 0).start()

    for e in range(E):
        copy(step, slot, e, 0).wait()
    acc = scatter(0)
    span = [base(e, j + 1) - base(e, j) for e in range(E)]
    widest = functools.reduce(jnp.maximum, span)

    def extra(c, acc):
        for e in range(E):
            @pl.when(span[e] > c * rc)
            def _():
                cp = copy(step, slot, e, c)
                cp.start()
                cp.wait()
        return acc + scatter(c)

    acc = lax.fori_loop(1, (widest + rc - 1) // rc, extra, acc)
    xn = x_ref[0] + mod_ref[0, 5:6, :] * acc
    if final:
        xn = _rms(xn, gf_ref[...])
    o_ref[0] = xn


def _combine(x, rel, aff, mod, g_final, ye, base_flat, final):
    B, N, D = x.shape
    E = rel.shape[1]
    nt = N // MOE_TT
    nmod = mod.shape[0]
    mod_map = (lambda b, j, s: (b, 0, 0)) if nmod > 1 else (lambda b, j, s: (0, 0, 0))
    tok = pl.BlockSpec((1, MOE_TT, D), lambda b, j, s: (b, j, 0))
    slots = pl.BlockSpec((1, E, MOE_TT), lambda b, j, s: (b, 0, j))
    return pl.pallas_call(
        functools.partial(_combine_kernel, nt=nt, rc=MOE_RC, final=final),
        grid_spec=pltpu.PrefetchScalarGridSpec(
            num_scalar_prefetch=1,
            grid=(B, nt),
            in_specs=[tok, slots, slots, pl.BlockSpec((1, 6, D), mod_map),
                      pl.BlockSpec((1, D), lambda b, j, s: (0, 0)), pl.BlockSpec(memory_space=pl.ANY)],
            out_specs=tok,
            scratch_shapes=[pltpu.VMEM((2, E * MOE_RC, D), BF), pltpu.SemaphoreType.DMA((2, E))],
        ),
        out_shape=jax.ShapeDtypeStruct((B, N, D), F32),
        compiler_params=_cparams(("arbitrary", "arbitrary")),
        name="combine_final" if final else "combine",
    )(base_flat, x, rel, aff, mod, g_final, ye)


def _moe(x, h2, lg_t, mod, g_final, w_g, w_u, w_d, final):
    B, N, D = h2.shape
    T = B * N
    nt = N // MOE_TT
    cap = EC_CAPACITY * T // N_EXPERTS
    tm = 512 if cap >= 2048 else 256
    worst = cap + (SLOT_ALIGN - 1) * (T // MOE_TT) + MOE_RC
    rows = -(-worst // tm) * tm
    aff, rel, base = _route(lg_t, cap)
    base_flat = base[:, :, :nt + 1].reshape(-1)
    total = base[B - 1, :, nt]
    xe = _dispatch(h2, rel, base_flat, rows)
    ye = _expert_ffn(xe, total, w_g, w_u, w_d, tm)
    return _combine(x, rel, aff, mod, g_final, ye, base_flat, final)


def _rope_tables(n):
    t = jnp.arange(n)
    pos = {1: (t // GRID_W).astype(F32), 2: (t % GRID_W).astype(F32)}

    def group(quarter):
        freqs = jnp.float32(ROPE_BASE) ** (-jnp.arange(quarter, dtype=F32) / quarter)
        cs, ss = [], []
        for kind in (1, 2):
            ang = pos[kind][:, None] * freqs
            c, s = jnp.cos(ang), jnp.sin(ang)
            cs += [c, c]
            ss += [-s, s]
        return jnp.concatenate(cs, axis=1), jnp.concatenate(ss, axis=1)

    c32, s32 = group(8)
    c64, s64 = group(16)
    one = lambda w: jnp.ones((n, w), F32)
    zero = lambda w: jnp.zeros((n, w), F32)
    ta = (jnp.tile(c32, (1, 16)), jnp.tile(s32, (1, 16)))
    tc = (jnp.tile(c64, (1, 6)), jnp.tile(s64, (1, 6)))
    qc = jnp.concatenate([one(64), c32, one(32)], axis=1)
    qs = jnp.concatenate([zero(64), s32, zero(32)], axis=1)
    tq = (jnp.tile(qc, (1, 4)), jnp.tile(qs, (1, 4)))
    tk = (jnp.concatenate([c32, one(96)], axis=1), jnp.concatenate([s32, zero(96)], axis=1))
    return (ta[0], ta[1], tc[0], tc[1], tq[0], tq[1], tk[0], tk[1])


LAT_TQ = 512
C_HEAD_ORDER = (0, 2, 1, 3)


def _prep_layer(l, w_in, mla_w_uq, mla_w_ukv, w_gate, w_branch, w_out, w_router):
    D = D_MODEL
    wi = w_in[l]
    cq = wi[:, 1536:1792].reshape(D, 4, 64)[:, C_HEAD_ORDER, :].reshape(D, 256)
    def spread(cols, heads):
        return jnp.pad(cols.reshape(D, heads, 64), ((0, 0), (0, 0), (0, 64))).reshape(D, heads * LANES)

    w_in_p = jnp.concatenate(
        [wi[:, :512], spread(wi[:, 512:768], 4), wi[:, 768:1280], spread(wi[:, 1280:1536], 4),
         cq, spread(wi[:, 1920:2048], 2), wi[:, 1792:1920], wi[:, 2048:],
         jnp.zeros((D, LANES - MLA_ROPE), F32)], axis=1).astype(BF)
    uq = mla_w_uq[l].reshape(256, 4, MLA_NOPE + MLA_ROPE)
    w_uq_p = jnp.concatenate([uq, jnp.zeros((256, 4, 32), F32)], axis=2).reshape(256, 512).astype(BF)
    ukv = mla_w_ukv[l].reshape(128, 4, 128)
    w_k = jnp.concatenate([ukv[:, :, :64], jnp.zeros((128, 4, 64), F32)], axis=2).reshape(128, 512).astype(BF)
    w_v = jnp.concatenate([ukv[:, :, 64:], jnp.zeros((128, 4, 64), F32)], axis=2).reshape(128, 512).astype(BF)
    wb = w_branch[l]
    wb_c = wb[2].reshape(4, 64, D)[C_HEAD_ORDER, :, :].reshape(256, D)
    w_branch_p = jnp.stack([wb[0], wb[1], wb_c, wb[3]]).astype(BF)
    w_router_p = w_router[l].T.astype(BF)
    return dict(w_in=w_in_p, w_uq=w_uq_p, w_k=w_k, w_v=w_v, w_gate=w_gate[l].astype(BF),
                w_branch=w_branch_p, w_out=w_out[l].astype(BF), w_router=w_router_p)


def _kpe_placement():
    e = np.zeros((128, 512), np.float32)
    for h in range(4):
        for i in range(MLA_ROPE):
            e[i, 128 * h + MLA_NOPE + i] = 1.0
    return jnp.asarray(e, BF)


def _layer(x, mod, lw, lp, lam_init, ctx, tables, final, g_final):
    B, N, D = x.shape
    latent = ctx is not None
    tm = 512 if latent else 256
    out_dtype = BF if latent else F32
    qkv_a, qkv_b, qkv_c, q_d, ckv_n, kpe_r = _in_proj(
        x, mod, lp["g_norm1"], lw["w_in"], lp["mla_q_norm"], lw["w_uq"], lp["mla_kv_norm"],
        tables if latent else None, out_dtype, tm)
    diff = (lp["da_lambda"], jnp.tile(lp["da_subln"], (1, 2)), lam_init)
    if latent:
        ck_a, cv_a, ck_b, cv_b, ck_c, cv_c, ckv_c, kpe_c = ctx
        P = ck_a.shape[1]

        def with_ones(v, heads):
            v = v.reshape(B, P, heads, 64).astype(BF)
            return jnp.concatenate([v, jnp.ones_like(v)], axis=-1).reshape(B, P, heads * LANES)

        k_a = jnp.concatenate([ck_a.astype(BF), qkv_a[:, :, 256:512]], axis=1)
        v_a = jnp.concatenate([with_ones(cv_a, 4), qkv_a[:, :, 512:1024]], axis=1)
        o_a = _dense_attn(qkv_a, (256, 0), k_a, (256, 0), v_a, (512, 0), groups=GROUPS_A,
                          tq=LAT_TQ, tk=(N + P) // 2, diff=diff, name="diff_attn")
        o_b = _nbr_attn(qkv_b, ck_b.astype(BF), with_ones(cv_b, 4), lp["na_bias"])
        o_c = _window_attn(qkv_c, ck_c.astype(BF), with_ones(cv_c, 2), lp["sw_sink"], 256)
        ckv_all = jnp.concatenate([ckv_c.astype(BF), ckv_n], axis=1)
        kpe_all = jnp.concatenate([jnp.pad(kpe_c, ((0, 0), (0, 0), (0, 128 - MLA_ROPE))).astype(BF), kpe_r], axis=1)
        k_d, v_d = _mla_expand(ckv_all, kpe_all, lw["w_k"], lw["w_v"], lp["e_place"], 256)
        o_d = _dense_attn(q_d, (512, 0), k_d, (512, 0), v_d, (512, 0), groups=GROUPS_D,
                          tq=LAT_TQ, tk=(N + P) // 2, name="mla_attn")
    else:
        o_a = _dense_attn(qkv_a, (256, 0), qkv_a, (256, 1), qkv_a, (512, 1), groups=GROUPS_A,
                          tq=N, tk=N, diff=diff, name="diff_attn_ctx")
        o_b = _dense_attn(qkv_b, (256, 0), qkv_b, (256, 1), qkv_b, (512, 1), groups=GROUPS_B,
                          tq=N, tk=N, name="dense_attn_ctx")
        o_c = _dense_attn(qkv_c, (256, 0), qkv_c, (128, 4), qkv_c, (256, 1), groups=GROUPS_C,
                          tq=N, tk=N, sink=lp["sw_sink"], name="gqa_attn_ctx")
        k_d, v_d = _mla_expand(ckv_n, kpe_r, lw["w_k"], lw["w_v"], lp["e_place"], N)
        o_d = _dense_attn(q_d, (512, 0), k_d, (512, 0), v_d, (512, 0), groups=GROUPS_D,
                          tq=N, tk=N, name="mla_attn_ctx")
    x_new, h2, logits = _merge(x, mod, lp["g_norm1"], lp["g_norm2"], o_a, o_b, o_c, o_d,
                               lw["w_gate"], lw["w_branch"], lw["w_out"], lw["w_router"], tm)
    x_out = _moe(x_new, h2, logits, mod, g_final, lp["w_e_gate"], lp["w_e_up"], lp["w_e_down"], final)
    cache = None
    if not latent:
        strip = lambda v, heads: v.reshape(B, N, heads, LANES)[..., :64]
        cache = (qkv_a[:, :, 256:512].reshape(B, N, 4, 64), strip(qkv_a[:, :, 512:1024], 4),
                 qkv_b[:, :, 256:512].reshape(B, N, 4, 64), strip(qkv_b[:, :, 512:1024], 4),
                 qkv_c[:, :, 512:640].reshape(B, N, 2, 64), strip(qkv_c[:, :, 256:512], 2),
                 ckv_n, kpe_r[:, :, :MLA_ROPE])
    return x_out, cache


def kernel(x_prompt, x_sample, cache_diff_k, cache_diff_v, cache_na_k, cache_na_v, cache_swa_k, cache_swa_v, cache_mla_ckv, cache_mla_kpe, c, c_ctx, w_mod, b_mod, g_norm1, g_norm2, w_in, da_lambda, da_subln, na_rpb, sw_sink, mla_q_norm, mla_w_uq, mla_kv_norm, mla_w_ukv, w_gate, w_branch, w_out, w_router, w_e_gate, w_e_up, w_e_down, g_final):
    D = D_MODEL
    BS, NS, _ = x_sample.shape
    P = cache_diff_k.shape[2]
    cond = jnp.concatenate([c_ctx[None], c, jnp.zeros((16 - 1 - BS, D), F32)], axis=0)
    mod_all = _modulation(cond, w_mod, b_mod).reshape(DEPTH, 16, 6, D)
    tables = _rope_tables(NS)
    e_place = _kpe_placement()
    gf = g_final.reshape(1, D)
    xp, xs = x_prompt, x_sample
    caches = []
    for l in range(DEPTH):
        lw = _prep_layer(l, w_in, mla_w_uq, mla_w_ukv, w_gate, w_branch, w_out, w_router)
        lp = dict(
            g_norm1=g_norm1[l].reshape(1, D), g_norm2=g_norm2[l].reshape(1, D),
            mla_q_norm=mla_q_norm[l].reshape(1, 256), mla_kv_norm=mla_kv_norm[l].reshape(1, 128),
            da_lambda=da_lambda[l], da_subln=da_subln[l].reshape(1, 64), sw_sink=sw_sink[l],
            w_e_gate=w_e_gate[l], w_e_up=w_e_up[l], w_e_down=w_e_down[l], e_place=e_place,
            na_bias=_nbr_bias_tables(na_rpb[l], NS // GRID_W))
        lam_init = 0.8 - 0.6 * math.exp(-0.3 * l)
        final = l == DEPTH - 1
        xp, cache_l = _layer(xp, mod_all[l, 0:1], lw, lp, lam_init, None, None, final, gf)
        caches.append(cache_l)
        ctx = (cache_diff_k[:, l].reshape(BS, P, 256), cache_diff_v[:, l].reshape(BS, P, 256),
               cache_na_k[:, l].reshape(BS, P, 256), cache_na_v[:, l].reshape(BS, P, 256),
               cache_swa_k[:, l].reshape(BS, P, 128), cache_swa_v[:, l].reshape(BS, P, 128),
               cache_mla_ckv[:, l], cache_mla_kpe[:, l])
        xs, _ = _layer(xs, mod_all[l, 1:1 + BS], lw, lp, lam_init, ctx, tables, final, gf)
    outs = tuple(jnp.stack([cl[i] for cl in caches], axis=1) for i in range(8))
    return (xp, xs) + outs
```

```python
import functools
import math

import numpy as np
import jax
import jax.numpy as jnp
from jax import lax
from jax.experimental import pallas as pl
from jax.experimental.pallas import tpu as pltpu

BF = jnp.bfloat16
F32 = jnp.float32

D_MODEL = 1024
DEPTH = 4
GRID_W = 64
ROPE_BASE = 10000.0
NORM_EPS = 1e-6
NEG_INF = -1e30
LOG2E = math.log2(math.e)

DA_QK = 32
NA_WIN_R = 8
NA_WIN_C = 16
SW_WINDOW = 128
MLA_NOPE = 64
MLA_ROPE = 32
N_EXPERTS = 16
EC_CAPACITY = 2

IN_PAD = 3200
LANES = 128

VMEM_BIG = 56 * 1024 * 1024
VMEM_MID = 40 * 1024 * 1024


def _cparams(sem, vmem=None):
    return pltpu.CompilerParams(dimension_semantics=sem, vmem_limit_bytes=vmem)


def _dot(a, b):
    return jnp.dot(a, b, preferred_element_type=F32)


def _dot_nt(a, b):
    return lax.dot_general(a, b, (((1,), (1,)), ((), ())), preferred_element_type=F32)


def _rms(x, g):
    var = jnp.mean(x * x, axis=-1, keepdims=True)
    return x * lax.rsqrt(var + NORM_EPS) * g


def _lane_mask(width, lo, length):
    lane = lax.broadcasted_iota(jnp.int32, (1, width), 1)
    return (lane >= lo) & (lane < lo + length)


def _mod_kernel(c_ref, w_ref, b_ref, o_ref):
    c = c_ref[...]
    s = c * (1.0 / (1.0 + jnp.exp(-c)))
    o_ref[0] = jnp.dot(s, w_ref[0], preferred_element_type=F32, precision=lax.Precision.HIGHEST) + b_ref[0]


def _modulation(cond, w_mod, b_mod):
    R, D = cond.shape
    L = w_mod.shape[0]
    nj = w_mod.shape[2] // D
    return pl.pallas_call(
        _mod_kernel,
        grid=(L, nj),
        in_specs=[
            pl.BlockSpec((R, D), lambda l, j: (0, 0)),
            pl.BlockSpec((1, D, D), lambda l, j: (l, 0, j)),
            pl.BlockSpec((1, 1, D), lambda l, j: (l, 0, j)),
        ],
        out_specs=pl.BlockSpec((1, R, D), lambda l, j: (l, 0, j)),
        out_shape=jax.ShapeDtypeStruct((L, R, nj * D), F32),
        compiler_params=_cparams(("parallel", "parallel")),
        name="modulation",
    )(cond, w_mod, b_mod.reshape(L, 1, nj * D))


def _rope(x, c, s, d):
    w = x.shape[-1]
    lane = lax.broadcasted_iota(jnp.int32, (1, LANES), 1)
    first = (lane % (2 * d)) < d
    outs = []
    for b in range(w // LANES):
        sl = slice(b * LANES, (b + 1) * LANES)
        xb = x[:, sl]
        partner = jnp.where(first, pltpu.roll(xb, LANES - d, 1), pltpu.roll(xb, d, 1))
        outs.append(xb * c[:, sl] + partner * s[:, sl])
    return outs[0] if len(outs) == 1 else jnp.concatenate(outs, axis=-1)


def _inproj_kernel(*refs, rope):
    if rope:
        (x_ref, mod_ref, g1_ref, w_ref, qn_ref, wuq_ref, kvn_ref,
         tac, tas, tcc, tcs, tqc, tqs, tkc, tks,
         oa, ob, oc, oq, ockv, okpe) = refs
    else:
        (x_ref, mod_ref, g1_ref, w_ref, qn_ref, wuq_ref, kvn_ref,
         oa, ob, oc, oq, ockv, okpe) = refs
    dt = oa.dtype
    x = x_ref[0]
    h = (_rms(x, g1_ref[...]) * (1.0 + mod_ref[0, 1:2, :]) + mod_ref[0, 0:1, :]).astype(BF)

    def seg(lo, w):
        return _dot(h, w_ref[:, lo:lo + w])

    ua = seg(0, 1024)
    qk = ua[:, :512]
    if rope:
        qk = _rope(qk, tac[...], tas[...], 8)
    oa[0, :, 0:256] = (qk[:, :256] * (DA_QK ** -0.5 * LOG2E)).astype(dt)
    oa[0, :, 256:512] = qk[:, 256:].astype(dt)
    ones_lane = lax.broadcasted_iota(jnp.int32, (1, 512), 1) % LANES >= 64
    oa[0, :, 512:1024] = jnp.where(ones_lane, 1.0, ua[:, 512:]).astype(dt)
    ub = seg(1024, 1024)
    ob[0, :, 0:256] = (ub[:, :256] * (0.125 * LOG2E)).astype(dt)
    ob[0, :, 256:512] = ub[:, 256:512].astype(dt)
    ob[0, :, 512:1024] = jnp.where(ones_lane, 1.0, ub[:, 512:]).astype(dt)
    uc = seg(2048, 640)
    qk = jnp.concatenate([uc[:, :256], uc[:, 512:]], axis=-1)
    if rope:
        qk = _rope(qk, tcc[...], tcs[...], 16)
    oc[0, :, 0:256] = (qk[:, :256] * (0.125 * LOG2E)).astype(dt)
    oc[0, :, 256:512] = jnp.where(ones_lane[:, :256], 1.0, uc[:, 256:512]).astype(dt)
    oc[0, :, 512:640] = qk[:, 256:].astype(dt)
    cq = _rms(seg(2688, 256), qn_ref[...]).astype(BF)
    qd = _dot(cq, wuq_ref[...])
    if rope:
        qd = _rope(qd, tqc[...], tqs[...], 8)
    oq[0] = (qd * ((MLA_NOPE + MLA_ROPE) ** -0.5 * LOG2E)).astype(dt)
    ockv[0] = _rms(seg(2944, 128), kvn_ref[...]).astype(dt)
    kpe = seg(3072, 128)
    if rope:
        kpe = _rope(kpe, tkc[...], tks[...], 8)
    okpe[0] = kpe.astype(dt)


def _in_proj(x, mod, g1, w_in_p, q_norm, w_uq_p, kv_norm, tables, out_dtype, tm):
    B, N, D = x.shape
    rope = tables is not None
    nmod = mod.shape[0]
    mod_map = (lambda j, b: (b, 0, 0)) if nmod > 1 else (lambda j, b: (0, 0, 0))
    const2 = lambda j, b: (0, 0)
    in_specs = [
        pl.BlockSpec((1, tm, D), lambda j, b: (b, j, 0)),
        pl.BlockSpec((1, 6, D), mod_map),
        pl.BlockSpec((1, D), const2),
        pl.BlockSpec((D, IN_PAD), const2),
        pl.BlockSpec((1, 256), const2),
        pl.BlockSpec((256, 512), const2),
        pl.BlockSpec((1, 128), const2),
    ]
    args = [x, mod, g1, w_in_p, q_norm, w_uq_p, kv_norm]
    if rope:
        for t in tables:
            in_specs.append(pl.BlockSpec((tm, t.shape[1]), lambda j, b: (j, 0)))
            args.append(t)
    widths = (1024, 1024, 640, 512, 128, 128)
    out_specs = [pl.BlockSpec((1, tm, w), lambda j, b: (b, j, 0)) for w in widths]
    out_shape = [jax.ShapeDtypeStruct((B, N, w), out_dtype) for w in widths]
    return pl.pallas_call(
        functools.partial(_inproj_kernel, rope=rope),
        grid=(N // tm, B),
        in_specs=in_specs,
        out_specs=out_specs,
        out_shape=out_shape,
        compiler_params=_cparams(("parallel", "parallel"), VMEM_BIG),
        name="in_proj_rope" if rope else "in_proj",
    )(*args)


def _mla_expand_kernel(ckv_ref, kpe_ref, wk_ref, wv_ref, e_ref, ok, ov):
    ckv = ckv_ref[0].astype(BF)
    kpe = kpe_ref[0].astype(BF)
    ok[0] = (_dot(ckv, wk_ref[...]) + _dot(kpe, e_ref[...])).astype(ok.dtype)
    ones_lane = lax.broadcasted_iota(jnp.int32, (1, 512), 1) % LANES >= 64
    ov[0] = jnp.where(ones_lane, 1.0, _dot(ckv, wv_ref[...])).astype(ov.dtype)


def _mla_expand(ckv, kpe, w_k, w_v, e_place, tk):
    B, K, _ = ckv.shape
    const2 = lambda b, j: (0, 0)
    return pl.pallas_call(
        _mla_expand_kernel,
        grid=(B, K // tk),
        in_specs=[
            pl.BlockSpec((1, tk, 128), lambda b, j: (b, j, 0)),
            pl.BlockSpec((1, tk, 128), lambda b, j: (b, j, 0)),
            pl.BlockSpec((128, 512), const2),
            pl.BlockSpec((128, 512), const2),
            pl.BlockSpec((128, 512), const2),
        ],
        out_specs=[pl.BlockSpec((1, tk, 512), lambda b, j: (b, j, 0)),
                   pl.BlockSpec((1, tk, 512), lambda b, j: (b, j, 0))],
        out_shape=[jax.ShapeDtypeStruct((B, K, 512), BF), jax.ShapeDtypeStruct((B, K, 512), BF)],
        compiler_params=_cparams(("parallel", "parallel")),
        name="mla_expand",
    )(ckv, kpe, w_k, w_v, e_place)


def _head_softmax(scores, values):
    m = functools.reduce(jnp.maximum, [jnp.max(s, axis=-1, keepdims=True) for s in scores])
    acc = None
    for s, v in zip(scores, values):
        part = _dot(jnp.exp2(s - m).astype(BF), v)
        acc = part if acc is None else acc + part
    return acc, m


def _normalise(acc):
    return acc * pltpu.roll(1.0 / acc, 64, 1)


def _store_heads(o_ref, heads):
    half = _lane_mask(LANES, 0, 64)
    for hp in range(len(heads) // 2):
        slab = jnp.where(half, heads[2 * hp], pltpu.roll(heads[2 * hp + 1], 64, 1))
        o_ref[0, :, hp * LANES:(hp + 1) * LANES] = slab.astype(o_ref.dtype)


def _dense_attn_kernel(*refs, groups, tk, n_chunks, has_sink, diff):
    it = iter(refs)
    q_ref, k_ref, v_ref = next(it), next(it), next(it)
    sink_ref = lam_ref = subln_ref = None
    if has_sink:
        sink_ref = next(it)
    if diff:
        lam_ref, subln_ref, lam_init_ref = next(it), next(it), next(it)
    o_ref = next(it)
    tq = q_ref.shape[1]

    lam = lam_init = None
    if diff:
        lam_init = lam_init_ref[0]
        dl = lam_ref[...]
        lam = (jnp.exp(jnp.sum(dl[0:1] * dl[1:2], axis=-1, keepdims=True))
               - jnp.exp(jnp.sum(dl[2:3] * dl[3:4], axis=-1, keepdims=True)) + lam_init)

    ng = len(groups)
    scratch = list(it)
    qg_s, m_s, acc_s = (scratch[j * ng:(j + 1) * ng] for j in range(3))
    for gi, (q_lo, k_lo, w, qmask, v_lo, head, sink_idx, sign) in enumerate(groups):
        qg = q_ref[0, :, q_lo:q_lo + w].astype(BF)
        if qmask is not None:
            qg = jnp.where(_lane_mask(w, *qmask), qg, jnp.zeros_like(qg))
        qg_s[gi][...] = qg
        m_s[gi][...] = jnp.full((tq, 1), NEG_INF, F32)
        acc_s[gi][...] = jnp.zeros((tq, LANES), F32)

    def keys_step(c):
        off = c * tk
        for gi, (q_lo, k_lo, w, qmask, v_lo, head, sink_idx, sign) in enumerate(groups):
            s = _dot_nt(qg_s[gi][...], k_ref[0, pl.ds(off, tk), k_lo:k_lo + w].astype(BF))
            m = m_s[gi][...]
            m_new = jnp.maximum(m, jnp.max(s, axis=-1, keepdims=True))
            p = jnp.exp2(s - m_new).astype(BF)
            pv = _dot(p, v_ref[0, pl.ds(off, tk), v_lo:v_lo + LANES].astype(BF))
            acc_s[gi][...] = jnp.exp2(m - m_new) * acc_s[gi][...] + pv
            m_s[gi][...] = m_new

    for c in range(n_chunks):
        keys_step(c)

    heads = {}
    ones_half = _lane_mask(LANES, 64, 64)
    for gi, (q_lo, k_lo, w, qmask, v_lo, head, sink_idx, sign) in enumerate(groups):
        acc = acc_s[gi][...]
        if sink_idx is not None:
            acc = acc + jnp.where(ones_half, jnp.exp2(sink_ref[sink_idx] * LOG2E - m_s[gi][...]), 0.0)
        part = _normalise(acc)
        if sign < 0:
            part = -lam * part
        heads[head] = part if head not in heads else heads[head] + part
    if diff:
        half = _lane_mask(LANES, 0, 64)
        for hd, o in heads.items():
            ms = jnp.sum(jnp.where(half, o * o, 0.0), axis=-1, keepdims=True) * (1.0 / 64.0)
            heads[hd] = o * lax.rsqrt(ms + NORM_EPS) * subln_ref[...] * (1.0 - lam_init)
    _store_heads(o_ref, heads)


def _dense_attn(q_arr, q_blk, k_arr, k_blk, v_arr, v_blk, *, groups, tq, tk, sink=None, diff=None, name):
    B, N, _ = q_arr.shape
    K = k_arr.shape[1]
    n_chunks = K // tk
    assert n_chunks * tk == K and n_chunks <= 2
    qw, qi = q_blk
    kw, ki = k_blk
    vw, vi = v_blk
    in_specs = [
        pl.BlockSpec((1, tq, qw), lambda b, i: (b, i, qi)),
        pl.BlockSpec((1, K, kw), lambda b, i: (b, 0, ki)),
        pl.BlockSpec((1, K, vw), lambda b, i: (b, 0, vi)),
    ]
    args = [q_arr, k_arr, v_arr]
    if sink is not None:
        in_specs.append(pl.BlockSpec(memory_space=pltpu.SMEM))
        args.append(sink)
    if diff is not None:
        da_lambda, subln, lam_init = diff
        in_specs += [pl.BlockSpec((4, DA_QK), lambda b, i: (0, 0)),
                     pl.BlockSpec((1, LANES), lambda b, i: (0, 0)),
                     pl.BlockSpec(memory_space=pltpu.SMEM)]
        args += [da_lambda, subln, jnp.full((1,), lam_init, F32)]
    kern = functools.partial(_dense_attn_kernel, groups=groups, tk=tk, n_chunks=n_chunks,
                             has_sink=sink is not None, diff=diff is not None)
    scratch = ([pltpu.VMEM((tq, g[2]), BF) for g in groups] + [pltpu.VMEM((tq, 1), F32) for _ in groups]
               + [pltpu.VMEM((tq, LANES), F32) for _ in groups])
    return pl.pallas_call(
        kern,
        grid=(B, N // tq),
        in_specs=in_specs,
        out_specs=pl.BlockSpec((1, tq, 256), lambda b, i: (b, i, 0)),
        out_shape=jax.ShapeDtypeStruct((B, N, 256), BF),
        scratch_shapes=scratch,
        compiler_params=_cparams(("parallel", "parallel"), VMEM_BIG),
        name=name,
    )(*args)


GROUPS_A = tuple((128 * (g // 4), 128 * (g // 4), 128, (32 * (g % 4), 32), 128 * (g // 2), g // 2, None,
                  1 if g % 2 == 0 else -1) for g in range(8))
GROUPS_B = tuple((128 * (h // 2), 128 * (h // 2), 128, (64 * (h % 2), 64), 128 * h, h, None, 1) for h in range(4))
GROUPS_C = tuple((128 * g, 0, 128, (64 * j, 64), 128 * j, 2 * g + j, 2 * j + g, 1) for g in range(2) for j in range(2))
GROUPS_D = tuple((128 * h, 128 * h, 128, None, 128 * h, h, None, 1) for h in range(4))


def _window_attn_kernel(q_ref, k_ref, v_ref, ck_ref, cv_ref, sink_ref, o_ref, *, n, band):
    tq = q_ref.shape[1]
    i = pl.program_id(1)
    start = jnp.clip(i * tq - SW_WINDOW, 0, n - band)
    start = pl.multiple_of(start, 128)
    kb = k_ref[0, pl.ds(start, band), :]
    vb = v_ref[0, pl.ds(start, band), :]
    ck = ck_ref[0]
    cv = cv_ref[0]
    qpos = i * tq + lax.broadcasted_iota(jnp.int32, (tq, band), 0)
    kpos = start + lax.broadcasted_iota(jnp.int32, (tq, band), 1)
    valid = jnp.abs(qpos - kpos) <= SW_WINDOW
    ones_half = _lane_mask(LANES, 64, 64)
    heads = {}
    for g in range(2):
        for j in range(2):
            qg = q_ref[0, :, LANES * g:LANES * (g + 1)]
            qg = jnp.where(_lane_mask(LANES, 64 * j, 64), qg, jnp.zeros_like(qg))
            vs = slice(LANES * j, LANES * (j + 1))
            acc, m = _head_softmax([_dot_nt(qg, ck), jnp.where(valid, _dot_nt(qg, kb), NEG_INF)],
                                   [cv[:, vs], vb[:, vs]])
            acc = acc + jnp.where(ones_half, jnp.exp2(sink_ref[2 * j + g] * LOG2E - m), 0.0)
            heads[2 * g + j] = _normalise(acc)
    _store_heads(o_ref, heads)


def _window_attn(qkv_c, ck, cv, sink, tq):
    B, N, _ = qkv_c.shape
    band = tq + 2 * SW_WINDOW
    P = ck.shape[1]
    return pl.pallas_call(
        functools.partial(_window_attn_kernel, n=N, band=band),
        grid=(B, N // tq),
        in_specs=[
            pl.BlockSpec((1, tq, 256), lambda b, i: (b, i, 0)),
            pl.BlockSpec((1, N, 128), lambda b, i: (b, 0, 4)),
            pl.BlockSpec((1, N, 256), lambda b, i: (b, 0, 1)),
            pl.BlockSpec((1, P, 128), lambda b, i: (b, 0, 0)),
            pl.BlockSpec((1, P, 256), lambda b, i: (b, 0, 0)),
            pl.BlockSpec(memory_space=pltpu.SMEM),
        ],
        out_specs=pl.BlockSpec((1, tq, 256), lambda b, i: (b, i, 0)),
        out_shape=jax.ShapeDtypeStruct((B, N, 256), BF),
        compiler_params=_cparams(("parallel", "parallel")),
        name="window_attn",
    )(qkv_c, qkv_c, qkv_c, ck, cv, sink)


NA_TILE_ROWS = 4
NA_KEY_ROWS = NA_TILE_ROWS + NA_WIN_R


def _nbr_attn_kernel(q_ref, k_ref, v_ref, ck_ref, cv_ref, bias_ref, o_ref, *, rows):
    i = pl.program_id(1)
    krow = jnp.clip(i * NA_TILE_ROWS - NA_WIN_R // 2, 0, rows - NA_KEY_ROWS)
    start = pl.multiple_of(krow * GRID_W, GRID_W)
    nk = NA_KEY_ROWS * GRID_W
    kb = k_ref[0, pl.ds(start, nk), :]
    vb = v_ref[0, pl.ds(start, nk), :]
    ck = ck_ref[0]
    cv = cv_ref[0]
    q = q_ref[0]
    heads = {}
    for h in range(4):
        ks = slice(LANES * (h // 2), LANES * (h // 2 + 1))
        qg = jnp.where(_lane_mask(LANES, 64 * (h % 2), 64), q[:, ks], jnp.zeros_like(q[:, ks]))
        vs = slice(LANES * h, LANES * (h + 1))
        acc, _ = _head_softmax([_dot_nt(qg, ck[:, ks]), _dot_nt(qg, kb[:, ks]) + bias_ref[0, h]],
                               [cv[:, vs], vb[:, vs]])
        heads[h] = _normalise(acc)
    _store_heads(o_ref, heads)


def _nbr_attn(qkv_b, ck, cv, bias):
    B, N, _ = qkv_b.shape
    rows = N // GRID_W
    tq = NA_TILE_ROWS * GRID_W
    nt = N // tq
    nk = NA_KEY_ROWS * GRID_W
    P = ck.shape[1]

    def bias_map(b, i):
        return (jnp.where(i == 0, 0, jnp.where(i == nt - 1, 2, 1)), 0, 0, 0)

    return pl.pallas_call(
        functools.partial(_nbr_attn_kernel, rows=rows),
        grid=(B, nt),
        in_specs=[
            pl.BlockSpec((1, tq, 256), lambda b, i: (b, i, 0)),
            pl.BlockSpec((1, N, 256), lambda b, i: (b, 0, 1)),
            pl.BlockSpec((1, N, 512), lambda b, i: (b, 0, 1)),
            pl.BlockSpec((1, P, 256), lambda b, i: (b, 0, 0)),
            pl.BlockSpec((1, P, 512), lambda b, i: (b, 0, 0)),
            pl.BlockSpec((1, 4, tq, nk), bias_map),
        ],
        out_specs=pl.BlockSpec((1, tq, 256), lambda b, i: (b, i, 0)),
        out_shape=jax.ShapeDtypeStruct((B, N, 256), BF),
        compiler_params=_cparams(("parallel", "parallel"), VMEM_MID),
        name="nbr_attn",
    )(qkv_b, qkv_b, qkv_b, ck, cv, bias)


def _nbr_bias_tables(rpb, rows):
    L, H = rpb.shape[:2]
    rpb = rpb.reshape((L * H,) + rpb.shape[2:])
    rp = jnp.pad(rpb.astype(F32), ((0, 0), (NA_KEY_ROWS, NA_KEY_ROWS), (GRID_W, GRID_W)))
    c0 = NA_WIN_C - 1 + GRID_W
    tcol = jnp.stack([rp[:, :, c0 - qc:c0 - qc + GRID_W] for qc in range(GRID_W)], axis=2)
    tabs = []
    nt = rows // NA_TILE_ROWS
    for r0 in (0, NA_TILE_ROWS * min(1, nt - 1) + NA_TILE_ROWS, rows - NA_TILE_ROWS):
        ks = int(np.clip(r0 - NA_WIN_R // 2, 0, rows - NA_KEY_ROWS))
        per_row = []
        for a in range(NA_TILE_ROWS):
            s0 = ks - (r0 + a) + NA_WIN_R - 1 + NA_KEY_ROWS
            per_row.append(tcol[:, s0:s0 + NA_KEY_ROWS])
        t = jnp.stack(per_row, axis=1).transpose(0, 1, 3, 2, 4)
        r = r0 + np.arange(NA_TILE_ROWS)[:, None, None, None]
        qc = np.arange(GRID_W)[None, :, None, None]
        kr = ks + np.arange(NA_KEY_ROWS)[None, None, :, None]
        kc = np.arange(GRID_W)[None, None, None, :]
        rs = np.clip(r - NA_WIN_R // 2, 0, rows - NA_WIN_R)
        cs = np.clip(qc - NA_WIN_C // 2, 0, GRID_W - NA_WIN_C)
        ok = (kr >= rs) & (kr < rs + NA_WIN_R) & (kc >= cs) & (kc < cs + NA_WIN_C)
        t = jnp.where(ok[None], t * LOG2E, NEG_INF)
        tabs.append(t.reshape(L, H, NA_TILE_ROWS * GRID_W, NA_KEY_ROWS * GRID_W))
    return jnp.stack(tabs, axis=1)


def _merge_kernel(x_ref, mod_ref, g1_ref, g2_ref, oa, ob, oc, od, wg_ref, wb_ref, wo_ref, wr_ref,
                  xo_ref, h2_ref, lg_ref):
    D = x_ref.shape[2]
    x = x_ref[0]
    h = (_rms(x, g1_ref[...]) * (1.0 + mod_ref[0, 1:2, :]) + mod_ref[0, 0:1, :]).astype(BF)
    merged = None
    for i, o_ref in enumerate((oa, ob, oc, od)):
        z = _dot(h, wg_ref[:, i * D:(i + 1) * D])
        gate = 1.0 / (1.0 + jnp.exp(-z))
        term = gate * _dot(o_ref[0], wb_ref[i])
        merged = term if merged is None else merged + term
    mixed = _dot(merged.astype(BF), wo_ref[...])
    xn = x + mod_ref[0, 2:3, :] * mixed
    xo_ref[0] = xn
    h2 = (_rms(xn, g2_ref[...]) * (1.0 + mod_ref[0, 4:5, :]) + mod_ref[0, 3:4, :]).astype(BF)
    h2_ref[0] = h2
    lg_ref[0] = _dot_nt(wr_ref[...], h2)


def _merge(x, mod, g1, g2, o_a, o_b, o_c, o_d, w_gate, w_branch, w_out, w_router_p, tm):
    B, N, D = x.shape
    nmod = mod.shape[0]
    mod_map = (lambda b, j: (b, 0, 0)) if nmod > 1 else (lambda b, j: (0, 0, 0))
    tok = lambda w: pl.BlockSpec((1, tm, w), lambda b, j: (b, j, 0))
    c2 = lambda b, j: (0, 0)
    return pl.pallas_call(
        _merge_kernel,
        grid=(B, N // tm),
        in_specs=[
            tok(D), pl.BlockSpec((1, 6, D), mod_map), pl.BlockSpec((1, D), c2), pl.BlockSpec((1, D), c2),
            tok(256), tok(256), tok(256), tok(256),
            pl.BlockSpec((D, 4 * D), c2),
            pl.BlockSpec((4, 256, D), lambda b, j: (0, 0, 0)),
            pl.BlockSpec((D, D), c2),
            pl.BlockSpec((N_EXPERTS, D), c2),
        ],
        out_specs=[tok(D), tok(D), pl.BlockSpec((1, N_EXPERTS, tm), lambda b, j: (b, 0, j))],
        out_shape=[jax.ShapeDtypeStruct((B, N, D), F32), jax.ShapeDtypeStruct((B, N, D), BF),
                   jax.ShapeDtypeStruct((B, N_EXPERTS, N), F32)],
        compiler_params=_cparams(("parallel", "parallel"), VMEM_BIG),
        name="merge",
    )(x, mod, g1, g2, o_a, o_b, o_c, o_d, w_gate, w_branch, w_out, w_router_p)


MOE_TT = 256
MOE_RC = 64
SLOT_ALIGN = 16


def _excl_cumsum(x, tri, reset_every):
    R, N = x.shape
    outs = []
    carry = jnp.zeros((R, 1), F32)
    for c in range(N // LANES):
        if reset_every and (c * LANES) % reset_every == 0:
            carry = jnp.zeros((R, 1), F32)
        xc = x[:, c * LANES:(c + 1) * LANES]
        inc = _dot(xc.astype(BF), tri)
        outs.append(inc - xc + carry)
        carry = carry + inc[:, LANES - 1:LANES]
    return (outs[0] if len(outs) == 1 else jnp.concatenate(outs, axis=1)), carry


def _route_kernel(lg_ref, aff_ref, rel_ref, base_ref, *, cap, tt):
    B, E, N = lg_ref.shape
    R = B * E
    nt = N // tt
    lg = lg_ref[...]
    ex = jnp.exp(lg - jnp.max(lg, axis=1, keepdims=True))
    aff = ex / jnp.sum(ex, axis=1, keepdims=True)
    aff_ref[...] = aff
    bits = pltpu.bitcast(aff.reshape(R, N), jnp.int32)

    def per_expert(v):
        t = v[0:E]
        for b in range(1, B):
            t = t + v[b * E:(b + 1) * E]
        return t

    def per_row(v):
        return jnp.concatenate([v] * B, axis=0) if B > 1 else v

    def batch_prefix(v):
        parts, run = [], jnp.zeros((E, 1), F32)
        for b in range(B):
            parts.append(run)
            run = run + v[b * E:(b + 1) * E]
        return jnp.concatenate(parts, axis=0) if B > 1 else parts[0]

    def count(mask):
        return per_expert(jnp.sum(jnp.where(mask, 1.0, 0.0), axis=1, keepdims=True))

    def search(i, thr):
        cand = thr | jnp.left_shift(jnp.int32(1), 30 - i)
        return jnp.where(count(bits >= per_row(cand)) >= float(cap), cand, thr)

    thr = per_row(lax.fori_loop(0, 31, search, jnp.zeros((E, 1), jnp.int32)))
    gt = bits > thr
    eq = bits == thr
    need = float(cap) - count(gt)
    ri = lax.broadcasted_iota(jnp.int32, (LANES, LANES), 0)
    ci = lax.broadcasted_iota(jnp.int32, (LANES, LANES), 1)
    tri = jnp.where(ri <= ci, 1.0, 0.0).astype(BF)
    rank_eq, eq_tot = _excl_cumsum(jnp.where(eq, 1.0, 0.0), tri, None)
    rank_eq = rank_eq + batch_prefix(eq_tot)
    sel = gt | (eq & (rank_eq < per_row(need)))
    self = jnp.where(sel, 1.0, 0.0)
    local, _ = _excl_cumsum(self, tri, tt)
    rel_ref[...] = jnp.where(sel, local, -1.0).astype(jnp.int32).reshape(B, E, N)
    tn = jnp.right_shift(lax.broadcasted_iota(jnp.int32, (N, LANES), 0), tt.bit_length() - 1)
    tj = lax.broadcasted_iota(jnp.int32, (N, LANES), 1)
    cnt = _dot(self.astype(BF), jnp.where(tn == tj, 1.0, 0.0).astype(BF)).astype(jnp.int32)
    sh = SLOT_ALIGN.bit_length() - 1
    ru = jnp.left_shift(jnp.right_shift(cnt + (SLOT_ALIGN - 1), sh), sh).astype(F32)
    base_b = _dot(ru.astype(BF), jnp.where(ri < ci, 1.0, 0.0).astype(BF))
    base = base_b + batch_prefix(base_b[:, nt:nt + 1])
    base_ref[...] = base.astype(jnp.int32).reshape(B, E, LANES)


def _route(lg_t, cap):
    B, E, N = lg_t.shape
    full = lambda s: pl.BlockSpec(s, lambda i: (0,) * len(s))
    return pl.pallas_call(
        functools.partial(_route_kernel, cap=cap, tt=MOE_TT),
        grid=(1,),
        in_specs=[full((B, E, N))],
        out_specs=[full((B, E, N)), full((B, E, N)), full((B, E, LANES))],
        out_shape=[jax.ShapeDtypeStruct((B, E, N), F32), jax.ShapeDtypeStruct((B, E, N), jnp.int32),
                   jax.ShapeDtypeStruct((B, E, LANES), jnp.int32)],
        compiler_params=_cparams(("arbitrary",), VMEM_BIG),
        name="route",
    )(lg_t)


def _slot_onehot(rel, values, c, rc):
    E, tt = rel.shape
    r = lax.broadcasted_iota(jnp.int32, (E, rc, tt), 1) + c * rc
    return jnp.where(rel[:, None, :] == r, values[:, None, :], 0.0).astype(BF).reshape(E * rc, tt)


def _dispatch_kernel(base_sm, x_ref, rel_ref, xe_in, xe_hbm, stage, stage2, sem, sem2, *, nt, rc):
    del xe_in
    b, j = pl.program_id(0), pl.program_id(1)
    E = rel_ref.shape[1]
    step = b * nt + j
    last = pl.num_programs(0) * nt - 1
    base = lambda e, jj: base_sm[(b * E + e) * (nt + 1) + jj]
    rel = rel_ref[0]
    x = x_ref[0]
    ones = jnp.ones(rel.shape, F32)

    def copy(buf, s, e, c):
        start = pl.multiple_of(base(e, j) + c * rc, SLOT_ALIGN)
        return pltpu.make_async_copy(buf.at[pl.ds(e * rc, rc)], xe_hbm.at[e, pl.ds(start, rc)], s.at[e])

    strip = _dot(_slot_onehot(rel, ones, 0, rc), x).astype(BF)

    @pl.when(step > 0)
    def _():
        for e in range(E):
            copy(stage, sem, e, 0).wait()

    stage[...] = strip
    for e in range(E):
        copy(stage, sem, e, 0).start()

    span = [base(e, j + 1) - base(e, j) for e in range(E)]
    widest = functools.reduce(jnp.maximum, span)

    def extra(c, carry):
        stage2[...] = _dot(_slot_onehot(rel, ones, c, rc), x).astype(BF)
        for e in range(E):
            @pl.when(span[e] > c * rc)
            def _():
                cp = copy(stage2, sem2, e, c)
                cp.start()
                cp.wait()
        return carry

    lax.fori_loop(1, (widest + rc - 1) // rc, extra, 0)

    @pl.when(step == last)
    def _():
        for e in range(E):
            copy(stage, sem, e, 0).wait()


def _dispatch(h2, rel, base_flat, rows):
    B, N, D = h2.shape
    E = rel.shape[1]
    nt = N // MOE_TT
    xe0 = jnp.zeros((E, rows, D), BF)
    return pl.pallas_call(
        functools.partial(_dispatch_kernel, nt=nt, rc=MOE_RC),
        grid_spec=pltpu.PrefetchScalarGridSpec(
            num_scalar_prefetch=1,
            grid=(B, nt),
            in_specs=[
                pl.BlockSpec((1, MOE_TT, D), lambda b, j, s: (b, j, 0)),
                pl.BlockSpec((1, E, MOE_TT), lambda b, j, s: (b, 0, j)),
                pl.BlockSpec(memory_space=pl.ANY),
            ],
            out_specs=pl.BlockSpec(memory_space=pl.ANY),
            scratch_shapes=[pltpu.VMEM((E * MOE_RC, D), BF), pltpu.VMEM((E * MOE_RC, D), BF),
                            pltpu.SemaphoreType.DMA((E,)), pltpu.SemaphoreType.DMA((E,))],
        ),
        out_shape=jax.ShapeDtypeStruct((E, rows, D), BF),
        input_output_aliases={3: 0},
        compiler_params=_cparams(("arbitrary", "arbitrary")),
        name="dispatch",
    )(base_flat, h2, rel, xe0)


def _ffn_kernel(tot_sm, x_ref, wg_hbm, wu_hbm, wd_hbm, y_ref, wbuf, wg_s, wu_s, wd_s, sem, *, tm, rc, layer):
    e, j = pl.program_id(0), pl.program_id(1)
    slot = e % 2

    def copy(ee, sl, k):
        return pltpu.make_async_copy((wg_hbm, wu_hbm, wd_hbm)[k].at[layer, ee], wbuf.at[sl, k], sem.at[sl, k])

    @pl.when(j == 0)
    def _():
        @pl.when(e == 0)
        def _():
            for k in range(3):
                copy(e, slot, k).start()

        @pl.when(e + 1 < pl.num_programs(0))
        def _():
            for k in range(3):
                copy(e + 1, 1 - slot, k).start()

        for k in range(3):
            copy(e, slot, k).wait()
        wg_s[...] = wbuf[slot, 0].astype(BF)
        wu_s[...] = wbuf[slot, 1].astype(BF)
        wd_s[...] = wbuf[slot, 2].astype(BF)

    used = j * tm < tot_sm[e] + rc

    @pl.when(used)
    def _():
        x = x_ref[0]
        hg = _dot(x, wg_s[...])
        hu = _dot(x, wu_s[...])
        act = (hg * (1.0 / (1.0 + jnp.exp(-hg))) * hu).astype(BF)
        y_ref[0] = _dot(act, wd_s[...]).astype(y_ref.dtype)

    @pl.when(jnp.logical_not(used))
    def _():
        y_ref[...] = jnp.zeros_like(y_ref)


def _expert_ffn(xe, total, w_g, w_u, w_d, layer, tm):
    E, rows, D = xe.shape
    DE = w_g.shape[3]
    tok = lambda e, j, tot: (e, jnp.minimum(j, (tot[e] + MOE_RC - 1) // tm), 0)
    assert D == DE
    hbm = pl.BlockSpec(memory_space=pl.ANY)
    return pl.pallas_call(
        functools.partial(_ffn_kernel, tm=tm, rc=MOE_RC, layer=layer),
        grid_spec=pltpu.PrefetchScalarGridSpec(
            num_scalar_prefetch=1,
            grid=(E, rows // tm),
            in_specs=[pl.BlockSpec((1, tm, D), tok), hbm, hbm, hbm],
            out_specs=pl.BlockSpec((1, tm, D), lambda e, j, tot: (e, j, 0)),
            scratch_shapes=[pltpu.VMEM((2, 3, D, DE), F32), pltpu.VMEM((D, DE), BF), pltpu.VMEM((D, DE), BF),
                            pltpu.VMEM((DE, D), BF), pltpu.SemaphoreType.DMA((2, 3))],
        ),
        out_shape=jax.ShapeDtypeStruct((E, rows, D), BF),
        compiler_params=_cparams(("arbitrary", "arbitrary"), VMEM_BIG),
        name="expert_ffn",
    )(total, xe, w_g, w_u, w_d)


def _combine_kernel(base_sm, x_ref, rel_ref, aff_ref, mod_ref, gf_ref, ye_hbm, o_ref, stage, sem, *,
                    nt, rc, final):
    b, j = pl.program_id(0), pl.program_id(1)
    E = rel_ref.shape[1]
    step = b * nt + j
    nsteps = pl.num_programs(0) * nt
    slot = step % 2
    base = lambda e, jj: base_sm[(b * E + e) * (nt + 1) + jj]
    rel = rel_ref[0]
    aff = aff_ref[0]

    def copy(st, sl, e, c):
        first = base_sm[(st // nt * E + e) * (nt + 1) + st % nt]
        start = pl.multiple_of(first + c * rc, SLOT_ALIGN)
        return pltpu.make_async_copy(ye_hbm.at[e, pl.ds(start, rc)], stage.at[sl, pl.ds(e * rc, rc)],
                                     sem.at[sl, e])

    def scatter(c):
        w = _slot_onehot(rel, aff, c, rc)
        return lax.dot_general(w, stage[slot], (((0,), (0,)), ((), ())), preferred_element_type=F32)

    @pl.when(step == 0)
    def _():
        for e in range(E):
            copy(step, slot, e, 0).start()

    @pl.when(step + 1 < nsteps)
    def _():
        for e in range(E):
            copy(step + 1, 1 - slot, e, 0).start()

    for e in range(E):
        copy(step, slot, e, 0).wait()
    acc = scatter(0)
    span = [base(e, j + 1) - base(e, j) for e in range(E)]
    widest = functools.reduce(jnp.maximum, span)

    def extra(c, acc):
        for e in range(E):
            @pl.when(span[e] > c * rc)
            def _():
                cp = copy(step, slot, e, c)
                cp.start()
                cp.wait()
        return acc + scatter(c)

    acc = lax.fori_loop(1, (widest + rc - 1) // rc, extra, acc)
    xn = x_ref[0] + mod_ref[0, 5:6, :] * acc
    if final:
        xn = _rms(xn, gf_ref[...])
    o_ref[0] = xn


def _combine(x, rel, aff, mod, g_final, ye, base_flat, final):
    B, N, D = x.shape
    E = rel.shape[1]
    nt = N // MOE_TT
    nmod = mod.shape[0]
    mod_map = (lambda b, j, s: (b, 0, 0)) if nmod > 1 else (lambda b, j, s: (0, 0, 0))
    tok = pl.BlockSpec((1, MOE_TT, D), lambda b, j, s: (b, j, 0))
    slots = pl.BlockSpec((1, E, MOE_TT), lambda b, j, s: (b, 0, j))
    return pl.pallas_call(
        functools.partial(_combine_kernel, nt=nt, rc=MOE_RC, final=final),
        grid_spec=pltpu.PrefetchScalarGridSpec(
            num_scalar_prefetch=1,
            grid=(B, nt),
            in_specs=[tok, slots, slots, pl.BlockSpec((1, 6, D), mod_map),
                      pl.BlockSpec((1, D), lambda b, j, s: (0, 0)), pl.BlockSpec(memory_space=pl.ANY)],
            out_specs=tok,
            scratch_shapes=[pltpu.VMEM((2, E * MOE_RC, D), BF), pltpu.SemaphoreType.DMA((2, E))],
        ),
        out_shape=jax.ShapeDtypeStruct((B, N, D), F32),
        compiler_params=_cparams(("arbitrary", "arbitrary")),
        name="combine_final" if final else "combine",
    )(base_flat, x, rel, aff, mod, g_final, ye)


def _moe(x, h2, lg_t, mod, g_final, w_g, w_u, w_d, layer, final):
    B, N, D = h2.shape
    T = B * N
    nt = N // MOE_TT
    cap = EC_CAPACITY * T // N_EXPERTS
    tm = 512 if cap >= 2048 else 256
    worst = cap + (SLOT_ALIGN - 1) * (T // MOE_TT) + MOE_RC
    rows = -(-worst // tm) * tm
    aff, rel, base = _route(lg_t, cap)
    base_flat = base[:, :, :nt + 1].reshape(-1)
    total = base[B - 1, :, nt]
    xe = _dispatch(h2, rel, base_flat, rows)
    ye = _expert_ffn(xe, total, w_g, w_u, w_d, layer, tm)
    return _combine(x, rel, aff, mod, g_final, ye, base_flat, final)


def _rope_tables(n):
    t = jnp.arange(n)
    pos = {1: (t // GRID_W).astype(F32), 2: (t % GRID_W).astype(F32)}

    def group(quarter):
        freqs = jnp.float32(ROPE_BASE) ** (-jnp.arange(quarter, dtype=F32) / quarter)
        cs, ss = [], []
        for kind in (1, 2):
            ang = pos[kind][:, None] * freqs
            c, s = jnp.cos(ang), jnp.sin(ang)
            cs += [c, c]
            ss += [-s, s]
        return jnp.concatenate(cs, axis=1), jnp.concatenate(ss, axis=1)

    c32, s32 = group(8)
    c64, s64 = group(16)
    one = lambda w: jnp.ones((n, w), F32)
    zero = lambda w: jnp.zeros((n, w), F32)
    ta = (jnp.tile(c32, (1, 16)), jnp.tile(s32, (1, 16)))
    tc = (jnp.tile(c64, (1, 6)), jnp.tile(s64, (1, 6)))
    qc = jnp.concatenate([one(64), c32, one(32)], axis=1)
    qs = jnp.concatenate([zero(64), s32, zero(32)], axis=1)
    tq = (jnp.tile(qc, (1, 4)), jnp.tile(qs, (1, 4)))
    tk = (jnp.concatenate([c32, one(96)], axis=1), jnp.concatenate([s32, zero(96)], axis=1))
    return (ta[0], ta[1], tc[0], tc[1], tq[0], tq[1], tk[0], tk[1])


LAT_TQ = 512
C_HEAD_ORDER = (0, 2, 1, 3)


def _prep_layer(l, w_in, mla_w_uq, mla_w_ukv, w_gate, w_branch, w_out, w_router):
    D = D_MODEL
    wi = w_in[l]
    cq = wi[:, 1536:1792].reshape(D, 4, 64)[:, C_HEAD_ORDER, :].reshape(D, 256)
    def spread(cols, heads):
        return jnp.pad(cols.reshape(D, heads, 64), ((0, 0), (0, 0), (0, 64))).reshape(D, heads * LANES)

    w_in_p = jnp.concatenate(
        [wi[:, :512], spread(wi[:, 512:768], 4), wi[:, 768:1280], spread(wi[:, 1280:1536], 4),
         cq, spread(wi[:, 1920:2048], 2), wi[:, 1792:1920], wi[:, 2048:],
         jnp.zeros((D, LANES - MLA_ROPE), F32)], axis=1).astype(BF)
    uq = mla_w_uq[l].reshape(256, 4, MLA_NOPE + MLA_ROPE)
    w_uq_p = jnp.concatenate([uq, jnp.zeros((256, 4, 32), F32)], axis=2).reshape(256, 512).astype(BF)
    ukv = mla_w_ukv[l].reshape(128, 4, 128)
    w_k = jnp.concatenate([ukv[:, :, :64], jnp.zeros((128, 4, 64), F32)], axis=2).reshape(128, 512).astype(BF)
    w_v = jnp.concatenate([ukv[:, :, 64:], jnp.zeros((128, 4, 64), F32)], axis=2).reshape(128, 512).astype(BF)
    wb = w_branch[l]
    wb_c = wb[2].reshape(4, 64, D)[C_HEAD_ORDER, :, :].reshape(256, D)
    w_branch_p = jnp.stack([wb[0], wb[1], wb_c, wb[3]]).astype(BF)
    w_router_p = w_router[l].T.astype(BF)
    return dict(w_in=w_in_p, w_uq=w_uq_p, w_k=w_k, w_v=w_v, w_gate=w_gate[l].astype(BF),
                w_branch=w_branch_p, w_out=w_out[l].astype(BF), w_router=w_router_p)


def _kpe_placement():
    e = np.zeros((128, 512), np.float32)
    for h in range(4):
        for i in range(MLA_ROPE):
            e[i, 128 * h + MLA_NOPE + i] = 1.0
    return jnp.asarray(e, BF)


def _layer(x, mod, lw, lp, lam_init, ctx, tables, final, g_final):
    B, N, D = x.shape
    latent = ctx is not None
    tm = 512 if latent else 256
    out_dtype = BF if latent else F32
    qkv_a, qkv_b, qkv_c, q_d, ckv_n, kpe_r = _in_proj(
        x, mod, lp["g_norm1"], lw["w_in"], lp["mla_q_norm"], lw["w_uq"], lp["mla_kv_norm"],
        tables if latent else None, out_dtype, tm)
    diff = (lp["da_lambda"], jnp.tile(lp["da_subln"], (1, 2)), lam_init)
    if latent:
        ck_a, cv_a, ck_b, cv_b, ck_c, cv_c, ckv_c, kpe_c = ctx
        P = ck_a.shape[1]

        def with_ones(v, heads):
            v = v.reshape(B, P, heads, 64).astype(BF)
            return jnp.concatenate([v, jnp.ones_like(v)], axis=-1).reshape(B, P, heads * LANES)

        k_a = jnp.concatenate([ck_a.astype(BF), qkv_a[:, :, 256:512]], axis=1)
        v_a = jnp.concatenate([with_ones(cv_a, 4), qkv_a[:, :, 512:1024]], axis=1)
        o_a = _dense_attn(qkv_a, (256, 0), k_a, (256, 0), v_a, (512, 0), groups=GROUPS_A,
                          tq=LAT_TQ, tk=(N + P) // 2, diff=diff, name="diff_attn")
        o_b = _nbr_attn(qkv_b, ck_b.astype(BF), with_ones(cv_b, 4), lp["na_bias"])
        o_c = _window_attn(qkv_c, ck_c.astype(BF), with_ones(cv_c, 2), lp["sw_sink"], 256)
        ckv_all = jnp.concatenate([ckv_c.astype(BF), ckv_n], axis=1)
        kpe_all = jnp.concatenate([jnp.pad(kpe_c, ((0, 0), (0, 0), (0, 128 - MLA_ROPE))).astype(BF), kpe_r], axis=1)
        k_d, v_d = _mla_expand(ckv_all, kpe_all, lw["w_k"], lw["w_v"], lp["e_place"], 256)
        o_d = _dense_attn(q_d, (512, 0), k_d, (512, 0), v_d, (512, 0), groups=GROUPS_D,
                          tq=LAT_TQ, tk=(N + P) // 2, name="mla_attn")
    else:
        o_a = _dense_attn(qkv_a, (256, 0), qkv_a, (256, 1), qkv_a, (512, 1), groups=GROUPS_A,
                          tq=N, tk=N, diff=diff, name="diff_attn_ctx")
        o_b = _dense_attn(qkv_b, (256, 0), qkv_b, (256, 1), qkv_b, (512, 1), groups=GROUPS_B,
                          tq=N, tk=N, name="dense_attn_ctx")
        o_c = _dense_attn(qkv_c, (256, 0), qkv_c, (128, 4), qkv_c, (256, 1), groups=GROUPS_C,
                          tq=N, tk=N, sink=lp["sw_sink"], name="gqa_attn_ctx")
        k_d, v_d = _mla_expand(ckv_n, kpe_r, lw["w_k"], lw["w_v"], lp["e_place"], N)
        o_d = _dense_attn(q_d, (512, 0), k_d, (512, 0), v_d, (512, 0), groups=GROUPS_D,
                          tq=N, tk=N, name="mla_attn_ctx")
    x_new, h2, logits = _merge(x, mod, lp["g_norm1"], lp["g_norm2"], o_a, o_b, o_c, o_d,
                               lw["w_gate"], lw["w_branch"], lw["w_out"], lw["w_router"], tm)
    x_out = _moe(x_new, h2, logits, mod, g_final, lp["w_e_gate"], lp["w_e_up"], lp["w_e_down"], lp["layer"], final)
    cache = None
    if not latent:
        strip = lambda v, heads: v.reshape(B, N, heads, LANES)[..., :64]
        cache = (qkv_a[:, :, 256:512].reshape(B, N, 4, 64), strip(qkv_a[:, :, 512:1024], 4),
                 qkv_b[:, :, 256:512].reshape(B, N, 4, 64), strip(qkv_b[:, :, 512:1024], 4),
                 qkv_c[:, :, 512:640].reshape(B, N, 2, 64), strip(qkv_c[:, :, 256:512], 2),
                 ckv_n, kpe_r[:, :, :MLA_ROPE])
    return x_out, cache


def kernel(x_prompt, x_sample, cache_diff_k, cache_diff_v, cache_na_k, cache_na_v, cache_swa_k, cache_swa_v, cache_mla_ckv, cache_mla_kpe, c, c_ctx, w_mod, b_mod, g_norm1, g_norm2, w_in, da_lambda, da_subln, na_rpb, sw_sink, mla_q_norm, mla_w_uq, mla_kv_norm, mla_w_ukv, w_gate, w_branch, w_out, w_router, w_e_gate, w_e_up, w_e_down, g_final):
    D = D_MODEL
    BS, NS, _ = x_sample.shape
    P = cache_diff_k.shape[2]
    cond = jnp.concatenate([c_ctx[None], c, jnp.zeros((16 - 1 - BS, D), F32)], axis=0)
    mod_all = _modulation(cond, w_mod, b_mod).reshape(DEPTH, 16, 6, D)
    tables = _rope_tables(NS)
    e_place = _kpe_placement()
    gf = g_final.reshape(1, D)
    na_bias = _nbr_bias_tables(na_rpb, NS // GRID_W)
    xp, xs = x_prompt, x_sample
    caches = []
    for l in range(DEPTH):
        lw = _prep_layer(l, w_in, mla_w_uq, mla_w_ukv, w_gate, w_branch, w_out, w_router)
        lp = dict(
            g_norm1=g_norm1[l].reshape(1, D), g_norm2=g_norm2[l].reshape(1, D),
            mla_q_norm=mla_q_norm[l].reshape(1, 256), mla_kv_norm=mla_kv_norm[l].reshape(1, 128),
            da_lambda=da_lambda[l], da_subln=da_subln[l].reshape(1, 64), sw_sink=sw_sink[l],
            w_e_gate=w_e_gate, w_e_up=w_e_up, w_e_down=w_e_down, layer=l, e_place=e_place, na_bias=na_bias[l])
        lam_init = 0.8 - 0.6 * math.exp(-0.3 * l)
        final = l == DEPTH - 1
        xp, cache_l = _layer(xp, mod_all[l, 0:1], lw, lp, lam_init, None, None, final, gf)
        caches.append(cache_l)
        ctx = (cache_diff_k[:, l].reshape(BS, P, 256), cache_diff_v[:, l].reshape(BS, P, 256),
               cache_na_k[:, l].reshape(BS, P, 256), cache_na_v[:, l].reshape(BS, P, 256),
               cache_swa_k[:, l].reshape(BS, P, 128), cache_swa_v[:, l].reshape(BS, P, 128),
               cache_mla_ckv[:, l], cache_mla_kpe[:, l])
        xs, _ = _layer(xs, mod_all[l, 1:1 + BS], lw, lp, lam_init, ctx, tables, final, gf)
    outs = tuple(jnp.stack([cl[i] for cl in caches], axis=1) for i in range(8))
    return (xp, xs) + outs
```

```python
import functools
import math

import numpy as np
import jax
import jax.numpy as jnp
from jax import lax
from jax.experimental import pallas as pl
from jax.experimental.pallas import tpu as pltpu

BF = jnp.bfloat16
F32 = jnp.float32

D_MODEL = 1024
DEPTH = 4
GRID_W = 64
ROPE_BASE = 10000.0
NORM_EPS = 1e-6
NEG_INF = -1e30
LOG2E = math.log2(math.e)

DA_QK = 32
NA_WIN_R = 8
NA_WIN_C = 16
SW_WINDOW = 128
MLA_NOPE = 64
MLA_ROPE = 32
N_EXPERTS = 16
EC_CAPACITY = 2

IN_PAD = 3200
LANES = 128

VMEM_BIG = 56 * 1024 * 1024
VMEM_MID = 40 * 1024 * 1024


def _cparams(sem, vmem=None):
    return pltpu.CompilerParams(dimension_semantics=sem, vmem_limit_bytes=vmem)


def _dot(a, b):
    return jnp.dot(a, b, preferred_element_type=F32)


def _dot_nt(a, b):
    return lax.dot_general(a, b, (((1,), (1,)), ((), ())), preferred_element_type=F32)


def _rms(x, g):
    var = jnp.mean(x * x, axis=-1, keepdims=True)
    return x * lax.rsqrt(var + NORM_EPS) * g


def _lane_mask(width, lo, length):
    lane = lax.broadcasted_iota(jnp.int32, (1, width), 1)
    return (lane >= lo) & (lane < lo + length)


def _mod_kernel(c_ref, w_ref, b_ref, o_ref):
    c = c_ref[...]
    s = c * (1.0 / (1.0 + jnp.exp(-c)))
    o_ref[0] = jnp.dot(s, w_ref[0], preferred_element_type=F32, precision=lax.Precision.HIGHEST) + b_ref[0]


def _modulation(cond, w_mod, b_mod):
    R, D = cond.shape
    L = w_mod.shape[0]
    nj = w_mod.shape[2] // D
    return pl.pallas_call(
        _mod_kernel,
        grid=(L, nj),
        in_specs=[
            pl.BlockSpec((R, D), lambda l, j: (0, 0)),
            pl.BlockSpec((1, D, D), lambda l, j: (l, 0, j)),
            pl.BlockSpec((1, 1, D), lambda l, j: (l, 0, j)),
        ],
        out_specs=pl.BlockSpec((1, R, D), lambda l, j: (l, 0, j)),
        out_shape=jax.ShapeDtypeStruct((L, R, nj * D), F32),
        compiler_params=_cparams(("parallel", "parallel")),
        name="modulation",
    )(cond, w_mod, b_mod.reshape(L, 1, nj * D))


def _rope(x, c, s, d):
    w = x.shape[-1]
    lane = lax.broadcasted_iota(jnp.int32, (1, LANES), 1)
    first = (lane % (2 * d)) < d
    outs = []
    for b in range(w // LANES):
        sl = slice(b * LANES, (b + 1) * LANES)
        xb = x[:, sl]
        partner = jnp.where(first, pltpu.roll(xb, LANES - d, 1), pltpu.roll(xb, d, 1))
        outs.append(xb * c[:, sl] + partner * s[:, sl])
    return outs[0] if len(outs) == 1 else jnp.concatenate(outs, axis=-1)


def _inproj_kernel(*refs, rope):
    if rope:
        (x_ref, mod_ref, g1_ref, w_ref, qn_ref, wuq_ref, kvn_ref,
         tac, tas, tcc, tcs, tqc, tqs, tkc, tks,
         oa, ob, oc, oq, ockv, okpe) = refs
    else:
        (x_ref, mod_ref, g1_ref, w_ref, qn_ref, wuq_ref, kvn_ref,
         oa, ob, oc, oq, ockv, okpe) = refs
    dt = oa.dtype
    x = x_ref[0]
    h = (_rms(x, g1_ref[...]) * (1.0 + mod_ref[0, 1:2, :]) + mod_ref[0, 0:1, :]).astype(BF)

    def seg(lo, w):
        return _dot(h, w_ref[:, lo:lo + w])

    ua = seg(0, 1024)
    qk = ua[:, :512]
    if rope:
        qk = _rope(qk, tac[...], tas[...], 8)
    oa[0, :, 0:256] = (qk[:, :256] * (DA_QK ** -0.5 * LOG2E)).astype(dt)
    oa[0, :, 256:512] = qk[:, 256:].astype(dt)
    ones_lane = lax.broadcasted_iota(jnp.int32, (1, 512), 1) % LANES >= 64
    oa[0, :, 512:1024] = jnp.where(ones_lane, 1.0, ua[:, 512:]).astype(dt)
    ub = seg(1024, 1024)
    ob[0, :, 0:256] = (ub[:, :256] * (0.125 * LOG2E)).astype(dt)
    ob[0, :, 256:512] = ub[:, 256:512].astype(dt)
    ob[0, :, 512:1024] = jnp.where(ones_lane, 1.0, ub[:, 512:]).astype(dt)
    uc = seg(2048, 640)
    qk = jnp.concatenate([uc[:, :256], uc[:, 512:]], axis=-1)
    if rope:
        qk = _rope(qk, tcc[...], tcs[...], 16)
    oc[0, :, 0:256] = (qk[:, :256] * (0.125 * LOG2E)).astype(dt)
    oc[0, :, 256:512] = jnp.where(ones_lane[:, :256], 1.0, uc[:, 256:512]).astype(dt)
    oc[0, :, 512:640] = qk[:, 256:].astype(dt)
    cq = _rms(seg(2688, 256), qn_ref[...]).astype(BF)
    qd = _dot(cq, wuq_ref[...])
    if rope:
        qd = _rope(qd, tqc[...], tqs[...], 8)
    oq[0] = (qd * ((MLA_NOPE + MLA_ROPE) ** -0.5 * LOG2E)).astype(dt)
    ockv[0] = _rms(seg(2944, 128), kvn_ref[...]).astype(dt)
    kpe = seg(3072, 128)
    if rope:
        kpe = _rope(kpe, tkc[...], tks[...], 8)
    okpe[0] = kpe.astype(dt)


def _in_proj(x, mod, g1, w_in_p, q_norm, w_uq_p, kv_norm, tables, out_dtype, tm):
    B, N, D = x.shape
    rope = tables is not None
    nmod = mod.shape[0]
    mod_map = (lambda j, b: (b, 0, 0)) if nmod > 1 else (lambda j, b: (0, 0, 0))
    const2 = lambda j, b: (0, 0)
    in_specs = [
        pl.BlockSpec((1, tm, D), lambda j, b: (b, j, 0)),
        pl.BlockSpec((1, 6, D), mod_map),
        pl.BlockSpec((1, D), const2),
        pl.BlockSpec((D, IN_PAD), const2),
        pl.BlockSpec((1, 256), const2),
        pl.BlockSpec((256, 512), const2),
        pl.BlockSpec((1, 128), const2),
    ]
    args = [x, mod, g1, w_in_p, q_norm, w_uq_p, kv_norm]
    if rope:
        for t in tables:
            in_specs.append(pl.BlockSpec((tm, t.shape[1]), lambda j, b: (j, 0)))
            args.append(t)
    widths = (1024, 1024, 640, 512, 128, 128)
    out_specs = [pl.BlockSpec((1, tm, w), lambda j, b: (b, j, 0)) for w in widths]
    out_shape = [jax.ShapeDtypeStruct((B, N, w), out_dtype) for w in widths]
    return pl.pallas_call(
        functools.partial(_inproj_kernel, rope=rope),
        grid=(N // tm, B),
        in_specs=in_specs,
        out_specs=out_specs,
        out_shape=out_shape,
        compiler_params=_cparams(("parallel", "parallel"), VMEM_BIG),
        name="in_proj_rope" if rope else "in_proj",
    )(*args)


def _mla_expand_kernel(ckv_ref, kpe_ref, wk_ref, wv_ref, e_ref, ok, ov):
    ckv = ckv_ref[0].astype(BF)
    kpe = kpe_ref[0].astype(BF)
    ok[0] = (_dot(ckv, wk_ref[...]) + _dot(kpe, e_ref[...])).astype(ok.dtype)
    ones_lane = lax.broadcasted_iota(jnp.int32, (1, 512), 1) % LANES >= 64
    ov[0] = jnp.where(ones_lane, 1.0, _dot(ckv, wv_ref[...])).astype(ov.dtype)


def _mla_expand(ckv, kpe, w_k, w_v, e_place, tk):
    B, K, _ = ckv.shape
    const2 = lambda b, j: (0, 0)
    return pl.pallas_call(
        _mla_expand_kernel,
        grid=(B, K // tk),
        in_specs=[
            pl.BlockSpec((1, tk, 128), lambda b, j: (b, j, 0)),
            pl.BlockSpec((1, tk, 128), lambda b, j: (b, j, 0)),
            pl.BlockSpec((128, 512), const2),
            pl.BlockSpec((128, 512), const2),
            pl.BlockSpec((128, 512), const2),
        ],
        out_specs=[pl.BlockSpec((1, tk, 512), lambda b, j: (b, j, 0)),
                   pl.BlockSpec((1, tk, 512), lambda b, j: (b, j, 0))],
        out_shape=[jax.ShapeDtypeStruct((B, K, 512), BF), jax.ShapeDtypeStruct((B, K, 512), BF)],
        compiler_params=_cparams(("parallel", "parallel")),
        name="mla_expand",
    )(ckv, kpe, w_k, w_v, e_place)


def _head_softmax(scores, values):
    m = functools.reduce(jnp.maximum, [jnp.max(s, axis=-1, keepdims=True) for s in scores])
    acc = None
    for s, v in zip(scores, values):
        part = _dot(jnp.exp2(s - m).astype(BF), v)
        acc = part if acc is None else acc + part
    return acc, m


def _normalise(acc):
    return acc * pltpu.roll(1.0 / acc, 64, 1)


def _store_heads(o_ref, heads):
    half = _lane_mask(LANES, 0, 64)
    for hp in range(len(heads) // 2):
        slab = jnp.where(half, heads[2 * hp], pltpu.roll(heads[2 * hp + 1], 64, 1))
        o_ref[0, :, hp * LANES:(hp + 1) * LANES] = slab.astype(o_ref.dtype)


def _dense_attn_kernel(*refs, groups, tk, n_chunks, has_sink, diff):
    it = iter(refs)
    q_ref, k_ref, v_ref = next(it), next(it), next(it)
    sink_ref = lam_ref = subln_ref = None
    if has_sink:
        sink_ref = next(it)
    if diff:
        lam_ref, subln_ref, lam_init_ref = next(it), next(it), next(it)
    o_ref = next(it)
    tq = q_ref.shape[1]

    lam = lam_init = None
    if diff:
        lam_init = lam_init_ref[0]
        dl = lam_ref[...]
        lam = (jnp.exp(jnp.sum(dl[0:1] * dl[1:2], axis=-1, keepdims=True))
               - jnp.exp(jnp.sum(dl[2:3] * dl[3:4], axis=-1, keepdims=True)) + lam_init)

    ng = len(groups)
    scratch = list(it)
    qg_s, m_s, acc_s = (scratch[j * ng:(j + 1) * ng] for j in range(3))
    for gi, (q_lo, k_lo, w, qmask, v_lo, head, sink_idx, sign) in enumerate(groups):
        qg = q_ref[0, :, q_lo:q_lo + w].astype(BF)
        if qmask is not None:
            qg = jnp.where(_lane_mask(w, *qmask), qg, jnp.zeros_like(qg))
        qg_s[gi][...] = qg
        m_s[gi][...] = jnp.full((tq, 1), NEG_INF, F32)
        acc_s[gi][...] = jnp.zeros((tq, LANES), F32)

    def keys_step(c):
        off = c * tk
        for gi, (q_lo, k_lo, w, qmask, v_lo, head, sink_idx, sign) in enumerate(groups):
            s = _dot_nt(qg_s[gi][...], k_ref[0, pl.ds(off, tk), k_lo:k_lo + w].astype(BF))
            m = m_s[gi][...]
            m_new = jnp.maximum(m, jnp.max(s, axis=-1, keepdims=True))
            p = jnp.exp2(s - m_new).astype(BF)
            pv = _dot(p, v_ref[0, pl.ds(off, tk), v_lo:v_lo + LANES].astype(BF))
            acc_s[gi][...] = jnp.exp2(m - m_new) * acc_s[gi][...] + pv
            m_s[gi][...] = m_new

    for c in range(n_chunks):
        keys_step(c)

    heads = {}
    ones_half = _lane_mask(LANES, 64, 64)
    for gi, (q_lo, k_lo, w, qmask, v_lo, head, sink_idx, sign) in enumerate(groups):
        acc = acc_s[gi][...]
        if sink_idx is not None:
            acc = acc + jnp.where(ones_half, jnp.exp2(sink_ref[sink_idx] * LOG2E - m_s[gi][...]), 0.0)
        part = _normalise(acc)
        if sign < 0:
            part = -lam * part
        heads[head] = part if head not in heads else heads[head] + part
    if diff:
        half = _lane_mask(LANES, 0, 64)
        for hd, o in heads.items():
            ms = jnp.sum(jnp.where(half, o * o, 0.0), axis=-1, keepdims=True) * (1.0 / 64.0)
            heads[hd] = o * lax.rsqrt(ms + NORM_EPS) * subln_ref[...] * (1.0 - lam_init)
    _store_heads(o_ref, heads)


def _dense_attn(q_arr, q_blk, k_arr, k_blk, v_arr, v_blk, *, groups, tq, tk, sink=None, diff=None, name):
    B, N, _ = q_arr.shape
    K = k_arr.shape[1]
    n_chunks = K // tk
    assert n_chunks * tk == K and n_chunks <= 2
    qw, qi = q_blk
    kw, ki = k_blk
    vw, vi = v_blk
    in_specs = [
        pl.BlockSpec((1, tq, qw), lambda b, i: (b, i, qi)),
        pl.BlockSpec((1, K, kw), lambda b, i: (b, 0, ki)),
        pl.BlockSpec((1, K, vw), lambda b, i: (b, 0, vi)),
    ]
    args = [q_arr, k_arr, v_arr]
    if sink is not None:
        in_specs.append(pl.BlockSpec(memory_space=pltpu.SMEM))
        args.append(sink)
    if diff is not None:
        da_lambda, subln, lam_init = diff
        in_specs += [pl.BlockSpec((4, DA_QK), lambda b, i: (0, 0)),
                     pl.BlockSpec((1, LANES), lambda b, i: (0, 0)),
                     pl.BlockSpec(memory_space=pltpu.SMEM)]
        args += [da_lambda, subln, jnp.full((1,), lam_init, F32)]
    kern = functools.partial(_dense_attn_kernel, groups=groups, tk=tk, n_chunks=n_chunks,
                             has_sink=sink is not None, diff=diff is not None)
    scratch = ([pltpu.VMEM((tq, g[2]), BF) for g in groups] + [pltpu.VMEM((tq, 1), F32) for _ in groups]
               + [pltpu.VMEM((tq, LANES), F32) for _ in groups])
    return pl.pallas_call(
        kern,
        grid=(B, N // tq),
        in_specs=in_specs,
        out_specs=pl.BlockSpec((1, tq, 256), lambda b, i: (b, i, 0)),
        out_shape=jax.ShapeDtypeStruct((B, N, 256), BF),
        scratch_shapes=scratch,
        compiler_params=_cparams(("parallel", "parallel"), VMEM_BIG),
        name=name,
    )(*args)


GROUPS_A = tuple((128 * (g // 4), 128 * (g // 4), 128, (32 * (g % 4), 32), 128 * (g // 2), g // 2, None,
                  1 if g % 2 == 0 else -1) for g in range(8))
GROUPS_B = tuple((128 * (h // 2), 128 * (h // 2), 128, (64 * (h % 2), 64), 128 * h, h, None, 1) for h in range(4))
GROUPS_C = tuple((128 * g, 0, 128, (64 * j, 64), 128 * j, 2 * g + j, 2 * j + g, 1) for g in range(2) for j in range(2))
GROUPS_D = tuple((128 * h, 128 * h, 128, None, 128 * h, h, None, 1) for h in range(4))


def _window_attn_kernel(q_ref, k_ref, v_ref, ck_ref, cv_ref, sink_ref, o_ref, *, n, band):
    tq = q_ref.shape[1]
    i = pl.program_id(1)
    start = jnp.clip(i * tq - SW_WINDOW, 0, n - band)
    start = pl.multiple_of(start, 128)
    kb = k_ref[0, pl.ds(start, band), :]
    vb = v_ref[0, pl.ds(start, band), :]
    ck = ck_ref[0]
    cv = cv_ref[0]
    qpos = i * tq + lax.broadcasted_iota(jnp.int32, (tq, band), 0)
    kpos = start + lax.broadcasted_iota(jnp.int32, (tq, band), 1)
    valid = jnp.abs(qpos - kpos) <= SW_WINDOW
    ones_half = _lane_mask(LANES, 64, 64)
    heads = {}
    for g in range(2):
        for j in range(2):
            qg = q_ref[0, :, LANES * g:LANES * (g + 1)]
            qg = jnp.where(_lane_mask(LANES, 64 * j, 64), qg, jnp.zeros_like(qg))
            vs = slice(LANES * j, LANES * (j + 1))
            acc, m = _head_softmax([_dot_nt(qg, ck), jnp.where(valid, _dot_nt(qg, kb), NEG_INF)],
                                   [cv[:, vs], vb[:, vs]])
            acc = acc + jnp.where(ones_half, jnp.exp2(sink_ref[2 * j + g] * LOG2E - m), 0.0)
            heads[2 * g + j] = _normalise(acc)
    _store_heads(o_ref, heads)


def _window_attn(qkv_c, ck, cv, sink, tq):
    B, N, _ = qkv_c.shape
    band = tq + 2 * SW_WINDOW
    P = ck.shape[1]
    return pl.pallas_call(
        functools.partial(_window_attn_kernel, n=N, band=band),
        grid=(B, N // tq),
        in_specs=[
            pl.BlockSpec((1, tq, 256), lambda b, i: (b, i, 0)),
            pl.BlockSpec((1, N, 128), lambda b, i: (b, 0, 4)),
            pl.BlockSpec((1, N, 256), lambda b, i: (b, 0, 1)),
            pl.BlockSpec((1, P, 128), lambda b, i: (b, 0, 0)),
            pl.BlockSpec((1, P, 256), lambda b, i: (b, 0, 0)),
            pl.BlockSpec(memory_space=pltpu.SMEM),
        ],
        out_specs=pl.BlockSpec((1, tq, 256), lambda b, i: (b, i, 0)),
        out_shape=jax.ShapeDtypeStruct((B, N, 256), BF),
        compiler_params=_cparams(("parallel", "parallel")),
        name="window_attn",
    )(qkv_c, qkv_c, qkv_c, ck, cv, sink)


NA_TILE_ROWS = 4
NA_KEY_ROWS = NA_TILE_ROWS + NA_WIN_R


def _nbr_attn_kernel(q_ref, k_ref, v_ref, ck_ref, cv_ref, bias_ref, o_ref, *, rows):
    i = pl.program_id(1)
    krow = jnp.clip(i * NA_TILE_ROWS - NA_WIN_R // 2, 0, rows - NA_KEY_ROWS)
    start = pl.multiple_of(krow * GRID_W, GRID_W)
    nk = NA_KEY_ROWS * GRID_W
    kb = k_ref[0, pl.ds(start, nk), :]
    vb = v_ref[0, pl.ds(start, nk), :]
    ck = ck_ref[0]
    cv = cv_ref[0]
    q = q_ref[0]
    heads = {}
    for h in range(4):
        ks = slice(LANES * (h // 2), LANES * (h // 2 + 1))
        qg = jnp.where(_lane_mask(LANES, 64 * (h % 2), 64), q[:, ks], jnp.zeros_like(q[:, ks]))
        vs = slice(LANES * h, LANES * (h + 1))
        acc, _ = _head_softmax([_dot_nt(qg, ck[:, ks]), _dot_nt(qg, kb[:, ks]) + bias_ref[0, h]],
                               [cv[:, vs], vb[:, vs]])
        heads[h] = _normalise(acc)
    _store_heads(o_ref, heads)


def _nbr_attn(qkv_b, ck, cv, bias):
    B, N, _ = qkv_b.shape
    rows = N // GRID_W
    tq = NA_TILE_ROWS * GRID_W
    nt = N // tq
    nk = NA_KEY_ROWS * GRID_W
    P = ck.shape[1]

    def bias_map(b, i):
        return (jnp.where(i == 0, 0, jnp.where(i == nt - 1, 2, 1)), 0, 0, 0)

    return pl.pallas_call(
        functools.partial(_nbr_attn_kernel, rows=rows),
        grid=(B, nt),
        in_specs=[
            pl.BlockSpec((1, tq, 256), lambda b, i: (b, i, 0)),
            pl.BlockSpec((1, N, 256), lambda b, i: (b, 0, 1)),
            pl.BlockSpec((1, N, 512), lambda b, i: (b, 0, 1)),
            pl.BlockSpec((1, P, 256), lambda b, i: (b, 0, 0)),
            pl.BlockSpec((1, P, 512), lambda b, i: (b, 0, 0)),
            pl.BlockSpec((1, 4, tq, nk), bias_map),
        ],
        out_specs=pl.BlockSpec((1, tq, 256), lambda b, i: (b, i, 0)),
        out_shape=jax.ShapeDtypeStruct((B, N, 256), BF),
        compiler_params=_cparams(("parallel", "parallel"), VMEM_MID),
        name="nbr_attn",
    )(qkv_b, qkv_b, qkv_b, ck, cv, bias)


def _nbr_bias_tables(rpb, rows):
    L, H = rpb.shape[:2]
    rpb = rpb.reshape((L * H,) + rpb.shape[2:])
    rp = jnp.pad(rpb.astype(F32), ((0, 0), (NA_KEY_ROWS, NA_KEY_ROWS), (GRID_W, GRID_W)))
    c0 = NA_WIN_C - 1 + GRID_W
    tcol = jnp.stack([rp[:, :, c0 - qc:c0 - qc + GRID_W] for qc in range(GRID_W)], axis=2)
    tabs = []
    nt = rows // NA_TILE_ROWS
    for r0 in (0, NA_TILE_ROWS * min(1, nt - 1) + NA_TILE_ROWS, rows - NA_TILE_ROWS):
        ks = int(np.clip(r0 - NA_WIN_R // 2, 0, rows - NA_KEY_ROWS))
        per_row = []
        for a in range(NA_TILE_ROWS):
            s0 = ks - (r0 + a) + NA_WIN_R - 1 + NA_KEY_ROWS
            per_row.append(tcol[:, s0:s0 + NA_KEY_ROWS])
        t = jnp.stack(per_row, axis=1).transpose(0, 1, 3, 2, 4)
        r = r0 + np.arange(NA_TILE_ROWS)[:, None, None, None]
        qc = np.arange(GRID_W)[None, :, None, None]
        kr = ks + np.arange(NA_KEY_ROWS)[None, None, :, None]
        kc = np.arange(GRID_W)[None, None, None, :]
        rs = np.clip(r - NA_WIN_R // 2, 0, rows - NA_WIN_R)
        cs = np.clip(qc - NA_WIN_C // 2, 0, GRID_W - NA_WIN_C)
        ok = (kr >= rs) & (kr < rs + NA_WIN_R) & (kc >= cs) & (kc < cs + NA_WIN_C)
        t = jnp.where(ok[None], t * LOG2E, NEG_INF)
        tabs.append(t.reshape(L, H, NA_TILE_ROWS * GRID_W, NA_KEY_ROWS * GRID_W))
    return jnp.stack(tabs, axis=1)


def _merge_kernel(x_ref, mod_ref, g1_ref, g2_ref, oa, ob, oc, od, wg_ref, wb_ref, wo_ref, wr_ref,
                  xo_ref, h2_ref, lg_ref):
    D = x_ref.shape[2]
    x = x_ref[0]
    h = (_rms(x, g1_ref[...]) * (1.0 + mod_ref[0, 1:2, :]) + mod_ref[0, 0:1, :]).astype(BF)
    merged = None
    for i, o_ref in enumerate((oa, ob, oc, od)):
        z = _dot(h, wg_ref[:, i * D:(i + 1) * D])
        gate = 1.0 / (1.0 + jnp.exp(-z))
        term = gate * _dot(o_ref[0], wb_ref[i])
        merged = term if merged is None else merged + term
    mixed = _dot(merged.astype(BF), wo_ref[...])
    xn = x + mod_ref[0, 2:3, :] * mixed
    xo_ref[0] = xn
    h2 = (_rms(xn, g2_ref[...]) * (1.0 + mod_ref[0, 4:5, :]) + mod_ref[0, 3:4, :]).astype(BF)
    h2_ref[0] = h2
    lg_ref[0] = _dot_nt(wr_ref[...], h2)


def _merge(x, mod, g1, g2, o_a, o_b, o_c, o_d, w_gate, w_branch, w_out, w_router_p, tm):
    B, N, D = x.shape
    nmod = mod.shape[0]
    mod_map = (lambda b, j: (b, 0, 0)) if nmod > 1 else (lambda b, j: (0, 0, 0))
    tok = lambda w: pl.BlockSpec((1, tm, w), lambda b, j: (b, j, 0))
    c2 = lambda b, j: (0, 0)
    return pl.pallas_call(
        _merge_kernel,
        grid=(B, N // tm),
        in_specs=[
            tok(D), pl.BlockSpec((1, 6, D), mod_map), pl.BlockSpec((1, D), c2), pl.BlockSpec((1, D), c2),
            tok(256), tok(256), tok(256), tok(256),
            pl.BlockSpec((D, 4 * D), c2),
            pl.BlockSpec((4, 256, D), lambda b, j: (0, 0, 0)),
            pl.BlockSpec((D, D), c2),
            pl.BlockSpec((N_EXPERTS, D), c2),
        ],
        out_specs=[tok(D), tok(D), pl.BlockSpec((1, N_EXPERTS, tm), lambda b, j: (b, 0, j))],
        out_shape=[jax.ShapeDtypeStruct((B, N, D), F32), jax.ShapeDtypeStruct((B, N, D), BF),
                   jax.ShapeDtypeStruct((B, N_EXPERTS, N), F32)],
        compiler_params=_cparams(("parallel", "parallel"), VMEM_BIG),
        name="merge",
    )(x, mod, g1, g2, o_a, o_b, o_c, o_d, w_gate, w_branch, w_out, w_router_p)


MOE_TT = 256
MOE_RC = 64
SLOT_ALIGN = 16


def _excl_cumsum(x, tri, reset_every):
    R, N = x.shape
    outs = []
    carry = jnp.zeros((R, 1), F32)
    for c in range(N // LANES):
        if reset_every and (c * LANES) % reset_every == 0:
            carry = jnp.zeros((R, 1), F32)
        xc = x[:, c * LANES:(c + 1) * LANES]
        inc = _dot(xc.astype(BF), tri)
        outs.append(inc - xc + carry)
        carry = carry + inc[:, LANES - 1:LANES]
    return (outs[0] if len(outs) == 1 else jnp.concatenate(outs, axis=1)), carry


def _route_kernel(lg_ref, aff_ref, rel_ref, base_ref, *, cap, tt):
    B, E, N = lg_ref.shape
    R = B * E
    nt = N // tt
    lg = lg_ref[...]
    ex = jnp.exp(lg - jnp.max(lg, axis=1, keepdims=True))
    aff = ex / jnp.sum(ex, axis=1, keepdims=True)
    aff_ref[...] = aff
    bits = pltpu.bitcast(aff.reshape(R, N), jnp.int32)

    def per_expert(v):
        t = v[0:E]
        for b in range(1, B):
            t = t + v[b * E:(b + 1) * E]
        return t

    def per_row(v):
        return jnp.concatenate([v] * B, axis=0) if B > 1 else v

    def batch_prefix(v):
        parts, run = [], jnp.zeros((E, 1), F32)
        for b in range(B):
            parts.append(run)
            run = run + v[b * E:(b + 1) * E]
        return jnp.concatenate(parts, axis=0) if B > 1 else parts[0]

    def count(mask):
        return per_expert(jnp.sum(jnp.where(mask, 1.0, 0.0), axis=1, keepdims=True))

    def search(i, thr):
        cand = thr | jnp.left_shift(jnp.int32(1), 30 - i)
        return jnp.where(count(bits >= per_row(cand)) >= float(cap), cand, thr)

    thr = per_row(lax.fori_loop(0, 31, search, jnp.zeros((E, 1), jnp.int32)))
    gt = bits > thr
    eq = bits == thr
    need = float(cap) - count(gt)
    ri = lax.broadcasted_iota(jnp.int32, (LANES, LANES), 0)
    ci = lax.broadcasted_iota(jnp.int32, (LANES, LANES), 1)
    tri = jnp.where(ri <= ci, 1.0, 0.0).astype(BF)
    rank_eq, eq_tot = _excl_cumsum(jnp.where(eq, 1.0, 0.0), tri, None)
    rank_eq = rank_eq + batch_prefix(eq_tot)
    sel = gt | (eq & (rank_eq < per_row(need)))
    self = jnp.where(sel, 1.0, 0.0)
    local, _ = _excl_cumsum(self, tri, tt)
    rel_ref[...] = jnp.where(sel, local, -1.0).astype(jnp.int32).reshape(B, E, N)
    tn = jnp.right_shift(lax.broadcasted_iota(jnp.int32, (N, LANES), 0), tt.bit_length() - 1)
    tj = lax.broadcasted_iota(jnp.int32, (N, LANES), 1)
    cnt = _dot(self.astype(BF), jnp.where(tn == tj, 1.0, 0.0).astype(BF)).astype(jnp.int32)
    sh = SLOT_ALIGN.bit_length() - 1
    ru = jnp.left_shift(jnp.right_shift(cnt + (SLOT_ALIGN - 1), sh), sh).astype(F32)
    base_b = _dot(ru.astype(BF), jnp.where(ri < ci, 1.0, 0.0).astype(BF))
    base = base_b + batch_prefix(base_b[:, nt:nt + 1])
    base_ref[...] = base.astype(jnp.int32).reshape(B, E, LANES)


def _route(lg_t, cap):
    B, E, N = lg_t.shape
    full = lambda s: pl.BlockSpec(s, lambda i: (0,) * len(s))
    return pl.pallas_call(
        functools.partial(_route_kernel, cap=cap, tt=MOE_TT),
        grid=(1,),
        in_specs=[full((B, E, N))],
        out_specs=[full((B, E, N)), full((B, E, N)), full((B, E, LANES))],
        out_shape=[jax.ShapeDtypeStruct((B, E, N), F32), jax.ShapeDtypeStruct((B, E, N), jnp.int32),
                   jax.ShapeDtypeStruct((B, E, LANES), jnp.int32)],
        compiler_params=_cparams(("arbitrary",), VMEM_BIG),
        name="route",
    )(lg_t)


def _slot_onehot(rel, values, c, rc):
    E, tt = rel.shape
    r = lax.broadcasted_iota(jnp.int32, (E, rc, tt), 1) + c * rc
    return jnp.where(rel[:, None, :] == r, values[:, None, :], 0.0).astype(BF).reshape(E * rc, tt)


def _dispatch_kernel(base_sm, x_ref, rel_ref, xe_in, xe_hbm, stage, stage2, sem, sem2, *, nt, rc, row_off):
    del xe_in
    b, j = pl.program_id(0), pl.program_id(1)
    E = rel_ref.shape[1]
    step = b * nt + j
    last = pl.num_programs(0) * nt - 1
    base = lambda e, jj: base_sm[(b * E + e) * (nt + 1) + jj]
    rel = rel_ref[0]
    x = x_ref[0]
    ones = jnp.ones(rel.shape, F32)

    def copy(buf, s, e, c):
        start = pl.multiple_of(row_off + base(e, j) + c * rc, SLOT_ALIGN)
        return pltpu.make_async_copy(buf.at[pl.ds(e * rc, rc)], xe_hbm.at[e, pl.ds(start, rc)], s.at[e])

    strip = _dot(_slot_onehot(rel, ones, 0, rc), x).astype(BF)

    @pl.when(step > 0)
    def _():
        for e in range(E):
            copy(stage, sem, e, 0).wait()

    stage[...] = strip
    for e in range(E):
        copy(stage, sem, e, 0).start()

    span = [base(e, j + 1) - base(e, j) for e in range(E)]
    widest = functools.reduce(jnp.maximum, span)

    def extra(c, carry):
        stage2[...] = _dot(_slot_onehot(rel, ones, c, rc), x).astype(BF)
        for e in range(E):
            @pl.when(span[e] > c * rc)
            def _():
                cp = copy(stage2, sem2, e, c)
                cp.start()
                cp.wait()
        return carry

    lax.fori_loop(1, (widest + rc - 1) // rc, extra, 0)

    @pl.when(step == last)
    def _():
        for e in range(E):
            copy(stage, sem, e, 0).wait()


def _dispatch(h2, rel, base_flat, xe, row_off):
    B, N, D = h2.shape
    E, rows, _ = xe.shape
    nt = N // MOE_TT
    return pl.pallas_call(
        functools.partial(_dispatch_kernel, nt=nt, rc=MOE_RC, row_off=row_off),
        grid_spec=pltpu.PrefetchScalarGridSpec(
            num_scalar_prefetch=1,
            grid=(B, nt),
            in_specs=[
                pl.BlockSpec((1, MOE_TT, D), lambda b, j, s: (b, j, 0)),
                pl.BlockSpec((1, E, MOE_TT), lambda b, j, s: (b, 0, j)),
                pl.BlockSpec(memory_space=pl.ANY),
            ],
            out_specs=pl.BlockSpec(memory_space=pl.ANY),
            scratch_shapes=[pltpu.VMEM((E * MOE_RC, D), BF), pltpu.VMEM((E * MOE_RC, D), BF),
                            pltpu.SemaphoreType.DMA((E,)), pltpu.SemaphoreType.DMA((E,))],
        ),
        out_shape=jax.ShapeDtypeStruct((E, rows, D), BF),
        input_output_aliases={3: 0},
        compiler_params=_cparams(("arbitrary", "arbitrary")),
        name="dispatch",
    )(base_flat, h2, rel, xe)


def _ffn_kernel(tot_sm, x_ref, wg_hbm, wu_hbm, wd_hbm, y_ref, wbuf, wg_s, wu_s, wd_s, sem, *, tm, rc, layer,
                n_first):
    e, j = pl.program_id(0), pl.program_id(1)
    slot = e % 2

    def copy(ee, sl, k):
        return pltpu.make_async_copy((wg_hbm, wu_hbm, wd_hbm)[k].at[layer, ee], wbuf.at[sl, k], sem.at[sl, k])

    @pl.when(j == 0)
    def _():
        @pl.when(e == 0)
        def _():
            for k in range(3):
                copy(e, slot, k).start()

        @pl.when(e + 1 < pl.num_programs(0))
        def _():
            for k in range(3):
                copy(e + 1, 1 - slot, k).start()

        for k in range(3):
            copy(e, slot, k).wait()
        wg_s[...] = wbuf[slot, 0].astype(BF)
        wu_s[...] = wbuf[slot, 1].astype(BF)
        wd_s[...] = wbuf[slot, 2].astype(BF)

    second = j >= n_first
    jj = jnp.where(second, j - n_first, j)
    used = jj * tm < tot_sm[jnp.where(second, pl.num_programs(0) + e, e)] + rc

    @pl.when(used)
    def _():
        x = x_ref[0]
        hg = _dot(x, wg_s[...])
        hu = _dot(x, wu_s[...])
        act = (hg * (1.0 / (1.0 + jnp.exp(-hg))) * hu).astype(BF)
        y_ref[0] = _dot(act, wd_s[...]).astype(y_ref.dtype)

    @pl.when(jnp.logical_not(used))
    def _():
        y_ref[...] = jnp.zeros_like(y_ref)


def _expert_ffn(xe, totals, rows_first, w_g, w_u, w_d, layer, tm):
    E, rows, D = xe.shape
    DE = w_g.shape[3]
    n_first = rows_first // tm

    def tok(e, j, tot):
        second = j >= n_first
        last = (tot[jnp.where(second, E + e, e)] + MOE_RC - 1) // tm
        return (e, jnp.where(second, n_first + jnp.minimum(j - n_first, last), jnp.minimum(j, last)), 0)

    assert D == DE
    hbm = pl.BlockSpec(memory_space=pl.ANY)
    return pl.pallas_call(
        functools.partial(_ffn_kernel, tm=tm, rc=MOE_RC, layer=layer, n_first=n_first),
        grid_spec=pltpu.PrefetchScalarGridSpec(
            num_scalar_prefetch=1,
            grid=(E, rows // tm),
            in_specs=[pl.BlockSpec((1, tm, D), tok), hbm, hbm, hbm],
            out_specs=pl.BlockSpec((1, tm, D), lambda e, j, tot: (e, j, 0)),
            scratch_shapes=[pltpu.VMEM((2, 3, D, DE), F32), pltpu.VMEM((D, DE), BF), pltpu.VMEM((D, DE), BF),
                            pltpu.VMEM((DE, D), BF), pltpu.SemaphoreType.DMA((2, 3))],
        ),
        out_shape=jax.ShapeDtypeStruct((E, rows, D), BF),
        compiler_params=_cparams(("arbitrary", "arbitrary"), VMEM_BIG),
        name="expert_ffn",
    )(totals, xe, w_g, w_u, w_d)


def _combine_kernel(base_sm, x_ref, rel_ref, aff_ref, mod_ref, gf_ref, ye_hbm, o_ref, stage, sem, *,
                    nt, rc, final, row_off):
    b, j = pl.program_id(0), pl.program_id(1)
    E = rel_ref.shape[1]
    step = b * nt + j
    nsteps = pl.num_programs(0) * nt
    slot = step % 2
    base = lambda e, jj: base_sm[(b * E + e) * (nt + 1) + jj]
    rel = rel_ref[0]
    aff = aff_ref[0]

    def copy(st, sl, e, c):
        first = base_sm[(st // nt * E + e) * (nt + 1) + st % nt]
        start = pl.multiple_of(row_off + first + c * rc, SLOT_ALIGN)
        return pltpu.make_async_copy(ye_hbm.at[e, pl.ds(start, rc)], stage.at[sl, pl.ds(e * rc, rc)],
                                     sem.at[sl, e])

    def scatter(c):
        w = _slot_onehot(rel, aff, c, rc)
        return lax.dot_general(w, stage[slot], (((0,), (0,)), ((), ())), preferred_element_type=F32)

    @pl.when(step == 0)
    def _():
        for e in range(E):
            copy(step, slot, e, 0).start()

    @pl.when(step + 1 < nsteps)
    def _():
        for e in range(E):
            copy(step + 1, 1 - slot, e, 0).start()

    for e in range(E):
        copy(step, slot, e, 0).wait()
    acc = scatter(0)
    span = [base(e, j + 1) - base(e, j) for e in range(E)]
    widest = functools.reduce(jnp.maximum, span)

    def extra(c, acc):
        for e in range(E):
            @pl.when(span[e] > c * rc)
            def _():
                cp = copy(step, slot, e, c)
                cp.start()
                cp.wait()
        return acc + scatter(c)

    acc = lax.fori_loop(1, (widest + rc - 1) // rc, extra, acc)
    xn = x_ref[0] + mod_ref[0, 5:6, :] * acc
    if final:
        xn = _rms(xn, gf_ref[...])
    o_ref[0] = xn


def _combine(x, rel, aff, mod, g_final, ye, base_flat, row_off, final):
    B, N, D = x.shape
    E = rel.shape[1]
    nt = N // MOE_TT
    nmod = mod.shape[0]
    mod_map = (lambda b, j, s: (b, 0, 0)) if nmod > 1 else (lambda b, j, s: (0, 0, 0))
    tok = pl.BlockSpec((1, MOE_TT, D), lambda b, j, s: (b, j, 0))
    slots = pl.BlockSpec((1, E, MOE_TT), lambda b, j, s: (b, 0, j))
    return pl.pallas_call(
        functools.partial(_combine_kernel, nt=nt, rc=MOE_RC, final=final, row_off=row_off),
        grid_spec=pltpu.PrefetchScalarGridSpec(
            num_scalar_prefetch=1,
            grid=(B, nt),
            in_specs=[tok, slots, slots, pl.BlockSpec((1, 6, D), mod_map),
                      pl.BlockSpec((1, D), lambda b, j, s: (0, 0)), pl.BlockSpec(memory_space=pl.ANY)],
            out_specs=tok,
            scratch_shapes=[pltpu.VMEM((2, E * MOE_RC, D), BF), pltpu.SemaphoreType.DMA((2, E))],
        ),
        out_shape=jax.ShapeDtypeStruct((B, N, D), F32),
        compiler_params=_cparams(("arbitrary", "arbitrary")),
        name="combine_final" if final else "combine",
    )(base_flat, x, rel, aff, mod, g_final, ye)


FFN_TM = 512


def _slot_rows(B, N):
    T = B * N
    worst = EC_CAPACITY * T // N_EXPERTS + (SLOT_ALIGN - 1) * (T // MOE_TT) + MOE_RC
    return -(-worst // FFN_TM) * FFN_TM


def _moe_pair(sets, g_final, w_g, w_u, w_d, layer, final):
    D = sets[0][1].shape[2]
    rows = [_slot_rows(*st[1].shape[:2]) for st in sets]
    xe = jnp.zeros((N_EXPERTS, sum(rows), D), BF)
    routed, row_off = [], 0
    for (x, h2, lg_t, mod), r in zip(sets, rows):
        B, N, _ = h2.shape
        nt = N // MOE_TT
        aff, rel, base = _route(lg_t, EC_CAPACITY * B * N // N_EXPERTS)
        base_flat = base[:, :, :nt + 1].reshape(-1)
        xe = _dispatch(h2, rel, base_flat, xe, row_off)
        routed.append((aff, rel, base_flat, base[B - 1, :, nt], row_off))
        row_off += r
    ye = _expert_ffn(xe, jnp.concatenate([routed[0][3], routed[1][3]]), rows[0], w_g, w_u, w_d, layer, FFN_TM)
    return [_combine(x, rel, aff, mod, g_final, ye, base_flat, off, final)
            for (x, h2, lg_t, mod), (aff, rel, base_flat, _, off) in zip(sets, routed)]


def _rope_tables(n):
    t = jnp.arange(n)
    pos = {1: (t // GRID_W).astype(F32), 2: (t % GRID_W).astype(F32)}

    def group(quarter):
        freqs = jnp.float32(ROPE_BASE) ** (-jnp.arange(quarter, dtype=F32) / quarter)
        cs, ss = [], []
        for kind in (1, 2):
            ang = pos[kind][:, None] * freqs
            c, s = jnp.cos(ang), jnp.sin(ang)
            cs += [c, c]
            ss += [-s, s]
        return jnp.concatenate(cs, axis=1), jnp.concatenate(ss, axis=1)

    c32, s32 = group(8)
    c64, s64 = group(16)
    one = lambda w: jnp.ones((n, w), F32)
    zero = lambda w: jnp.zeros((n, w), F32)
    ta = (jnp.tile(c32, (1, 16)), jnp.tile(s32, (1, 16)))
    tc = (jnp.tile(c64, (1, 6)), jnp.tile(s64, (1, 6)))
    qc = jnp.concatenate([one(64), c32, one(32)], axis=1)
    qs = jnp.concatenate([zero(64), s32, zero(32)], axis=1)
    tq = (jnp.tile(qc, (1, 4)), jnp.tile(qs, (1, 4)))
    tk = (jnp.concatenate([c32, one(96)], axis=1), jnp.concatenate([s32, zero(96)], axis=1))
    return (ta[0], ta[1], tc[0], tc[1], tq[0], tq[1], tk[0], tk[1])


LAT_TQ = 512
C_HEAD_ORDER = (0, 2, 1, 3)


def _prep_layer(l, w_in, mla_w_uq, mla_w_ukv, w_gate, w_branch, w_out, w_router):
    D = D_MODEL
    wi = w_in[l]
    cq = wi[:, 1536:1792].reshape(D, 4, 64)[:, C_HEAD_ORDER, :].reshape(D, 256)
    def spread(cols, heads):
        return jnp.pad(cols.reshape(D, heads, 64), ((0, 0), (0, 0), (0, 64))).reshape(D, heads * LANES)

    w_in_p = jnp.concatenate(
        [wi[:, :512], spread(wi[:, 512:768], 4), wi[:, 768:1280], spread(wi[:, 1280:1536], 4),
         cq, spread(wi[:, 1920:2048], 2), wi[:, 1792:1920], wi[:, 2048:],
         jnp.zeros((D, LANES - MLA_ROPE), F32)], axis=1).astype(BF)
    uq = mla_w_uq[l].reshape(256, 4, MLA_NOPE + MLA_ROPE)
    w_uq_p = jnp.concatenate([uq, jnp.zeros((256, 4, 32), F32)], axis=2).reshape(256, 512).astype(BF)
    ukv = mla_w_ukv[l].reshape(128, 4, 128)
    w_k = jnp.concatenate([ukv[:, :, :64], jnp.zeros((128, 4, 64), F32)], axis=2).reshape(128, 512).astype(BF)
    w_v = jnp.concatenate([ukv[:, :, 64:], jnp.zeros((128, 4, 64), F32)], axis=2).reshape(128, 512).astype(BF)
    wb = w_branch[l]
    wb_c = wb[2].reshape(4, 64, D)[C_HEAD_ORDER, :, :].reshape(256, D)
    w_branch_p = jnp.stack([wb[0], wb[1], wb_c, wb[3]]).astype(BF)
    w_router_p = w_router[l].T.astype(BF)
    return dict(w_in=w_in_p, w_uq=w_uq_p, w_k=w_k, w_v=w_v, w_gate=w_gate[l].astype(BF),
                w_branch=w_branch_p, w_out=w_out[l].astype(BF), w_router=w_router_p)


def _kpe_placement():
    e = np.zeros((128, 512), np.float32)
    for h in range(4):
        for i in range(MLA_ROPE):
            e[i, 128 * h + MLA_NOPE + i] = 1.0
    return jnp.asarray(e, BF)


def _mixing(x, mod, lw, lp, lam_init, ctx, tables):
    B, N, D = x.shape
    latent = ctx is not None
    tm = 512 if latent else 256
    out_dtype = BF if latent else F32
    qkv_a, qkv_b, qkv_c, q_d, ckv_n, kpe_r = _in_proj(
        x, mod, lp["g_norm1"], lw["w_in"], lp["mla_q_norm"], lw["w_uq"], lp["mla_kv_norm"],
        tables if latent else None, out_dtype, tm)
    diff = (lp["da_lambda"], jnp.tile(lp["da_subln"], (1, 2)), lam_init)
    if latent:
        ck_a, cv_a, ck_b, cv_b, ck_c, cv_c, ckv_c, kpe_c = ctx
        P = ck_a.shape[1]

        def with_ones(v, heads):
            v = v.reshape(B, P, heads, 64).astype(BF)
            return jnp.concatenate([v, jnp.ones_like(v)], axis=-1).reshape(B, P, heads * LANES)

        k_a = jnp.concatenate([ck_a.astype(BF), qkv_a[:, :, 256:512]], axis=1)
        v_a = jnp.concatenate([with_ones(cv_a, 4), qkv_a[:, :, 512:1024]], axis=1)
        o_a = _dense_attn(qkv_a, (256, 0), k_a, (256, 0), v_a, (512, 0), groups=GROUPS_A,
                          tq=LAT_TQ, tk=(N + P) // 2, diff=diff, name="diff_attn")
        o_b = _nbr_attn(qkv_b, ck_b.astype(BF), with_ones(cv_b, 4), lp["na_bias"])
        o_c = _window_attn(qkv_c, ck_c.astype(BF), with_ones(cv_c, 2), lp["sw_sink"], 256)
        ckv_all = jnp.concatenate([ckv_c.astype(BF), ckv_n], axis=1)
        kpe_all = jnp.concatenate([jnp.pad(kpe_c, ((0, 0), (0, 0), (0, 128 - MLA_ROPE))).astype(BF), kpe_r], axis=1)
        k_d, v_d = _mla_expand(ckv_all, kpe_all, lw["w_k"], lw["w_v"], lp["e_place"], (N + P) // 2)
        o_d = _dense_attn(q_d, (512, 0), k_d, (512, 0), v_d, (512, 0), groups=GROUPS_D,
                          tq=LAT_TQ, tk=(N + P) // 2, name="mla_attn")
    else:
        o_a = _dense_attn(qkv_a, (256, 0), qkv_a, (256, 1), qkv_a, (512, 1), groups=GROUPS_A,
                          tq=N, tk=N, diff=diff, name="diff_attn_ctx")
        o_b = _dense_attn(qkv_b, (256, 0), qkv_b, (256, 1), qkv_b, (512, 1), groups=GROUPS_B,
                          tq=N, tk=N, name="dense_attn_ctx")
        o_c = _dense_attn(qkv_c, (256, 0), qkv_c, (128, 4), qkv_c, (256, 1), groups=GROUPS_C,
                          tq=N, tk=N, sink=lp["sw_sink"], name="gqa_attn_ctx")
        k_d, v_d = _mla_expand(ckv_n, kpe_r, lw["w_k"], lw["w_v"], lp["e_place"], N)
        o_d = _dense_attn(q_d, (512, 0), k_d, (512, 0), v_d, (512, 0), groups=GROUPS_D,
                          tq=N, tk=N, name="mla_attn_ctx")
    x_new, h2, logits = _merge(x, mod, lp["g_norm1"], lp["g_norm2"], o_a, o_b, o_c, o_d,
                               lw["w_gate"], lw["w_branch"], lw["w_out"], lw["w_router"], tm)
    cache = None
    if not latent:
        strip = lambda v, heads: v.reshape(B, N, heads, LANES)[..., :64]
        cache = (qkv_a[:, :, 256:512].reshape(B, N, 4, 64), strip(qkv_a[:, :, 512:1024], 4),
                 qkv_b[:, :, 256:512].reshape(B, N, 4, 64), strip(qkv_b[:, :, 512:1024], 4),
                 qkv_c[:, :, 512:640].reshape(B, N, 2, 64), strip(qkv_c[:, :, 256:512], 2),
                 ckv_n, kpe_r[:, :, :MLA_ROPE])
    return x_new, h2, logits, cache


def kernel(x_prompt, x_sample, cache_diff_k, cache_diff_v, cache_na_k, cache_na_v, cache_swa_k, cache_swa_v, cache_mla_ckv, cache_mla_kpe, c, c_ctx, w_mod, b_mod, g_norm1, g_norm2, w_in, da_lambda, da_subln, na_rpb, sw_sink, mla_q_norm, mla_w_uq, mla_kv_norm, mla_w_ukv, w_gate, w_branch, w_out, w_router, w_e_gate, w_e_up, w_e_down, g_final):
    D = D_MODEL
    BS, NS, _ = x_sample.shape
    P = cache_diff_k.shape[2]
    cond = jnp.concatenate([c_ctx[None], c, jnp.zeros((16 - 1 - BS, D), F32)], axis=0)
    mod_all = _modulation(cond, w_mod, b_mod).reshape(DEPTH, 16, 6, D)
    tables = _rope_tables(NS)
    e_place = _kpe_placement()
    gf = g_final.reshape(1, D)
    na_bias = _nbr_bias_tables(na_rpb, NS // GRID_W)
    xp, xs = x_prompt, x_sample
    caches = []
    for l in range(DEPTH):
        lw = _prep_layer(l, w_in, mla_w_uq, mla_w_ukv, w_gate, w_branch, w_out, w_router)
        lp = dict(
            g_norm1=g_norm1[l].reshape(1, D), g_norm2=g_norm2[l].reshape(1, D),
            mla_q_norm=mla_q_norm[l].reshape(1, 256), mla_kv_norm=mla_kv_norm[l].reshape(1, 128),
            da_lambda=da_lambda[l], da_subln=da_subln[l].reshape(1, 64), sw_sink=sw_sink[l],
            e_place=e_place, na_bias=na_bias[l])
        lam_init = 0.8 - 0.6 * math.exp(-0.3 * l)
        final = l == DEPTH - 1
        mod_p, mod_s = mod_all[l, 0:1], mod_all[l, 1:1 + BS]
        xp, h2_p, lg_p, cache_l = _mixing(xp, mod_p, lw, lp, lam_init, None, None)
        caches.append(cache_l)
        ctx = (cache_diff_k[:, l].reshape(BS, P, 256), cache_diff_v[:, l].reshape(BS, P, 256),
               cache_na_k[:, l].reshape(BS, P, 256), cache_na_v[:, l].reshape(BS, P, 256),
               cache_swa_k[:, l].reshape(BS, P, 128), cache_swa_v[:, l].reshape(BS, P, 128),
               cache_mla_ckv[:, l], cache_mla_kpe[:, l])
        xs, h2_s, lg_s, _ = _mixing(xs, mod_s, lw, lp, lam_init, ctx, tables)
        xs, xp = _moe_pair([(xs, h2_s, lg_s, mod_s), (xp, h2_p, lg_p, mod_p)], gf, w_e_gate, w_e_up, w_e_down,
                           l, final)
    outs = tuple(jnp.stack([cl[i] for cl in caches], axis=1) for i in range(8))
    return (xp, xs) + outs
```

```python
import functools
import math

import numpy as np
import jax
import jax.numpy as jnp
from jax import lax
from jax.experimental import pallas as pl
from jax.experimental.pallas import tpu as pltpu

BF = jnp.bfloat16
F32 = jnp.float32

D_MODEL = 1024
DEPTH = 4
GRID_W = 64
ROPE_BASE = 10000.0
NORM_EPS = 1e-6
NEG_INF = -1e30
LOG2E = math.log2(math.e)

DA_QK = 32
NA_WIN_R = 8
NA_WIN_C = 16
SW_WINDOW = 128
MLA_NOPE = 64
MLA_ROPE = 32
N_EXPERTS = 16
EC_CAPACITY = 2

IN_PAD = 3200
LANES = 128

VMEM_BIG = 56 * 1024 * 1024
VMEM_MID = 40 * 1024 * 1024


def _cparams(sem, vmem=None):
    return pltpu.CompilerParams(dimension_semantics=sem, vmem_limit_bytes=vmem)


def _dot(a, b):
    return jnp.dot(a, b, preferred_element_type=F32)


def _dot_nt(a, b):
    return lax.dot_general(a, b, (((1,), (1,)), ((), ())), preferred_element_type=F32)


def _rms(x, g):
    var = jnp.mean(x * x, axis=-1, keepdims=True)
    return x * lax.rsqrt(var + NORM_EPS) * g


def _lane_mask(width, lo, length):
    lane = lax.broadcasted_iota(jnp.int32, (1, width), 1)
    return (lane >= lo) & (lane < lo + length)


def _mod_kernel(c_ref, w_ref, b_ref, o_ref):
    c = c_ref[...]
    s = c * (1.0 / (1.0 + jnp.exp(-c)))
    o_ref[0] = jnp.dot(s, w_ref[0], preferred_element_type=F32, precision=lax.Precision.HIGHEST) + b_ref[0]


def _modulation(cond, w_mod, b_mod):
    R, D = cond.shape
    L = w_mod.shape[0]
    nj = w_mod.shape[2] // D
    return pl.pallas_call(
        _mod_kernel,
        grid=(L, nj),
        in_specs=[
            pl.BlockSpec((R, D), lambda l, j: (0, 0)),
            pl.BlockSpec((1, D, D), lambda l, j: (l, 0, j)),
            pl.BlockSpec((1, 1, D), lambda l, j: (l, 0, j)),
        ],
        out_specs=pl.BlockSpec((1, R, D), lambda l, j: (l, 0, j)),
        out_shape=jax.ShapeDtypeStruct((L, R, nj * D), F32),
        compiler_params=_cparams(("parallel", "parallel")),
        name="modulation",
    )(cond, w_mod, b_mod.reshape(L, 1, nj * D))


def _rope(x, c, s, d):
    w = x.shape[-1]
    lane = lax.broadcasted_iota(jnp.int32, (1, LANES), 1)
    first = (lane % (2 * d)) < d
    outs = []
    for b in range(w // LANES):
        sl = slice(b * LANES, (b + 1) * LANES)
        xb = x[:, sl]
        partner = jnp.where(first, pltpu.roll(xb, LANES - d, 1), pltpu.roll(xb, d, 1))
        outs.append(xb * c[:, sl] + partner * s[:, sl])
    return outs[0] if len(outs) == 1 else jnp.concatenate(outs, axis=-1)


def _inproj_kernel(*refs, rope):
    if rope:
        (x_ref, mod_ref, g1_ref, w_ref, qn_ref, wuq_ref, kvn_ref,
         tac, tas, tcc, tcs, tqc, tqs, tkc, tks,
         oa, ob, oc, oq, ockv, okpe) = refs
    else:
        (x_ref, mod_ref, g1_ref, w_ref, qn_ref, wuq_ref, kvn_ref,
         oa, ob, oc, oq, ockv, okpe) = refs
    dt = oa.dtype
    x = x_ref[0]
    h = (_rms(x, g1_ref[...]) * (1.0 + mod_ref[0, 1:2, :]) + mod_ref[0, 0:1, :]).astype(BF)

    def seg(lo, w):
        return _dot(h, w_ref[:, lo:lo + w])

    ua = seg(0, 1024)
    qk = ua[:, :512]
    if rope:
        qk = _rope(qk, tac[...], tas[...], 8)
    oa[0, :, 0:256] = (qk[:, :256] * (DA_QK ** -0.5 * LOG2E)).astype(dt)
    oa[0, :, 256:512] = qk[:, 256:].astype(dt)
    ones_lane = lax.broadcasted_iota(jnp.int32, (1, 512), 1) % LANES >= 64
    oa[0, :, 512:1024] = jnp.where(ones_lane, 1.0, ua[:, 512:]).astype(dt)
    ub = seg(1024, 1024)
    ob[0, :, 0:256] = (ub[:, :256] * (0.125 * LOG2E)).astype(dt)
    ob[0, :, 256:512] = ub[:, 256:512].astype(dt)
    ob[0, :, 512:1024] = jnp.where(ones_lane, 1.0, ub[:, 512:]).astype(dt)
    uc = seg(2048, 640)
    qk = jnp.concatenate([uc[:, :256], uc[:, 512:]], axis=-1)
    if rope:
        qk = _rope(qk, tcc[...], tcs[...], 16)
    oc[0, :, 0:256] = (qk[:, :256] * (0.125 * LOG2E)).astype(dt)
    oc[0, :, 256:512] = jnp.where(ones_lane[:, :256], 1.0, uc[:, 256:512]).astype(dt)
    oc[0, :, 512:640] = qk[:, 256:].astype(dt)
    cq = _rms(seg(2688, 256), qn_ref[...]).astype(BF)
    qd = _dot(cq, wuq_ref[...])
    if rope:
        qd = _rope(qd, tqc[...], tqs[...], 8)
    oq[0] = (qd * ((MLA_NOPE + MLA_ROPE) ** -0.5 * LOG2E)).astype(dt)
    ockv[0] = _rms(seg(2944, 128), kvn_ref[...]).astype(dt)
    kpe = seg(3072, 128)
    if rope:
        kpe = _rope(kpe, tkc[...], tks[...], 8)
    okpe[0] = kpe.astype(dt)


def _in_proj(x, mod, g1, w_in_p, q_norm, w_uq_p, kv_norm, tables, out_dtype, tm):
    B, N, D = x.shape
    rope = tables is not None
    nmod = mod.shape[0]
    mod_map = (lambda j, b: (b, 0, 0)) if nmod > 1 else (lambda j, b: (0, 0, 0))
    const2 = lambda j, b: (0, 0)
    in_specs = [
        pl.BlockSpec((1, tm, D), lambda j, b: (b, j, 0)),
        pl.BlockSpec((1, 6, D), mod_map),
        pl.BlockSpec((1, D), const2),
        pl.BlockSpec((D, IN_PAD), const2),
        pl.BlockSpec((1, 256), const2),
        pl.BlockSpec((256, 512), const2),
        pl.BlockSpec((1, 128), const2),
    ]
    args = [x, mod, g1, w_in_p, q_norm, w_uq_p, kv_norm]
    if rope:
        for t in tables:
            in_specs.append(pl.BlockSpec((tm, t.shape[1]), lambda j, b: (j, 0)))
            args.append(t)
    widths = (1024, 1024, 640, 512, 128, 128)
    out_specs = [pl.BlockSpec((1, tm, w), lambda j, b: (b, j, 0)) for w in widths]
    out_shape = [jax.ShapeDtypeStruct((B, N, w), out_dtype) for w in widths]
    return pl.pallas_call(
        functools.partial(_inproj_kernel, rope=rope),
        grid=(N // tm, B),
        in_specs=in_specs,
        out_specs=out_specs,
        out_shape=out_shape,
        compiler_params=_cparams(("parallel", "parallel"), VMEM_BIG),
        name="in_proj_rope" if rope else "in_proj",
    )(*args)


def _mla_expand_kernel(ckv_ref, kpe_ref, wk_ref, wv_ref, e_ref, ok, ov):
    ckv = ckv_ref[0].astype(BF)
    kpe = kpe_ref[0].astype(BF)
    ok[0] = (_dot(ckv, wk_ref[...]) + _dot(kpe, e_ref[...])).astype(ok.dtype)
    ones_lane = lax.broadcasted_iota(jnp.int32, (1, 512), 1) % LANES >= 64
    ov[0] = jnp.where(ones_lane, 1.0, _dot(ckv, wv_ref[...])).astype(ov.dtype)


def _mla_expand(ckv, kpe, w_k, w_v, e_place, tk):
    B, K, _ = ckv.shape
    const2 = lambda b, j: (0, 0)
    return pl.pallas_call(
        _mla_expand_kernel,
        grid=(B, K // tk),
        in_specs=[
            pl.BlockSpec((1, tk, 128), lambda b, j: (b, j, 0)),
            pl.BlockSpec((1, tk, 128), lambda b, j: (b, j, 0)),
            pl.BlockSpec((128, 512), const2),
            pl.BlockSpec((128, 512), const2),
            pl.BlockSpec((128, 512), const2),
        ],
        out_specs=[pl.BlockSpec((1, tk, 512), lambda b, j: (b, j, 0)),
                   pl.BlockSpec((1, tk, 512), lambda b, j: (b, j, 0))],
        out_shape=[jax.ShapeDtypeStruct((B, K, 512), BF), jax.ShapeDtypeStruct((B, K, 512), BF)],
        compiler_params=_cparams(("parallel", "parallel")),
        name="mla_expand",
    )(ckv, kpe, w_k, w_v, e_place)


def _head_softmax(scores, values):
    m = functools.reduce(jnp.maximum, [jnp.max(s, axis=-1, keepdims=True) for s in scores])
    acc = None
    for s, v in zip(scores, values):
        part = _dot(jnp.exp2(s - m).astype(BF), v)
        acc = part if acc is None else acc + part
    return acc, m


def _normalise(acc):
    return acc * pltpu.roll(1.0 / acc, 64, 1)


def _store_heads(o_ref, heads):
    half = _lane_mask(LANES, 0, 64)
    for hp in range(len(heads) // 2):
        slab = jnp.where(half, heads[2 * hp], pltpu.roll(heads[2 * hp + 1], 64, 1))
        o_ref[0, :, hp * LANES:(hp + 1) * LANES] = slab.astype(o_ref.dtype)


def _dense_attn_kernel(*refs, groups, tk, n_chunks, has_sink, diff):
    it = iter(refs)
    q_ref, k_ref, v_ref = next(it), next(it), next(it)
    sink_ref = lam_ref = subln_ref = None
    if has_sink:
        sink_ref = next(it)
    if diff:
        lam_ref, subln_ref, lam_init_ref = next(it), next(it), next(it)
    o_ref = next(it)
    tq = q_ref.shape[1]

    lam = lam_init = None
    if diff:
        lam_init = lam_init_ref[0]
        dl = lam_ref[...]
        lam = (jnp.exp(jnp.sum(dl[0:1] * dl[1:2], axis=-1, keepdims=True))
               - jnp.exp(jnp.sum(dl[2:3] * dl[3:4], axis=-1, keepdims=True)) + lam_init)

    ng = len(groups)
    scratch = list(it)
    qg_s, m_s, acc_s = (scratch[j * ng:(j + 1) * ng] for j in range(3))
    for gi, (q_lo, k_lo, w, qmask, v_lo, head, sink_idx, sign) in enumerate(groups):
        qg = q_ref[0, :, q_lo:q_lo + w].astype(BF)
        if qmask is not None:
            qg = jnp.where(_lane_mask(w, *qmask), qg, jnp.zeros_like(qg))
        qg_s[gi][...] = qg
        m_s[gi][...] = jnp.full((tq, 1), NEG_INF, F32)
        acc_s[gi][...] = jnp.zeros((tq, LANES), F32)

    def keys_step(c):
        off = c * tk
        for gi, (q_lo, k_lo, w, qmask, v_lo, head, sink_idx, sign) in enumerate(groups):
            s = _dot_nt(qg_s[gi][...], k_ref[0, pl.ds(off, tk), k_lo:k_lo + w].astype(BF))
            m = m_s[gi][...]
            m_new = jnp.maximum(m, jnp.max(s, axis=-1, keepdims=True))
            p = jnp.exp2(s - m_new).astype(BF)
            pv = _dot(p, v_ref[0, pl.ds(off, tk), v_lo:v_lo + LANES].astype(BF))
            acc_s[gi][...] = jnp.exp2(m - m_new) * acc_s[gi][...] + pv
            m_s[gi][...] = m_new

    for c in range(n_chunks):
        keys_step(c)

    heads = {}
    ones_half = _lane_mask(LANES, 64, 64)
    for gi, (q_lo, k_lo, w, qmask, v_lo, head, sink_idx, sign) in enumerate(groups):
        acc = acc_s[gi][...]
        if sink_idx is not None:
            acc = acc + jnp.where(ones_half, jnp.exp2(sink_ref[sink_idx] * LOG2E - m_s[gi][...]), 0.0)
        part = _normalise(acc)
        if sign < 0:
            part = -lam * part
        heads[head] = part if head not in heads else heads[head] + part
    if diff:
        half = _lane_mask(LANES, 0, 64)
        for hd, o in heads.items():
            ms = jnp.sum(jnp.where(half, o * o, 0.0), axis=-1, keepdims=True) * (1.0 / 64.0)
            heads[hd] = o * lax.rsqrt(ms + NORM_EPS) * subln_ref[...] * (1.0 - lam_init)
    _store_heads(o_ref, heads)


def _dense_attn(q_arr, q_blk, k_arr, k_blk, v_arr, v_blk, *, groups, tq, tk, sink=None, diff=None, name):
    B, N, _ = q_arr.shape
    K = k_arr.shape[1]
    n_chunks = K // tk
    assert n_chunks * tk == K and n_chunks <= 2
    qw, qi = q_blk
    kw, ki = k_blk
    vw, vi = v_blk
    in_specs = [
        pl.BlockSpec((1, tq, qw), lambda b, i: (b, i, qi)),
        pl.BlockSpec((1, K, kw), lambda b, i: (b, 0, ki)),
        pl.BlockSpec((1, K, vw), lambda b, i: (b, 0, vi)),
    ]
    args = [q_arr, k_arr, v_arr]
    if sink is not None:
        in_specs.append(pl.BlockSpec(memory_space=pltpu.SMEM))
        args.append(sink)
    if diff is not None:
        da_lambda, subln, lam_init = diff
        in_specs += [pl.BlockSpec((4, DA_QK), lambda b, i: (0, 0)),
                     pl.BlockSpec((1, LANES), lambda b, i: (0, 0)),
                     pl.BlockSpec(memory_space=pltpu.SMEM)]
        args += [da_lambda, subln, jnp.full((1,), lam_init, F32)]
    kern = functools.partial(_dense_attn_kernel, groups=groups, tk=tk, n_chunks=n_chunks,
                             has_sink=sink is not None, diff=diff is not None)
    scratch = ([pltpu.VMEM((tq, g[2]), BF) for g in groups] + [pltpu.VMEM((tq, 1), F32) for _ in groups]
               + [pltpu.VMEM((tq, LANES), F32) for _ in groups])
    return pl.pallas_call(
        kern,
        grid=(B, N // tq),
        in_specs=in_specs,
        out_specs=pl.BlockSpec((1, tq, 256), lambda b, i: (b, i, 0)),
        out_shape=jax.ShapeDtypeStruct((B, N, 256), BF),
        scratch_shapes=scratch,
        compiler_params=_cparams(("parallel", "parallel"), VMEM_BIG),
        name=name,
    )(*args)


GROUPS_A = tuple((128 * (g // 4), 128 * (g // 4), 128, (32 * (g % 4), 32), 128 * (g // 2), g // 2, None,
                  1 if g % 2 == 0 else -1) for g in range(8))
GROUPS_B = tuple((128 * (h // 2), 128 * (h // 2), 128, (64 * (h % 2), 64), 128 * h, h, None, 1) for h in range(4))
GROUPS_C = tuple((128 * g, 0, 128, (64 * j, 64), 128 * j, 2 * g + j, 2 * j + g, 1) for g in range(2) for j in range(2))
GROUPS_D = tuple((128 * h, 128 * h, 128, None, 128 * h, h, None, 1) for h in range(4))


def _window_attn_kernel(q_ref, k_ref, v_ref, ck_ref, cv_ref, sink_ref, o_ref, *, n, band):
    tq = q_ref.shape[1]
    i = pl.program_id(1)
    start = jnp.clip(i * tq - SW_WINDOW, 0, n - band)
    start = pl.multiple_of(start, 128)
    kb = k_ref[0, pl.ds(start, band), :]
    vb = v_ref[0, pl.ds(start, band), :]
    ck = ck_ref[0]
    cv = cv_ref[0]
    qpos = i * tq + lax.broadcasted_iota(jnp.int32, (tq, band), 0)
    kpos = start + lax.broadcasted_iota(jnp.int32, (tq, band), 1)
    valid = jnp.abs(qpos - kpos) <= SW_WINDOW
    ones_half = _lane_mask(LANES, 64, 64)
    heads = {}
    for g in range(2):
        for j in range(2):
            qg = q_ref[0, :, LANES * g:LANES * (g + 1)]
            qg = jnp.where(_lane_mask(LANES, 64 * j, 64), qg, jnp.zeros_like(qg))
            vs = slice(LANES * j, LANES * (j + 1))
            acc, m = _head_softmax([_dot_nt(qg, ck), jnp.where(valid, _dot_nt(qg, kb), NEG_INF)],
                                   [cv[:, vs], vb[:, vs]])
            acc = acc + jnp.where(ones_half, jnp.exp2(sink_ref[2 * j + g] * LOG2E - m), 0.0)
            heads[2 * g + j] = _normalise(acc)
    _store_heads(o_ref, heads)


def _window_attn(qkv_c, ck, cv, sink, tq):
    B, N, _ = qkv_c.shape
    band = tq + 2 * SW_WINDOW
    P = ck.shape[1]
    return pl.pallas_call(
        functools.partial(_window_attn_kernel, n=N, band=band),
        grid=(B, N // tq),
        in_specs=[
            pl.BlockSpec((1, tq, 256), lambda b, i: (b, i, 0)),
            pl.BlockSpec((1, N, 128), lambda b, i: (b, 0, 4)),
            pl.BlockSpec((1, N, 256), lambda b, i: (b, 0, 1)),
            pl.BlockSpec((1, P, 128), lambda b, i: (b, 0, 0)),
            pl.BlockSpec((1, P, 256), lambda b, i: (b, 0, 0)),
            pl.BlockSpec(memory_space=pltpu.SMEM),
        ],
        out_specs=pl.BlockSpec((1, tq, 256), lambda b, i: (b, i, 0)),
        out_shape=jax.ShapeDtypeStruct((B, N, 256), BF),
        compiler_params=_cparams(("parallel", "parallel")),
        name="window_attn",
    )(qkv_c, qkv_c, qkv_c, ck, cv, sink)


NA_TILE_ROWS = 4
NA_KEY_ROWS = NA_TILE_ROWS + NA_WIN_R


def _nbr_attn_kernel(q_ref, k_ref, v_ref, ck_ref, cv_ref, bias_ref, o_ref, *, rows):
    i = pl.program_id(1)
    krow = jnp.clip(i * NA_TILE_ROWS - NA_WIN_R // 2, 0, rows - NA_KEY_ROWS)
    start = pl.multiple_of(krow * GRID_W, GRID_W)
    nk = NA_KEY_ROWS * GRID_W
    kb = k_ref[0, pl.ds(start, nk), :]
    vb = v_ref[0, pl.ds(start, nk), :]
    ck = ck_ref[0]
    cv = cv_ref[0]
    q = q_ref[0]
    heads = {}
    for h in range(4):
        ks = slice(LANES * (h // 2), LANES * (h // 2 + 1))
        qg = jnp.where(_lane_mask(LANES, 64 * (h % 2), 64), q[:, ks], jnp.zeros_like(q[:, ks]))
        vs = slice(LANES * h, LANES * (h + 1))
        acc, _ = _head_softmax([_dot_nt(qg, ck[:, ks]), _dot_nt(qg, kb[:, ks]) + bias_ref[0, h]],
                               [cv[:, vs], vb[:, vs]])
        heads[h] = _normalise(acc)
    _store_heads(o_ref, heads)


def _nbr_attn(qkv_b, ck, cv, bias):
    B, N, _ = qkv_b.shape
    rows = N // GRID_W
    tq = NA_TILE_ROWS * GRID_W
    nt = N // tq
    nk = NA_KEY_ROWS * GRID_W
    P = ck.shape[1]

    def bias_map(b, i):
        return (jnp.where(i == 0, 0, jnp.where(i == nt - 1, 2, 1)), 0, 0, 0)

    return pl.pallas_call(
        functools.partial(_nbr_attn_kernel, rows=rows),
        grid=(B, nt),
        in_specs=[
            pl.BlockSpec((1, tq, 256), lambda b, i: (b, i, 0)),
            pl.BlockSpec((1, N, 256), lambda b, i: (b, 0, 1)),
            pl.BlockSpec((1, N, 512), lambda b, i: (b, 0, 1)),
            pl.BlockSpec((1, P, 256), lambda b, i: (b, 0, 0)),
            pl.BlockSpec((1, P, 512), lambda b, i: (b, 0, 0)),
            pl.BlockSpec((1, 4, tq, nk), bias_map),
        ],
        out_specs=pl.BlockSpec((1, tq, 256), lambda b, i: (b, i, 0)),
        out_shape=jax.ShapeDtypeStruct((B, N, 256), BF),
        compiler_params=_cparams(("parallel", "parallel"), VMEM_MID),
        name="nbr_attn",
    )(qkv_b, qkv_b, qkv_b, ck, cv, bias)


def _nbr_bias_tables(rpb, rows):
    L, H = rpb.shape[:2]
    rpb = rpb.reshape((L * H,) + rpb.shape[2:])
    rp = jnp.pad(rpb.astype(F32), ((0, 0), (NA_KEY_ROWS, NA_KEY_ROWS), (GRID_W, GRID_W)))
    c0 = NA_WIN_C - 1 + GRID_W
    tcol = jnp.stack([rp[:, :, c0 - qc:c0 - qc + GRID_W] for qc in range(GRID_W)], axis=2)
    tabs = []
    nt = rows // NA_TILE_ROWS
    for r0 in (0, NA_TILE_ROWS * min(1, nt - 1) + NA_TILE_ROWS, rows - NA_TILE_ROWS):
        ks = int(np.clip(r0 - NA_WIN_R // 2, 0, rows - NA_KEY_ROWS))
        per_row = []
        for a in range(NA_TILE_ROWS):
            s0 = ks - (r0 + a) + NA_WIN_R - 1 + NA_KEY_ROWS
            per_row.append(tcol[:, s0:s0 + NA_KEY_ROWS])
        t = jnp.stack(per_row, axis=1).transpose(0, 1, 3, 2, 4)
        r = r0 + np.arange(NA_TILE_ROWS)[:, None, None, None]
        qc = np.arange(GRID_W)[None, :, None, None]
        kr = ks + np.arange(NA_KEY_ROWS)[None, None, :, None]
        kc = np.arange(GRID_W)[None, None, None, :]
        rs = np.clip(r - NA_WIN_R // 2, 0, rows - NA_WIN_R)
        cs = np.clip(qc - NA_WIN_C // 2, 0, GRID_W - NA_WIN_C)
        ok = (kr >= rs) & (kr < rs + NA_WIN_R) & (kc >= cs) & (kc < cs + NA_WIN_C)
        t = jnp.where(ok[None], t * LOG2E, NEG_INF)
        tabs.append(t.reshape(L, H, NA_TILE_ROWS * GRID_W, NA_KEY_ROWS * GRID_W))
    return jnp.stack(tabs, axis=1)


def _merge_kernel(x_ref, mod_ref, g1_ref, g2_ref, oa, ob, oc, od, wg_ref, wb_ref, wo_ref, wr_ref,
                  xo_ref, h2_ref, lg_ref):
    D = x_ref.shape[2]
    x = x_ref[0]
    h = (_rms(x, g1_ref[...]) * (1.0 + mod_ref[0, 1:2, :]) + mod_ref[0, 0:1, :]).astype(BF)
    merged = None
    for i, o_ref in enumerate((oa, ob, oc, od)):
        z = _dot(h, wg_ref[:, i * D:(i + 1) * D])
        gate = 1.0 / (1.0 + jnp.exp(-z))
        term = gate * _dot(o_ref[0], wb_ref[i])
        merged = term if merged is None else merged + term
    mixed = _dot(merged.astype(BF), wo_ref[...])
    xn = x + mod_ref[0, 2:3, :] * mixed
    xo_ref[0] = xn
    h2 = (_rms(xn, g2_ref[...]) * (1.0 + mod_ref[0, 4:5, :]) + mod_ref[0, 3:4, :]).astype(BF)
    h2_ref[0] = h2
    lg_ref[0] = _dot_nt(wr_ref[...], h2)


def _merge(x, mod, g1, g2, o_a, o_b, o_c, o_d, w_gate, w_branch, w_out, w_router_p, tm):
    B, N, D = x.shape
    nmod = mod.shape[0]
    mod_map = (lambda b, j: (b, 0, 0)) if nmod > 1 else (lambda b, j: (0, 0, 0))
    tok = lambda w: pl.BlockSpec((1, tm, w), lambda b, j: (b, j, 0))
    c2 = lambda b, j: (0, 0)
    return pl.pallas_call(
        _merge_kernel,
        grid=(B, N // tm),
        in_specs=[
            tok(D), pl.BlockSpec((1, 6, D), mod_map), pl.BlockSpec((1, D), c2), pl.BlockSpec((1, D), c2),
            tok(256), tok(256), tok(256), tok(256),
            pl.BlockSpec((D, 4 * D), c2),
            pl.BlockSpec((4, 256, D), lambda b, j: (0, 0, 0)),
            pl.BlockSpec((D, D), c2),
            pl.BlockSpec((N_EXPERTS, D), c2),
        ],
        out_specs=[tok(D), tok(D), pl.BlockSpec((1, N_EXPERTS, tm), lambda b, j: (b, 0, j))],
        out_shape=[jax.ShapeDtypeStruct((B, N, D), F32), jax.ShapeDtypeStruct((B, N, D), BF),
                   jax.ShapeDtypeStruct((B, N_EXPERTS, N), F32)],
        compiler_params=_cparams(("parallel", "parallel"), VMEM_BIG),
        name="merge",
    )(x, mod, g1, g2, o_a, o_b, o_c, o_d, w_gate, w_branch, w_out, w_router_p)


MOE_TT = 256
MOE_RC = 48
SLOT_ALIGN = 16


def _excl_cumsum(x, tri, reset_every):
    R, N = x.shape
    outs = []
    carry = jnp.zeros((R, 1), F32)
    for c in range(N // LANES):
        if reset_every and (c * LANES) % reset_every == 0:
            carry = jnp.zeros((R, 1), F32)
        xc = x[:, c * LANES:(c + 1) * LANES]
        inc = _dot(xc.astype(BF), tri)
        outs.append(inc - xc + carry)
        carry = carry + inc[:, LANES - 1:LANES]
    return (outs[0] if len(outs) == 1 else jnp.concatenate(outs, axis=1)), carry


def _route_kernel(lg_ref, aff_ref, rel_ref, base_ref, *, cap, tt):
    B, E, N = lg_ref.shape
    R = B * E
    nt = N // tt
    lg = lg_ref[...]
    ex = jnp.exp(lg - jnp.max(lg, axis=1, keepdims=True))
    aff = ex / jnp.sum(ex, axis=1, keepdims=True)
    aff_ref[...] = aff
    bits = pltpu.bitcast(aff.reshape(R, N), jnp.int32)

    def per_expert(v):
        t = v[0:E]
        for b in range(1, B):
            t = t + v[b * E:(b + 1) * E]
        return t

    def per_row(v):
        return jnp.concatenate([v] * B, axis=0) if B > 1 else v

    def batch_prefix(v):
        parts, run = [], jnp.zeros((E, 1), F32)
        for b in range(B):
            parts.append(run)
            run = run + v[b * E:(b + 1) * E]
        return jnp.concatenate(parts, axis=0) if B > 1 else parts[0]

    def count(mask):
        return per_expert(jnp.sum(jnp.where(mask, 1.0, 0.0), axis=1, keepdims=True))

    def search(i, thr):
        cand = thr | jnp.left_shift(jnp.int32(1), 30 - i)
        return jnp.where(count(bits >= per_row(cand)) >= float(cap), cand, thr)

    thr = per_row(lax.fori_loop(0, 31, search, jnp.zeros((E, 1), jnp.int32)))
    gt = bits > thr
    eq = bits == thr
    need = float(cap) - count(gt)
    ri = lax.broadcasted_iota(jnp.int32, (LANES, LANES), 0)
    ci = lax.broadcasted_iota(jnp.int32, (LANES, LANES), 1)
    tri = jnp.where(ri <= ci, 1.0, 0.0).astype(BF)
    rank_eq, eq_tot = _excl_cumsum(jnp.where(eq, 1.0, 0.0), tri, None)
    rank_eq = rank_eq + batch_prefix(eq_tot)
    sel = gt | (eq & (rank_eq < per_row(need)))
    self = jnp.where(sel, 1.0, 0.0)
    local, _ = _excl_cumsum(self, tri, tt)
    rel_ref[...] = jnp.where(sel, local, -1.0).astype(jnp.int32).reshape(B, E, N)
    tn = jnp.right_shift(lax.broadcasted_iota(jnp.int32, (N, LANES), 0), tt.bit_length() - 1)
    tj = lax.broadcasted_iota(jnp.int32, (N, LANES), 1)
    cnt = _dot(self.astype(BF), jnp.where(tn == tj, 1.0, 0.0).astype(BF)).astype(jnp.int32)
    sh = SLOT_ALIGN.bit_length() - 1
    ru = jnp.left_shift(jnp.right_shift(cnt + (SLOT_ALIGN - 1), sh), sh).astype(F32)
    base_b = _dot(ru.astype(BF), jnp.where(ri < ci, 1.0, 0.0).astype(BF))
    base = base_b + batch_prefix(base_b[:, nt:nt + 1])
    base_ref[...] = base.astype(jnp.int32).reshape(B, E, LANES)


def _route(lg_t, cap):
    B, E, N = lg_t.shape
    full = lambda s: pl.BlockSpec(s, lambda i: (0,) * len(s))
    return pl.pallas_call(
        functools.partial(_route_kernel, cap=cap, tt=MOE_TT),
        grid=(1,),
        in_specs=[full((B, E, N))],
        out_specs=[full((B, E, N)), full((B, E, N)), full((B, E, LANES))],
        out_shape=[jax.ShapeDtypeStruct((B, E, N), F32), jax.ShapeDtypeStruct((B, E, N), jnp.int32),
                   jax.ShapeDtypeStruct((B, E, LANES), jnp.int32)],
        compiler_params=_cparams(("arbitrary",), VMEM_BIG),
        name="route",
    )(lg_t)


def _slot_onehot(rel, values, c, rc):
    E, tt = rel.shape
    r = lax.broadcasted_iota(jnp.int32, (E, rc, tt), 1) + c * rc
    return jnp.where(rel[:, None, :] == r, values[:, None, :], 0.0).astype(BF).reshape(E * rc, tt)


def _dispatch_kernel(base_sm, x_ref, rel_ref, xe_in, xe_hbm, stage, stage2, sem, sem2, *, nt, rc, row_off):
    del xe_in
    b, j = pl.program_id(0), pl.program_id(1)
    E = rel_ref.shape[1]
    step = b * nt + j
    last = pl.num_programs(0) * nt - 1
    base = lambda e, jj: base_sm[(b * E + e) * (nt + 1) + jj]
    rel = rel_ref[0]
    x = x_ref[0]
    ones = jnp.ones(rel.shape, F32)

    def copy(buf, s, e, c):
        start = pl.multiple_of(row_off + base(e, j) + c * rc, SLOT_ALIGN)
        return pltpu.make_async_copy(buf.at[pl.ds(e * rc, rc)], xe_hbm.at[e, pl.ds(start, rc)], s.at[e])

    strip = _dot(_slot_onehot(rel, ones, 0, rc), x).astype(BF)

    @pl.when(step > 0)
    def _():
        for e in range(E):
            copy(stage, sem, e, 0).wait()

    stage[...] = strip
    for e in range(E):
        copy(stage, sem, e, 0).start()

    span = [base(e, j + 1) - base(e, j) for e in range(E)]
    widest = functools.reduce(jnp.maximum, span)

    def extra(c, carry):
        stage2[...] = _dot(_slot_onehot(rel, ones, c, rc), x).astype(BF)
        for e in range(E):
            @pl.when(span[e] > c * rc)
            def _():
                cp = copy(stage2, sem2, e, c)
                cp.start()
                cp.wait()
        return carry

    lax.fori_loop(1, (widest + rc - 1) // rc, extra, 0)

    @pl.when(step == last)
    def _():
        for e in range(E):
            copy(stage, sem, e, 0).wait()


def _dispatch(h2, rel, base_flat, xe, row_off):
    B, N, D = h2.shape
    E, rows, _ = xe.shape
    nt = N // MOE_TT
    return pl.pallas_call(
        functools.partial(_dispatch_kernel, nt=nt, rc=MOE_RC, row_off=row_off),
        grid_spec=pltpu.PrefetchScalarGridSpec(
            num_scalar_prefetch=1,
            grid=(B, nt),
            in_specs=[
                pl.BlockSpec((1, MOE_TT, D), lambda b, j, s: (b, j, 0)),
                pl.BlockSpec((1, E, MOE_TT), lambda b, j, s: (b, 0, j)),
                pl.BlockSpec(memory_space=pl.ANY),
            ],
            out_specs=pl.BlockSpec(memory_space=pl.ANY),
            scratch_shapes=[pltpu.VMEM((E * MOE_RC, D), BF), pltpu.VMEM((E * MOE_RC, D), BF),
                            pltpu.SemaphoreType.DMA((E,)), pltpu.SemaphoreType.DMA((E,))],
        ),
        out_shape=jax.ShapeDtypeStruct((E, rows, D), BF),
        input_output_aliases={3: 0},
        compiler_params=_cparams(("arbitrary", "arbitrary")),
        name="dispatch",
    )(base_flat, h2, rel, xe)


def _ffn_kernel(tot_sm, x_ref, wg_hbm, wu_hbm, wd_hbm, y_ref, wbuf, wg_s, wu_s, wd_s, sem, *, tm, rc, layer,
                n_first):
    e, j = pl.program_id(0), pl.program_id(1)
    slot = e % 2

    def copy(ee, sl, k):
        return pltpu.make_async_copy((wg_hbm, wu_hbm, wd_hbm)[k].at[layer, ee], wbuf.at[sl, k], sem.at[sl, k])

    @pl.when(j == 0)
    def _():
        @pl.when(e == 0)
        def _():
            for k in range(3):
                copy(e, slot, k).start()

        @pl.when(e + 1 < pl.num_programs(0))
        def _():
            for k in range(3):
                copy(e + 1, 1 - slot, k).start()

        for k in range(3):
            copy(e, slot, k).wait()
        wg_s[...] = wbuf[slot, 0].astype(BF)
        wu_s[...] = wbuf[slot, 1].astype(BF)
        wd_s[...] = wbuf[slot, 2].astype(BF)

    second = j >= n_first
    jj = jnp.where(second, j - n_first, j)
    used = jj * tm < tot_sm[jnp.where(second, pl.num_programs(0) + e, e)] + rc

    @pl.when(used)
    def _():
        x = x_ref[0]
        hg = _dot(x, wg_s[...])
        hu = _dot(x, wu_s[...])
        act = (hg * (1.0 / (1.0 + jnp.exp(-hg))) * hu).astype(BF)
        y_ref[0] = _dot(act, wd_s[...]).astype(y_ref.dtype)

    @pl.when(jnp.logical_not(used))
    def _():
        y_ref[...] = jnp.zeros_like(y_ref)


def _expert_ffn(xe, totals, rows_first, w_g, w_u, w_d, layer, tm):
    E, rows, D = xe.shape
    DE = w_g.shape[3]
    n_first = rows_first // tm

    def tok(e, j, tot):
        second = j >= n_first
        last = (tot[jnp.where(second, E + e, e)] + MOE_RC - 1) // tm
        return (e, jnp.where(second, n_first + jnp.minimum(j - n_first, last), jnp.minimum(j, last)), 0)

    assert D == DE
    hbm = pl.BlockSpec(memory_space=pl.ANY)
    return pl.pallas_call(
        functools.partial(_ffn_kernel, tm=tm, rc=MOE_RC, layer=layer, n_first=n_first),
        grid_spec=pltpu.PrefetchScalarGridSpec(
            num_scalar_prefetch=1,
            grid=(E, rows // tm),
            in_specs=[pl.BlockSpec((1, tm, D), tok), hbm, hbm, hbm],
            out_specs=pl.BlockSpec((1, tm, D), lambda e, j, tot: (e, j, 0)),
            scratch_shapes=[pltpu.VMEM((2, 3, D, DE), F32), pltpu.VMEM((D, DE), BF), pltpu.VMEM((D, DE), BF),
                            pltpu.VMEM((DE, D), BF), pltpu.SemaphoreType.DMA((2, 3))],
        ),
        out_shape=jax.ShapeDtypeStruct((E, rows, D), BF),
        compiler_params=_cparams(("arbitrary", "arbitrary"), VMEM_BIG),
        name="expert_ffn",
    )(totals, xe, w_g, w_u, w_d)


def _combine_kernel(base_sm, x_ref, rel_ref, aff_ref, mod_ref, gf_ref, ye_hbm, o_ref, stage, sem, *,
                    nt, rc, final, row_off):
    b, j = pl.program_id(0), pl.program_id(1)
    E = rel_ref.shape[1]
    step = b * nt + j
    nsteps = pl.num_programs(0) * nt
    slot = step % 2
    base = lambda e, jj: base_sm[(b * E + e) * (nt + 1) + jj]
    rel = rel_ref[0]
    aff = aff_ref[0]

    def copy(st, sl, e, c):
        first = base_sm[(st // nt * E + e) * (nt + 1) + st % nt]
        start = pl.multiple_of(row_off + first + c * rc, SLOT_ALIGN)
        return pltpu.make_async_copy(ye_hbm.at[e, pl.ds(start, rc)], stage.at[sl, pl.ds(e * rc, rc)],
                                     sem.at[sl, e])

    def scatter(c):
        w = _slot_onehot(rel, aff, c, rc)
        return lax.dot_general(w, stage[slot], (((0,), (0,)), ((), ())), preferred_element_type=F32)

    @pl.when(step == 0)
    def _():
        for e in range(E):
            copy(step, slot, e, 0).start()

    @pl.when(step + 1 < nsteps)
    def _():
        for e in range(E):
            copy(step + 1, 1 - slot, e, 0).start()

    for e in range(E):
        copy(step, slot, e, 0).wait()
    acc = scatter(0)
    span = [base(e, j + 1) - base(e, j) for e in range(E)]
    widest = functools.reduce(jnp.maximum, span)

    def extra(c, acc):
        for e in range(E):
            @pl.when(span[e] > c * rc)
            def _():
                cp = copy(step, slot, e, c)
                cp.start()
                cp.wait()
        return acc + scatter(c)

    acc = lax.fori_loop(1, (widest + rc - 1) // rc, extra, acc)
    xn = x_ref[0] + mod_ref[0, 5:6, :] * acc
    if final:
        xn = _rms(xn, gf_ref[...])
    o_ref[0] = xn


def _combine(x, rel, aff, mod, g_final, ye, base_flat, row_off, final):
    B, N, D = x.shape
    E = rel.shape[1]
    nt = N // MOE_TT
    nmod = mod.shape[0]
    mod_map = (lambda b, j, s: (b, 0, 0)) if nmod > 1 else (lambda b, j, s: (0, 0, 0))
    tok = pl.BlockSpec((1, MOE_TT, D), lambda b, j, s: (b, j, 0))
    slots = pl.BlockSpec((1, E, MOE_TT), lambda b, j, s: (b, 0, j))
    return pl.pallas_call(
        functools.partial(_combine_kernel, nt=nt, rc=MOE_RC, final=final, row_off=row_off),
        grid_spec=pltpu.PrefetchScalarGridSpec(
            num_scalar_prefetch=1,
            grid=(B, nt),
            in_specs=[tok, slots, slots, pl.BlockSpec((1, 6, D), mod_map),
                      pl.BlockSpec((1, D), lambda b, j, s: (0, 0)), pl.BlockSpec(memory_space=pl.ANY)],
            out_specs=tok,
            scratch_shapes=[pltpu.VMEM((2, E * MOE_RC, D), BF), pltpu.SemaphoreType.DMA((2, E))],
        ),
        out_shape=jax.ShapeDtypeStruct((B, N, D), F32),
        compiler_params=_cparams(("arbitrary", "arbitrary")),
        name="combine_final" if final else "combine",
    )(base_flat, x, rel, aff, mod, g_final, ye)


FFN_TM = 512


def _slot_rows(B, N):
    T = B * N
    worst = EC_CAPACITY * T // N_EXPERTS + (SLOT_ALIGN - 1) * (T // MOE_TT) + MOE_RC
    return -(-worst // FFN_TM) * FFN_TM


def _moe_pair(sets, g_final, w_g, w_u, w_d, layer, final):
    D = sets[0][1].shape[2]
    rows = [_slot_rows(*st[1].shape[:2]) for st in sets]
    xe = jnp.zeros((N_EXPERTS, sum(rows), D), BF)
    routed, row_off = [], 0
    for (x, h2, lg_t, mod), r in zip(sets, rows):
        B, N, _ = h2.shape
        nt = N // MOE_TT
        aff, rel, base = _route(lg_t, EC_CAPACITY * B * N // N_EXPERTS)
        base_flat = base[:, :, :nt + 1].reshape(-1)
        xe = _dispatch(h2, rel, base_flat, xe, row_off)
        routed.append((aff, rel, base_flat, base[B - 1, :, nt], row_off))
        row_off += r
    ye = _expert_ffn(xe, jnp.concatenate([routed[0][3], routed[1][3]]), rows[0], w_g, w_u, w_d, layer, FFN_TM)
    return [_combine(x, rel, aff, mod, g_final, ye, base_flat, off, final)
            for (x, h2, lg_t, mod), (aff, rel, base_flat, _, off) in zip(sets, routed)]


def _rope_tables(n):
    t = jnp.arange(n)
    pos = {1: (t // GRID_W).astype(F32), 2: (t % GRID_W).astype(F32)}

    def group(quarter):
        freqs = jnp.float32(ROPE_BASE) ** (-jnp.arange(quarter, dtype=F32) / quarter)
        cs, ss = [], []
        for kind in (1, 2):
            ang = pos[kind][:, None] * freqs
            c, s = jnp.cos(ang), jnp.sin(ang)
            cs += [c, c]
            ss += [-s, s]
        return jnp.concatenate(cs, axis=1), jnp.concatenate(ss, axis=1)

    c32, s32 = group(8)
    c64, s64 = group(16)
    one = lambda w: jnp.ones((n, w), F32)
    zero = lambda w: jnp.zeros((n, w), F32)
    ta = (jnp.tile(c32, (1, 16)), jnp.tile(s32, (1, 16)))
    tc = (jnp.tile(c64, (1, 6)), jnp.tile(s64, (1, 6)))
    qc = jnp.concatenate([one(64), c32, one(32)], axis=1)
    qs = jnp.concatenate([zero(64), s32, zero(32)], axis=1)
    tq = (jnp.tile(qc, (1, 4)), jnp.tile(qs, (1, 4)))
    tk = (jnp.concatenate([c32, one(96)], axis=1), jnp.concatenate([s32, zero(96)], axis=1))
    return (ta[0], ta[1], tc[0], tc[1], tq[0], tq[1], tk[0], tk[1])


LAT_TQ = 512
C_HEAD_ORDER = (0, 2, 1, 3)


def _prep_layer(l, w_in, mla_w_uq, mla_w_ukv, w_gate, w_branch, w_out, w_router):
    D = D_MODEL
    wi = w_in[l]
    cq = wi[:, 1536:1792].reshape(D, 4, 64)[:, C_HEAD_ORDER, :].reshape(D, 256)
    def spread(cols, heads):
        return jnp.pad(cols.reshape(D, heads, 64), ((0, 0), (0, 0), (0, 64))).reshape(D, heads * LANES)

    w_in_p = jnp.concatenate(
        [wi[:, :512], spread(wi[:, 512:768], 4), wi[:, 768:1280], spread(wi[:, 1280:1536], 4),
         cq, spread(wi[:, 1920:2048], 2), wi[:, 1792:1920], wi[:, 2048:],
         jnp.zeros((D, LANES - MLA_ROPE), F32)], axis=1).astype(BF)
    uq = mla_w_uq[l].reshape(256, 4, MLA_NOPE + MLA_ROPE)
    w_uq_p = jnp.concatenate([uq, jnp.zeros((256, 4, 32), F32)], axis=2).reshape(256, 512).astype(BF)
    ukv = mla_w_ukv[l].reshape(128, 4, 128)
    w_k = jnp.concatenate([ukv[:, :, :64], jnp.zeros((128, 4, 64), F32)], axis=2).reshape(128, 512).astype(BF)
    w_v = jnp.concatenate([ukv[:, :, 64:], jnp.zeros((128, 4, 64), F32)], axis=2).reshape(128, 512).astype(BF)
    wb = w_branch[l]
    wb_c = wb[2].reshape(4, 64, D)[C_HEAD_ORDER, :, :].reshape(256, D)
    w_branch_p = jnp.stack([wb[0], wb[1], wb_c, wb[3]]).astype(BF)
    w_router_p = w_router[l].T.astype(BF)
    return dict(w_in=w_in_p, w_uq=w_uq_p, w_k=w_k, w_v=w_v, w_gate=w_gate[l].astype(BF),
                w_branch=w_branch_p, w_out=w_out[l].astype(BF), w_router=w_router_p)


def _kpe_placement():
    e = np.zeros((128, 512), np.float32)
    for h in range(4):
        for i in range(MLA_ROPE):
            e[i, 128 * h + MLA_NOPE + i] = 1.0
    return jnp.asarray(e, BF)


def _mixing(x, mod, lw, lp, lam_init, ctx, tables):
    B, N, D = x.shape
    latent = ctx is not None
    tm = 512 if latent else 256
    out_dtype = BF if latent else F32
    qkv_a, qkv_b, qkv_c, q_d, ckv_n, kpe_r = _in_proj(
        x, mod, lp["g_norm1"], lw["w_in"], lp["mla_q_norm"], lw["w_uq"], lp["mla_kv_norm"],
        tables if latent else None, out_dtype, tm)
    diff = (lp["da_lambda"], jnp.tile(lp["da_subln"], (1, 2)), lam_init)
    if latent:
        ck_a, cv_a, ck_b, cv_b, ck_c, cv_c, ckv_c, kpe_c = ctx
        P = ck_a.shape[1]

        def with_ones(v, heads):
            v = v.reshape(B, P, heads, 64).astype(BF)
            return jnp.concatenate([v, jnp.ones_like(v)], axis=-1).reshape(B, P, heads * LANES)

        k_a = jnp.concatenate([ck_a.astype(BF), qkv_a[:, :, 256:512]], axis=1)
        v_a = jnp.concatenate([with_ones(cv_a, 4), qkv_a[:, :, 512:1024]], axis=1)
        o_a = _dense_attn(qkv_a, (256, 0), k_a, (256, 0), v_a, (512, 0), groups=GROUPS_A,
                          tq=LAT_TQ, tk=(N + P) // 2, diff=diff, name="diff_attn")
        o_b = _nbr_attn(qkv_b, ck_b.astype(BF), with_ones(cv_b, 4), lp["na_bias"])
        o_c = _window_attn(qkv_c, ck_c.astype(BF), with_ones(cv_c, 2), lp["sw_sink"], 256)
        ckv_all = jnp.concatenate([ckv_c.astype(BF), ckv_n], axis=1)
        kpe_all = jnp.concatenate([jnp.pad(kpe_c, ((0, 0), (0, 0), (0, 128 - MLA_ROPE))).astype(BF), kpe_r], axis=1)
        k_d, v_d = _mla_expand(ckv_all, kpe_all, lw["w_k"], lw["w_v"], lp["e_place"], (N + P) // 2)
        o_d = _dense_attn(q_d, (512, 0), k_d, (512, 0), v_d, (512, 0), groups=GROUPS_D,
                          tq=LAT_TQ, tk=(N + P) // 2, name="mla_attn")
    else:
        o_a = _dense_attn(qkv_a, (256, 0), qkv_a, (256, 1), qkv_a, (512, 1), groups=GROUPS_A,
                          tq=N, tk=N, diff=diff, name="diff_attn_ctx")
        o_b = _dense_attn(qkv_b, (256, 0), qkv_b, (256, 1), qkv_b, (512, 1), groups=GROUPS_B,
                          tq=N, tk=N, name="dense_attn_ctx")
        o_c = _dense_attn(qkv_c, (256, 0), qkv_c, (128, 4), qkv_c, (256, 1), groups=GROUPS_C,
                          tq=N, tk=N, sink=lp["sw_sink"], name="gqa_attn_ctx")
        k_d, v_d = _mla_expand(ckv_n, kpe_r, lw["w_k"], lw["w_v"], lp["e_place"], N)
        o_d = _dense_attn(q_d, (512, 0), k_d, (512, 0), v_d, (512, 0), groups=GROUPS_D,
                          tq=N, tk=N, name="mla_attn_ctx")
    x_new, h2, logits = _merge(x, mod, lp["g_norm1"], lp["g_norm2"], o_a, o_b, o_c, o_d,
                               lw["w_gate"], lw["w_branch"], lw["w_out"], lw["w_router"], tm)
    cache = None
    if not latent:
        strip = lambda v, heads: v.reshape(B, N, heads, LANES)[..., :64]
        cache = (qkv_a[:, :, 256:512].reshape(B, N, 4, 64), strip(qkv_a[:, :, 512:1024], 4),
                 qkv_b[:, :, 256:512].reshape(B, N, 4, 64), strip(qkv_b[:, :, 512:1024], 4),
                 qkv_c[:, :, 512:640].reshape(B, N, 2, 64), strip(qkv_c[:, :, 256:512], 2),
                 ckv_n, kpe_r[:, :, :MLA_ROPE])
    return x_new, h2, logits, cache


def kernel(x_prompt, x_sample, cache_diff_k, cache_diff_v, cache_na_k, cache_na_v, cache_swa_k, cache_swa_v, cache_mla_ckv, cache_mla_kpe, c, c_ctx, w_mod, b_mod, g_norm1, g_norm2, w_in, da_lambda, da_subln, na_rpb, sw_sink, mla_q_norm, mla_w_uq, mla_kv_norm, mla_w_ukv, w_gate, w_branch, w_out, w_router, w_e_gate, w_e_up, w_e_down, g_final):
    D = D_MODEL
    BS, NS, _ = x_sample.shape
    P = cache_diff_k.shape[2]
    cond = jnp.concatenate([c_ctx[None], c, jnp.zeros((16 - 1 - BS, D), F32)], axis=0)
    mod_all = _modulation(cond, w_mod, b_mod).reshape(DEPTH, 16, 6, D)
    tables = _rope_tables(NS)
    e_place = _kpe_placement()
    gf = g_final.reshape(1, D)
    na_bias = _nbr_bias_tables(na_rpb, NS // GRID_W)
    xp, xs = x_prompt, x_sample
    caches = []
    for l in range(DEPTH):
        lw = _prep_layer(l, w_in, mla_w_uq, mla_w_ukv, w_gate, w_branch, w_out, w_router)
        lp = dict(
            g_norm1=g_norm1[l].reshape(1, D), g_norm2=g_norm2[l].reshape(1, D),
            mla_q_norm=mla_q_norm[l].reshape(1, 256), mla_kv_norm=mla_kv_norm[l].reshape(1, 128),
            da_lambda=da_lambda[l], da_subln=da_subln[l].reshape(1, 64), sw_sink=sw_sink[l],
            e_place=e_place, na_bias=na_bias[l])
        lam_init = 0.8 - 0.6 * math.exp(-0.3 * l)
        final = l == DEPTH - 1
        mod_p, mod_s = mod_all[l, 0:1], mod_all[l, 1:1 + BS]
        xp, h2_p, lg_p, cache_l = _mixing(xp, mod_p, lw, lp, lam_init, None, None)
        caches.append(cache_l)
        ctx = (cache_diff_k[:, l].reshape(BS, P, 256), cache_diff_v[:, l].reshape(BS, P, 256),
               cache_na_k[:, l].reshape(BS, P, 256), cache_na_v[:, l].reshape(BS, P, 256),
               cache_swa_k[:, l].reshape(BS, P, 128), cache_swa_v[:, l].reshape(BS, P, 128),
               cache_mla_ckv[:, l], cache_mla_kpe[:, l])
        xs, h2_s, lg_s, _ = _mixing(xs, mod_s, lw, lp, lam_init, ctx, tables)
        xs, xp = _moe_pair([(xs, h2_s, lg_s, mod_s), (xp, h2_p, lg_p, mod_p)], gf, w_e_gate, w_e_up, w_e_down,
                           l, final)
    outs = tuple(jnp.stack([cl[i] for cl in caches], axis=1) for i in range(8))
    return (xp, xs) + outs
```

```python
import functools
import math

import numpy as np
import jax
import jax.numpy as jnp
from jax import lax
from jax.experimental import pallas as pl
from jax.experimental.pallas import tpu as pltpu

BF = jnp.bfloat16
F32 = jnp.float32

D_MODEL = 1024
DEPTH = 4
GRID_W = 64
ROPE_BASE = 10000.0
NORM_EPS = 1e-6
NEG_INF = -1e30
LOG2E = math.log2(math.e)

DA_QK = 32
NA_WIN_R = 8
NA_WIN_C = 16
SW_WINDOW = 128
MLA_NOPE = 64
MLA_ROPE = 32
N_EXPERTS = 16
EC_CAPACITY = 2

IN_PAD = 3200
LANES = 128

VMEM_BIG = 56 * 1024 * 1024
VMEM_MID = 40 * 1024 * 1024


def _cparams(sem, vmem=None):
    return pltpu.CompilerParams(dimension_semantics=sem, vmem_limit_bytes=vmem)


def _dot(a, b):
    return jnp.dot(a, b, preferred_element_type=F32)


def _dot_nt(a, b):
    return lax.dot_general(a, b, (((1,), (1,)), ((), ())), preferred_element_type=F32)


def _rms(x, g):
    var = jnp.mean(x * x, axis=-1, keepdims=True)
    return x * lax.rsqrt(var + NORM_EPS) * g


def _lane_mask(width, lo, length):
    lane = lax.broadcasted_iota(jnp.int32, (1, width), 1)
    return (lane >= lo) & (lane < lo + length)


def _mod_kernel(c_ref, w_ref, b_ref, o_ref):
    c = c_ref[...]
    s = c * (1.0 / (1.0 + jnp.exp(-c)))
    o_ref[0] = jnp.dot(s, w_ref[0], preferred_element_type=F32, precision=lax.Precision.HIGHEST) + b_ref[0]


def _modulation(cond, w_mod, b_mod):
    R, D = cond.shape
    L = w_mod.shape[0]
    nj = w_mod.shape[2] // D
    return pl.pallas_call(
        _mod_kernel,
        grid=(L, nj),
        in_specs=[
            pl.BlockSpec((R, D), lambda l, j: (0, 0)),
            pl.BlockSpec((1, D, D), lambda l, j: (l, 0, j)),
            pl.BlockSpec((1, 1, D), lambda l, j: (l, 0, j)),
        ],
        out_specs=pl.BlockSpec((1, R, D), lambda l, j: (l, 0, j)),
        out_shape=jax.ShapeDtypeStruct((L, R, nj * D), F32),
        compiler_params=_cparams(("parallel", "parallel")),
        name="modulation",
    )(cond, w_mod, b_mod.reshape(L, 1, nj * D))


def _rope(x, c, s, d):
    w = x.shape[-1]
    lane = lax.broadcasted_iota(jnp.int32, (1, LANES), 1)
    first = (lane % (2 * d)) < d
    outs = []
    for b in range(w // LANES):
        sl = slice(b * LANES, (b + 1) * LANES)
        xb = x[:, sl]
        partner = jnp.where(first, pltpu.roll(xb, LANES - d, 1), pltpu.roll(xb, d, 1))
        outs.append(xb * c[:, sl] + partner * s[:, sl])
    return outs[0] if len(outs) == 1 else jnp.concatenate(outs, axis=-1)


def _inproj_kernel(*refs, rope):
    if rope:
        (x_ref, mod_ref, g1_ref, w_ref, qn_ref, wuq_ref, kvn_ref,
         tac, tas, tcc, tcs, tqc, tqs, tkc, tks,
         oaq, oak, oav, ob, oc, oq, ockv, okpe) = refs
    else:
        (x_ref, mod_ref, g1_ref, w_ref, qn_ref, wuq_ref, kvn_ref,
         oaq, oak, oav, ob, oc, oq, ockv, okpe) = refs
    dt = oaq.dtype
    x = x_ref[0]
    h = (_rms(x, g1_ref[...]) * (1.0 + mod_ref[0, 1:2, :]) + mod_ref[0, 0:1, :]).astype(BF)

    def seg(lo, w):
        return _dot(h, w_ref[:, lo:lo + w])

    ua = seg(0, 1024)
    qk = ua[:, :512]
    if rope:
        qk = _rope(qk, tac[...], tas[...], 8)
    oaq[0] = (qk[:, :256] * (DA_QK ** -0.5 * LOG2E)).astype(dt)
    oak[0] = qk[:, 256:].astype(dt)
    ones_lane = lax.broadcasted_iota(jnp.int32, (1, 512), 1) % LANES >= 64
    oav[0] = jnp.where(ones_lane, 1.0, ua[:, 512:]).astype(dt)
    ub = seg(1024, 1024)
    ob[0, :, 0:256] = (ub[:, :256] * (0.125 * LOG2E)).astype(dt)
    ob[0, :, 256:512] = ub[:, 256:512].astype(dt)
    ob[0, :, 512:1024] = jnp.where(ones_lane, 1.0, ub[:, 512:]).astype(dt)
    uc = seg(2048, 640)
    qk = jnp.concatenate([uc[:, :256], uc[:, 512:]], axis=-1)
    if rope:
        qk = _rope(qk, tcc[...], tcs[...], 16)
    oc[0, :, 0:256] = (qk[:, :256] * (0.125 * LOG2E)).astype(dt)
    oc[0, :, 256:512] = jnp.where(ones_lane[:, :256], 1.0, uc[:, 256:512]).astype(dt)
    oc[0, :, 512:640] = qk[:, 256:].astype(dt)
    cq = _rms(seg(2688, 256), qn_ref[...]).astype(BF)
    qd = _dot(cq, wuq_ref[...])
    if rope:
        qd = _rope(qd, tqc[...], tqs[...], 8)
    oq[0] = (qd * ((MLA_NOPE + MLA_ROPE) ** -0.5 * LOG2E)).astype(dt)
    ockv[0] = _rms(seg(2944, 128), kvn_ref[...]).astype(dt)
    kpe = seg(3072, 128)
    if rope:
        kpe = _rope(kpe, tkc[...], tks[...], 8)
    okpe[0] = kpe.astype(dt)


def _in_proj(x, mod, g1, w_in_p, q_norm, w_uq_p, kv_norm, tables, out_dtype, tm, ctx_rows=0):
    B, N, D = x.shape
    rope = tables is not None
    nmod = mod.shape[0]
    mod_map = (lambda j, b: (b, 0, 0)) if nmod > 1 else (lambda j, b: (0, 0, 0))
    const2 = lambda j, b: (0, 0)
    in_specs = [
        pl.BlockSpec((1, tm, D), lambda j, b: (b, j, 0)),
        pl.BlockSpec((1, 6, D), mod_map),
        pl.BlockSpec((1, D), const2),
        pl.BlockSpec((D, IN_PAD), const2),
        pl.BlockSpec((1, 256), const2),
        pl.BlockSpec((256, 512), const2),
        pl.BlockSpec((1, 128), const2),
    ]
    args = [x, mod, g1, w_in_p, q_norm, w_uq_p, kv_norm]
    if rope:
        for t in tables:
            in_specs.append(pl.BlockSpec((tm, t.shape[1]), lambda j, b: (j, 0)))
            args.append(t)
    outs = ((256, N), (256, N + ctx_rows), (512, N + ctx_rows), (1024, N), (640, N), (512, N),
            (128, N + ctx_rows), (128, N + ctx_rows))
    out_specs = [pl.BlockSpec((1, tm, w), lambda j, b: (b, j, 0)) for w, _ in outs]
    out_shape = [jax.ShapeDtypeStruct((B, rows, w), out_dtype) for w, rows in outs]
    return pl.pallas_call(
        functools.partial(_inproj_kernel, rope=rope),
        grid=(N // tm, B),
        in_specs=in_specs,
        out_specs=out_specs,
        out_shape=out_shape,
        compiler_params=_cparams(("parallel", "parallel"), VMEM_BIG),
        name="in_proj_rope" if rope else "in_proj",
    )(*args)


def _mla_expand_kernel(ckv_ref, kpe_ref, wk_ref, wv_ref, e_ref, ok, ov):
    ckv = ckv_ref[0].astype(BF)
    kpe = kpe_ref[0].astype(BF)
    ok[0] = (_dot(ckv, wk_ref[...]) + _dot(kpe, e_ref[...])).astype(ok.dtype)
    ones_lane = lax.broadcasted_iota(jnp.int32, (1, 512), 1) % LANES >= 64
    ov[0] = jnp.where(ones_lane, 1.0, _dot(ckv, wv_ref[...])).astype(ov.dtype)


def _mla_expand(ckv, kpe, w_k, w_v, e_place, tk):
    B, K, _ = ckv.shape
    const2 = lambda b, j: (0, 0)
    return pl.pallas_call(
        _mla_expand_kernel,
        grid=(B, K // tk),
        in_specs=[
            pl.BlockSpec((1, tk, 128), lambda b, j: (b, j, 0)),
            pl.BlockSpec((1, tk, 128), lambda b, j: (b, j, 0)),
            pl.BlockSpec((128, 512), const2),
            pl.BlockSpec((128, 512), const2),
            pl.BlockSpec((128, 512), const2),
        ],
        out_specs=[pl.BlockSpec((1, tk, 512), lambda b, j: (b, j, 0)),
                   pl.BlockSpec((1, tk, 512), lambda b, j: (b, j, 0))],
        out_shape=[jax.ShapeDtypeStruct((B, K, 512), BF), jax.ShapeDtypeStruct((B, K, 512), BF)],
        compiler_params=_cparams(("parallel", "parallel")),
        name="mla_expand",
    )(ckv, kpe, w_k, w_v, e_place)


def _head_softmax(scores, values):
    m = functools.reduce(jnp.maximum, [jnp.max(s, axis=-1, keepdims=True) for s in scores])
    acc = None
    for s, v in zip(scores, values):
        part = _dot(jnp.exp2(s - m).astype(BF), v)
        acc = part if acc is None else acc + part
    return acc, m


def _normalise(acc):
    return acc * pltpu.roll(1.0 / acc, 64, 1)


def _store_heads(o_ref, heads):
    half = _lane_mask(LANES, 0, 64)
    for hp in range(len(heads) // 2):
        slab = jnp.where(half, heads[2 * hp], pltpu.roll(heads[2 * hp + 1], 64, 1))
        o_ref[0, :, hp * LANES:(hp + 1) * LANES] = slab.astype(o_ref.dtype)


def _dense_attn_kernel(*refs, groups, tk, n_chunks, has_sink, diff):
    it = iter(refs)
    q_ref, k_ref, v_ref = next(it), next(it), next(it)
    sink_ref = lam_ref = subln_ref = None
    if has_sink:
        sink_ref = next(it)
    if diff:
        lam_ref, subln_ref, lam_init_ref = next(it), next(it), next(it)
    o_ref = next(it)
    tq = q_ref.shape[1]

    lam = lam_init = None
    if diff:
        lam_init = lam_init_ref[0]
        dl = lam_ref[...]
        lam = (jnp.exp(jnp.sum(dl[0:1] * dl[1:2], axis=-1, keepdims=True))
               - jnp.exp(jnp.sum(dl[2:3] * dl[3:4], axis=-1, keepdims=True)) + lam_init)

    ng = len(groups)
    scratch = list(it)
    qg_s, m_s, acc_s = (scratch[j * ng:(j + 1) * ng] for j in range(3))
    for gi, (q_lo, k_lo, w, qmask, v_lo, head, sink_idx, sign) in enumerate(groups):
        qg = q_ref[0, :, q_lo:q_lo + w].astype(BF)
        if qmask is not None:
            qg = jnp.where(_lane_mask(w, *qmask), qg, jnp.zeros_like(qg))
        qg_s[gi][...] = qg
        m_s[gi][...] = jnp.full((tq, 1), NEG_INF, F32)
        acc_s[gi][...] = jnp.zeros((tq, LANES), F32)

    def keys_step(c):
        off = c * tk
        for gi, (q_lo, k_lo, w, qmask, v_lo, head, sink_idx, sign) in enumerate(groups):
            s = _dot_nt(qg_s[gi][...], k_ref[0, pl.ds(off, tk), k_lo:k_lo + w].astype(BF))
            m = m_s[gi][...]
            m_new = jnp.maximum(m, jnp.max(s, axis=-1, keepdims=True))
            p = jnp.exp2(s - m_new).astype(BF)
            pv = _dot(p, v_ref[0, pl.ds(off, tk), v_lo:v_lo + LANES].astype(BF))
            acc_s[gi][...] = jnp.exp2(m - m_new) * acc_s[gi][...] + pv
            m_s[gi][...] = m_new

    for c in range(n_chunks):
        keys_step(c)

    heads = {}
    ones_half = _lane_mask(LANES, 64, 64)
    for gi, (q_lo, k_lo, w, qmask, v_lo, head, sink_idx, sign) in enumerate(groups):
        acc = acc_s[gi][...]
        if sink_idx is not None:
            acc = acc + jnp.where(ones_half, jnp.exp2(sink_ref[sink_idx] * LOG2E - m_s[gi][...]), 0.0)
        part = _normalise(acc)
        if sign < 0:
            part = -lam * part
        heads[head] = part if head not in heads else heads[head] + part
    if diff:
        half = _lane_mask(LANES, 0, 64)
        for hd, o in heads.items():
            ms = jnp.sum(jnp.where(half, o * o, 0.0), axis=-1, keepdims=True) * (1.0 / 64.0)
            heads[hd] = o * lax.rsqrt(ms + NORM_EPS) * subln_ref[...] * (1.0 - lam_init)
    _store_heads(o_ref, heads)


def _dense_attn(q_arr, q_blk, k_arr, k_blk, v_arr, v_blk, *, groups, tq, tk, sink=None, diff=None, name):
    B, N, _ = q_arr.shape
    K = k_arr.shape[1]
    n_chunks = K // tk
    assert n_chunks * tk == K and n_chunks <= 2
    qw, qi = q_blk
    kw, ki = k_blk
    vw, vi = v_blk
    in_specs = [
        pl.BlockSpec((1, tq, qw), lambda b, i: (b, i, qi)),
        pl.BlockSpec((1, K, kw), lambda b, i: (b, 0, ki)),
        pl.BlockSpec((1, K, vw), lambda b, i: (b, 0, vi)),
    ]
    args = [q_arr, k_arr, v_arr]
    if sink is not None:
        in_specs.append(pl.BlockSpec(memory_space=pltpu.SMEM))
        args.append(sink)
    if diff is not None:
        da_lambda, subln, lam_init = diff
        in_specs += [pl.BlockSpec((4, DA_QK), lambda b, i: (0, 0)),
                     pl.BlockSpec((1, LANES), lambda b, i: (0, 0)),
                     pl.BlockSpec(memory_space=pltpu.SMEM)]
        args += [da_lambda, subln, jnp.full((1,), lam_init, F32)]
    kern = functools.partial(_dense_attn_kernel, groups=groups, tk=tk, n_chunks=n_chunks,
                             has_sink=sink is not None, diff=diff is not None)
    scratch = ([pltpu.VMEM((tq, g[2]), BF) for g in groups] + [pltpu.VMEM((tq, 1), F32) for _ in groups]
               + [pltpu.VMEM((tq, LANES), F32) for _ in groups])
    return pl.pallas_call(
        kern,
        grid=(B, N // tq),
        in_specs=in_specs,
        out_specs=pl.BlockSpec((1, tq, 256), lambda b, i: (b, i, 0)),
        out_shape=jax.ShapeDtypeStruct((B, N, 256), BF),
        scratch_shapes=scratch,
        compiler_params=_cparams(("parallel", "parallel"), VMEM_BIG),
        name=name,
    )(*args)


GROUPS_A = tuple((128 * (g // 4), 128 * (g // 4), 128, (32 * (g % 4), 32), 128 * (g // 2), g // 2, None,
                  1 if g % 2 == 0 else -1) for g in range(8))
GROUPS_B = tuple((128 * (h // 2), 128 * (h // 2), 128, (64 * (h % 2), 64), 128 * h, h, None, 1) for h in range(4))
GROUPS_C = tuple((128 * g, 0, 128, (64 * j, 64), 128 * j, 2 * g + j, 2 * j + g, 1) for g in range(2) for j in range(2))
GROUPS_D = tuple((128 * h, 128 * h, 128, None, 128 * h, h, None, 1) for h in range(4))


def _window_attn_kernel(q_ref, k_ref, v_ref, ck_ref, cv_ref, sink_ref, o_ref, *, n, band):
    tq = q_ref.shape[1]
    i = pl.program_id(1)
    start = jnp.clip(i * tq - SW_WINDOW, 0, n - band)
    start = pl.multiple_of(start, 128)
    kb = k_ref[0, pl.ds(start, band), :]
    vb = v_ref[0, pl.ds(start, band), :]
    ck = ck_ref[0]
    cv = cv_ref[0]
    qpos = i * tq + lax.broadcasted_iota(jnp.int32, (tq, band), 0)
    kpos = start + lax.broadcasted_iota(jnp.int32, (tq, band), 1)
    valid = jnp.abs(qpos - kpos) <= SW_WINDOW
    ones_half = _lane_mask(LANES, 64, 64)
    heads = {}
    for g in range(2):
        for j in range(2):
            qg = q_ref[0, :, LANES * g:LANES * (g + 1)]
            qg = jnp.where(_lane_mask(LANES, 64 * j, 64), qg, jnp.zeros_like(qg))
            vs = slice(LANES * j, LANES * (j + 1))
            acc, m = _head_softmax([_dot_nt(qg, ck), jnp.where(valid, _dot_nt(qg, kb), NEG_INF)],
                                   [cv[:, vs], vb[:, vs]])
            acc = acc + jnp.where(ones_half, jnp.exp2(sink_ref[2 * j + g] * LOG2E - m), 0.0)
            heads[2 * g + j] = _normalise(acc)
    _store_heads(o_ref, heads)


def _window_attn(qkv_c, ck, cv, sink, tq):
    B, N, _ = qkv_c.shape
    band = tq + 2 * SW_WINDOW
    P = ck.shape[1]
    return pl.pallas_call(
        functools.partial(_window_attn_kernel, n=N, band=band),
        grid=(B, N // tq),
        in_specs=[
            pl.BlockSpec((1, tq, 256), lambda b, i: (b, i, 0)),
            pl.BlockSpec((1, N, 128), lambda b, i: (b, 0, 4)),
            pl.BlockSpec((1, N, 256), lambda b, i: (b, 0, 1)),
            pl.BlockSpec((1, P, 128), lambda b, i: (b, 0, 0)),
            pl.BlockSpec((1, P, 256), lambda b, i: (b, 0, 0)),
            pl.BlockSpec(memory_space=pltpu.SMEM),
        ],
        out_specs=pl.BlockSpec((1, tq, 256), lambda b, i: (b, i, 0)),
        out_shape=jax.ShapeDtypeStruct((B, N, 256), BF),
        compiler_params=_cparams(("parallel", "parallel")),
        name="window_attn",
    )(qkv_c, qkv_c, qkv_c, ck, cv, sink)


NA_TILE_ROWS = 4
NA_KEY_ROWS = NA_TILE_ROWS + NA_WIN_R


def _nbr_attn_kernel(q_ref, k_ref, v_ref, ck_ref, cv_ref, bias_ref, o_ref, *, rows):
    i = pl.program_id(1)
    krow = jnp.clip(i * NA_TILE_ROWS - NA_WIN_R // 2, 0, rows - NA_KEY_ROWS)
    start = pl.multiple_of(krow * GRID_W, GRID_W)
    nk = NA_KEY_ROWS * GRID_W
    kb = k_ref[0, pl.ds(start, nk), :]
    vb = v_ref[0, pl.ds(start, nk), :]
    ck = ck_ref[0]
    cv = cv_ref[0]
    q = q_ref[0]
    heads = {}
    for h in range(4):
        ks = slice(LANES * (h // 2), LANES * (h // 2 + 1))
        qg = jnp.where(_lane_mask(LANES, 64 * (h % 2), 64), q[:, ks], jnp.zeros_like(q[:, ks]))
        vs = slice(LANES * h, LANES * (h + 1))
        acc, _ = _head_softmax([_dot_nt(qg, ck[:, ks]), _dot_nt(qg, kb[:, ks]) + bias_ref[0, h]],
                               [cv[:, vs], vb[:, vs]])
        heads[h] = _normalise(acc)
    _store_heads(o_ref, heads)


def _nbr_attn(qkv_b, ck, cv, bias):
    B, N, _ = qkv_b.shape
    rows = N // GRID_W
    tq = NA_TILE_ROWS * GRID_W
    nt = N // tq
    nk = NA_KEY_ROWS * GRID_W
    P = ck.shape[1]

    def bias_map(b, i):
        return (jnp.where(i == 0, 0, jnp.where(i == nt - 1, 2, 1)), 0, 0, 0)

    return pl.pallas_call(
        functools.partial(_nbr_attn_kernel, rows=rows),
        grid=(B, nt),
        in_specs=[
            pl.BlockSpec((1, tq, 256), lambda b, i: (b, i, 0)),
            pl.BlockSpec((1, N, 256), lambda b, i: (b, 0, 1)),
            pl.BlockSpec((1, N, 512), lambda b, i: (b, 0, 1)),
            pl.BlockSpec((1, P, 256), lambda b, i: (b, 0, 0)),
            pl.BlockSpec((1, P, 512), lambda b, i: (b, 0, 0)),
            pl.BlockSpec((1, 4, tq, nk), bias_map),
        ],
        out_specs=pl.BlockSpec((1, tq, 256), lambda b, i: (b, i, 0)),
        out_shape=jax.ShapeDtypeStruct((B, N, 256), BF),
        compiler_params=_cparams(("parallel", "parallel"), VMEM_MID),
        name="nbr_attn",
    )(qkv_b, qkv_b, qkv_b, ck, cv, bias)


def _nbr_bias_tables(rpb, rows):
    L, H = rpb.shape[:2]
    rpb = rpb.reshape((L * H,) + rpb.shape[2:])
    rp = jnp.pad(rpb.astype(F32), ((0, 0), (NA_KEY_ROWS, NA_KEY_ROWS), (GRID_W, GRID_W)))
    c0 = NA_WIN_C - 1 + GRID_W
    tcol = jnp.stack([rp[:, :, c0 - qc:c0 - qc + GRID_W] for qc in range(GRID_W)], axis=2)
    tabs = []
    nt = rows // NA_TILE_ROWS
    for r0 in (0, NA_TILE_ROWS * min(1, nt - 1) + NA_TILE_ROWS, rows - NA_TILE_ROWS):
        ks = int(np.clip(r0 - NA_WIN_R // 2, 0, rows - NA_KEY_ROWS))
        per_row = []
        for a in range(NA_TILE_ROWS):
            s0 = ks - (r0 + a) + NA_WIN_R - 1 + NA_KEY_ROWS
            per_row.append(tcol[:, s0:s0 + NA_KEY_ROWS])
        t = jnp.stack(per_row, axis=1).transpose(0, 1, 3, 2, 4)
        r = r0 + np.arange(NA_TILE_ROWS)[:, None, None, None]
        qc = np.arange(GRID_W)[None, :, None, None]
        kr = ks + np.arange(NA_KEY_ROWS)[None, None, :, None]
        kc = np.arange(GRID_W)[None, None, None, :]
        rs = np.clip(r - NA_WIN_R // 2, 0, rows - NA_WIN_R)
        cs = np.clip(qc - NA_WIN_C // 2, 0, GRID_W - NA_WIN_C)
        ok = (kr >= rs) & (kr < rs + NA_WIN_R) & (kc >= cs) & (kc < cs + NA_WIN_C)
        t = jnp.where(ok[None], t * LOG2E, NEG_INF)
        tabs.append(t.reshape(L, H, NA_TILE_ROWS * GRID_W, NA_KEY_ROWS * GRID_W))
    return jnp.stack(tabs, axis=1)


def _merge_kernel(x_ref, mod_ref, g1_ref, g2_ref, oa, ob, oc, od, wg_ref, wb_ref, wo_ref, wr_ref,
                  xo_ref, h2_ref, lg_ref):
    D = x_ref.shape[2]
    x = x_ref[0]
    h = (_rms(x, g1_ref[...]) * (1.0 + mod_ref[0, 1:2, :]) + mod_ref[0, 0:1, :]).astype(BF)
    merged = None
    for i, o_ref in enumerate((oa, ob, oc, od)):
        z = _dot(h, wg_ref[:, i * D:(i + 1) * D])
        gate = 1.0 / (1.0 + jnp.exp(-z))
        term = gate * _dot(o_ref[0], wb_ref[i])
        merged = term if merged is None else merged + term
    mixed = _dot(merged.astype(BF), wo_ref[...])
    xn = x + mod_ref[0, 2:3, :] * mixed
    xo_ref[0] = xn
    h2 = (_rms(xn, g2_ref[...]) * (1.0 + mod_ref[0, 4:5, :]) + mod_ref[0, 3:4, :]).astype(BF)
    h2_ref[0] = h2
    lg_ref[0] = _dot_nt(wr_ref[...], h2)


def _merge(x, mod, g1, g2, o_a, o_b, o_c, o_d, w_gate, w_branch, w_out, w_router_p, tm):
    B, N, D = x.shape
    nmod = mod.shape[0]
    mod_map = (lambda b, j: (b, 0, 0)) if nmod > 1 else (lambda b, j: (0, 0, 0))
    tok = lambda w: pl.BlockSpec((1, tm, w), lambda b, j: (b, j, 0))
    c2 = lambda b, j: (0, 0)
    return pl.pallas_call(
        _merge_kernel,
        grid=(B, N // tm),
        in_specs=[
            tok(D), pl.BlockSpec((1, 6, D), mod_map), pl.BlockSpec((1, D), c2), pl.BlockSpec((1, D), c2),
            tok(256), tok(256), tok(256), tok(256),
            pl.BlockSpec((D, 4 * D), c2),
            pl.BlockSpec((4, 256, D), lambda b, j: (0, 0, 0)),
            pl.BlockSpec((D, D), c2),
            pl.BlockSpec((N_EXPERTS, D), c2),
        ],
        out_specs=[tok(D), tok(D), pl.BlockSpec((1, N_EXPERTS, tm), lambda b, j: (b, 0, j))],
        out_shape=[jax.ShapeDtypeStruct((B, N, D), F32), jax.ShapeDtypeStruct((B, N, D), BF),
                   jax.ShapeDtypeStruct((B, N_EXPERTS, N), F32)],
        compiler_params=_cparams(("parallel", "parallel"), VMEM_BIG),
        name="merge",
    )(x, mod, g1, g2, o_a, o_b, o_c, o_d, w_gate, w_branch, w_out, w_router_p)


MOE_TT = 256
MOE_RC = 64
SLOT_ALIGN = 16


def _excl_cumsum(x, tri, reset_every):
    R, N = x.shape
    outs = []
    carry = jnp.zeros((R, 1), F32)
    for c in range(N // LANES):
        if reset_every and (c * LANES) % reset_every == 0:
            carry = jnp.zeros((R, 1), F32)
        xc = x[:, c * LANES:(c + 1) * LANES]
        inc = _dot(xc.astype(BF), tri)
        outs.append(inc - xc + carry)
        carry = carry + inc[:, LANES - 1:LANES]
    return (outs[0] if len(outs) == 1 else jnp.concatenate(outs, axis=1)), carry


def _route_kernel(lg_ref, aff_ref, rel_ref, base_ref, *, cap, tt):
    B, E, N = lg_ref.shape
    R = B * E
    nt = N // tt
    lg = lg_ref[...]
    ex = jnp.exp(lg - jnp.max(lg, axis=1, keepdims=True))
    aff = ex / jnp.sum(ex, axis=1, keepdims=True)
    aff_ref[...] = aff
    bits = pltpu.bitcast(aff.reshape(R, N), jnp.int32)

    def per_expert(v):
        t = v[0:E]
        for b in range(1, B):
            t = t + v[b * E:(b + 1) * E]
        return t

    def per_row(v):
        return jnp.concatenate([v] * B, axis=0) if B > 1 else v

    def batch_prefix(v):
        parts, run = [], jnp.zeros((E, 1), F32)
        for b in range(B):
            parts.append(run)
            run = run + v[b * E:(b + 1) * E]
        return jnp.concatenate(parts, axis=0) if B > 1 else parts[0]

    def count(mask):
        return per_expert(jnp.sum(jnp.where(mask, 1.0, 0.0), axis=1, keepdims=True))

    def search(i, thr):
        cand = thr | jnp.left_shift(jnp.int32(1), 30 - i)
        return jnp.where(count(bits >= per_row(cand)) >= float(cap), cand, thr)

    thr = per_row(lax.fori_loop(0, 31, search, jnp.zeros((E, 1), jnp.int32)))
    gt = bits > thr
    eq = bits == thr
    need = float(cap) - count(gt)
    ri = lax.broadcasted_iota(jnp.int32, (LANES, LANES), 0)
    ci = lax.broadcasted_iota(jnp.int32, (LANES, LANES), 1)
    tri = jnp.where(ri <= ci, 1.0, 0.0).astype(BF)
    rank_eq, eq_tot = _excl_cumsum(jnp.where(eq, 1.0, 0.0), tri, None)
    rank_eq = rank_eq + batch_prefix(eq_tot)
    sel = gt | (eq & (rank_eq < per_row(need)))
    self = jnp.where(sel, 1.0, 0.0)
    local, _ = _excl_cumsum(self, tri, tt)
    rel_ref[...] = jnp.where(sel, local, -1.0).astype(jnp.int32).reshape(B, E, N)
    tn = jnp.right_shift(lax.broadcasted_iota(jnp.int32, (N, LANES), 0), tt.bit_length() - 1)
    tj = lax.broadcasted_iota(jnp.int32, (N, LANES), 1)
    cnt = _dot(self.astype(BF), jnp.where(tn == tj, 1.0, 0.0).astype(BF)).astype(jnp.int32)
    sh = SLOT_ALIGN.bit_length() - 1
    ru = jnp.left_shift(jnp.right_shift(cnt + (SLOT_ALIGN - 1), sh), sh).astype(F32)
    base_b = _dot(ru.astype(BF), jnp.where(ri < ci, 1.0, 0.0).astype(BF))
    base = base_b + batch_prefix(base_b[:, nt:nt + 1])
    base_ref[...] = base.astype(jnp.int32).reshape(B, E, LANES)


def _route(lg_t, cap):
    B, E, N = lg_t.shape
    full = lambda s: pl.BlockSpec(s, lambda i: (0,) * len(s))
    return pl.pallas_call(
        functools.partial(_route_kernel, cap=cap, tt=MOE_TT),
        grid=(1,),
        in_specs=[full((B, E, N))],
        out_specs=[full((B, E, N)), full((B, E, N)), full((B, E, LANES))],
        out_shape=[jax.ShapeDtypeStruct((B, E, N), F32), jax.ShapeDtypeStruct((B, E, N), jnp.int32),
                   jax.ShapeDtypeStruct((B, E, LANES), jnp.int32)],
        compiler_params=_cparams(("arbitrary",), VMEM_BIG),
        name="route",
    )(lg_t)


def _slot_onehot(rel, values, c, rc):
    E, tt = rel.shape
    r = lax.broadcasted_iota(jnp.int32, (E, rc, tt), 1) + c * rc
    return jnp.where(rel[:, None, :] == r, values[:, None, :], 0.0).astype(BF).reshape(E * rc, tt)


def _dispatch_kernel(base_sm, x_ref, rel_ref, xe_in, xe_hbm, stage, stage2, sem, sem2, *, nt, rc, row_off):
    del xe_in
    b, j = pl.program_id(0), pl.program_id(1)
    E = rel_ref.shape[1]
    step = b * nt + j
    last = pl.num_programs(0) * nt - 1
    base = lambda e, jj: base_sm[(b * E + e) * (nt + 1) + jj]
    rel = rel_ref[0]
    x = x_ref[0]
    ones = jnp.ones(rel.shape, F32)

    def copy(buf, s, e, c):
        start = pl.multiple_of(row_off + base(e, j) + c * rc, SLOT_ALIGN)
        return pltpu.make_async_copy(buf.at[pl.ds(e * rc, rc)], xe_hbm.at[e, pl.ds(start, rc)], s.at[e])

    strip = _dot(_slot_onehot(rel, ones, 0, rc), x).astype(BF)

    @pl.when(step > 0)
    def _():
        for e in range(E):
            copy(stage, sem, e, 0).wait()

    stage[...] = strip
    for e in range(E):
        copy(stage, sem, e, 0).start()

    span = [base(e, j + 1) - base(e, j) for e in range(E)]
    widest = functools.reduce(jnp.maximum, span)

    def extra(c, carry):
        stage2[...] = _dot(_slot_onehot(rel, ones, c, rc), x).astype(BF)
        for e in range(E):
            @pl.when(span[e] > c * rc)
            def _():
                cp = copy(stage2, sem2, e, c)
                cp.start()
                cp.wait()
        return carry

    lax.fori_loop(1, (widest + rc - 1) // rc, extra, 0)

    @pl.when(step == last)
    def _():
        for e in range(E):
            copy(stage, sem, e, 0).wait()


def _dispatch(h2, rel, base_flat, xe, row_off):
    B, N, D = h2.shape
    E, rows, _ = xe.shape
    nt = N // MOE_TT
    return pl.pallas_call(
        functools.partial(_dispatch_kernel, nt=nt, rc=MOE_RC, row_off=row_off),
        grid_spec=pltpu.PrefetchScalarGridSpec(
            num_scalar_prefetch=1,
            grid=(B, nt),
            in_specs=[
                pl.BlockSpec((1, MOE_TT, D), lambda b, j, s: (b, j, 0)),
                pl.BlockSpec((1, E, MOE_TT), lambda b, j, s: (b, 0, j)),
                pl.BlockSpec(memory_space=pl.ANY),
            ],
            out_specs=pl.BlockSpec(memory_space=pl.ANY),
            scratch_shapes=[pltpu.VMEM((E * MOE_RC, D), BF), pltpu.VMEM((E * MOE_RC, D), BF),
                            pltpu.SemaphoreType.DMA((E,)), pltpu.SemaphoreType.DMA((E,))],
        ),
        out_shape=jax.ShapeDtypeStruct((E, rows, D), BF),
        input_output_aliases={3: 0},
        compiler_params=_cparams(("arbitrary", "arbitrary")),
        name="dispatch",
    )(base_flat, h2, rel, xe)


def _ffn_kernel(tot_sm, x_ref, wg_hbm, wu_hbm, wd_hbm, y_ref, wbuf, wg_s, wu_s, wd_s, sem, *, tm, rc, layer,
                n_first):
    e, j = pl.program_id(0), pl.program_id(1)
    slot = e % 2

    def copy(ee, sl, k):
        return pltpu.make_async_copy((wg_hbm, wu_hbm, wd_hbm)[k].at[layer, ee], wbuf.at[sl, k], sem.at[sl, k])

    @pl.when(j == 0)
    def _():
        @pl.when(e == 0)
        def _():
            for k in range(3):
                copy(e, slot, k).start()

        @pl.when(e + 1 < pl.num_programs(0))
        def _():
            for k in range(3):
                copy(e + 1, 1 - slot, k).start()

        for k in range(3):
            copy(e, slot, k).wait()
        wg_s[...] = wbuf[slot, 0].astype(BF)
        wu_s[...] = wbuf[slot, 1].astype(BF)
        wd_s[...] = wbuf[slot, 2].astype(BF)

    second = j >= n_first
    jj = jnp.where(second, j - n_first, j)
    used = jj * tm < tot_sm[jnp.where(second, pl.num_programs(0) + e, e)] + rc

    @pl.when(used)
    def _():
        x = x_ref[0]
        hg = _dot(x, wg_s[...])
        hu = _dot(x, wu_s[...])
        act = (hg * (1.0 / (1.0 + jnp.exp(-hg))) * hu).astype(BF)
        y_ref[0] = _dot(act, wd_s[...]).astype(y_ref.dtype)

    @pl.when(jnp.logical_not(used))
    def _():
        y_ref[...] = jnp.zeros_like(y_ref)


def _expert_ffn(xe, totals, rows_first, w_g, w_u, w_d, layer, tm):
    E, rows, D = xe.shape
    DE = w_g.shape[3]
    n_first = rows_first // tm

    def tok(e, j, tot):
        second = j >= n_first
        last = (tot[jnp.where(second, E + e, e)] + MOE_RC - 1) // tm
        return (e, jnp.where(second, n_first + jnp.minimum(j - n_first, last), jnp.minimum(j, last)), 0)

    assert D == DE
    hbm = pl.BlockSpec(memory_space=pl.ANY)
    return pl.pallas_call(
        functools.partial(_ffn_kernel, tm=tm, rc=MOE_RC, layer=layer, n_first=n_first),
        grid_spec=pltpu.PrefetchScalarGridSpec(
            num_scalar_prefetch=1,
            grid=(E, rows // tm),
            in_specs=[pl.BlockSpec((1, tm, D), tok), hbm, hbm, hbm],
            out_specs=pl.BlockSpec((1, tm, D), lambda e, j, tot: (e, j, 0)),
            scratch_shapes=[pltpu.VMEM((2, 3, D, DE), F32), pltpu.VMEM((D, DE), BF), pltpu.VMEM((D, DE), BF),
                            pltpu.VMEM((DE, D), BF), pltpu.SemaphoreType.DMA((2, 3))],
        ),
        out_shape=jax.ShapeDtypeStruct((E, rows, D), BF),
        compiler_params=_cparams(("arbitrary", "arbitrary"), VMEM_BIG),
        name="expert_ffn",
    )(totals, xe, w_g, w_u, w_d)


def _combine_kernel(base_sm, x_ref, rel_ref, aff_ref, mod_ref, gf_ref, ye_hbm, o_ref, stage, sem, *,
                    nt, rc, final, row_off):
    b, j = pl.program_id(0), pl.program_id(1)
    E = rel_ref.shape[1]
    step = b * nt + j
    nsteps = pl.num_programs(0) * nt
    slot = step % 2
    base = lambda e, jj: base_sm[(b * E + e) * (nt + 1) + jj]
    rel = rel_ref[0]
    aff = aff_ref[0]

    def copy(st, sl, e, c):
        first = base_sm[(st // nt * E + e) * (nt + 1) + st % nt]
        start = pl.multiple_of(row_off + first + c * rc, SLOT_ALIGN)
        return pltpu.make_async_copy(ye_hbm.at[e, pl.ds(start, rc)], stage.at[sl, pl.ds(e * rc, rc)],
                                     sem.at[sl, e])

    def scatter(c):
        w = _slot_onehot(rel, aff, c, rc)
        return lax.dot_general(w, stage[slot], (((0,), (0,)), ((), ())), preferred_element_type=F32)

    @pl.when(step == 0)
    def _():
        for e in range(E):
            copy(step, slot, e, 0).start()

    @pl.when(step + 1 < nsteps)
    def _():
        for e in range(E):
            copy(step + 1, 1 - slot, e, 0).start()

    for e in range(E):
        copy(step, slot, e, 0).wait()
    acc = scatter(0)
    span = [base(e, j + 1) - base(e, j) for e in range(E)]
    widest = functools.reduce(jnp.maximum, span)

    def extra(c, acc):
        for e in range(E):
            @pl.when(span[e] > c * rc)
            def _():
                cp = copy(step, slot, e, c)
                cp.start()
                cp.wait()
        return acc + scatter(c)

    acc = lax.fori_loop(1, (widest + rc - 1) // rc, extra, acc)
    xn = x_ref[0] + mod_ref[0, 5:6, :] * acc
    if final:
        xn = _rms(xn, gf_ref[...])
    o_ref[0] = xn


def _combine(x, rel, aff, mod, g_final, ye, base_flat, row_off, final):
    B, N, D = x.shape
    E = rel.shape[1]
    nt = N // MOE_TT
    nmod = mod.shape[0]
    mod_map = (lambda b, j, s: (b, 0, 0)) if nmod > 1 else (lambda b, j, s: (0, 0, 0))
    tok = pl.BlockSpec((1, MOE_TT, D), lambda b, j, s: (b, j, 0))
    slots = pl.BlockSpec((1, E, MOE_TT), lambda b, j, s: (b, 0, j))
    return pl.pallas_call(
        functools.partial(_combine_kernel, nt=nt, rc=MOE_RC, final=final, row_off=row_off),
        grid_spec=pltpu.PrefetchScalarGridSpec(
            num_scalar_prefetch=1,
            grid=(B, nt),
            in_specs=[tok, slots, slots, pl.BlockSpec((1, 6, D), mod_map),
                      pl.BlockSpec((1, D), lambda b, j, s: (0, 0)), pl.BlockSpec(memory_space=pl.ANY)],
            out_specs=tok,
            scratch_shapes=[pltpu.VMEM((2, E * MOE_RC, D), BF), pltpu.SemaphoreType.DMA((2, E))],
        ),
        out_shape=jax.ShapeDtypeStruct((B, N, D), F32),
        compiler_params=_cparams(("arbitrary", "arbitrary")),
        name="combine_final" if final else "combine",
    )(base_flat, x, rel, aff, mod, g_final, ye)


FFN_TM = 512


def _slot_rows(B, N):
    T = B * N
    worst = EC_CAPACITY * T // N_EXPERTS + (SLOT_ALIGN - 1) * (T // MOE_TT) + MOE_RC
    return -(-worst // FFN_TM) * FFN_TM


def _moe_pair(sets, g_final, w_g, w_u, w_d, layer, final):
    D = sets[0][1].shape[2]
    rows = [_slot_rows(*st[1].shape[:2]) for st in sets]
    xe = jnp.zeros((N_EXPERTS, sum(rows), D), BF)
    routed, row_off = [], 0
    for (x, h2, lg_t, mod), r in zip(sets, rows):
        B, N, _ = h2.shape
        nt = N // MOE_TT
        aff, rel, base = _route(lg_t, EC_CAPACITY * B * N // N_EXPERTS)
        base_flat = base[:, :, :nt + 1].reshape(-1)
        xe = _dispatch(h2, rel, base_flat, xe, row_off)
        routed.append((aff, rel, base_flat, base[B - 1, :, nt], row_off))
        row_off += r
    ye = _expert_ffn(xe, jnp.concatenate([routed[0][3], routed[1][3]]), rows[0], w_g, w_u, w_d, layer, FFN_TM)
    return [_combine(x, rel, aff, mod, g_final, ye, base_flat, off, final)
            for (x, h2, lg_t, mod), (aff, rel, base_flat, _, off) in zip(sets, routed)]


def _rope_tables(n):
    t = jnp.arange(n)
    pos = {1: (t // GRID_W).astype(F32), 2: (t % GRID_W).astype(F32)}

    def group(quarter):
        freqs = jnp.float32(ROPE_BASE) ** (-jnp.arange(quarter, dtype=F32) / quarter)
        cs, ss = [], []
        for kind in (1, 2):
            ang = pos[kind][:, None] * freqs
            c, s = jnp.cos(ang), jnp.sin(ang)
            cs += [c, c]
            ss += [-s, s]
        return jnp.concatenate(cs, axis=1), jnp.concatenate(ss, axis=1)

    c32, s32 = group(8)
    c64, s64 = group(16)
    one = lambda w: jnp.ones((n, w), F32)
    zero = lambda w: jnp.zeros((n, w), F32)
    ta = (jnp.tile(c32, (1, 16)), jnp.tile(s32, (1, 16)))
    tc = (jnp.tile(c64, (1, 6)), jnp.tile(s64, (1, 6)))
    qc = jnp.concatenate([one(64), c32, one(32)], axis=1)
    qs = jnp.concatenate([zero(64), s32, zero(32)], axis=1)
    tq = (jnp.tile(qc, (1, 4)), jnp.tile(qs, (1, 4)))
    tk = (jnp.concatenate([c32, one(96)], axis=1), jnp.concatenate([s32, zero(96)], axis=1))
    return (ta[0], ta[1], tc[0], tc[1], tq[0], tq[1], tk[0], tk[1])


LAT_TQ = 512
C_HEAD_ORDER = (0, 2, 1, 3)


def _prep_layer(l, w_in, mla_w_uq, mla_w_ukv, w_gate, w_branch, w_out, w_router):
    D = D_MODEL
    wi = w_in[l]
    cq = wi[:, 1536:1792].reshape(D, 4, 64)[:, C_HEAD_ORDER, :].reshape(D, 256)
    def spread(cols, heads):
        return jnp.pad(cols.reshape(D, heads, 64), ((0, 0), (0, 0), (0, 64))).reshape(D, heads * LANES)

    w_in_p = jnp.concatenate(
        [wi[:, :512], spread(wi[:, 512:768], 4), wi[:, 768:1280], spread(wi[:, 1280:1536], 4),
         cq, spread(wi[:, 1920:2048], 2), wi[:, 1792:1920], wi[:, 2048:],
         jnp.zeros((D, LANES - MLA_ROPE), F32)], axis=1).astype(BF)
    uq = mla_w_uq[l].reshape(256, 4, MLA_NOPE + MLA_ROPE)
    w_uq_p = jnp.concatenate([uq, jnp.zeros((256, 4, 32), F32)], axis=2).reshape(256, 512).astype(BF)
    ukv = mla_w_ukv[l].reshape(128, 4, 128)
    w_k = jnp.concatenate([ukv[:, :, :64], jnp.zeros((128, 4, 64), F32)], axis=2).reshape(128, 512).astype(BF)
    w_v = jnp.concatenate([ukv[:, :, 64:], jnp.zeros((128, 4, 64), F32)], axis=2).reshape(128, 512).astype(BF)
    wb = w_branch[l]
    wb_c = wb[2].reshape(4, 64, D)[C_HEAD_ORDER, :, :].reshape(256, D)
    w_branch_p = jnp.stack([wb[0], wb[1], wb_c, wb[3]]).astype(BF)
    w_router_p = w_router[l].T.astype(BF)
    return dict(w_in=w_in_p, w_uq=w_uq_p, w_k=w_k, w_v=w_v, w_gate=w_gate[l].astype(BF),
                w_branch=w_branch_p, w_out=w_out[l].astype(BF), w_router=w_router_p)


def _kpe_placement():
    e = np.zeros((128, 512), np.float32)
    for h in range(4):
        for i in range(MLA_ROPE):
            e[i, 128 * h + MLA_NOPE + i] = 1.0
    return jnp.asarray(e, BF)


def _mixing(x, mod, lw, lp, lam_init, ctx, tables):
    B, N, D = x.shape
    latent = ctx is not None
    tm = 512 if latent else 256
    out_dtype = BF if latent else F32
    q_a, k_a, v_a, qkv_b, qkv_c, q_d, ckv_n, kpe_r = _in_proj(
        x, mod, lp["g_norm1"], lw["w_in"], lp["mla_q_norm"], lw["w_uq"], lp["mla_kv_norm"],
        tables if latent else None, out_dtype, tm, ctx[0].shape[1] if latent else 0)
    diff = (lp["da_lambda"], jnp.tile(lp["da_subln"], (1, 2)), lam_init)
    if latent:
        ck_a, cv_a, ck_b, cv_b, ck_c, cv_c, ckv_c, kpe_c = ctx
        P = ck_a.shape[1]

        def with_ones(v, heads):
            v = v.reshape(B, P, heads, 64).astype(BF)
            return jnp.concatenate([v, jnp.ones_like(v)], axis=-1).reshape(B, P, heads * LANES)

        put = lambda buf, rows: lax.dynamic_update_slice(buf, rows, (0, N, 0))
        k_a = put(k_a, ck_a.astype(BF))
        v_a = put(v_a, with_ones(cv_a, 4))
        o_a = _dense_attn(q_a, (256, 0), k_a, (256, 0), v_a, (512, 0), groups=GROUPS_A,
                          tq=LAT_TQ, tk=(N + P) // 2, diff=diff, name="diff_attn")
        o_b = _nbr_attn(qkv_b, ck_b.astype(BF), with_ones(cv_b, 4), lp["na_bias"])
        o_c = _window_attn(qkv_c, ck_c.astype(BF), with_ones(cv_c, 2), lp["sw_sink"], 256)
        ckv_all = put(ckv_n, ckv_c.astype(BF))
        kpe_all = put(kpe_r, jnp.pad(kpe_c, ((0, 0), (0, 0), (0, 128 - MLA_ROPE))).astype(BF))
        k_d, v_d = _mla_expand(ckv_all, kpe_all, lw["w_k"], lw["w_v"], lp["e_place"], (N + P) // 2)
        o_d = _dense_attn(q_d, (512, 0), k_d, (512, 0), v_d, (512, 0), groups=GROUPS_D,
                          tq=LAT_TQ, tk=(N + P) // 2, name="mla_attn")
    else:
        o_a = _dense_attn(q_a, (256, 0), k_a, (256, 0), v_a, (512, 0), groups=GROUPS_A,
                          tq=N, tk=N, diff=diff, name="diff_attn_ctx")
        o_b = _dense_attn(qkv_b, (256, 0), qkv_b, (256, 1), qkv_b, (512, 1), groups=GROUPS_B,
                          tq=N, tk=N, name="dense_attn_ctx")
        o_c = _dense_attn(qkv_c, (256, 0), qkv_c, (128, 4), qkv_c, (256, 1), groups=GROUPS_C,
                          tq=N, tk=N, sink=lp["sw_sink"], name="gqa_attn_ctx")
        k_d, v_d = _mla_expand(ckv_n, kpe_r, lw["w_k"], lw["w_v"], lp["e_place"], N)
        o_d = _dense_attn(q_d, (512, 0), k_d, (512, 0), v_d, (512, 0), groups=GROUPS_D,
                          tq=N, tk=N, name="mla_attn_ctx")
    x_new, h2, logits = _merge(x, mod, lp["g_norm1"], lp["g_norm2"], o_a, o_b, o_c, o_d,
                               lw["w_gate"], lw["w_branch"], lw["w_out"], lw["w_router"], tm)
    cache = None
    if not latent:
        strip = lambda v, heads: v.reshape(B, N, heads, LANES)[..., :64]
        cache = (k_a.reshape(B, N, 4, 64), strip(v_a, 4),
                 qkv_b[:, :, 256:512].reshape(B, N, 4, 64), strip(qkv_b[:, :, 512:1024], 4),
                 qkv_c[:, :, 512:640].reshape(B, N, 2, 64), strip(qkv_c[:, :, 256:512], 2),
                 ckv_n, kpe_r[:, :, :MLA_ROPE])
    return x_new, h2, logits, cache


def kernel(x_prompt, x_sample, cache_diff_k, cache_diff_v, cache_na_k, cache_na_v, cache_swa_k, cache_swa_v, cache_mla_ckv, cache_mla_kpe, c, c_ctx, w_mod, b_mod, g_norm1, g_norm2, w_in, da_lambda, da_subln, na_rpb, sw_sink, mla_q_norm, mla_w_uq, mla_kv_norm, mla_w_ukv, w_gate, w_branch, w_out, w_router, w_e_gate, w_e_up, w_e_down, g_final):
    D = D_MODEL
    BS, NS, _ = x_sample.shape
    P = cache_diff_k.shape[2]
    cond = jnp.concatenate([c_ctx[None], c, jnp.zeros((16 - 1 - BS, D), F32)], axis=0)
    mod_all = _modulation(cond, w_mod, b_mod).reshape(DEPTH, 16, 6, D)
    tables = _rope_tables(NS)
    e_place = _kpe_placement()
    gf = g_final.reshape(1, D)
    na_bias = _nbr_bias_tables(na_rpb, NS // GRID_W)
    xp, xs = x_prompt, x_sample
    caches = []
    for l in range(DEPTH):
        lw = _prep_layer(l, w_in, mla_w_uq, mla_w_ukv, w_gate, w_branch, w_out, w_router)
        lp = dict(
            g_norm1=g_norm1[l].reshape(1, D), g_norm2=g_norm2[l].reshape(1, D),
            mla_q_norm=mla_q_norm[l].reshape(1, 256), mla_kv_norm=mla_kv_norm[l].reshape(1, 128),
            da_lambda=da_lambda[l], da_subln=da_subln[l].reshape(1, 64), sw_sink=sw_sink[l],
            e_place=e_place, na_bias=na_bias[l])
        lam_init = 0.8 - 0.6 * math.exp(-0.3 * l)
        final = l == DEPTH - 1
        mod_p, mod_s = mod_all[l, 0:1], mod_all[l, 1:1 + BS]
        xp, h2_p, lg_p, cache_l = _mixing(xp, mod_p, lw, lp, lam_init, None, None)
        caches.append(cache_l)
        ctx = (cache_diff_k[:, l].reshape(BS, P, 256), cache_diff_v[:, l].reshape(BS, P, 256),
               cache_na_k[:, l].reshape(BS, P, 256), cache_na_v[:, l].reshape(BS, P, 256),
               cache_swa_k[:, l].reshape(BS, P, 128), cache_swa_v[:, l].reshape(BS, P, 128),
               cache_mla_ckv[:, l], cache_mla_kpe[:, l])
        xs, h2_s, lg_s, _ = _mixing(xs, mod_s, lw, lp, lam_init, ctx, tables)
        xs, xp = _moe_pair([(xs, h2_s, lg_s, mod_s), (xp, h2_p, lg_p, mod_p)], gf, w_e_gate, w_e_up, w_e_down,
                           l, final)
    outs = tuple(jnp.stack([cl[i] for cl in caches], axis=1) for i in range(8))
    return (xp, xs) + outs
```

```python
import functools
import math

import numpy as np
import jax
import jax.numpy as jnp
from jax import lax
from jax.experimental import pallas as pl
from jax.experimental.pallas import tpu as pltpu

BF = jnp.bfloat16
F32 = jnp.float32

D_MODEL = 1024
DEPTH = 4
GRID_W = 64
ROPE_BASE = 10000.0
NORM_EPS = 1e-6
NEG_INF = -1e30
LOG2E = math.log2(math.e)

DA_QK = 32
NA_WIN_R = 8
NA_WIN_C = 16
SW_WINDOW = 128
MLA_NOPE = 64
MLA_ROPE = 32
N_EXPERTS = 16
EC_CAPACITY = 2

IN_PAD = 3200
LANES = 128

VMEM_BIG = 56 * 1024 * 1024
VMEM_MID = 40 * 1024 * 1024


def _cparams(sem, vmem=None):
    return pltpu.CompilerParams(dimension_semantics=sem, vmem_limit_bytes=vmem)


def _dot(a, b):
    return jnp.dot(a, b, preferred_element_type=F32)


def _dot_nt(a, b):
    return lax.dot_general(a, b, (((1,), (1,)), ((), ())), preferred_element_type=F32)


def _rms(x, g):
    var = jnp.mean(x * x, axis=-1, keepdims=True)
    return x * lax.rsqrt(var + NORM_EPS) * g


def _lane_mask(width, lo, length):
    lane = lax.broadcasted_iota(jnp.int32, (1, width), 1)
    return (lane >= lo) & (lane < lo + length)


def _mod_kernel(c_ref, w_ref, b_ref, o_ref):
    c = c_ref[...]
    s = c * (1.0 / (1.0 + jnp.exp(-c)))
    o_ref[0] = jnp.dot(s, w_ref[0], preferred_element_type=F32, precision=lax.Precision.HIGHEST) + b_ref[0]


def _modulation(cond, w_mod, b_mod):
    R, D = cond.shape
    L = w_mod.shape[0]
    nj = w_mod.shape[2] // D
    return pl.pallas_call(
        _mod_kernel,
        grid=(L, nj),
        in_specs=[
            pl.BlockSpec((R, D), lambda l, j: (0, 0)),
            pl.BlockSpec((1, D, D), lambda l, j: (l, 0, j)),
            pl.BlockSpec((1, 1, D), lambda l, j: (l, 0, j)),
        ],
        out_specs=pl.BlockSpec((1, R, D), lambda l, j: (l, 0, j)),
        out_shape=jax.ShapeDtypeStruct((L, R, nj * D), F32),
        compiler_params=_cparams(("parallel", "parallel")),
        name="modulation",
    )(cond, w_mod, b_mod.reshape(L, 1, nj * D))


def _rope(x, c, s, d):
    w = x.shape[-1]
    lane = lax.broadcasted_iota(jnp.int32, (1, LANES), 1)
    first = (lane % (2 * d)) < d
    outs = []
    for b in range(w // LANES):
        sl = slice(b * LANES, (b + 1) * LANES)
        xb = x[:, sl]
        partner = jnp.where(first, pltpu.roll(xb, LANES - d, 1), pltpu.roll(xb, d, 1))
        outs.append(xb * c[:, sl] + partner * s[:, sl])
    return outs[0] if len(outs) == 1 else jnp.concatenate(outs, axis=-1)


def _inproj_kernel(*refs, rope):
    if rope:
        (x_ref, mod_ref, g1_ref, w_ref, qn_ref, wuq_ref, kvn_ref,
         tac, tas, tcc, tcs, tqc, tqs, tkc, tks, _, _, _, _,
         oaq, oak, oav, ob, oc, oq, ockv, okpe) = refs
    else:
        (x_ref, mod_ref, g1_ref, w_ref, qn_ref, wuq_ref, kvn_ref,
         oaq, oak, oav, ob, oc, oq, ockv, okpe) = refs
    dt = oaq.dtype
    x = x_ref[0]
    h = (_rms(x, g1_ref[...]) * (1.0 + mod_ref[0, 1:2, :]) + mod_ref[0, 0:1, :]).astype(BF)

    def seg(lo, w):
        return _dot(h, w_ref[:, lo:lo + w])

    ua = seg(0, 1024)
    qk = ua[:, :512]
    if rope:
        qk = _rope(qk, tac[...], tas[...], 8)
    oaq[0] = (qk[:, :256] * (DA_QK ** -0.5 * LOG2E)).astype(dt)
    oak[0] = qk[:, 256:].astype(dt)
    ones_lane = lax.broadcasted_iota(jnp.int32, (1, 512), 1) % LANES >= 64
    oav[0] = jnp.where(ones_lane, 1.0, ua[:, 512:]).astype(dt)
    ub = seg(1024, 1024)
    ob[0, :, 0:256] = (ub[:, :256] * (0.125 * LOG2E)).astype(dt)
    ob[0, :, 256:512] = ub[:, 256:512].astype(dt)
    ob[0, :, 512:1024] = jnp.where(ones_lane, 1.0, ub[:, 512:]).astype(dt)
    uc = seg(2048, 640)
    qk = jnp.concatenate([uc[:, :256], uc[:, 512:]], axis=-1)
    if rope:
        qk = _rope(qk, tcc[...], tcs[...], 16)
    oc[0, :, 0:256] = (qk[:, :256] * (0.125 * LOG2E)).astype(dt)
    oc[0, :, 256:512] = jnp.where(ones_lane[:, :256], 1.0, uc[:, 256:512]).astype(dt)
    oc[0, :, 512:640] = qk[:, 256:].astype(dt)
    cq = _rms(seg(2688, 256), qn_ref[...]).astype(BF)
    qd = _dot(cq, wuq_ref[...])
    if rope:
        qd = _rope(qd, tqc[...], tqs[...], 8)
    oq[0] = (qd * ((MLA_NOPE + MLA_ROPE) ** -0.5 * LOG2E)).astype(dt)
    ockv[0] = _rms(seg(2944, 128), kvn_ref[...]).astype(dt)
    kpe = seg(3072, 128)
    if rope:
        kpe = _rope(kpe, tkc[...], tks[...], 8)
    okpe[0] = kpe.astype(dt)


def _in_proj(x, mod, g1, w_in_p, q_norm, w_uq_p, kv_norm, tables, ctx_tail, out_dtype, tm):
    B, N, D = x.shape
    rope = tables is not None
    ctx_rows = ctx_tail[0].shape[1] if rope else 0
    nmod = mod.shape[0]
    mod_map = (lambda j, b: (b, 0, 0)) if nmod > 1 else (lambda j, b: (0, 0, 0))
    const2 = lambda j, b: (0, 0)
    in_specs = [
        pl.BlockSpec((1, tm, D), lambda j, b: (b, j, 0)),
        pl.BlockSpec((1, 6, D), mod_map),
        pl.BlockSpec((1, D), const2),
        pl.BlockSpec((D, IN_PAD), const2),
        pl.BlockSpec((1, 256), const2),
        pl.BlockSpec((256, 512), const2),
        pl.BlockSpec((1, 128), const2),
    ]
    args = [x, mod, g1, w_in_p, q_norm, w_uq_p, kv_norm]
    if rope:
        for t in tables:
            in_specs.append(pl.BlockSpec((tm, t.shape[1]), lambda j, b: (j, 0)))
            args.append(t)
        for t in ctx_tail:
            in_specs.append(pl.BlockSpec(memory_space=pl.ANY))
            args.append(jnp.pad(t, ((0, 0), (N, 0), (0, 0))))
    outs = ((256, N), (256, N + ctx_rows), (512, N + ctx_rows), (1024, N), (640, N), (512, N),
            (128, N + ctx_rows), (128, N + ctx_rows))
    out_specs = [pl.BlockSpec((1, tm, w), lambda j, b: (b, j, 0)) for w, _ in outs]
    out_shape = [jax.ShapeDtypeStruct((B, rows, w), out_dtype) for w, rows in outs]
    return pl.pallas_call(
        functools.partial(_inproj_kernel, rope=rope),
        grid=(N // tm, B),
        in_specs=in_specs,
        out_specs=out_specs,
        out_shape=out_shape,
        input_output_aliases={len(args) - 4 + i: o for i, o in enumerate((1, 2, 6, 7))} if rope else {},
        compiler_params=_cparams(("parallel", "parallel"), VMEM_BIG),
        name="in_proj_rope" if rope else "in_proj",
    )(*args)


def _mla_expand_kernel(ckv_ref, kpe_ref, wk_ref, wv_ref, e_ref, ok, ov):
    ckv = ckv_ref[0].astype(BF)
    kpe = kpe_ref[0].astype(BF)
    ok[0] = (_dot(ckv, wk_ref[...]) + _dot(kpe, e_ref[...])).astype(ok.dtype)
    ones_lane = lax.broadcasted_iota(jnp.int32, (1, 512), 1) % LANES >= 64
    ov[0] = jnp.where(ones_lane, 1.0, _dot(ckv, wv_ref[...])).astype(ov.dtype)


def _mla_expand(ckv, kpe, w_k, w_v, e_place, tk):
    B, K, _ = ckv.shape
    const2 = lambda b, j: (0, 0)
    return pl.pallas_call(
        _mla_expand_kernel,
        grid=(B, K // tk),
        in_specs=[
            pl.BlockSpec((1, tk, 128), lambda b, j: (b, j, 0)),
            pl.BlockSpec((1, tk, 128), lambda b, j: (b, j, 0)),
            pl.BlockSpec((128, 512), const2),
            pl.BlockSpec((128, 512), const2),
            pl.BlockSpec((128, 512), const2),
        ],
        out_specs=[pl.BlockSpec((1, tk, 512), lambda b, j: (b, j, 0)),
                   pl.BlockSpec((1, tk, 512), lambda b, j: (b, j, 0))],
        out_shape=[jax.ShapeDtypeStruct((B, K, 512), BF), jax.ShapeDtypeStruct((B, K, 512), BF)],
        compiler_params=_cparams(("parallel", "parallel")),
        name="mla_expand",
    )(ckv, kpe, w_k, w_v, e_place)


def _head_softmax(scores, values):
    m = functools.reduce(jnp.maximum, [jnp.max(s, axis=-1, keepdims=True) for s in scores])
    acc = None
    for s, v in zip(scores, values):
        part = _dot(jnp.exp2(s - m).astype(BF), v)
        acc = part if acc is None else acc + part
    return acc, m


def _normalise(acc):
    return acc * pltpu.roll(1.0 / acc, 64, 1)


def _store_heads(o_ref, heads):
    half = _lane_mask(LANES, 0, 64)
    for hp in range(len(heads) // 2):
        slab = jnp.where(half, heads[2 * hp], pltpu.roll(heads[2 * hp + 1], 64, 1))
        o_ref[0, :, hp * LANES:(hp + 1) * LANES] = slab.astype(o_ref.dtype)


def _dense_attn_kernel(*refs, groups, tk, n_chunks, has_sink, diff):
    it = iter(refs)
    q_ref, k_ref, v_ref = next(it), next(it), next(it)
    sink_ref = lam_ref = subln_ref = None
    if has_sink:
        sink_ref = next(it)
    if diff:
        lam_ref, subln_ref, lam_init_ref = next(it), next(it), next(it)
    o_ref = next(it)
    tq = q_ref.shape[1]

    lam = lam_init = None
    if diff:
        lam_init = lam_init_ref[0]
        dl = lam_ref[...]
        lam = (jnp.exp(jnp.sum(dl[0:1] * dl[1:2], axis=-1, keepdims=True))
               - jnp.exp(jnp.sum(dl[2:3] * dl[3:4], axis=-1, keepdims=True)) + lam_init)

    ng = len(groups)
    scratch = list(it)
    qg_s, m_s, acc_s = (scratch[j * ng:(j + 1) * ng] for j in range(3))
    for gi, (q_lo, k_lo, w, qmask, v_lo, head, sink_idx, sign) in enumerate(groups):
        qg = q_ref[0, :, q_lo:q_lo + w].astype(BF)
        if qmask is not None:
            qg = jnp.where(_lane_mask(w, *qmask), qg, jnp.zeros_like(qg))
        qg_s[gi][...] = qg
        m_s[gi][...] = jnp.full((tq, 1), NEG_INF, F32)
        acc_s[gi][...] = jnp.zeros((tq, LANES), F32)

    def keys_step(c):
        off = c * tk
        for gi, (q_lo, k_lo, w, qmask, v_lo, head, sink_idx, sign) in enumerate(groups):
            s = _dot_nt(qg_s[gi][...], k_ref[0, pl.ds(off, tk), k_lo:k_lo + w].astype(BF))
            m = m_s[gi][...]
            m_new = jnp.maximum(m, jnp.max(s, axis=-1, keepdims=True))
            p = jnp.exp2(s - m_new).astype(BF)
            pv = _dot(p, v_ref[0, pl.ds(off, tk), v_lo:v_lo + LANES].astype(BF))
            acc_s[gi][...] = jnp.exp2(m - m_new) * acc_s[gi][...] + pv
            m_s[gi][...] = m_new

    for c in range(n_chunks):
        keys_step(c)

    heads = {}
    ones_half = _lane_mask(LANES, 64, 64)
    for gi, (q_lo, k_lo, w, qmask, v_lo, head, sink_idx, sign) in enumerate(groups):
        acc = acc_s[gi][...]
        if sink_idx is not None:
            acc = acc + jnp.where(ones_half, jnp.exp2(sink_ref[sink_idx] * LOG2E - m_s[gi][...]), 0.0)
        part = _normalise(acc)
        if sign < 0:
            part = -lam * part
        heads[head] = part if head not in heads else heads[head] + part
    if diff:
        half = _lane_mask(LANES, 0, 64)
        for hd, o in heads.items():
            ms = jnp.sum(jnp.where(half, o * o, 0.0), axis=-1, keepdims=True) * (1.0 / 64.0)
            heads[hd] = o * lax.rsqrt(ms + NORM_EPS) * subln_ref[...] * (1.0 - lam_init)
    _store_heads(o_ref, heads)


def _dense_attn(q_arr, q_blk, k_arr, k_blk, v_arr, v_blk, *, groups, tq, tk, sink=None, diff=None, name):
    B, N, _ = q_arr.shape
    K = k_arr.shape[1]
    n_chunks = K // tk
    assert n_chunks * tk == K and n_chunks <= 2
    qw, qi = q_blk
    kw, ki = k_blk
    vw, vi = v_blk
    in_specs = [
        pl.BlockSpec((1, tq, qw), lambda b, i: (b, i, qi)),
        pl.BlockSpec((1, K, kw), lambda b, i: (b, 0, ki)),
        pl.BlockSpec((1, K, vw), lambda b, i: (b, 0, vi)),
    ]
    args = [q_arr, k_arr, v_arr]
    if sink is not None:
        in_specs.append(pl.BlockSpec(memory_space=pltpu.SMEM))
        args.append(sink)
    if diff is not None:
        da_lambda, subln, lam_init = diff
        in_specs += [pl.BlockSpec((4, DA_QK), lambda b, i: (0, 0)),
                     pl.BlockSpec((1, LANES), lambda b, i: (0, 0)),
                     pl.BlockSpec(memory_space=pltpu.SMEM)]
        args += [da_lambda, subln, jnp.full((1,), lam_init, F32)]
    kern = functools.partial(_dense_attn_kernel, groups=groups, tk=tk, n_chunks=n_chunks,
                             has_sink=sink is not None, diff=diff is not None)
    scratch = ([pltpu.VMEM((tq, g[2]), BF) for g in groups] + [pltpu.VMEM((tq, 1), F32) for _ in groups]
               + [pltpu.VMEM((tq, LANES), F32) for _ in groups])
    return pl.pallas_call(
        kern,
        grid=(B, N // tq),
        in_specs=in_specs,
        out_specs=pl.BlockSpec((1, tq, 256), lambda b, i: (b, i, 0)),
        out_shape=jax.ShapeDtypeStruct((B, N, 256), BF),
        scratch_shapes=scratch,
        compiler_params=_cparams(("parallel", "parallel"), VMEM_BIG),
        name=name,
    )(*args)


GROUPS_A = tuple((128 * (g // 4), 128 * (g // 4), 128, (32 * (g % 4), 32), 128 * (g // 2), g // 2, None,
                  1 if g % 2 == 0 else -1) for g in range(8))
GROUPS_B = tuple((128 * (h // 2), 128 * (h // 2), 128, (64 * (h % 2), 64), 128 * h, h, None, 1) for h in range(4))
GROUPS_C = tuple((128 * g, 0, 128, (64 * j, 64), 128 * j, 2 * g + j, 2 * j + g, 1) for g in range(2) for j in range(2))
GROUPS_D = tuple((128 * h, 128 * h, 128, None, 128 * h, h, None, 1) for h in range(4))


def _window_attn_kernel(q_ref, k_ref, v_ref, ck_ref, cv_ref, sink_ref, o_ref, *, n, band):
    tq = q_ref.shape[1]
    i = pl.program_id(1)
    start = jnp.clip(i * tq - SW_WINDOW, 0, n - band)
    start = pl.multiple_of(start, 128)
    kb = k_ref[0, pl.ds(start, band), :]
    vb = v_ref[0, pl.ds(start, band), :]
    ck = ck_ref[0]
    cv = cv_ref[0]
    qpos = i * tq + lax.broadcasted_iota(jnp.int32, (tq, band), 0)
    kpos = start + lax.broadcasted_iota(jnp.int32, (tq, band), 1)
    valid = jnp.abs(qpos - kpos) <= SW_WINDOW
    ones_half = _lane_mask(LANES, 64, 64)
    heads = {}
    for g in range(2):
        for j in range(2):
            qg = q_ref[0, :, LANES * g:LANES * (g + 1)]
            qg = jnp.where(_lane_mask(LANES, 64 * j, 64), qg, jnp.zeros_like(qg))
            vs = slice(LANES * j, LANES * (j + 1))
            acc, m = _head_softmax([_dot_nt(qg, ck), jnp.where(valid, _dot_nt(qg, kb), NEG_INF)],
                                   [cv[:, vs], vb[:, vs]])
            acc = acc + jnp.where(ones_half, jnp.exp2(sink_ref[2 * j + g] * LOG2E - m), 0.0)
            heads[2 * g + j] = _normalise(acc)
    _store_heads(o_ref, heads)


def _window_attn(qkv_c, ck, cv, sink, tq):
    B, N, _ = qkv_c.shape
    band = tq + 2 * SW_WINDOW
    P = ck.shape[1]
    return pl.pallas_call(
        functools.partial(_window_attn_kernel, n=N, band=band),
        grid=(B, N // tq),
        in_specs=[
            pl.BlockSpec((1, tq, 256), lambda b, i: (b, i, 0)),
            pl.BlockSpec((1, N, 128), lambda b, i: (b, 0, 4)),
            pl.BlockSpec((1, N, 256), lambda b, i: (b, 0, 1)),
            pl.BlockSpec((1, P, 128), lambda b, i: (b, 0, 0)),
            pl.BlockSpec((1, P, 256), lambda b, i: (b, 0, 0)),
            pl.BlockSpec(memory_space=pltpu.SMEM),
        ],
        out_specs=pl.BlockSpec((1, tq, 256), lambda b, i: (b, i, 0)),
        out_shape=jax.ShapeDtypeStruct((B, N, 256), BF),
        compiler_params=_cparams(("parallel", "parallel")),
        name="window_attn",
    )(qkv_c, qkv_c, qkv_c, ck, cv, sink)


NA_TILE_ROWS = 4
NA_KEY_ROWS = NA_TILE_ROWS + NA_WIN_R


def _nbr_attn_kernel(q_ref, k_ref, v_ref, ck_ref, cv_ref, bias_ref, o_ref, *, rows):
    i = pl.program_id(1)
    krow = jnp.clip(i * NA_TILE_ROWS - NA_WIN_R // 2, 0, rows - NA_KEY_ROWS)
    start = pl.multiple_of(krow * GRID_W, GRID_W)
    nk = NA_KEY_ROWS * GRID_W
    kb = k_ref[0, pl.ds(start, nk), :]
    vb = v_ref[0, pl.ds(start, nk), :]
    ck = ck_ref[0]
    cv = cv_ref[0]
    q = q_ref[0]
    heads = {}
    for h in range(4):
        ks = slice(LANES * (h // 2), LANES * (h // 2 + 1))
        qg = jnp.where(_lane_mask(LANES, 64 * (h % 2), 64), q[:, ks], jnp.zeros_like(q[:, ks]))
        vs = slice(LANES * h, LANES * (h + 1))
        acc, _ = _head_softmax([_dot_nt(qg, ck[:, ks]), _dot_nt(qg, kb[:, ks]) + bias_ref[0, h]],
                               [cv[:, vs], vb[:, vs]])
        heads[h] = _normalise(acc)
    _store_heads(o_ref, heads)


def _nbr_attn(qkv_b, ck, cv, bias):
    B, N, _ = qkv_b.shape
    rows = N // GRID_W
    tq = NA_TILE_ROWS * GRID_W
    nt = N // tq
    nk = NA_KEY_ROWS * GRID_W
    P = ck.shape[1]

    def bias_map(b, i):
        return (jnp.where(i == 0, 0, jnp.where(i == nt - 1, 2, 1)), 0, 0, 0)

    return pl.pallas_call(
        functools.partial(_nbr_attn_kernel, rows=rows),
        grid=(B, nt),
        in_specs=[
            pl.BlockSpec((1, tq, 256), lambda b, i: (b, i, 0)),
            pl.BlockSpec((1, N, 256), lambda b, i: (b, 0, 1)),
            pl.BlockSpec((1, N, 512), lambda b, i: (b, 0, 1)),
            pl.BlockSpec((1, P, 256), lambda b, i: (b, 0, 0)),
            pl.BlockSpec((1, P, 512), lambda b, i: (b, 0, 0)),
            pl.BlockSpec((1, 4, tq, nk), bias_map),
        ],
        out_specs=pl.BlockSpec((1, tq, 256), lambda b, i: (b, i, 0)),
        out_shape=jax.ShapeDtypeStruct((B, N, 256), BF),
        compiler_params=_cparams(("parallel", "parallel"), VMEM_MID),
        name="nbr_attn",
    )(qkv_b, qkv_b, qkv_b, ck, cv, bias)


def _nbr_bias_tables(rpb, rows):
    L, H = rpb.shape[:2]
    rpb = rpb.reshape((L * H,) + rpb.shape[2:])
    rp = jnp.pad(rpb.astype(F32), ((0, 0), (NA_KEY_ROWS, NA_KEY_ROWS), (GRID_W, GRID_W)))
    c0 = NA_WIN_C - 1 + GRID_W
    tcol = jnp.stack([rp[:, :, c0 - qc:c0 - qc + GRID_W] for qc in range(GRID_W)], axis=2)
    tabs = []
    nt = rows // NA_TILE_ROWS
    for r0 in (0, NA_TILE_ROWS * min(1, nt - 1) + NA_TILE_ROWS, rows - NA_TILE_ROWS):
        ks = int(np.clip(r0 - NA_WIN_R // 2, 0, rows - NA_KEY_ROWS))
        per_row = []
        for a in range(NA_TILE_ROWS):
            s0 = ks - (r0 + a) + NA_WIN_R - 1 + NA_KEY_ROWS
            per_row.append(tcol[:, s0:s0 + NA_KEY_ROWS])
        t = jnp.stack(per_row, axis=1).transpose(0, 1, 3, 2, 4)
        r = r0 + np.arange(NA_TILE_ROWS)[:, None, None, None]
        qc = np.arange(GRID_W)[None, :, None, None]
        kr = ks + np.arange(NA_KEY_ROWS)[None, None, :, None]
        kc = np.arange(GRID_W)[None, None, None, :]
        rs = np.clip(r - NA_WIN_R // 2, 0, rows - NA_WIN_R)
        cs = np.clip(qc - NA_WIN_C // 2, 0, GRID_W - NA_WIN_C)
        ok = (kr >= rs) & (kr < rs + NA_WIN_R) & (kc >= cs) & (kc < cs + NA_WIN_C)
        t = jnp.where(ok[None], t * LOG2E, NEG_INF)
        tabs.append(t.reshape(L, H, NA_TILE_ROWS * GRID_W, NA_KEY_ROWS * GRID_W))
    return jnp.stack(tabs, axis=1)


def _merge_kernel(x_ref, mod_ref, g1_ref, g2_ref, oa, ob, oc, od, wg_ref, wb_ref, wo_ref, wr_ref,
                  xo_ref, h2_ref, lg_ref):
    D = x_ref.shape[2]
    x = x_ref[0]
    h = (_rms(x, g1_ref[...]) * (1.0 + mod_ref[0, 1:2, :]) + mod_ref[0, 0:1, :]).astype(BF)
    merged = None
    for i, o_ref in enumerate((oa, ob, oc, od)):
        z = _dot(h, wg_ref[:, i * D:(i + 1) * D])
        gate = 1.0 / (1.0 + jnp.exp(-z))
        term = gate * _dot(o_ref[0], wb_ref[i])
        merged = term if merged is None else merged + term
    mixed = _dot(merged.astype(BF), wo_ref[...])
    xn = x + mod_ref[0, 2:3, :] * mixed
    xo_ref[0] = xn
    h2 = (_rms(xn, g2_ref[...]) * (1.0 + mod_ref[0, 4:5, :]) + mod_ref[0, 3:4, :]).astype(BF)
    h2_ref[0] = h2
    lg_ref[0] = _dot_nt(wr_ref[...], h2)


def _merge(x, mod, g1, g2, o_a, o_b, o_c, o_d, w_gate, w_branch, w_out, w_router_p, tm):
    B, N, D = x.shape
    nmod = mod.shape[0]
    mod_map = (lambda b, j: (b, 0, 0)) if nmod > 1 else (lambda b, j: (0, 0, 0))
    tok = lambda w: pl.BlockSpec((1, tm, w), lambda b, j: (b, j, 0))
    c2 = lambda b, j: (0, 0)
    return pl.pallas_call(
        _merge_kernel,
        grid=(B, N // tm),
        in_specs=[
            tok(D), pl.BlockSpec((1, 6, D), mod_map), pl.BlockSpec((1, D), c2), pl.BlockSpec((1, D), c2),
            tok(256), tok(256), tok(256), tok(256),
            pl.BlockSpec((D, 4 * D), c2),
            pl.BlockSpec((4, 256, D), lambda b, j: (0, 0, 0)),
            pl.BlockSpec((D, D), c2),
            pl.BlockSpec((N_EXPERTS, D), c2),
        ],
        out_specs=[tok(D), tok(D), pl.BlockSpec((1, N_EXPERTS, tm), lambda b, j: (b, 0, j))],
        out_shape=[jax.ShapeDtypeStruct((B, N, D), F32), jax.ShapeDtypeStruct((B, N, D), BF),
                   jax.ShapeDtypeStruct((B, N_EXPERTS, N), F32)],
        compiler_params=_cparams(("parallel", "parallel"), VMEM_BIG),
        name="merge",
    )(x, mod, g1, g2, o_a, o_b, o_c, o_d, w_gate, w_branch, w_out, w_router_p)


MOE_TT = 256
MOE_RC = 64
SLOT_ALIGN = 16


def _excl_cumsum(x, tri, reset_every):
    R, N = x.shape
    outs = []
    carry = jnp.zeros((R, 1), F32)
    for c in range(N // LANES):
        if reset_every and (c * LANES) % reset_every == 0:
            carry = jnp.zeros((R, 1), F32)
        xc = x[:, c * LANES:(c + 1) * LANES]
        inc = _dot(xc.astype(BF), tri)
        outs.append(inc - xc + carry)
        carry = carry + inc[:, LANES - 1:LANES]
    return (outs[0] if len(outs) == 1 else jnp.concatenate(outs, axis=1)), carry


def _route_kernel(lg_ref, aff_ref, rel_ref, base_ref, *, cap, tt):
    B, E, N = lg_ref.shape
    R = B * E
    nt = N // tt
    lg = lg_ref[...]
    ex = jnp.exp(lg - jnp.max(lg, axis=1, keepdims=True))
    aff = ex / jnp.sum(ex, axis=1, keepdims=True)
    aff_ref[...] = aff
    bits = pltpu.bitcast(aff.reshape(R, N), jnp.int32)

    def per_expert(v):
        t = v[0:E]
        for b in range(1, B):
            t = t + v[b * E:(b + 1) * E]
        return t

    def per_row(v):
        return jnp.concatenate([v] * B, axis=0) if B > 1 else v

    def batch_prefix(v):
        parts, run = [], jnp.zeros((E, 1), F32)
        for b in range(B):
            parts.append(run)
            run = run + v[b * E:(b + 1) * E]
        return jnp.concatenate(parts, axis=0) if B > 1 else parts[0]

    def count(mask):
        return per_expert(jnp.sum(jnp.where(mask, 1.0, 0.0), axis=1, keepdims=True))

    def search(i, thr):
        cand = thr | jnp.left_shift(jnp.int32(1), 30 - i)
        return jnp.where(count(bits >= per_row(cand)) >= float(cap), cand, thr)

    thr = per_row(lax.fori_loop(0, 31, search, jnp.zeros((E, 1), jnp.int32)))
    gt = bits > thr
    eq = bits == thr
    need = float(cap) - count(gt)
    ri = lax.broadcasted_iota(jnp.int32, (LANES, LANES), 0)
    ci = lax.broadcasted_iota(jnp.int32, (LANES, LANES), 1)
    tri = jnp.where(ri <= ci, 1.0, 0.0).astype(BF)
    rank_eq, eq_tot = _excl_cumsum(jnp.where(eq, 1.0, 0.0), tri, None)
    rank_eq = rank_eq + batch_prefix(eq_tot)
    sel = gt | (eq & (rank_eq < per_row(need)))
    self = jnp.where(sel, 1.0, 0.0)
    local, _ = _excl_cumsum(self, tri, tt)
    rel_ref[...] = jnp.where(sel, local, -1.0).astype(jnp.int32).reshape(B, E, N)
    tn = jnp.right_shift(lax.broadcasted_iota(jnp.int32, (N, LANES), 0), tt.bit_length() - 1)
    tj = lax.broadcasted_iota(jnp.int32, (N, LANES), 1)
    cnt = _dot(self.astype(BF), jnp.where(tn == tj, 1.0, 0.0).astype(BF)).astype(jnp.int32)
    sh = SLOT_ALIGN.bit_length() - 1
    ru = jnp.left_shift(jnp.right_shift(cnt + (SLOT_ALIGN - 1), sh), sh).astype(F32)
    base_b = _dot(ru.astype(BF), jnp.where(ri < ci, 1.0, 0.0).astype(BF))
    base = base_b + batch_prefix(base_b[:, nt:nt + 1])
    base_ref[...] = base.astype(jnp.int32).reshape(B, E, LANES)


def _route(lg_t, cap):
    B, E, N = lg_t.shape
    full = lambda s: pl.BlockSpec(s, lambda i: (0,) * len(s))
    return pl.pallas_call(
        functools.partial(_route_kernel, cap=cap, tt=MOE_TT),
        grid=(1,),
        in_specs=[full((B, E, N))],
        out_specs=[full((B, E, N)), full((B, E, N)), full((B, E, LANES))],
        out_shape=[jax.ShapeDtypeStruct((B, E, N), F32), jax.ShapeDtypeStruct((B, E, N), jnp.int32),
                   jax.ShapeDtypeStruct((B, E, LANES), jnp.int32)],
        compiler_params=_cparams(("arbitrary",), VMEM_BIG),
        name="route",
    )(lg_t)


def _slot_onehot(rel, values, c, rc):
    E, tt = rel.shape
    r = lax.broadcasted_iota(jnp.int32, (E, rc, tt), 1) + c * rc
    return jnp.where(rel[:, None, :] == r, values[:, None, :], 0.0).astype(BF).reshape(E * rc, tt)


def _dispatch_kernel(base_sm, x_ref, rel_ref, xe_in, xe_hbm, stage, stage2, sem, sem2, *, nt, rc, row_off):
    del xe_in
    b, j = pl.program_id(0), pl.program_id(1)
    E = rel_ref.shape[1]
    step = b * nt + j
    last = pl.num_programs(0) * nt - 1
    base = lambda e, jj: base_sm[(b * E + e) * (nt + 1) + jj]
    rel = rel_ref[0]
    x = x_ref[0]
    ones = jnp.ones(rel.shape, F32)

    def copy(buf, s, e, c):
        start = pl.multiple_of(row_off + base(e, j) + c * rc, SLOT_ALIGN)
        return pltpu.make_async_copy(buf.at[pl.ds(e * rc, rc)], xe_hbm.at[e, pl.ds(start, rc)], s.at[e])

    strip = _dot(_slot_onehot(rel, ones, 0, rc), x).astype(BF)

    @pl.when(step > 0)
    def _():
        for e in range(E):
            copy(stage, sem, e, 0).wait()

    stage[...] = strip
    for e in range(E):
        copy(stage, sem, e, 0).start()

    span = [base(e, j + 1) - base(e, j) for e in range(E)]
    widest = functools.reduce(jnp.maximum, span)

    def extra(c, carry):
        stage2[...] = _dot(_slot_onehot(rel, ones, c, rc), x).astype(BF)
        for e in range(E):
            @pl.when(span[e] > c * rc)
            def _():
                cp = copy(stage2, sem2, e, c)
                cp.start()
                cp.wait()
        return carry

    lax.fori_loop(1, (widest + rc - 1) // rc, extra, 0)

    @pl.when(step == last)
    def _():
        for e in range(E):
            copy(stage, sem, e, 0).wait()


def _dispatch(h2, rel, base_flat, xe, row_off):
    B, N, D = h2.shape
    E, rows, _ = xe.shape
    nt = N // MOE_TT
    return pl.pallas_call(
        functools.partial(_dispatch_kernel, nt=nt, rc=MOE_RC, row_off=row_off),
        grid_spec=pltpu.PrefetchScalarGridSpec(
            num_scalar_prefetch=1,
            grid=(B, nt),
            in_specs=[
                pl.BlockSpec((1, MOE_TT, D), lambda b, j, s: (b, j, 0)),
                pl.BlockSpec((1, E, MOE_TT), lambda b, j, s: (b, 0, j)),
                pl.BlockSpec(memory_space=pl.ANY),
            ],
            out_specs=pl.BlockSpec(memory_space=pl.ANY),
            scratch_shapes=[pltpu.VMEM((E * MOE_RC, D), BF), pltpu.VMEM((E * MOE_RC, D), BF),
                            pltpu.SemaphoreType.DMA((E,)), pltpu.SemaphoreType.DMA((E,))],
        ),
        out_shape=jax.ShapeDtypeStruct((E, rows, D), BF),
        input_output_aliases={3: 0},
        compiler_params=_cparams(("arbitrary", "arbitrary")),
        name="dispatch",
    )(base_flat, h2, rel, xe)


def _ffn_kernel(tot_sm, x_ref, wg_hbm, wu_hbm, wd_hbm, y_ref, wbuf, wg_s, wu_s, wd_s, sem, *, tm, rc, layer,
                n_first):
    e, j = pl.program_id(0), pl.program_id(1)
    slot = e % 2

    def copy(ee, sl, k):
        return pltpu.make_async_copy((wg_hbm, wu_hbm, wd_hbm)[k].at[layer, ee], wbuf.at[sl, k], sem.at[sl, k])

    @pl.when(j == 0)
    def _():
        @pl.when(e == 0)
        def _():
            for k in range(3):
                copy(e, slot, k).start()

        @pl.when(e + 1 < pl.num_programs(0))
        def _():
            for k in range(3):
                copy(e + 1, 1 - slot, k).start()

        for k in range(3):
            copy(e, slot, k).wait()
        wg_s[...] = wbuf[slot, 0].astype(BF)
        wu_s[...] = wbuf[slot, 1].astype(BF)
        wd_s[...] = wbuf[slot, 2].astype(BF)

    second = j >= n_first
    jj = jnp.where(second, j - n_first, j)
    used = jj * tm < tot_sm[jnp.where(second, pl.num_programs(0) + e, e)] + rc

    @pl.when(used)
    def _():
        x = x_ref[0]
        hg = _dot(x, wg_s[...])
        hu = _dot(x, wu_s[...])
        act = (hg * (1.0 / (1.0 + jnp.exp(-hg))) * hu).astype(BF)
        y_ref[0] = _dot(act, wd_s[...]).astype(y_ref.dtype)

    @pl.when(jnp.logical_not(used))
    def _():
        y_ref[...] = jnp.zeros_like(y_ref)


def _expert_ffn(xe, totals, rows_first, w_g, w_u, w_d, layer, tm):
    E, rows, D = xe.shape
    DE = w_g.shape[3]
    n_first = rows_first // tm

    def tok(e, j, tot):
        second = j >= n_first
        last = (tot[jnp.where(second, E + e, e)] + MOE_RC - 1) // tm
        return (e, jnp.where(second, n_first + jnp.minimum(j - n_first, last), jnp.minimum(j, last)), 0)

    assert D == DE
    hbm = pl.BlockSpec(memory_space=pl.ANY)
    return pl.pallas_call(
        functools.partial(_ffn_kernel, tm=tm, rc=MOE_RC, layer=layer, n_first=n_first),
        grid_spec=pltpu.PrefetchScalarGridSpec(
            num_scalar_prefetch=1,
            grid=(E, rows // tm),
            in_specs=[pl.BlockSpec((1, tm, D), tok), hbm, hbm, hbm],
            out_specs=pl.BlockSpec((1, tm, D), lambda e, j, tot: (e, j, 0)),
            scratch_shapes=[pltpu.VMEM((2, 3, D, DE), F32), pltpu.VMEM((D, DE), BF), pltpu.VMEM((D, DE), BF),
                            pltpu.VMEM((DE, D), BF), pltpu.SemaphoreType.DMA((2, 3))],
        ),
        out_shape=jax.ShapeDtypeStruct((E, rows, D), BF),
        compiler_params=_cparams(("arbitrary", "arbitrary"), VMEM_BIG),
        name="expert_ffn",
    )(totals, xe, w_g, w_u, w_d)


def _combine_kernel(base_sm, x_ref, rel_ref, aff_ref, mod_ref, gf_ref, ye_hbm, o_ref, stage, sem, *,
                    nt, rc, final, row_off):
    b, j = pl.program_id(0), pl.program_id(1)
    E = rel_ref.shape[1]
    step = b * nt + j
    nsteps = pl.num_programs(0) * nt
    slot = step % 2
    base = lambda e, jj: base_sm[(b * E + e) * (nt + 1) + jj]
    rel = rel_ref[0]
    aff = aff_ref[0]

    def copy(st, sl, e, c):
        first = base_sm[(st // nt * E + e) * (nt + 1) + st % nt]
        start = pl.multiple_of(row_off + first + c * rc, SLOT_ALIGN)
        return pltpu.make_async_copy(ye_hbm.at[e, pl.ds(start, rc)], stage.at[sl, pl.ds(e * rc, rc)],
                                     sem.at[sl, e])

    def scatter(c):
        w = _slot_onehot(rel, aff, c, rc)
        return lax.dot_general(w, stage[slot], (((0,), (0,)), ((), ())), preferred_element_type=F32)

    @pl.when(step == 0)
    def _():
        for e in range(E):
            copy(step, slot, e, 0).start()

    @pl.when(step + 1 < nsteps)
    def _():
        for e in range(E):
            copy(step + 1, 1 - slot, e, 0).start()

    for e in range(E):
        copy(step, slot, e, 0).wait()
    acc = scatter(0)
    span = [base(e, j + 1) - base(e, j) for e in range(E)]
    widest = functools.reduce(jnp.maximum, span)

    def extra(c, acc):
        for e in range(E):
            @pl.when(span[e] > c * rc)
            def _():
                cp = copy(step, slot, e, c)
                cp.start()
                cp.wait()
        return acc + scatter(c)

    acc = lax.fori_loop(1, (widest + rc - 1) // rc, extra, acc)
    xn = x_ref[0] + mod_ref[0, 5:6, :] * acc
    if final:
        xn = _rms(xn, gf_ref[...])
    o_ref[0] = xn


def _combine(x, rel, aff, mod, g_final, ye, base_flat, row_off, final):
    B, N, D = x.shape
    E = rel.shape[1]
    nt = N // MOE_TT
    nmod = mod.shape[0]
    mod_map = (lambda b, j, s: (b, 0, 0)) if nmod > 1 else (lambda b, j, s: (0, 0, 0))
    tok = pl.BlockSpec((1, MOE_TT, D), lambda b, j, s: (b, j, 0))
    slots = pl.BlockSpec((1, E, MOE_TT), lambda b, j, s: (b, 0, j))
    return pl.pallas_call(
        functools.partial(_combine_kernel, nt=nt, rc=MOE_RC, final=final, row_off=row_off),
        grid_spec=pltpu.PrefetchScalarGridSpec(
            num_scalar_prefetch=1,
            grid=(B, nt),
            in_specs=[tok, slots, slots, pl.BlockSpec((1, 6, D), mod_map),
                      pl.BlockSpec((1, D), lambda b, j, s: (0, 0)), pl.BlockSpec(memory_space=pl.ANY)],
            out_specs=tok,
            scratch_shapes=[pltpu.VMEM((2, E * MOE_RC, D), BF), pltpu.SemaphoreType.DMA((2, E))],
        ),
        out_shape=jax.ShapeDtypeStruct((B, N, D), F32),
        compiler_params=_cparams(("arbitrary", "arbitrary")),
        name="combine_final" if final else "combine",
    )(base_flat, x, rel, aff, mod, g_final, ye)


FFN_TM = 512


def _slot_rows(B, N):
    T = B * N
    worst = EC_CAPACITY * T // N_EXPERTS + (SLOT_ALIGN - 1) * (T // MOE_TT) + MOE_RC
    return -(-worst // FFN_TM) * FFN_TM


def _moe_pair(sets, g_final, w_g, w_u, w_d, layer, final):
    D = sets[0][1].shape[2]
    rows = [_slot_rows(*st[1].shape[:2]) for st in sets]
    xe = jnp.zeros((N_EXPERTS, sum(rows), D), BF)
    routed, row_off = [], 0
    for (x, h2, lg_t, mod), r in zip(sets, rows):
        B, N, _ = h2.shape
        nt = N // MOE_TT
        aff, rel, base = _route(lg_t, EC_CAPACITY * B * N // N_EXPERTS)
        base_flat = base[:, :, :nt + 1].reshape(-1)
        xe = _dispatch(h2, rel, base_flat, xe, row_off)
        routed.append((aff, rel, base_flat, base[B - 1, :, nt], row_off))
        row_off += r
    ye = _expert_ffn(xe, jnp.concatenate([routed[0][3], routed[1][3]]), rows[0], w_g, w_u, w_d, layer, FFN_TM)
    return [_combine(x, rel, aff, mod, g_final, ye, base_flat, off, final)
            for (x, h2, lg_t, mod), (aff, rel, base_flat, _, off) in zip(sets, routed)]


def _rope_tables(n):
    t = jnp.arange(n)
    pos = {1: (t // GRID_W).astype(F32), 2: (t % GRID_W).astype(F32)}

    def group(quarter):
        freqs = jnp.float32(ROPE_BASE) ** (-jnp.arange(quarter, dtype=F32) / quarter)
        cs, ss = [], []
        for kind in (1, 2):
            ang = pos[kind][:, None] * freqs
            c, s = jnp.cos(ang), jnp.sin(ang)
            cs += [c, c]
            ss += [-s, s]
        return jnp.concatenate(cs, axis=1), jnp.concatenate(ss, axis=1)

    c32, s32 = group(8)
    c64, s64 = group(16)
    one = lambda w: jnp.ones((n, w), F32)
    zero = lambda w: jnp.zeros((n, w), F32)
    ta = (jnp.tile(c32, (1, 16)), jnp.tile(s32, (1, 16)))
    tc = (jnp.tile(c64, (1, 6)), jnp.tile(s64, (1, 6)))
    qc = jnp.concatenate([one(64), c32, one(32)], axis=1)
    qs = jnp.concatenate([zero(64), s32, zero(32)], axis=1)
    tq = (jnp.tile(qc, (1, 4)), jnp.tile(qs, (1, 4)))
    tk = (jnp.concatenate([c32, one(96)], axis=1), jnp.concatenate([s32, zero(96)], axis=1))
    return (ta[0], ta[1], tc[0], tc[1], tq[0], tq[1], tk[0], tk[1])


LAT_TQ = 512
C_HEAD_ORDER = (0, 2, 1, 3)


def _prep_layer(l, w_in, mla_w_uq, mla_w_ukv, w_gate, w_branch, w_out, w_router):
    D = D_MODEL
    wi = w_in[l]
    cq = wi[:, 1536:1792].reshape(D, 4, 64)[:, C_HEAD_ORDER, :].reshape(D, 256)
    def spread(cols, heads):
        return jnp.pad(cols.reshape(D, heads, 64), ((0, 0), (0, 0), (0, 64))).reshape(D, heads * LANES)

    w_in_p = jnp.concatenate(
        [wi[:, :512], spread(wi[:, 512:768], 4), wi[:, 768:1280], spread(wi[:, 1280:1536], 4),
         cq, spread(wi[:, 1920:2048], 2), wi[:, 1792:1920], wi[:, 2048:],
         jnp.zeros((D, LANES - MLA_ROPE), F32)], axis=1).astype(BF)
    uq = mla_w_uq[l].reshape(256, 4, MLA_NOPE + MLA_ROPE)
    w_uq_p = jnp.concatenate([uq, jnp.zeros((256, 4, 32), F32)], axis=2).reshape(256, 512).astype(BF)
    ukv = mla_w_ukv[l].reshape(128, 4, 128)
    w_k = jnp.concatenate([ukv[:, :, :64], jnp.zeros((128, 4, 64), F32)], axis=2).reshape(128, 512).astype(BF)
    w_v = jnp.concatenate([ukv[:, :, 64:], jnp.zeros((128, 4, 64), F32)], axis=2).reshape(128, 512).astype(BF)
    wb = w_branch[l]
    wb_c = wb[2].reshape(4, 64, D)[C_HEAD_ORDER, :, :].reshape(256, D)
    w_branch_p = jnp.stack([wb[0], wb[1], wb_c, wb[3]]).astype(BF)
    w_router_p = w_router[l].T.astype(BF)
    return dict(w_in=w_in_p, w_uq=w_uq_p, w_k=w_k, w_v=w_v, w_gate=w_gate[l].astype(BF),
                w_branch=w_branch_p, w_out=w_out[l].astype(BF), w_router=w_router_p)


def _kpe_placement():
    e = np.zeros((128, 512), np.float32)
    for h in range(4):
        for i in range(MLA_ROPE):
            e[i, 128 * h + MLA_NOPE + i] = 1.0
    return jnp.asarray(e, BF)


def _mixing(x, mod, lw, lp, lam_init, ctx, tables):
    B, N, D = x.shape
    latent = ctx is not None
    tm = 512 if latent else 256
    out_dtype = BF if latent else F32
    ctx_tail = None
    if latent:
        ck_a, cv_a, ck_b, cv_b, ck_c, cv_c, ckv_c, kpe_c = ctx
        P = ck_a.shape[1]

        def with_ones(v, heads):
            v = v.reshape(B, P, heads, 64).astype(BF)
            return jnp.concatenate([v, jnp.ones_like(v)], axis=-1).reshape(B, P, heads * LANES)

        ctx_tail = (ck_a.astype(BF), with_ones(cv_a, 4), ckv_c.astype(BF),
                    jnp.pad(kpe_c, ((0, 0), (0, 0), (0, 128 - MLA_ROPE))).astype(BF))
    q_a, k_a, v_a, qkv_b, qkv_c, q_d, ckv_n, kpe_r = _in_proj(
        x, mod, lp["g_norm1"], lw["w_in"], lp["mla_q_norm"], lw["w_uq"], lp["mla_kv_norm"],
        tables if latent else None, ctx_tail, out_dtype, tm)
    diff = (lp["da_lambda"], jnp.tile(lp["da_subln"], (1, 2)), lam_init)
    if latent:
        o_a = _dense_attn(q_a, (256, 0), k_a, (256, 0), v_a, (512, 0), groups=GROUPS_A,
                          tq=LAT_TQ, tk=(N + P) // 2, diff=diff, name="diff_attn")
        o_b = _nbr_attn(qkv_b, ck_b.astype(BF), with_ones(cv_b, 4), lp["na_bias"])
        o_c = _window_attn(qkv_c, ck_c.astype(BF), with_ones(cv_c, 2), lp["sw_sink"], 256)
        k_d, v_d = _mla_expand(ckv_n, kpe_r, lw["w_k"], lw["w_v"], lp["e_place"], (N + P) // 2)
        o_d = _dense_attn(q_d, (512, 0), k_d, (512, 0), v_d, (512, 0), groups=GROUPS_D,
                          tq=LAT_TQ, tk=(N + P) // 2, name="mla_attn")
    else:
        o_a = _dense_attn(q_a, (256, 0), k_a, (256, 0), v_a, (512, 0), groups=GROUPS_A,
                          tq=N, tk=N, diff=diff, name="diff_attn_ctx")
        o_b = _dense_attn(qkv_b, (256, 0), qkv_b, (256, 1), qkv_b, (512, 1), groups=GROUPS_B,
                          tq=N, tk=N, name="dense_attn_ctx")
        o_c = _dense_attn(qkv_c, (256, 0), qkv_c, (128, 4), qkv_c, (256, 1), groups=GROUPS_C,
                          tq=N, tk=N, sink=lp["sw_sink"], name="gqa_attn_ctx")
        k_d, v_d = _mla_expand(ckv_n, kpe_r, lw["w_k"], lw["w_v"], lp["e_place"], N)
        o_d = _dense_attn(q_d, (512, 0), k_d, (512, 0), v_d, (512, 0), groups=GROUPS_D,
                          tq=N, tk=N, name="mla_attn_ctx")
    x_new, h2, logits = _merge(x, mod, lp["g_norm1"], lp["g_norm2"], o_a, o_b, o_c, o_d,
                               lw["w_gate"], lw["w_branch"], lw["w_out"], lw["w_router"], tm)
    cache = None
    if not latent:
        strip = lambda v, heads: v.reshape(B, N, heads, LANES)[..., :64]
        cache = (k_a.reshape(B, N, 4, 64), strip(v_a, 4),
                 qkv_b[:, :, 256:512].reshape(B, N, 4, 64), strip(qkv_b[:, :, 512:1024], 4),
                 qkv_c[:, :, 512:640].reshape(B, N, 2, 64), strip(qkv_c[:, :, 256:512], 2),
                 ckv_n, kpe_r[:, :, :MLA_ROPE])
    return x_new, h2, logits, cache


def kernel(x_prompt, x_sample, cache_diff_k, cache_diff_v, cache_na_k, cache_na_v, cache_swa_k, cache_swa_v, cache_mla_ckv, cache_mla_kpe, c, c_ctx, w_mod, b_mod, g_norm1, g_norm2, w_in, da_lambda, da_subln, na_rpb, sw_sink, mla_q_norm, mla_w_uq, mla_kv_norm, mla_w_ukv, w_gate, w_branch, w_out, w_router, w_e_gate, w_e_up, w_e_down, g_final):
    D = D_MODEL
    BS, NS, _ = x_sample.shape
    P = cache_diff_k.shape[2]
    cond = jnp.concatenate([c_ctx[None], c, jnp.zeros((16 - 1 - BS, D), F32)], axis=0)
    mod_all = _modulation(cond, w_mod, b_mod).reshape(DEPTH, 16, 6, D)
    tables = _rope_tables(NS)
    e_place = _kpe_placement()
    gf = g_final.reshape(1, D)
    na_bias = _nbr_bias_tables(na_rpb, NS // GRID_W)
    xp, xs = x_prompt, x_sample
    caches = []
    for l in range(DEPTH):
        lw = _prep_layer(l, w_in, mla_w_uq, mla_w_ukv, w_gate, w_branch, w_out, w_router)
        lp = dict(
            g_norm1=g_norm1[l].reshape(1, D), g_norm2=g_norm2[l].reshape(1, D),
            mla_q_norm=mla_q_norm[l].reshape(1, 256), mla_kv_norm=mla_kv_norm[l].reshape(1, 128),
            da_lambda=da_lambda[l], da_subln=da_subln[l].reshape(1, 64), sw_sink=sw_sink[l],
            e_place=e_place, na_bias=na_bias[l])
        lam_init = 0.8 - 0.6 * math.exp(-0.3 * l)
        final = l == DEPTH - 1
        mod_p, mod_s = mod_all[l, 0:1], mod_all[l, 1:1 + BS]
        xp, h2_p, lg_p, cache_l = _mixing(xp, mod_p, lw, lp, lam_init, None, None)
        caches.append(cache_l)
        ctx = (cache_diff_k[:, l].reshape(BS, P, 256), cache_diff_v[:, l].reshape(BS, P, 256),
               cache_na_k[:, l].reshape(BS, P, 256), cache_na_v[:, l].reshape(BS, P, 256),
               cache_swa_k[:, l].reshape(BS, P, 128), cache_swa_v[:, l].reshape(BS, P, 128),
               cache_mla_ckv[:, l], cache_mla_kpe[:, l])
        xs, h2_s, lg_s, _ = _mixing(xs, mod_s, lw, lp, lam_init, ctx, tables)
        xs, xp = _moe_pair([(xs, h2_s, lg_s, mod_s), (xp, h2_p, lg_p, mod_p)], gf, w_e_gate, w_e_up, w_e_down,
                           l, final)
    outs = tuple(jnp.stack([cl[i] for cl in caches], axis=1) for i in range(8))
    return (xp, xs) + outs
```
